```python
import jax, jax.numpy as jnp
from jax import lax
import numpy as np

D_MODEL = 1024
BATCH = 32
SEQ = 2048
DEPTH = 4

CHUNK = 64
Q_BLOCK = 128
EPS = 1e-6
NEG_INF = -1e30

MLA_HEADS = 8
QK_NOPE = 64
QK_ROPE = 32
V_HEAD = 64
Q_LORA = 256
KV_LORA = 128
ROPE_THETA = 10000.0
MLA_WIDTH = MLA_HEADS * V_HEAD

POOL_WINDOWS = (2, 4, 8, 16)
POOL_GROUPS = 4
POOL_GROUP_DIM = 64
POOL_WIDTH = POOL_GROUPS * POOL_GROUP_DIM
POOL_MAX_WIN = 16

SGU_BLOCK = 128
SGU_GROUPS = 4
SGU_GROUP_DIM = 64
SGU_WIDTH = SGU_GROUPS * SGU_GROUP_DIM

CONV_K = 3
CONV_WIDTH = 256

N_BRANCH = 4
D_FF = ((8 * D_MODEL + 3 * 256 - 1) // (3 * 256)) * 256

IN_SIZES = (Q_LORA, KV_LORA, QK_ROPE, POOL_WIDTH, SGU_WIDTH, SGU_WIDTH,
            CONV_WIDTH, CONV_WIDTH, CONV_WIDTH, N_BRANCH * D_MODEL)
IN_WIDTH = sum(IN_SIZES)

kernel_name = "hybrid_gated_mla_pool_sgu_conv_trunk"


def rmsnorm(x, g):
    xf = x.astype(jnp.float32)
    y = xf * lax.rsqrt(jnp.mean(xf * xf, axis=-1, keepdims=True) + EPS)
    return (y * g.astype(jnp.float32)).astype(x.dtype)


def split_cols(p):
    out = []
    o = 0
    for n in IN_SIZES:
        out.append(p[..., o:o + n])
        o += n
    return out


def rope_tables(seq, dtype):
    inv = ROPE_THETA ** (-jnp.arange(0, QK_ROPE, 2, dtype=jnp.float32) / QK_ROPE)
    ang = jnp.arange(seq, dtype=jnp.float32)[:, None] * inv[None, :]
    return jnp.cos(ang).astype(dtype), jnp.sin(ang).astype(dtype)


def apply_rope(x, cos, sin):
    half = x.shape[-1] // 2
    x1, x2 = x[..., :half], x[..., half:]
    return jnp.concatenate([x1 * cos - x2 * sin, x2 * cos + x1 * sin], axis=-1)


def mla_branch(c_q, c_kv, k_r, g_cq, g_ckv, w_uq, w_ukv, cos, sin):
    B, S, _ = c_q.shape
    q = (rmsnorm(c_q, g_cq) @ w_uq).reshape(B, S, MLA_HEADS, QK_NOPE + QK_ROPE)
    q_nope = q[..., :QK_NOPE]
    q_rope = apply_rope(q[..., QK_NOPE:], cos[:, None, :], sin[:, None, :])
    kv = (rmsnorm(c_kv, g_ckv) @ w_ukv).reshape(B, S, MLA_HEADS, QK_NOPE + V_HEAD)
    k_nope, v = kv[..., :QK_NOPE], kv[..., QK_NOPE:]
    k_rope = apply_rope(k_r, cos, sin)
    scale = (QK_NOPE + QK_ROPE) ** -0.5
    chunk_id = jnp.arange(S) // CHUNK
    outs = []
    for blk in range(S // Q_BLOCK):
        q0 = blk * Q_BLOCK
        kend = q0 + Q_BLOCK
        s = (jnp.einsum('bqhd,bkhd->bhqk', q_nope[:, q0:kend], k_nope[:, :kend])
             + jnp.einsum('bqhr,bkr->bhqk', q_rope[:, q0:kend], k_rope[:, :kend]))
        s = s.astype(jnp.float32) * scale
        mask = chunk_id[q0:kend, None] >= chunk_id[None, :kend]
        s = jnp.where(mask[None, None], s, NEG_INF)
        p = jax.nn.softmax(s, axis=-1).astype(v.dtype)
        outs.append(jnp.einsum('bhqk,bkhd->bqhd', p, v[:, :kend]))
    return jnp.concatenate(outs, axis=1).reshape(B, S, MLA_WIDTH)


def pool_branch(z, w_grp, scale):
    B, S, _ = z.shape
    zg = z.reshape(B, S, POOL_GROUPS, POOL_GROUP_DIM).astype(jnp.float32)
    csp = jnp.pad(jnp.cumsum(zg, axis=1), ((0, 0), (POOL_MAX_WIN, 0), (0, 0), (0, 0)))
    t = jnp.arange(S)
    outs = []
    for g, w in enumerate(POOL_WINDOWS):
        win_sum = (csp[:, POOL_MAX_WIN:POOL_MAX_WIN + S, g]
                   - csp[:, POOL_MAX_WIN - w:POOL_MAX_WIN - w + S, g])
        count = jnp.minimum(t + 1, w).astype(jnp.float32)
        outs.append(win_sum / count[None, :, None] - zg[:, :, g])
    pooled = jnp.stack(outs, axis=2).astype(z.dtype)
    mixed = jnp.einsum('bsgc,gcd->bsgd', pooled, w_grp)
    return mixed.reshape(B, S, POOL_WIDTH) * scale


def sgu_branch(u, v, g_v, w_s, b_s):
    B, S, _ = u.shape
    n = S // SGU_BLOCK
    vn = rmsnorm(v, g_v).reshape(B, n, SGU_BLOCK, SGU_GROUPS, SGU_GROUP_DIM)
    pos_chunk = jnp.arange(SGU_BLOCK) // CHUNK
    mask = pos_chunk[:, None] >= pos_chunk[None, :]
    w = jnp.where(mask[None], w_s, 0)
    mixed = jnp.einsum('gij,bnjgc->bnigc', w, vn) + b_s.T[None, None, :, :, None]
    return u * mixed.reshape(B, S, SGU_WIDTH)


def conv_branch(b_gate, c_gate, x_in, w_conv):
    z = c_gate * x_in
    y = lax.conv_general_dilated(z, w_conv, window_strides=(1,),
                                 padding=[(CONV_K - 1, 0)],
                                 dimension_numbers=('NWC', 'WIO', 'NWC'),
                                 feature_group_count=CONV_WIDTH)
    return b_gate * y


def _fwd_setup_inputs(seed: int = 0) -> dict:
    key = jax.random.key(seed)
    ks = iter(jax.random.split(key, 32))

    def nrm(shape, fan_in):
        return jax.random.normal(next(ks), shape, jnp.float32) * (fan_in ** -0.5)

    def gain(shape):
        return 1.0 + 0.1 * jax.random.normal(next(ks), shape, jnp.float32)

    L = DEPTH
    return {
        "x": jax.random.normal(next(ks), (BATCH, SEQ, D_MODEL), jnp.float32),
        "w_in": nrm((L, D_MODEL, IN_WIDTH), D_MODEL),
        "g_pre_mix": gain((L, D_MODEL)),
        "g_cq": gain((L, Q_LORA)),
        "g_ckv": gain((L, KV_LORA)),
        "w_uq": nrm((L, Q_LORA, MLA_HEADS * (QK_NOPE + QK_ROPE)), Q_LORA),
        "w_ukv": nrm((L, KV_LORA, MLA_HEADS * (QK_NOPE + V_HEAD)), KV_LORA),
        "pool_w": nrm((L, POOL_GROUPS, POOL_GROUP_DIM, POOL_GROUP_DIM), POOL_GROUP_DIM),
        "pool_scale": gain((L, POOL_WIDTH)),
        "g_sgu_v": gain((L, SGU_WIDTH)),
        "sgu_w": nrm((L, SGU_GROUPS, SGU_BLOCK, SGU_BLOCK), SGU_BLOCK),
        "sgu_b": gain((L, SGU_GROUPS, SGU_BLOCK)),
        "conv_w": nrm((L, CONV_K, 1, CONV_WIDTH), CONV_K),
        "w_br_a": nrm((L, MLA_WIDTH, D_MODEL), MLA_WIDTH),
        "w_br_b": nrm((L, POOL_WIDTH, D_MODEL), POOL_WIDTH),
        "w_br_c": nrm((L, SGU_WIDTH, D_MODEL), SGU_WIDTH),
        "w_br_d": nrm((L, CONV_WIDTH, D_MODEL), CONV_WIDTH),
        "w_out": nrm((L, D_MODEL, D_MODEL), D_MODEL),
        "g_post_mix": gain((L, D_MODEL)),
        "g_pre_ffn": gain((L, D_MODEL)),
        "w_ffn_gate": nrm((L, D_MODEL, D_FF), D_MODEL),
        "w_ffn_up": nrm((L, D_MODEL, D_FF), D_MODEL),
        "w_ffn_down": nrm((L, D_FF, D_MODEL), D_FF),
        "g_post_ffn": gain((L, D_MODEL)),
    }


def _fwd_reference(x, w_in, g_pre_mix, g_cq, g_ckv, w_uq, w_ukv, pool_w, pool_scale,
              g_sgu_v, sgu_w, sgu_b, conv_w, w_br_a, w_br_b, w_br_c, w_br_d,
              w_out, g_post_mix, g_pre_ffn, w_ffn_gate, w_ffn_up, w_ffn_down,
              g_post_ffn):
    B, S, D = x.shape
    cos, sin = rope_tables(S, x.dtype)
    for l in range(DEPTH):
        h = rmsnorm(x, g_pre_mix[l])
        (c_q, c_kv, k_r, p_in, s_u, s_v, cv_b, cv_c, cv_x,
         gate_logits) = split_cols(h @ w_in[l])
        y_a = mla_branch(c_q, c_kv, k_r, g_cq[l], g_ckv[l], w_uq[l], w_ukv[l], cos, sin) @ w_br_a[l]
        y_b = pool_branch(p_in, pool_w[l], pool_scale[l]) @ w_br_b[l]
        y_c = sgu_branch(s_u, s_v, g_sgu_v[l], sgu_w[l], sgu_b[l]) @ w_br_c[l]
        y_d = conv_branch(cv_b, cv_c, cv_x, conv_w[l]) @ w_br_d[l]
        gates = jax.nn.sigmoid(gate_logits.astype(jnp.float32)).astype(x.dtype)
        gates = gates.reshape(B, S, N_BRANCH, D)
        merged = (gates[:, :, 0] * y_a + gates[:, :, 1] * y_b
                  + gates[:, :, 2] * y_c + gates[:, :, 3] * y_d)
        x = x + rmsnorm(merged @ w_out[l], g_post_mix[l])
        h = rmsnorm(x, g_pre_ffn[l])
        f = (jax.nn.silu(h @ w_ffn_gate[l]) * (h @ w_ffn_up[l])) @ w_ffn_down[l]
        x = x + rmsnorm(f, g_post_ffn[l])
    return x


import jax as _jax
import jax.numpy as _jnp

TWIN_FORMAT = 'train_step'
FWD_PARAMS = ['x', 'w_in', 'g_pre_mix', 'g_cq', 'g_ckv', 'w_uq', 'w_ukv', 'pool_w', 'pool_scale', 'g_sgu_v', 'sgu_w', 'sgu_b', 'conv_w', 'w_br_a', 'w_br_b', 'w_br_c', 'w_br_d', 'w_out', 'g_post_mix', 'g_pre_ffn', 'w_ffn_gate', 'w_ffn_up', 'w_ffn_down', 'g_post_ffn']
TWIN_WEIGHTS = ['w_in', 'g_pre_mix', 'g_cq', 'g_ckv', 'w_uq', 'w_ukv', 'pool_w', 'pool_scale', 'g_sgu_v', 'sgu_w', 'sgu_b', 'conv_w', 'w_br_a', 'w_br_b', 'w_br_c', 'w_br_d', 'w_out', 'g_post_mix', 'g_pre_ffn', 'w_ffn_gate', 'w_ffn_up', 'w_ffn_down', 'g_post_ffn']
TWIN_DIFF_INPUT = 'x'
TWIN_INPUTS = ['x', 'w_in', 'g_pre_mix', 'g_cq', 'g_ckv', 'w_uq', 'w_ukv', 'pool_w', 'pool_scale', 'g_sgu_v', 'sgu_w', 'sgu_b', 'conv_w', 'w_br_a', 'w_br_b', 'w_br_c', 'w_br_d', 'w_out', 'g_post_mix', 'g_pre_ffn', 'w_ffn_gate', 'w_ffn_up', 'w_ffn_down', 'g_post_ffn', 'loss_target', 'm_w_in', 'm_g_pre_mix', 'm_g_cq', 'm_g_ckv', 'm_w_uq', 'm_w_ukv', 'm_pool_w', 'm_pool_scale', 'm_g_sgu_v', 'm_sgu_w', 'm_sgu_b', 'm_conv_w', 'm_w_br_a', 'm_w_br_b', 'm_w_br_c', 'm_w_br_d', 'm_w_out', 'm_g_post_mix', 'm_g_pre_ffn', 'm_w_ffn_gate', 'm_w_ffn_up', 'm_w_ffn_down', 'm_g_post_ffn', 'v_w_in', 'v_g_pre_mix', 'v_g_cq', 'v_g_ckv', 'v_w_uq', 'v_w_ukv', 'v_pool_w', 'v_pool_scale', 'v_g_sgu_v', 'v_sgu_w', 'v_sgu_b', 'v_conv_w', 'v_w_br_a', 'v_w_br_b', 'v_w_br_c', 'v_w_br_d', 'v_w_out', 'v_g_post_mix', 'v_g_pre_ffn', 'v_w_ffn_gate', 'v_w_ffn_up', 'v_w_ffn_down', 'v_g_post_ffn']
TWIN_OUTPUTS = ['loss', 'grad_x', 'grad_w_in', 'grad_g_pre_mix', 'grad_g_cq', 'grad_g_ckv', 'grad_w_uq', 'grad_w_ukv', 'grad_pool_w', 'grad_pool_scale', 'grad_g_sgu_v', 'grad_sgu_w', 'grad_sgu_b', 'grad_conv_w', 'grad_w_br_a', 'grad_w_br_b', 'grad_w_br_c', 'grad_w_br_d', 'grad_w_out', 'grad_g_post_mix', 'grad_g_pre_ffn', 'grad_w_ffn_gate', 'grad_w_ffn_up', 'grad_w_ffn_down', 'grad_g_post_ffn', 'delta_w_in', 'delta_g_pre_mix', 'delta_g_cq', 'delta_g_ckv', 'delta_w_uq', 'delta_w_ukv', 'delta_pool_w', 'delta_pool_scale', 'delta_g_sgu_v', 'delta_sgu_w', 'delta_sgu_b', 'delta_conv_w', 'delta_w_br_a', 'delta_w_br_b', 'delta_w_br_c', 'delta_w_br_d', 'delta_w_out', 'delta_g_post_mix', 'delta_g_pre_ffn', 'delta_w_ffn_gate', 'delta_w_ffn_up', 'delta_w_ffn_down', 'delta_g_post_ffn', 'new_m_w_in', 'new_m_g_pre_mix', 'new_m_g_cq', 'new_m_g_ckv', 'new_m_w_uq', 'new_m_w_ukv', 'new_m_pool_w', 'new_m_pool_scale', 'new_m_g_sgu_v', 'new_m_sgu_w', 'new_m_sgu_b', 'new_m_conv_w', 'new_m_w_br_a', 'new_m_w_br_b', 'new_m_w_br_c', 'new_m_w_br_d', 'new_m_w_out', 'new_m_g_post_mix', 'new_m_g_pre_ffn', 'new_m_w_ffn_gate', 'new_m_w_ffn_up', 'new_m_w_ffn_down', 'new_m_g_post_ffn', 'new_v_w_in', 'new_v_g_pre_mix', 'new_v_g_cq', 'new_v_g_ckv', 'new_v_w_uq', 'new_v_w_ukv', 'new_v_pool_w', 'new_v_pool_scale', 'new_v_g_sgu_v', 'new_v_sgu_w', 'new_v_sgu_b', 'new_v_conv_w', 'new_v_w_br_a', 'new_v_w_br_b', 'new_v_w_br_c', 'new_v_w_br_d', 'new_v_w_out', 'new_v_g_post_mix', 'new_v_g_pre_ffn', 'new_v_w_ffn_gate', 'new_v_w_ffn_up', 'new_v_w_ffn_down', 'new_v_g_post_ffn']
TWIN_LEAF_KINDS = {'loss': 'loss', 'grad_x': 'grad_x', 'grad_w_in': 'grad_w', 'grad_g_pre_mix': 'grad_w', 'grad_g_cq': 'grad_w', 'grad_g_ckv': 'grad_w', 'grad_w_uq': 'grad_w', 'grad_w_ukv': 'grad_w', 'grad_pool_w': 'grad_w', 'grad_pool_scale': 'grad_w', 'grad_g_sgu_v': 'grad_w', 'grad_sgu_w': 'grad_w', 'grad_sgu_b': 'grad_w', 'grad_conv_w': 'grad_w', 'grad_w_br_a': 'grad_w', 'grad_w_br_b': 'grad_w', 'grad_w_br_c': 'grad_w', 'grad_w_br_d': 'grad_w', 'grad_w_out': 'grad_w', 'grad_g_post_mix': 'grad_w', 'grad_g_pre_ffn': 'grad_w', 'grad_w_ffn_gate': 'grad_w', 'grad_w_ffn_up': 'grad_w', 'grad_w_ffn_down': 'grad_w', 'grad_g_post_ffn': 'grad_w', 'delta_w_in': 'delta_w', 'delta_g_pre_mix': 'delta_w', 'delta_g_cq': 'delta_w', 'delta_g_ckv': 'delta_w', 'delta_w_uq': 'delta_w', 'delta_w_ukv': 'delta_w', 'delta_pool_w': 'delta_w', 'delta_pool_scale': 'delta_w', 'delta_g_sgu_v': 'delta_w', 'delta_sgu_w': 'delta_w', 'delta_sgu_b': 'delta_w', 'delta_conv_w': 'delta_w', 'delta_w_br_a': 'delta_w', 'delta_w_br_b': 'delta_w', 'delta_w_br_c': 'delta_w', 'delta_w_br_d': 'delta_w', 'delta_w_out': 'delta_w', 'delta_g_post_mix': 'delta_w', 'delta_g_pre_ffn': 'delta_w', 'delta_w_ffn_gate': 'delta_w', 'delta_w_ffn_up': 'delta_w', 'delta_w_ffn_down': 'delta_w', 'delta_g_post_ffn': 'delta_w', 'new_m_w_in': 'new_m', 'new_m_g_pre_mix': 'new_m', 'new_m_g_cq': 'new_m', 'new_m_g_ckv': 'new_m', 'new_m_w_uq': 'new_m', 'new_m_w_ukv': 'new_m', 'new_m_pool_w': 'new_m', 'new_m_pool_scale': 'new_m', 'new_m_g_sgu_v': 'new_m', 'new_m_sgu_w': 'new_m', 'new_m_sgu_b': 'new_m', 'new_m_conv_w': 'new_m', 'new_m_w_br_a': 'new_m', 'new_m_w_br_b': 'new_m', 'new_m_w_br_c': 'new_m', 'new_m_w_br_d': 'new_m', 'new_m_w_out': 'new_m', 'new_m_g_post_mix': 'new_m', 'new_m_g_pre_ffn': 'new_m', 'new_m_w_ffn_gate': 'new_m', 'new_m_w_ffn_up': 'new_m', 'new_m_w_ffn_down': 'new_m', 'new_m_g_post_ffn': 'new_m', 'new_v_w_in': 'new_v', 'new_v_g_pre_mix': 'new_v', 'new_v_g_cq': 'new_v', 'new_v_g_ckv': 'new_v', 'new_v_w_uq': 'new_v', 'new_v_w_ukv': 'new_v', 'new_v_pool_w': 'new_v', 'new_v_pool_scale': 'new_v', 'new_v_g_sgu_v': 'new_v', 'new_v_sgu_w': 'new_v', 'new_v_sgu_b': 'new_v', 'new_v_conv_w': 'new_v', 'new_v_w_br_a': 'new_v', 'new_v_w_br_b': 'new_v', 'new_v_w_br_c': 'new_v', 'new_v_w_br_d': 'new_v', 'new_v_w_out': 'new_v', 'new_v_g_post_mix': 'new_v', 'new_v_g_pre_ffn': 'new_v', 'new_v_w_ffn_gate': 'new_v', 'new_v_w_ffn_up': 'new_v', 'new_v_w_ffn_down': 'new_v', 'new_v_g_post_ffn': 'new_v'}


def _forward(args):
    return _fwd_reference(*[args[k] for k in FWD_PARAMS])


def _output_shape():
    out = _jax.eval_shape(lambda: _forward(_fwd_setup_inputs(0)))
    return out.shape, out.dtype

N_MICROBATCH = 1
ADAM_LR = 0.001
ADAM_B1 = 0.9
ADAM_B2 = 0.999
ADAM_EPS = 1e-08
ADAM_WD = 0.01
ADAM_STEP = 10
PER_EXAMPLE_BATCH_AXIS = {'x': 0, 'loss_target': 0}
SHARED_INPUTS = []
_WEIGHT_DTYPES = {'w_in': _jnp.float32, 'g_pre_mix': _jnp.float32, 'g_cq': _jnp.float32, 'g_ckv': _jnp.float32, 'w_uq': _jnp.float32, 'w_ukv': _jnp.float32, 'pool_w': _jnp.float32, 'pool_scale': _jnp.float32, 'g_sgu_v': _jnp.float32, 'sgu_w': _jnp.float32, 'sgu_b': _jnp.float32, 'conv_w': _jnp.float32, 'w_br_a': _jnp.float32, 'w_br_b': _jnp.float32, 'w_br_c': _jnp.float32, 'w_br_d': _jnp.float32, 'w_out': _jnp.float32, 'g_post_mix': _jnp.float32, 'g_pre_ffn': _jnp.float32, 'w_ffn_gate': _jnp.float32, 'w_ffn_up': _jnp.float32, 'w_ffn_down': _jnp.float32, 'g_post_ffn': _jnp.float32}
MOMENT_SCALE = {'w_in': 1.208257e+00, 'g_pre_mix': 2.967056e+00, 'g_cq': 3.177965e-01, 'g_ckv': 6.489606e-01, 'w_uq': 1.820160e-01, 'w_ukv': 2.168708e-01, 'pool_w': 2.720806e+00, 'pool_scale': 2.587443e+00, 'g_sgu_v': 1.747817e+00, 'sgu_w': 1.197751e+00, 'sgu_b': 1.339937e+00, 'conv_w': 2.144320e+00, 'w_br_a': 1.691053e-01, 'w_br_b': 1.374543e+00, 'w_br_c': 1.834465e+00, 'w_br_d': 1.020561e+00, 'w_out': 2.564372e+00, 'g_post_mix': 6.284946e+01, 'g_pre_ffn': 2.170852e+00, 'w_ffn_gate': 8.642705e-01, 'w_ffn_up': 1.005722e+00, 'w_ffn_down': 1.699766e+00, 'g_post_ffn': 6.306529e+01}


def _to_microbatches(a, axis):
    t = _jnp.moveaxis(a, axis, 0)
    t = t.reshape((N_MICROBATCH, t.shape[0] // N_MICROBATCH) + t.shape[1:])
    return _jnp.moveaxis(t, 1, axis + 1)


def setup_inputs(seed: int = 0) -> dict:
    inp = _fwd_setup_inputs(seed)
    key = _jax.random.fold_in(_jax.random.key(seed), 7919)
    shape, _ = _output_shape()
    out = dict(inp)
    out["loss_target"] = _jax.random.normal(_jax.random.fold_in(key, 0), shape, _jnp.float32)
    for i, name in enumerate(TWIN_WEIGHTS):
        w = inp[name].astype(_jnp.float32)
        if MOMENT_SCALE is None:
            s = _jnp.sqrt(_jnp.mean(_jnp.square(w)) + 1e-30)
        else:
            s = MOMENT_SCALE[name]
        km, kv = _jax.random.split(_jax.random.fold_in(key, i + 1))
        out[name] = w
        out["m_" + name] = s * _jax.random.normal(km, w.shape, _jnp.float32)
        out["v_" + name] = (s * s) * _jax.random.uniform(kv, w.shape, _jnp.float32, 0.5, 1.5)
    if N_MICROBATCH > 1:
        for name, axis in PER_EXAMPLE_BATCH_AXIS.items():
            out[name] = _to_microbatches(out[name], axis)
    return {'x': out['x'], 'w_in': out['w_in'], 'g_pre_mix': out['g_pre_mix'], 'g_cq': out['g_cq'], 'g_ckv': out['g_ckv'], 'w_uq': out['w_uq'], 'w_ukv': out['w_ukv'], 'pool_w': out['pool_w'], 'pool_scale': out['pool_scale'], 'g_sgu_v': out['g_sgu_v'], 'sgu_w': out['sgu_w'], 'sgu_b': out['sgu_b'], 'conv_w': out['conv_w'], 'w_br_a': out['w_br_a'], 'w_br_b': out['w_br_b'], 'w_br_c': out['w_br_c'], 'w_br_d': out['w_br_d'], 'w_out': out['w_out'], 'g_post_mix': out['g_post_mix'], 'g_pre_ffn': out['g_pre_ffn'], 'w_ffn_gate': out['w_ffn_gate'], 'w_ffn_up': out['w_ffn_up'], 'w_ffn_down': out['w_ffn_down'], 'g_post_ffn': out['g_post_ffn'], 'loss_target': out['loss_target'], 'm_w_in': out['m_w_in'], 'm_g_pre_mix': out['m_g_pre_mix'], 'm_g_cq': out['m_g_cq'], 'm_g_ckv': out['m_g_ckv'], 'm_w_uq': out['m_w_uq'], 'm_w_ukv': out['m_w_ukv'], 'm_pool_w': out['m_pool_w'], 'm_pool_scale': out['m_pool_scale'], 'm_g_sgu_v': out['m_g_sgu_v'], 'm_sgu_w': out['m_sgu_w'], 'm_sgu_b': out['m_sgu_b'], 'm_conv_w': out['m_conv_w'], 'm_w_br_a': out['m_w_br_a'], 'm_w_br_b': out['m_w_br_b'], 'm_w_br_c': out['m_w_br_c'], 'm_w_br_d': out['m_w_br_d'], 'm_w_out': out['m_w_out'], 'm_g_post_mix': out['m_g_post_mix'], 'm_g_pre_ffn': out['m_g_pre_ffn'], 'm_w_ffn_gate': out['m_w_ffn_gate'], 'm_w_ffn_up': out['m_w_ffn_up'], 'm_w_ffn_down': out['m_w_ffn_down'], 'm_g_post_ffn': out['m_g_post_ffn'], 'v_w_in': out['v_w_in'], 'v_g_pre_mix': out['v_g_pre_mix'], 'v_g_cq': out['v_g_cq'], 'v_g_ckv': out['v_g_ckv'], 'v_w_uq': out['v_w_uq'], 'v_w_ukv': out['v_w_ukv'], 'v_pool_w': out['v_pool_w'], 'v_pool_scale': out['v_pool_scale'], 'v_g_sgu_v': out['v_g_sgu_v'], 'v_sgu_w': out['v_sgu_w'], 'v_sgu_b': out['v_sgu_b'], 'v_conv_w': out['v_conv_w'], 'v_w_br_a': out['v_w_br_a'], 'v_w_br_b': out['v_w_br_b'], 'v_w_br_c': out['v_w_br_c'], 'v_w_br_d': out['v_w_br_d'], 'v_w_out': out['v_w_out'], 'v_g_post_mix': out['v_g_post_mix'], 'v_g_pre_ffn': out['v_g_pre_ffn'], 'v_w_ffn_gate': out['v_w_ffn_gate'], 'v_w_ffn_up': out['v_w_ffn_up'], 'v_w_ffn_down': out['v_w_ffn_down'], 'v_g_post_ffn': out['v_g_post_ffn']}


def _loss(weights, diff, rest, loss_target):
    with _jax.named_scope("forward"):
        args = {**rest, TWIN_DIFF_INPUT: diff, **{k: w.astype(_WEIGHT_DTYPES[k]) for k, w in weights.items()}}
        y = _forward(args)
    with _jax.named_scope("loss_head"):
        err = _jnp.square(y.astype(_jnp.float32) - loss_target)
        return 0.5 * _jnp.sum(_jnp.mean(err, axis=-1)) if err.ndim else 0.5 * err


def _adamw(w, g, m, v):
    m = ADAM_B1 * m + (1.0 - ADAM_B1) * g
    v = ADAM_B2 * v + (1.0 - ADAM_B2) * _jnp.square(g)
    m_hat = m / (1.0 - ADAM_B1 ** ADAM_STEP)
    v_hat = v / (1.0 - ADAM_B2 ** ADAM_STEP)
    delta = -ADAM_LR * (m_hat / (_jnp.sqrt(v_hat) + ADAM_EPS) + ADAM_WD * w)
    return delta, m, v


def reference(x, w_in, g_pre_mix, g_cq, g_ckv, w_uq, w_ukv, pool_w, pool_scale, g_sgu_v, sgu_w, sgu_b, conv_w, w_br_a, w_br_b, w_br_c, w_br_d, w_out, g_post_mix, g_pre_ffn, w_ffn_gate, w_ffn_up, w_ffn_down, g_post_ffn, loss_target, m_w_in, m_g_pre_mix, m_g_cq, m_g_ckv, m_w_uq, m_w_ukv, m_pool_w, m_pool_scale, m_g_sgu_v, m_sgu_w, m_sgu_b, m_conv_w, m_w_br_a, m_w_br_b, m_w_br_c, m_w_br_d, m_w_out, m_g_post_mix, m_g_pre_ffn, m_w_ffn_gate, m_w_ffn_up, m_w_ffn_down, m_g_post_ffn, v_w_in, v_g_pre_mix, v_g_cq, v_g_ckv, v_w_uq, v_w_ukv, v_pool_w, v_pool_scale, v_g_sgu_v, v_sgu_w, v_sgu_b, v_conv_w, v_w_br_a, v_w_br_b, v_w_br_c, v_w_br_d, v_w_out, v_g_post_mix, v_g_pre_ffn, v_w_ffn_gate, v_w_ffn_up, v_w_ffn_down, v_g_post_ffn):
    given = dict(x=x, w_in=w_in, g_pre_mix=g_pre_mix, g_cq=g_cq, g_ckv=g_ckv, w_uq=w_uq, w_ukv=w_ukv, pool_w=pool_w, pool_scale=pool_scale, g_sgu_v=g_sgu_v, sgu_w=sgu_w, sgu_b=sgu_b, conv_w=conv_w, w_br_a=w_br_a, w_br_b=w_br_b, w_br_c=w_br_c, w_br_d=w_br_d, w_out=w_out, g_post_mix=g_post_mix, g_pre_ffn=g_pre_ffn, w_ffn_gate=w_ffn_gate, w_ffn_up=w_ffn_up, w_ffn_down=w_ffn_down, g_post_ffn=g_post_ffn, loss_target=loss_target, m_w_in=m_w_in, m_g_pre_mix=m_g_pre_mix, m_g_cq=m_g_cq, m_g_ckv=m_g_ckv, m_w_uq=m_w_uq, m_w_ukv=m_w_ukv, m_pool_w=m_pool_w, m_pool_scale=m_pool_scale, m_g_sgu_v=m_g_sgu_v, m_sgu_w=m_sgu_w, m_sgu_b=m_sgu_b, m_conv_w=m_conv_w, m_w_br_a=m_w_br_a, m_w_br_b=m_w_br_b, m_w_br_c=m_w_br_c, m_w_br_d=m_w_br_d, m_w_out=m_w_out, m_g_post_mix=m_g_post_mix, m_g_pre_ffn=m_g_pre_ffn, m_w_ffn_gate=m_w_ffn_gate, m_w_ffn_up=m_w_ffn_up, m_w_ffn_down=m_w_ffn_down, m_g_post_ffn=m_g_post_ffn, v_w_in=v_w_in, v_g_pre_mix=v_g_pre_mix, v_g_cq=v_g_cq, v_g_ckv=v_g_ckv, v_w_uq=v_w_uq, v_w_ukv=v_w_ukv, v_pool_w=v_pool_w, v_pool_scale=v_pool_scale, v_g_sgu_v=v_g_sgu_v, v_sgu_w=v_sgu_w, v_sgu_b=v_sgu_b, v_conv_w=v_conv_w, v_w_br_a=v_w_br_a, v_w_br_b=v_w_br_b, v_w_br_c=v_w_br_c, v_w_br_d=v_w_br_d, v_w_out=v_w_out, v_g_post_mix=v_g_post_mix, v_g_pre_ffn=v_g_pre_ffn, v_w_ffn_gate=v_w_ffn_gate, v_w_ffn_up=v_w_ffn_up, v_w_ffn_down=v_w_ffn_down, v_g_post_ffn=v_g_post_ffn)
    weights = {n: given[n] for n in TWIN_WEIGHTS}
    shared = {n: given[n] for n in SHARED_INPUTS}
    per_example = {n: given[n] for n in ['x']}
    grad_fn = _jax.value_and_grad(_loss, argnums=(0, 1))

    def one_microbatch(ex, loss_target):
        ex = dict(ex)
        diff = ex.pop(TWIN_DIFF_INPUT)
        return grad_fn(weights, diff, {**shared, **ex}, loss_target)

    if N_MICROBATCH == 1:
        loss, (grad_w, grad_x) = one_microbatch(per_example, given["loss_target"])
    else:
        def body(carry, xs):
            loss_sum, grad_sum = carry
            l_k, (gw_k, gx_k) = one_microbatch(xs[0], xs[1])
            with _jax.named_scope("update"):
                return (loss_sum + l_k, _jax.tree.map(_jnp.add, grad_sum, gw_k)), gx_k

        init = (_jnp.zeros((), _jnp.float32), _jax.tree.map(_jnp.zeros_like, weights))
        (loss, grad_w), grad_x = _jax.lax.scan(body, init, (per_example, given["loss_target"]))
    with _jax.named_scope("update"):
        delta_w, new_m, new_v = {}, {}, {}
        for n in TWIN_WEIGHTS:
            delta_w[n], new_m[n], new_v[n] = _adamw(weights[n], grad_w[n], given["m_" + n], given["v_" + n])
    return (loss, grad_x, *[grad_w[n] for n in TWIN_WEIGHTS], *[delta_w[n] for n in TWIN_WEIGHTS],
            *[new_m[n] for n in TWIN_WEIGHTS], *[new_v[n] for n in TWIN_WEIGHTS])
```

```python
import functools

import numpy as np
import jax
import jax.numpy as jnp
from jax import lax
from jax.experimental import pallas as pl
from jax.experimental.pallas import tpu as pltpu

F32, BF16 = jnp.float32, jnp.bfloat16
MESH = pl.DeviceIdType.MESH
N_DEV = 8

D = 1024
DEPTH = 4
CHUNK = 64
EPS = 1e-6
NEG_INF = -1e30
HEADS, QK_NOPE, QK_ROPE, V_HEAD = 8, 64, 32, 64
Q_LORA, KV_LORA = 256, 128
ROPE_THETA = 10000.0
POOL_WINDOWS = (2, 4, 8, 16)
GROUP = 64
BR = 256
SGU_BLOCK = 128
D_FF = 2816
IN_SIZES = (256, 128, 32, 256, 256, 256, 256, 256, 256, 4096)
HP = 128
QW = HEADS * HP
C_CQ, C_CKV, C_KR, C_PIN, C_SU, C_SV, C_CVB, C_CVC, C_CVX, C_GATE = 0, 256, 384, 512, 768, 1024, 1280, 1536, 1792, 2048
W1_COLS = 6144
SMALL_COLS = 2048

ADAM_LR, ADAM_B1, ADAM_B2, ADAM_EPS, ADAM_WD, ADAM_STEP = 0.001, 0.9, 0.999, 1e-08, 0.01, 10

WEIGHTS = ['w_in', 'g_pre_mix', 'g_cq', 'g_ckv', 'w_uq', 'w_ukv', 'pool_w', 'pool_scale', 'g_sgu_v', 'sgu_w', 'sgu_b',
           'conv_w', 'w_br_a', 'w_br_b', 'w_br_c', 'w_br_d', 'w_out', 'g_post_mix', 'g_pre_ffn', 'w_ffn_gate',
           'w_ffn_up', 'w_ffn_down', 'g_post_ffn']
SHARDED = {'w_in': 2, 'w_uq': 2, 'w_ukv': 2, 'conv_w': 3, 'w_br_a': 2, 'w_br_b': 2, 'w_br_c': 2, 'w_br_d': 2,
           'w_out': 1, 'w_ffn_gate': 2, 'w_ffn_up': 2, 'w_ffn_down': 1}
REPLICATED = [n for n in WEIGHTS if n not in SHARDED]

VMEM_LIMIT = 56 * 1024 * 1024
TQ = 256
PACK_COLS = 1024
PACK_ROW_ALIGN = 16


def _cparams(sem=None):
    return pltpu.CompilerParams(vmem_limit_bytes=VMEM_LIMIT, dimension_semantics=sem)


def _tile(n, cap, align=128):
    if n <= cap:
        return n
    for t in range(cap - cap % align, 0, -align):
        if n % t == 0:
            return t
    return n


def win(arr, width, idx):
    return ('win', arr, width, idx)


def tab(arr):
    return ('tab', arr)


def _rw(name, fn, ins, consts, outs, accs=(), tile=256):
    first = ins[0][1] if isinstance(ins[0], tuple) else ins[0]
    T = first.shape[0]
    tile = min(tile, T)
    grid = (T // tile,)
    arrays, in_specs = [], []
    for x in ins:
        if isinstance(x, tuple) and x[0] == 'win':
            _, a, w, k = x
            arrays.append(a)
            in_specs.append(pl.BlockSpec((tile, w), lambda i, k=k: (i, k)))
        elif isinstance(x, tuple) and x[0] == 'tab':
            a = x[1]
            nb = a.shape[0] // tile
            arrays.append(a)
            in_specs.append(pl.BlockSpec((tile, a.shape[1]), lambda i, nb=nb: (i % nb, 0)))
        else:
            arrays.append(x)
            in_specs.append(pl.BlockSpec((tile, x.shape[1]), lambda i: (i, 0)))
    for c in consts:
        arrays.append(c)
        in_specs.append(pl.BlockSpec(c.shape, lambda i, n=c.ndim: (0,) * n))
    out_shape = [jax.ShapeDtypeStruct((T, f), dt) for f, dt in outs]
    out_specs = [pl.BlockSpec((tile, f), lambda i: (i, 0)) for f, _ in outs]
    for shp in accs:
        out_shape.append(jax.ShapeDtypeStruct(shp, F32))
        out_specs.append(pl.BlockSpec(shp, lambda i, n=len(shp): (0,) * n))
    ni, nc, no = len(ins), len(consts), len(outs)

    def body(*refs):
        vals = fn(*[r[...] for r in refs[:ni + nc]])
        if not isinstance(vals, (tuple, list)):
            vals = (vals,)
        o_refs = refs[ni + nc:]
        for r, v in zip(o_refs[:no], vals[:no]):
            r[...] = v.astype(r.dtype)
        i = pl.program_id(0)
        for r, v in zip(o_refs[no:], vals[no:]):
            @pl.when(i == 0)
            def _(r=r, v=v):
                r[...] = v

            @pl.when(i > 0)
            def _(r=r, v=v):
                r[...] += v

    res = pl.pallas_call(body, name=name, grid=grid, in_specs=in_specs, out_specs=out_specs, out_shape=out_shape,
                         compiler_params=_cparams(("arbitrary",)))(*arrays)
    return res


def _mm(name, a, b, tb=False, out_dtype=F32, tm=512, tn=512):
    M, K = a.shape
    N = b.shape[0] if tb else b.shape[1]
    tm, tn = _tile(M, tm, 8), _tile(N, tn)
    dims = (((1,), (1,)), ((), ())) if tb else (((1,), (0,)), ((), ()))

    def body(a_ref, b_ref, o_ref):
        o_ref[...] = lax.dot_general(a_ref[...].astype(BF16), b_ref[...].astype(BF16), dims,
                                     preferred_element_type=F32).astype(o_ref.dtype)

    b_spec = pl.BlockSpec((tn, K), lambda i, j: (j, 0)) if tb else pl.BlockSpec((K, tn), lambda i, j: (0, j))
    return pl.pallas_call(
        body, name=name, grid=(M // tm, N // tn),
        in_specs=[pl.BlockSpec((tm, K), lambda i, j: (i, 0)), b_spec],
        out_specs=pl.BlockSpec((tm, tn), lambda i, j: (i, j)),
        out_shape=jax.ShapeDtypeStruct((M, N), out_dtype),
        compiler_params=_cparams(("parallel", "parallel")))(a, b)


def _mm_t(name, a, g, tk=1408, tn=1408, tt=512):
    T, K = a.shape
    N = g.shape[1]
    tk, tn, tt = _tile(K, tk), _tile(N, tn), _tile(T, tt, 8)

    def body(a_ref, g_ref, o_ref):
        p = lax.dot_general(a_ref[...].astype(BF16), g_ref[...].astype(BF16), (((0,), (0,)), ((), ())),
                            preferred_element_type=F32)
        t = pl.program_id(2)

        @pl.when(t == 0)
        def _():
            o_ref[...] = p

        @pl.when(t > 0)
        def _():
            o_ref[...] += p

    return pl.pallas_call(
        body, name=name, grid=(K // tk, N // tn, T // tt),
        in_specs=[pl.BlockSpec((tt, tk), lambda i, j, t: (t, i)), pl.BlockSpec((tt, tn), lambda i, j, t: (t, j))],
        out_specs=pl.BlockSpec((tk, tn), lambda i, j, t: (i, j)),
        out_shape=jax.ShapeDtypeStruct((K, N), F32),
        compiler_params=_cparams(("parallel", "parallel", "arbitrary")))(a, g)


def _dot(a, b):
    return jnp.dot(a.astype(BF16), b.astype(BF16), preferred_element_type=F32)


def _dot_nt(a, b):
    return lax.dot_general(a.astype(BF16), b.astype(BF16), (((1,), (1,)), ((), ())), preferred_element_type=F32)


def _dot_tn(a, b):
    return lax.dot_general(a.astype(BF16), b.astype(BF16), (((0,), (0,)), ((), ())), preferred_element_type=F32)


def _rms(x, g):
    r = lax.rsqrt(jnp.mean(x * x, axis=-1, keepdims=True) + EPS)
    return x * r * g


def _rms_bwd(x, g, dy):
    r = lax.rsqrt(jnp.mean(x * x, axis=-1, keepdims=True) + EPS)
    xh = x * r
    dxh = dy * g
    dx = r * (dxh - xh * jnp.mean(dxh * xh, axis=-1, keepdims=True))
    return dx, jnp.sum(dy * xh, axis=0, keepdims=True)


def _sigmoid(x):
    return 1.0 / (1.0 + jnp.exp(-x))


def _shift_down(x, k):
    n = x.shape[0]
    t = lax.broadcasted_iota(jnp.int32, x.shape, 0)
    return jnp.where(t >= k, pltpu.roll(x, k, 0), 0.0)


def _shift_up(x, k):
    n = x.shape[0]
    t = lax.broadcasted_iota(jnp.int32, x.shape, 0)
    return jnp.where(t < n - k, pltpu.roll(x, n - k, 0), 0.0)


def _group_select(vals):
    lane = lax.broadcasted_iota(jnp.int32, vals[0].shape, 1) // GROUP
    out = vals[-1]
    for g in range(len(vals) - 2, -1, -1):
        out = jnp.where(lane == g, vals[g], out)
    return out


def _pool_count(shape):
    t = lax.broadcasted_iota(jnp.int32, shape, 0)
    lane = lax.broadcasted_iota(jnp.int32, shape, 1) // GROUP
    w = jnp.where(lane == 0, POOL_WINDOWS[0], jnp.where(lane == 1, POOL_WINDOWS[1],
                  jnp.where(lane == 2, POOL_WINDOWS[2], POOL_WINDOWS[3])))
    return jnp.minimum(t + 1, w).astype(F32)


def _pooled(z):
    s = [z]
    for k in (1, 2, 4, 8):
        s.append(s[-1] + _shift_down(s[-1], k))
    return _group_select(s[1:]) / _pool_count(z.shape) - z


def _pooled_bwd(dp):
    u = dp / _pool_count(dp.shape)
    s = [u]
    for k in (1, 2, 4, 8):
        s.append(s[-1] + _shift_up(s[-1], k))
    return _group_select(s[1:]) - dp


def _chunk_mask(q0, k0):
    r = (q0 + lax.broadcasted_iota(jnp.int32, (TQ, TQ), 0)) // CHUNK
    c = (k0 + lax.broadcasted_iota(jnp.int32, (TQ, TQ), 1)) // CHUNK
    return r >= c


def _attn_fwd(q, k, v, B, S):
    nq = S // TQ

    def body(q_ref, k_ref, v_ref, o_ref, lse_ref):
        qi = pl.program_id(2)
        qb = q_ref[...]

        def step(kj, carry):
            m, l, acc = carry
            off = pl.multiple_of(kj * TQ, TQ)
            kb, vb = k_ref[pl.ds(off, TQ), :], v_ref[pl.ds(off, TQ), :]
            s = jnp.where(_chunk_mask(qi * TQ, off), _dot_nt(qb, kb), NEG_INF)
            m_new = jnp.maximum(m, jnp.max(s, axis=-1, keepdims=True))
            p = jnp.exp(s - m_new)
            alpha = jnp.exp(m - m_new)
            return m_new, alpha * l + jnp.sum(p, axis=-1, keepdims=True), alpha * acc + _dot(p, vb)

        m, l, acc = lax.fori_loop(0, qi + 1, step, (jnp.full((TQ, 1), NEG_INF, F32), jnp.zeros((TQ, 1), F32),
                                                    jnp.zeros((TQ, HP), F32)))
        o_ref[...] = acc / l
        lse_ref[...] = jnp.broadcast_to(m + jnp.log(l), (TQ, HP))

    qspec = pl.BlockSpec((TQ, HP), lambda b, h, i: (b * nq + i, h))
    kspec = pl.BlockSpec((S, HP), lambda b, h, i: (b, h))
    return pl.pallas_call(
        body, name="attn_fwd", grid=(B, HEADS, nq), in_specs=[qspec, kspec, kspec], out_specs=[qspec, qspec],
        out_shape=[jax.ShapeDtypeStruct((B * S, QW), F32)] * 2,
        compiler_params=_cparams(("parallel", "parallel", "arbitrary")))(q, k, v)


def _attn_dq(q, k, v, o, do, lse, B, S):
    nq = S // TQ

    def body(q_ref, k_ref, v_ref, o_ref, do_ref, lse_ref, dq_ref):
        qi = pl.program_id(2)
        qb, dob = q_ref[...], do_ref[...]
        delta = jnp.sum(o_ref[...] * dob, axis=-1, keepdims=True)
        lse_b = lse_ref[...][:, :1]

        def step(kj, dq):
            off = pl.multiple_of(kj * TQ, TQ)
            kb, vb = k_ref[pl.ds(off, TQ), :], v_ref[pl.ds(off, TQ), :]
            p = jnp.where(_chunk_mask(qi * TQ, off), jnp.exp(_dot_nt(qb, kb) - lse_b), 0.0)
            ds = p * (_dot_nt(dob, vb) - delta)
            return dq + _dot(ds, kb)

        dq_ref[...] = lax.fori_loop(0, qi + 1, step, jnp.zeros((TQ, HP), F32))

    qspec = pl.BlockSpec((TQ, HP), lambda b, h, i: (b * nq + i, h))
    kspec = pl.BlockSpec((S, HP), lambda b, h, i: (b, h))
    return pl.pallas_call(
        body, name="attn_dq", grid=(B, HEADS, nq), in_specs=[qspec, kspec, kspec, qspec, qspec, qspec],
        out_specs=qspec, out_shape=jax.ShapeDtypeStruct((B * S, QW), F32),
        compiler_params=_cparams(("parallel", "parallel", "arbitrary")))(q, k, v, o, do, lse)


def _attn_dkv(q, k, v, o, do, lse, B, S):
    nq = S // TQ

    def body(q_ref, k_ref, v_ref, o_ref, do_ref, lse_ref, dk_ref, dv_ref):
        kj = pl.program_id(2)
        kb, vb = k_ref[...], v_ref[...]

        def step(qi, carry):
            dk, dv = carry
            off = pl.multiple_of(qi * TQ, TQ)
            qb, dob, ob = q_ref[pl.ds(off, TQ), :], do_ref[pl.ds(off, TQ), :], o_ref[pl.ds(off, TQ), :]
            lse_b = lse_ref[pl.ds(off, TQ), :][:, :1]
            p = jnp.where(_chunk_mask(off, kj * TQ), jnp.exp(_dot_nt(qb, kb) - lse_b), 0.0)
            delta = jnp.sum(ob * dob, axis=-1, keepdims=True)
            ds = p * (_dot_nt(dob, vb) - delta)
            return dk + _dot_tn(ds, qb), dv + _dot_tn(p, dob)

        dk, dv = lax.fori_loop(kj, nq, step, (jnp.zeros((TQ, HP), F32), jnp.zeros((TQ, HP), F32)))
        dk_ref[...] = dk.astype(BF16)
        dv_ref[...] = dv.astype(BF16)

    kspec = pl.BlockSpec((TQ, HP), lambda b, h, j: (b * nq + j, h))
    qspec = pl.BlockSpec((S, HP), lambda b, h, j: (b, h))
    return pl.pallas_call(
        body, name="attn_dkv", grid=(B, HEADS, nq), in_specs=[qspec, kspec, kspec, qspec, qspec, qspec],
        out_specs=[kspec, kspec], out_shape=[jax.ShapeDtypeStruct((B * S, QW), BF16)] * 2,
        compiler_params=_cparams(("parallel", "parallel", "arbitrary")))(q, k, v, o, do, lse)


def _all_gather(name, x):
    R, C = x.shape

    def body(x_ref, out_ref, send_sems, recv_sems, local_sem):
        x_, y_, c_ = lax.axis_index("x"), lax.axis_index("y"), lax.axis_index("c")
        me, sibling = (x_, y_, c_), (x_, y_, 1 - c_)
        chips = [(1 - x_, y_), (x_, 1 - y_), (1 - x_, 1 - y_)]

        def slot(px, py, pc):
            return out_ref.at[4 * px + 2 * py + pc]

        def copy(k, block, to, src=None):
            return pltpu.make_async_remote_copy(
                src_ref=slot(*block) if src is None else src, dst_ref=slot(*block),
                send_sem=send_sems.at[k], recv_sem=recv_sems.at[k], device_id=to, device_id_type=MESH)

        mine = pltpu.make_async_copy(x_ref, slot(*me), local_sem)
        mine.start()
        first = [copy(0, me, sibling, src=x_ref)]
        first += [copy(1 + j, me, (*chip, c_), src=x_ref) for j, chip in enumerate(chips)]
        for cp in first:
            cp.start()
        passed = [copy(4 + j, (*chip, c_), sibling) for j, chip in enumerate(chips)]
        for j, chip in enumerate(chips):
            copy(1 + j, (*chip, c_), me).wait_recv()
            passed[j].start()
        copy(0, sibling, me).wait_recv()
        for j, chip in enumerate(chips):
            copy(4 + j, (*chip, 1 - c_), me).wait_recv()
        for cp in first + passed:
            cp.wait_send()
        mine.wait()

    return pl.pallas_call(
        body, name=name, out_shape=jax.ShapeDtypeStruct((N_DEV, R, C), x.dtype),
        in_specs=[pl.BlockSpec(memory_space=pl.ANY)], out_specs=pl.BlockSpec(memory_space=pl.ANY),
        scratch_shapes=[pltpu.SemaphoreType.DMA((7,)), pltpu.SemaphoreType.DMA((7,)), pltpu.SemaphoreType.DMA(())],
    )(x)


def _all_to_all(name, x):
    _, R, C = x.shape

    def body(x_ref, out_ref, send_sems, recv_sems, local_sem):
        x_, y_, c_ = lax.axis_index("x"), lax.axis_index("y"), lax.axis_index("c")
        me = 4 * x_ + 2 * y_ + c_
        mine = pltpu.make_async_copy(x_ref.at[me], out_ref.at[me], local_sem)
        mine.start()
        sends, recvs = [], []
        for k in range(1, N_DEV):
            px, py, pc = (x_ + (k >> 2)) % 2, (y_ + ((k >> 1) & 1)) % 2, (c_ + (k & 1)) % 2
            peer = 4 * px + 2 * py + pc
            sends.append(pltpu.make_async_remote_copy(
                src_ref=x_ref.at[peer], dst_ref=out_ref.at[me], send_sem=send_sems.at[k - 1],
                recv_sem=recv_sems.at[k - 1], device_id=(px, py, pc), device_id_type=MESH))
            recvs.append(pltpu.make_async_remote_copy(
                src_ref=x_ref.at[me], dst_ref=out_ref.at[peer], send_sem=send_sems.at[k - 1],
                recv_sem=recv_sems.at[k - 1], device_id=(px, py, pc), device_id_type=MESH))
        for cp in sends:
            cp.start()
        for cp in recvs:
            cp.wait_recv()
        for cp in sends:
            cp.wait_send()
        mine.wait()

    return pl.pallas_call(
        body, name=name, out_shape=jax.ShapeDtypeStruct(x.shape, x.dtype),
        in_specs=[pl.BlockSpec(memory_space=pl.ANY)], out_specs=pl.BlockSpec(memory_space=pl.ANY),
        scratch_shapes=[pltpu.SemaphoreType.DMA((7,)), pltpu.SemaphoreType.DMA((7,)), pltpu.SemaphoreType.DMA(())],
    )(x)


def _adamw(name, parts, w, m, v):
    R, C = w.shape
    tr = _tile(R, 512, 8)

    def body(p_ref, w_ref, m_ref, v_ref, g_ref, d_ref, nm_ref, nv_ref):
        g = p_ref[0].astype(F32)
        for j in range(1, N_DEV):
            g = g + p_ref[j].astype(F32)
        m_new = ADAM_B1 * m_ref[...] + (1.0 - ADAM_B1) * g
        v_new = ADAM_B2 * v_ref[...] + (1.0 - ADAM_B2) * (g * g)
        m_hat = m_new / (1.0 - ADAM_B1 ** ADAM_STEP)
        v_hat = v_new / (1.0 - ADAM_B2 ** ADAM_STEP)
        g_ref[...] = g
        d_ref[...] = -ADAM_LR * (m_hat / (jnp.sqrt(v_hat) + ADAM_EPS) + ADAM_WD * w_ref[...])
        nm_ref[...] = m_new
        nv_ref[...] = v_new

    spec = pl.BlockSpec((tr, C), lambda i: (i, 0))
    return pl.pallas_call(
        body, name=name, grid=(R // tr,), in_specs=[pl.BlockSpec((N_DEV, tr, C), lambda i: (0, i, 0)), spec, spec, spec],
        out_specs=[spec] * 4, out_shape=[jax.ShapeDtypeStruct((R, C), F32)] * 4,
        compiler_params=_cparams(("parallel",)))(parts, w, m, v)


def _rot_cols(w):
    h = w.shape[-1] // 2
    return jnp.concatenate([-w[..., h:], w[..., :h]], axis=-1)


def _unrot_cols(d):
    h = d.shape[-1] // 2
    return jnp.concatenate([d[..., h:], -d[..., :h]], axis=-1)


def _split_in(w):
    out, o = [], 0
    for n in IN_SIZES:
        out.append(w[..., o:o + n])
        o += n
    return out


def _layer_weights(G):
    L = DEPTH
    cq, ckv, kr, pin, su, sv, cvb, cvc, cvx, gate = _split_in(G['w_in'])
    W1 = jnp.concatenate([cq, ckv, kr, _rot_cols(kr), jnp.zeros((L, D, 64), BF16), pin, su, sv, cvb, cvc, cvx, gate], axis=-1)
    wq = G['w_uq'].reshape(L, Q_LORA, HEADS, QK_NOPE + QK_ROPE)
    nope, rope = wq[..., :QK_NOPE], wq[..., QK_NOPE:]
    z32 = jnp.zeros((L, Q_LORA, HEADS, HP - QK_NOPE - QK_ROPE), BF16)
    wq_a = jnp.concatenate([nope, rope, z32], axis=-1).reshape(L, Q_LORA, QW)
    wq_b = jnp.concatenate([jnp.zeros_like(nope), _rot_cols(rope), z32], axis=-1).reshape(L, Q_LORA, QW)
    wkv = G['w_ukv'].reshape(L, KV_LORA, HEADS, QK_NOPE + V_HEAD)
    z64 = jnp.zeros((L, KV_LORA, HEADS, HP - QK_NOPE), BF16)
    wkn = jnp.concatenate([wkv[..., :QK_NOPE], z64], axis=-1).reshape(L, KV_LORA, QW)
    wv = jnp.concatenate([wkv[..., QK_NOPE:], z64], axis=-1).reshape(L, KV_LORA, QW)
    wa = G['w_br_a'].reshape(L, HEADS, V_HEAD, D)
    wa = jnp.concatenate([wa, jnp.zeros((L, HEADS, HP - V_HEAD, D), BF16)], axis=2).reshape(L, QW, D)
    return dict(
        W1=W1, Wq=jnp.concatenate([wq_a, wq_b], axis=-1), Wkv=jnp.concatenate([wkn, wv], axis=-1), Wa=wa,
        Wb=G['w_br_b'], Wc=G['w_br_c'], Wd=G['w_br_d'], Wout=G['w_out'],
        Wgu=jnp.concatenate([G['w_ffn_gate'], G['w_ffn_up']], axis=-1), Wdn=G['w_ffn_down'],
        conv=G['conv_w'].reshape(L, 3, BR).astype(F32))


def _tables(S):
    inv = ROPE_THETA ** (-jnp.arange(0, QK_ROPE, 2, dtype=F32) / QK_ROPE)
    ang = jnp.arange(S, dtype=F32)[:, None] * inv[None, :]
    cos, sin = jnp.cos(ang), jnp.sin(ang)
    scale = (QK_NOPE + QK_ROPE) ** -0.5
    one, zero = jnp.ones((S, QK_NOPE), F32), jnp.zeros((S, QK_NOPE), F32)
    z32 = jnp.zeros((S, HP - QK_NOPE - QK_ROPE), F32)
    cq = jnp.tile(jnp.concatenate([one, cos, cos, z32], axis=1) * scale, (1, HEADS))
    sq = jnp.tile(jnp.concatenate([zero, sin, sin, z32], axis=1) * scale, (1, HEADS))
    ck = jnp.concatenate([cos, cos, sin, sin, jnp.zeros((S, HP - 2 * QK_ROPE), F32)], axis=1)
    return cq, sq, ck


def _place_k():
    p = np.zeros((HP, QW), np.float32)
    for r in range(2 * QK_ROPE):
        for h in range(HEADS):
            p[r, h * HP + QK_NOPE + r % QK_ROPE] = 1.0
    return jnp.asarray(p, BF16)


def _layer_fwd(x, W, P, tabs, B, S):
    cq_t, sq_t, ck_t, place = tabs
    g1, g_cq, g_ckv = P['g_pre_mix'], P['g_cq'], P['g_ckv']
    h = _rw("pre_mix", lambda x, g: _rms(x, g), [x], [g1], [(D, BF16)], tile=512)[0]
    P1 = _mm("in_proj", h, W['W1'], tm=512, tn=512)

    def q_prep(c, cq, sq, g, w):
        qq = _dot(_rms(c, g), w)
        return qq[:, :QW] * cq + qq[:, QW:] * sq
    q = _rw("q_prep", q_prep, [win(P1, 256, 0), tab(cq_t), tab(sq_t)], [g_cq, W['Wq']], [(QW, BF16)])[0]

    def kv_prep(c, seg, ck, g, w, place):
        kv = _dot(_rms(c, g), w)
        return kv[:, :QW] + _dot(seg * ck, place), kv[:, QW:]
    k, v = _rw("kv_prep", kv_prep, [win(P1, 128, 2), win(P1, 128, 3), tab(ck_t)], [g_ckv, W['Wkv'], place],
               [(QW, BF16), (QW, BF16)])
    o, lse = _attn_fwd(q, k, v, B, S)

    def pool_f(z, wbd, scale):
        return _dot(_pooled(z), wbd) * scale
    bo = _rw("pool_fwd", pool_f, [win(P1, BR, 2)], [P['pool_bd'], P['pool_scale']], [(BR, BF16)], tile=S)[0]

    def sgu_f(u, sv, g, wm, bias):
        return u * _sgu_mix(_rms(sv, g), wm, bias)
    co = _rw("sgu_fwd", sgu_f, [win(P1, BR, 3), win(P1, BR, 4)], [P['g_sgu_v'], P['sgu_wm'], P['sgu_bias']],
             [(BR, BF16)], tile=SGU_BLOCK)[0]

    def conv_f(b, c, xi, w):
        z = c * xi
        return b * (w[0:1] * _shift_down(z, 2) + w[1:2] * _shift_down(z, 1) + w[2:3] * z)
    do_ = _rw("conv_fwd", conv_f, [win(P1, BR, 5), win(P1, BR, 6), win(P1, BR, 7)], [W['conv']], [(BR, BF16)], tile=S)[0]

    def merge_f(a, b, c, d, l0, l1, l2, l3, wa, wb, wc, wd):
        return (_sigmoid(l0) * _dot(a, wa) + _sigmoid(l1) * _dot(b, wb) + _sigmoid(l2) * _dot(c, wc)
                + _sigmoid(l3) * _dot(d, wd))
    merged = _rw("merge_fwd", merge_f, [o, bo, co, do_] + [win(P1, D, 2 + i) for i in range(4)],
                 [W['Wa'], W['Wb'], W['Wc'], W['Wd']], [(D, BF16)])[0]
    om = _mm("out_proj", merged, W['Wout'])
    x1 = _rw("post_mix", lambda x, o, g: x + _rms(o, g), [x, om], [P['g_post_mix']], [(D, F32)], tile=512)[0]
    h2 = _rw("pre_ffn", lambda x, g: _rms(x, g), [x1], [P['g_pre_ffn']], [(D, BF16)], tile=512)[0]
    ab = _mm("ffn_up", h2, W['Wgu'])

    def act_f(a, b):
        return a * _sigmoid(a) * b
    act = _rw("ffn_act", act_f, [win(ab, D_FF, 0), win(ab, D_FF, 1)], [], [(D_FF, BF16)])[0]
    f = _mm("ffn_down", act, W['Wdn'])
    x2 = _rw("post_ffn", lambda x, f, g: x + _rms(f, g), [x1, f], [P['g_post_ffn']], [(D, F32)], tile=512)[0]
    saved = dict(x=x, h=h, P1=P1, q=q, k=k, v=v, o=o, lse=lse, bo=bo, co=co, do=do_, merged=merged, om=om, x1=x1,
                 h2=h2, ab=ab, act=act, f=f)
    return x2, saved


def _sgu_mix(vn, wm, bias):
    outs = [_dot(wm[g], vn) for g in range(4)]
    return _group_select(outs) + bias


def _layer_bwd(dx2, sv, W, P, tabs, B, S):
    cq_t, sq_t, ck_t, place = tabs
    P1 = sv['P1']
    grads = {}

    def post_ffn_b(f, dy, g):
        return _rms_bwd(f, g, dy)
    df, grads['g_post_ffn'] = _rw("post_ffn_bwd", post_ffn_b, [sv['f'], dx2], [P['g_post_ffn']], [(D, BF16)], [(1, D)], tile=512)
    dact = _mm("ffn_down_dx", df, W['Wdn'], tb=True)
    grads['Wdn'] = _mm_t("ffn_down_dw", sv['act'], df)

    def act_b(a, b, d):
        s = _sigmoid(a)
        return d * b * (s + a * s * (1.0 - s)), d * a * s
    da, db = _rw("ffn_act_bwd", act_b, [win(sv['ab'], D_FF, 0), win(sv['ab'], D_FF, 1), dact], [],
                 [(D_FF, BF16), (D_FF, BF16)])
    dab = jnp.concatenate([da, db], axis=1)
    dh2 = _mm("ffn_up_dx", dab, W['Wgu'], tb=True)
    grads['Wgu'] = _mm_t("ffn_up_dw", sv['h2'], dab)

    def pre_ffn_b(x, dh, dy, g):
        dx, dg = _rms_bwd(x, g, dh)
        return dy + dx, dg
    dx1, grads['g_pre_ffn'] = _rw("pre_ffn_bwd", pre_ffn_b, [sv['x1'], dh2, dx2], [P['g_pre_ffn']], [(D, F32)], [(1, D)], tile=512)

    def post_mix_b(o, dy, g):
        return _rms_bwd(o, g, dy)
    dom, grads['g_post_mix'] = _rw("post_mix_bwd", post_mix_b, [sv['om'], dx1], [P['g_post_mix']], [(D, BF16)], [(1, D)], tile=512)
    dmerged = _mm("out_proj_dx", dom, W['Wout'], tb=True)
    grads['Wout'] = _mm_t("out_proj_dw", sv['merged'], dom)

    def merge_b(a, b, c, d, l0, l1, l2, l3, dm, wa, wb, wc, wd):
        outs_dl, outs_dy, outs_dbr = [], [], []
        for br, l, w in ((a, l0, wa), (b, l1, wb), (c, l2, wc), (d, l3, wd)):
            s = _sigmoid(l)
            y = _dot(br, w)
            dy = (dm * s).astype(BF16)
            outs_dl.append(dm * y * s * (1.0 - s))
            outs_dy.append(dy)
            outs_dbr.append(_dot_nt(dy, w))
        return (*outs_dl, *outs_dy, *outs_dbr)
    mb = _rw("merge_bwd", merge_b,
             [sv['o'], sv['bo'], sv['co'], sv['do']] + [win(P1, D, 2 + i) for i in range(4)] + [dmerged],
             [W['Wa'], W['Wb'], W['Wc'], W['Wd']],
             [(D, BF16)] * 4 + [(D, BF16)] * 4 + [(QW, F32), (BR, F32), (BR, F32), (BR, F32)])
    dl, dy, (d_o, d_bo, d_co, d_do) = mb[0:4], mb[4:8], mb[8:12]
    grads['Wa'] = _mm_t("br_a_dw", sv['o'], dy[0])
    grads['Wb'] = _mm_t("br_b_dw", sv['bo'], dy[1])
    grads['Wc'] = _mm_t("br_c_dw", sv['co'], dy[2])
    grads['Wd'] = _mm_t("br_d_dw", sv['do'], dy[3])

    def conv_b(b, c, xi, dout, w):
        z = c * xi
        z1, z2 = _shift_down(z, 1), _shift_down(z, 2)
        y = w[0:1] * z2 + w[1:2] * z1 + w[2:3] * z
        dyv = dout * b
        dz = w[2:3] * dyv + w[1:2] * _shift_up(dyv, 1) + w[0:1] * _shift_up(dyv, 2)
        return (dout * y, dz * xi, dz * c, jnp.sum(dyv * z2, axis=0, keepdims=True),
                jnp.sum(dyv * z1, axis=0, keepdims=True), jnp.sum(dyv * z, axis=0, keepdims=True))
    dcvb, dcvc, dcvx, dw0, dw1, dw2 = _rw("conv_bwd", conv_b, [win(P1, BR, 5), win(P1, BR, 6), win(P1, BR, 7), d_do],
                                          [W['conv']], [(BR, BF16)] * 3, [(1, BR)] * 3, tile=S)
    grads['conv'] = jnp.concatenate([dw0, dw1, dw2], axis=0)

    def sgu_b(u, s_v, dout, g, wm, bias):
        vn = _rms(s_v, g)
        mixed = _sgu_mix(vn, wm, bias)
        dmix = dout * u
        dvn = _group_select([_dot_tn(wm[gi], dmix) for gi in range(4)])
        dsv, dg = _rms_bwd(s_v, g, dvn)
        lane = lax.broadcasted_iota(jnp.int32, dmix.shape, 1) // GROUP
        dwm = [_dot_nt(jnp.where(lane == gi, dmix, 0.0), vn) for gi in range(4)]
        return (dout * mixed, dsv, dg, *dwm, dmix)
    dsu, dsv_, grads['g_sgu_v'], wm0, wm1, wm2, wm3, grads['sgu_bias'] = _rw(
        "sgu_bwd", sgu_b, [win(P1, BR, 3), win(P1, BR, 4), d_co], [P['g_sgu_v'], P['sgu_wm'], P['sgu_bias']],
        [(BR, BF16)] * 2, [(1, BR)] + [(SGU_BLOCK, SGU_BLOCK)] * 4 + [(SGU_BLOCK, BR)], tile=SGU_BLOCK)
    grads['sgu_wm'] = jnp.stack([wm0, wm1, wm2, wm3])

    def pool_b(z, dout, wbd, scale):
        pooled = _pooled(z)
        mixed = _dot(pooled, wbd)
        dmix = dout * scale
        dz = _pooled_bwd(_dot_nt(dmix, wbd))
        return dz, _dot_tn(pooled, dmix), jnp.sum(dout * mixed, axis=0, keepdims=True)
    dpin, grads['pool_bd'], grads['pool_scale'] = _rw(
        "pool_bwd", pool_b, [win(P1, BR, 2), d_bo], [P['pool_bd'], P['pool_scale']], [(BR, BF16)], [(BR, BR), (1, BR)], tile=S)

    dq = _attn_dq(sv['q'], sv['k'], sv['v'], sv['o'], d_o, sv['lse'], B, S)
    dk, dv = _attn_dkv(sv['q'], sv['k'], sv['v'], sv['o'], d_o, sv['lse'], B, S)

    def q_prep_b(c, dq, cq, sq, g, w):
        dqq = jnp.concatenate([dq * cq, dq * sq], axis=1).astype(BF16)
        dc, dg = _rms_bwd(c, g, _dot_nt(dqq, w))
        return dc, _rms(c, g), dqq, dg
    dcq, cqn, dqq, grads['g_cq'] = _rw("q_prep_bwd", q_prep_b, [win(P1, 256, 0), dq, tab(cq_t), tab(sq_t)],
                                       [P['g_cq'], W['Wq']], [(Q_LORA, BF16), (Q_LORA, BF16), (2 * QW, BF16)], [(1, Q_LORA)])
    grads['Wq'] = _mm_t("uq_dw", cqn, dqq)

    def kv_prep_b(c, dk, dv, ck, g, w, place):
        dkv = jnp.concatenate([dk, dv], axis=1)
        dc, dg = _rms_bwd(c, g, _dot_nt(dkv, w))
        return dc, _dot_nt(dk, place) * ck, _rms(c, g), dkv, dg
    dckv, dseg, kvn, dkv, grads['g_ckv'] = _rw(
        "kv_prep_bwd", kv_prep_b, [win(P1, 128, 2), dk, dv, tab(ck_t)], [P['g_ckv'], W['Wkv'], place],
        [(KV_LORA, BF16), (HP, BF16), (KV_LORA, BF16), (2 * QW, BF16)], [(1, KV_LORA)])
    grads['Wkv'] = _mm_t("ukv_dw", kvn, dkv)

    dP = jnp.concatenate([dcq, dckv, dseg, dpin, dsu, dsv_, dcvb, dcvc, dcvx, dl[0], dl[1], dl[2], dl[3]], axis=1)
    dh = _mm("in_proj_dx", dP, W['W1'], tb=True)
    grads['W1'] = _mm_t("in_proj_dw", sv['h'], dP)

    def pre_mix_b(x, dh, dy, g):
        dx, dg = _rms_bwd(x, g, dh)
        return dy + dx, dg
    dx, grads['g_pre_mix'] = _rw("pre_mix_bwd", pre_mix_b, [sv['x'], dh, dx1], [P['g_pre_mix']], [(D, F32)], [(1, D)], tile=512)
    return dx, grads


def _param_grads(gl):
    st = {k: jnp.stack([g[k] for g in gl]) for k in gl[0]}
    L = DEPTH
    w1 = st['W1']
    seg = w1[..., C_KR:C_KR + 128]
    dkr = seg[..., :QK_ROPE] + _unrot_cols(seg[..., QK_ROPE:2 * QK_ROPE])
    out = {}
    out['w_in'] = jnp.concatenate([w1[..., :C_CKV], w1[..., C_CKV:C_KR], dkr, w1[..., C_PIN:]], axis=-1)
    qa = st['Wq'][..., :QW].reshape(L, Q_LORA, HEADS, HP)
    qb = st['Wq'][..., QW:].reshape(L, Q_LORA, HEADS, HP)
    drope = qa[..., QK_NOPE:QK_NOPE + QK_ROPE] + _unrot_cols(qb[..., QK_NOPE:QK_NOPE + QK_ROPE])
    out['w_uq'] = jnp.concatenate([qa[..., :QK_NOPE], drope], axis=-1).reshape(L, Q_LORA, HEADS * (QK_NOPE + QK_ROPE))
    kn = st['Wkv'][..., :QW].reshape(L, KV_LORA, HEADS, HP)[..., :QK_NOPE]
    vv = st['Wkv'][..., QW:].reshape(L, KV_LORA, HEADS, HP)[..., :V_HEAD]
    out['w_ukv'] = jnp.concatenate([kn, vv], axis=-1).reshape(L, KV_LORA, HEADS * (QK_NOPE + V_HEAD))
    out['w_br_a'] = st['Wa'].reshape(L, HEADS, HP, D)[:, :, :V_HEAD].reshape(L, HEADS * V_HEAD, D)
    out['w_br_b'], out['w_br_c'], out['w_br_d'], out['w_out'] = st['Wb'], st['Wc'], st['Wd'], st['Wout']
    out['w_ffn_gate'], out['w_ffn_up'], out['w_ffn_down'] = st['Wgu'][..., :D_FF], st['Wgu'][..., D_FF:], st['Wdn']
    out['conv_w'] = st['conv'].reshape(L, 3, 1, BR)
    for n in ('g_pre_mix', 'g_cq', 'g_ckv', 'g_sgu_v', 'g_post_mix', 'g_pre_ffn', 'g_post_ffn', 'pool_scale'):
        out[n] = st[n].reshape(L, -1)
    bd = st['pool_bd'].reshape(L, 4, GROUP, 4, GROUP)
    out['pool_w'] = jnp.stack([bd[:, g, :, g, :] for g in range(4)], axis=1)
    out['sgu_w'] = st['sgu_wm'] * _sgu_mask()[None, None]
    out['sgu_b'] = st['sgu_bias'].reshape(L, SGU_BLOCK, 4, GROUP).sum(-1).transpose(0, 2, 1)
    return out


def _sgu_mask():
    pc = np.arange(SGU_BLOCK) // CHUNK
    return jnp.asarray(pc[:, None] >= pc[None, :], F32)


def _pack(pieces, dtype, lead=None):
    axis = 0 if lead is None else 1
    flat = jnp.concatenate([p.astype(dtype) for p in pieces], axis=axis)
    n = flat.shape[axis]
    unit = PACK_COLS * PACK_ROW_ALIGN
    pad = (-n) % unit
    if pad:
        flat = jnp.concatenate([flat, jnp.zeros(flat.shape[:axis] + (pad,), dtype)], axis=axis)
    rows = (n + pad) // PACK_COLS
    return flat.reshape((rows, PACK_COLS) if lead is None else (lead, rows, PACK_COLS))


def _shard_split(a, axis):
    n = a.shape[axis]
    a = a.reshape(a.shape[:axis] + (N_DEV, n // N_DEV) + a.shape[axis + 1:])
    return jnp.moveaxis(a, axis, 0)


def _shard_join(a, axis):
    a = jnp.moveaxis(a, 0, axis)
    return a.reshape(a.shape[:axis] + (a.shape[axis] * a.shape[axis + 1],) + a.shape[axis + 2:])


def _as2d(a):
    return a.reshape(-1, a.shape[-1])


def kernel(x, w_in, g_pre_mix, g_cq, g_ckv, w_uq, w_ukv, pool_w, pool_scale, g_sgu_v, sgu_w, sgu_b, conv_w, w_br_a, w_br_b, w_br_c, w_br_d, w_out, g_post_mix, g_pre_ffn, w_ffn_gate, w_ffn_up, w_ffn_down, g_post_ffn, loss_target, m_w_in, m_g_pre_mix, m_g_cq, m_g_ckv, m_w_uq, m_w_ukv, m_pool_w, m_pool_scale, m_g_sgu_v, m_sgu_w, m_sgu_b, m_conv_w, m_w_br_a, m_w_br_b, m_w_br_c, m_w_br_d, m_w_out, m_g_post_mix, m_g_pre_ffn, m_w_ffn_gate, m_w_ffn_up, m_w_ffn_down, m_g_post_ffn, v_w_in, v_g_pre_mix, v_g_cq, v_g_ckv, v_w_uq, v_w_ukv, v_pool_w, v_pool_scale, v_g_sgu_v, v_sgu_w, v_sgu_b, v_conv_w, v_w_br_a, v_w_br_b, v_w_br_c, v_w_br_d, v_w_out, v_g_post_mix, v_g_pre_ffn, v_w_ffn_gate, v_w_ffn_up, v_w_ffn_down, v_g_post_ffn):
    args = dict(locals())
    w = {n: args[n] for n in WEIGHTS}
    m = {n: args['m_' + n] for n in WEIGHTS}
    v = {n: args['v_' + n] for n in WEIGHTS}
    B, S, _ = x.shape
    T = B * S
    L = DEPTH

    names = list(SHARDED)
    sizes = [int(np.prod(w[n].shape)) for n in names]
    gathered = _all_gather("gather_weights", _pack([w[n].reshape(-1) for n in names], BF16))
    gathered = gathered.reshape(N_DEV, -1)
    G, off = {}, 0
    for n, sz in zip(names, sizes):
        G[n] = _shard_join(gathered[:, off:off + sz].reshape((N_DEV,) + w[n].shape), SHARDED[n])
        off += sz
    W = _layer_weights(G)

    eye = jnp.eye(4, dtype=F32)
    pool_bd = (pool_w[:, :, :, None, :] * eye[None, :, None, :, None]).reshape(L, BR, BR).astype(BF16)
    sgu_wm = (sgu_w * _sgu_mask()[None, None]).astype(BF16)
    sgu_bias = jnp.repeat(sgu_b.transpose(0, 2, 1), GROUP, axis=2)
    tabs = _tables(S) + (_place_k(),)

    def layer_params(l):
        P = {n: w[n][l][None, :] for n in ('g_pre_mix', 'g_cq', 'g_ckv', 'g_sgu_v', 'g_post_mix', 'g_pre_ffn',
                                            'g_post_ffn', 'pool_scale')}
        P.update(pool_bd=pool_bd[l], sgu_wm=sgu_wm[l], sgu_bias=sgu_bias[l])
        return P

    xt = x.reshape(T, D)
    saved = []
    for l in range(L):
        xt, sv = _layer_fwd(xt, {k: a[l] for k, a in W.items()}, layer_params(l), tabs, B, S)
        saved.append(sv)

    def loss_f(y, t):
        e = y - t
        return e * (1.0 / D), jnp.sum(e * e, axis=0, keepdims=True)
    dy, sq = _rw("loss", loss_f, [xt, loss_target.reshape(T, D)], [], [(D, F32)], [(1, D)], tile=512)
    loss = lax.psum(0.5 * jnp.sum(sq) / D, ("x", "y", "c"))

    gl = [None] * L
    for l in reversed(range(L)):
        dy, gl[l] = _layer_bwd(dy, saved[l], {k: a[l] for k, a in W.items()}, layer_params(l), tabs, B, S)
    grad_x = dy.reshape(B, S, D)
    pg = _param_grads(gl)

    sent = _pack([_shard_split(pg[n], SHARDED[n]).reshape(N_DEV, -1) for n in names], BF16, lead=N_DEV)
    got = _all_to_all("scatter_grads", sent).reshape(N_DEV, -1)
    small_sizes = [int(np.prod(w[n].shape)) for n in REPLICATED]
    got_small = _all_gather("gather_small_grads", _pack([pg[n].reshape(-1) for n in REPLICATED], F32)).reshape(N_DEV, -1)

    res = {}
    off = 0
    for n, sz in zip(names, sizes):
        shp2 = _as2d(w[n]).shape
        res[n] = _adamw("adamw_" + n, got[:, off:off + sz].reshape((N_DEV,) + shp2), _as2d(w[n]), _as2d(m[n]), _as2d(v[n]))
        off += sz
    off = 0
    for n, sz in zip(REPLICATED, small_sizes):
        shp2 = _as2d(w[n]).shape
        res[n] = _adamw("adamw_" + n, got_small[:, off:off + sz].reshape((N_DEV,) + shp2), _as2d(w[n]), _as2d(m[n]), _as2d(v[n]))
        off += sz
    outs = [loss, grad_x]
    for k in range(4):
        outs += [res[n][k].reshape(w[n].shape) for n in WEIGHTS]
    return tuple(outs)
```

```python
import numpy as np
import jax
import jax.numpy as jnp
from jax import lax
from jax.experimental import pallas as pl
from jax.experimental.pallas import tpu as pltpu

F32, BF16 = jnp.float32, jnp.bfloat16
MESH = pl.DeviceIdType.MESH
N_DEV = 8

D = 1024
DEPTH = 4
CHUNK = 64
EPS = 1e-6
NEG_INF = -1e30
HEADS, QK_NOPE, QK_ROPE, V_HEAD = 8, 64, 32, 64
Q_LORA, KV_LORA = 256, 128
ROPE_THETA = 10000.0
POOL_WINDOWS = (2, 4, 8, 16)
GROUP = 64
BR = 256
SGU_BLOCK = 128
D_FF = 2816
IN_SIZES = (256, 128, 32, 256, 256, 256, 256, 256, 256, 4096)
HP = 128
QW = HEADS * HP
C_CQ, C_CKV, C_KR, C_PIN, C_SU, C_SV, C_CVB, C_CVC, C_CVX, C_GATE = 0, 256, 384, 512, 768, 1024, 1280, 1536, 1792, 2048
W1_COLS = 6144

ADAM_LR, ADAM_B1, ADAM_B2, ADAM_EPS, ADAM_WD, ADAM_STEP = 0.001, 0.9, 0.999, 1e-08, 0.01, 10

WEIGHTS = ['w_in', 'g_pre_mix', 'g_cq', 'g_ckv', 'w_uq', 'w_ukv', 'pool_w', 'pool_scale', 'g_sgu_v', 'sgu_w', 'sgu_b',
           'conv_w', 'w_br_a', 'w_br_b', 'w_br_c', 'w_br_d', 'w_out', 'g_post_mix', 'g_pre_ffn', 'w_ffn_gate',
           'w_ffn_up', 'w_ffn_down', 'g_post_ffn']
SHARDED = {'w_in': 2, 'w_uq': 2, 'w_ukv': 2, 'conv_w': 3, 'w_br_a': 2, 'w_br_b': 2, 'w_br_c': 2, 'w_br_d': 2,
           'w_out': 1, 'w_ffn_gate': 2, 'w_ffn_up': 2, 'w_ffn_down': 1}
REPLICATED = [n for n in WEIGHTS if n not in SHARDED]

VMEM_LIMIT = 56 * 1024 * 1024
TQ = 256


def _cparams(sem=None):
    return pltpu.CompilerParams(vmem_limit_bytes=VMEM_LIMIT, dimension_semantics=sem)


def _tile(n, cap, align=128):
    if n <= cap:
        return n
    for t in range(cap - cap % align, 0, -align):
        if n % t == 0:
            return t
    return n


def win(arr, width, idx):
    return ('win', arr, width, idx)


def tab(arr):
    return ('tab', arr)


def _rw(name, fn, ins, consts, outs, accs=(), tile=256):
    first = ins[0][1] if isinstance(ins[0], tuple) else ins[0]
    T = first.shape[0]
    tile = min(tile, T)
    grid = (T // tile,)
    arrays, in_specs = [], []
    for x in ins:
        if isinstance(x, tuple) and x[0] == 'win':
            _, a, w, k = x
            arrays.append(a)
            in_specs.append(pl.BlockSpec((tile, w), lambda i, k=k: (i, k)))
        elif isinstance(x, tuple) and x[0] == 'tab':
            a = x[1]
            nb = a.shape[0] // tile
            arrays.append(a)
            in_specs.append(pl.BlockSpec((tile, a.shape[1]), lambda i, nb=nb: (i % nb, 0)))
        else:
            arrays.append(x)
            in_specs.append(pl.BlockSpec((tile, x.shape[1]), lambda i: (i, 0)))
    for c in consts:
        arrays.append(c)
        in_specs.append(pl.BlockSpec(c.shape, lambda i, n=c.ndim: (0,) * n))
    out_shape = [jax.ShapeDtypeStruct((T, f), dt) for f, dt in outs]
    out_specs = [pl.BlockSpec((tile, f), lambda i: (i, 0)) for f, _ in outs]
    for shp in accs:
        out_shape.append(jax.ShapeDtypeStruct(shp, F32))
        out_specs.append(pl.BlockSpec(shp, lambda i, n=len(shp): (0,) * n))
    ni, nc, no = len(ins), len(consts), len(outs)

    def body(*refs):
        vals = fn(*[r[...] for r in refs[:ni + nc]])
        if not isinstance(vals, (tuple, list)):
            vals = (vals,)
        o_refs = refs[ni + nc:]
        for r, v in zip(o_refs[:no], vals[:no]):
            r[...] = v.astype(r.dtype)
        i = pl.program_id(0)
        for r, v in zip(o_refs[no:], vals[no:]):
            @pl.when(i == 0)
            def _(r=r, v=v):
                r[...] = v

            @pl.when(i > 0)
            def _(r=r, v=v):
                r[...] += v

    res = pl.pallas_call(body, name=name, grid=grid, in_specs=in_specs, out_specs=out_specs, out_shape=out_shape,
                         compiler_params=_cparams(("arbitrary",)))(*arrays)
    return res


def _mm(name, a, b, tb=False, out_dtype=F32, tm=512, tn=512, a2=None, b2=None):
    M, K = a.shape
    N = b.shape[0] if tb else b.shape[1]
    tm, tn = _tile(M, tm, 8), _tile(N, tn)
    dims = (((1,), (1,)), ((), ())) if tb else (((1,), (0,)), ((), ()))
    pairs = [(a, b)] + ([(a2, b2)] if a2 is not None else [])

    def body(*refs):
        o_ref = refs[-1]
        acc = None
        for i in range(len(pairs)):
            p = lax.dot_general(refs[2 * i][...].astype(BF16), refs[2 * i + 1][...].astype(BF16), dims,
                                preferred_element_type=F32)
            acc = p if acc is None else acc + p
        o_ref[...] = acc.astype(o_ref.dtype)

    in_specs, arrays = [], []
    for x, y in pairs:
        k = x.shape[1]
        in_specs.append(pl.BlockSpec((tm, k), lambda i, j: (i, 0)))
        in_specs.append(pl.BlockSpec((tn, k), lambda i, j: (j, 0)) if tb else pl.BlockSpec((k, tn), lambda i, j: (0, j)))
        arrays += [x, y]
    return pl.pallas_call(
        body, name=name, grid=(M // tm, N // tn), in_specs=in_specs,
        out_specs=pl.BlockSpec((tm, tn), lambda i, j: (i, j)),
        out_shape=jax.ShapeDtypeStruct((M, N), out_dtype),
        compiler_params=_cparams(("parallel", "parallel")))(*arrays)


def _mm_t(name, a, g, tk=1408, tn=1408, tt=512):
    T, K = a.shape
    N = g.shape[1]
    tk, tn, tt = _tile(K, tk), _tile(N, tn), _tile(T, tt, 8)

    def body(a_ref, g_ref, o_ref):
        p = lax.dot_general(a_ref[...].astype(BF16), g_ref[...].astype(BF16), (((0,), (0,)), ((), ())),
                            preferred_element_type=F32)
        t = pl.program_id(2)

        @pl.when(t == 0)
        def _():
            o_ref[...] = p

        @pl.when(t > 0)
        def _():
            o_ref[...] += p

    return pl.pallas_call(
        body, name=name, grid=(K // tk, N // tn, T // tt),
        in_specs=[pl.BlockSpec((tt, tk), lambda i, j, t: (t, i)), pl.BlockSpec((tt, tn), lambda i, j, t: (t, j))],
        out_specs=pl.BlockSpec((tk, tn), lambda i, j, t: (i, j)),
        out_shape=jax.ShapeDtypeStruct((K, N), F32),
        compiler_params=_cparams(("parallel", "parallel", "arbitrary")))(a, g)


def _dot(a, b):
    return jnp.dot(a.astype(BF16), b.astype(BF16), preferred_element_type=F32)


def _dot_nt(a, b):
    return lax.dot_general(a.astype(BF16), b.astype(BF16), (((1,), (1,)), ((), ())), preferred_element_type=F32)


def _dot_tn(a, b):
    return lax.dot_general(a.astype(BF16), b.astype(BF16), (((0,), (0,)), ((), ())), preferred_element_type=F32)


def _rms(x, g):
    r = lax.rsqrt(jnp.mean(x * x, axis=-1, keepdims=True) + EPS)
    return x * r * g


def _rms_bwd(x, g, dy):
    r = lax.rsqrt(jnp.mean(x * x, axis=-1, keepdims=True) + EPS)
    xh = x * r
    dxh = dy * g
    dx = r * (dxh - xh * jnp.mean(dxh * xh, axis=-1, keepdims=True))
    return dx, jnp.sum(dy * xh, axis=0, keepdims=True)


def _sigmoid(x):
    return 1.0 / (1.0 + jnp.exp(-x))


def _shift_down(x, k):
    n = x.shape[0]
    t = lax.broadcasted_iota(jnp.int32, x.shape, 0)
    return jnp.where(t >= k, pltpu.roll(x, k, 0), 0.0)


def _shift_up(x, k):
    n = x.shape[0]
    t = lax.broadcasted_iota(jnp.int32, x.shape, 0)
    return jnp.where(t < n - k, pltpu.roll(x, n - k, 0), 0.0)


def _group_select(vals):
    lane = lax.broadcasted_iota(jnp.int32, vals[0].shape, 1) // GROUP
    out = vals[-1]
    for g in range(len(vals) - 2, -1, -1):
        out = jnp.where(lane == g, vals[g], out)
    return out


def _pool_count(shape):
    t = lax.broadcasted_iota(jnp.int32, shape, 0)
    lane = lax.broadcasted_iota(jnp.int32, shape, 1) // GROUP
    w = jnp.where(lane == 0, POOL_WINDOWS[0], jnp.where(lane == 1, POOL_WINDOWS[1],
                  jnp.where(lane == 2, POOL_WINDOWS[2], POOL_WINDOWS[3])))
    return jnp.minimum(t + 1, w).astype(F32)


def _pooled(z):
    s = [z]
    for k in (1, 2, 4, 8):
        s.append(s[-1] + _shift_down(s[-1], k))
    return _group_select(s[1:]) / _pool_count(z.shape) - z


def _pooled_bwd(dp):
    u = dp / _pool_count(dp.shape)
    s = [u]
    for k in (1, 2, 4, 8):
        s.append(s[-1] + _shift_up(s[-1], k))
    return _group_select(s[1:]) - dp


def _diag_mask():
    r = lax.broadcasted_iota(jnp.int32, (TQ, TQ), 0) // CHUNK
    c = lax.broadcasted_iota(jnp.int32, (TQ, TQ), 1) // CHUNK
    return r >= c


def _attn_fwd(q, k, v, B, S):
    nq = S // TQ

    def body(q_ref, k_ref, v_ref, o_ref, lse_ref):
        qi = pl.program_id(2)
        qb = q_ref[...]

        def step(kj, carry, diag=False):
            m, l, acc = carry
            off = pl.multiple_of(kj * TQ, TQ)
            kb, vb = k_ref[pl.ds(off, TQ), :], v_ref[pl.ds(off, TQ), :]
            s = _dot_nt(qb, kb)
            if diag:
                s = jnp.where(_diag_mask(), s, NEG_INF)
            m_new = jnp.maximum(m, jnp.max(s, axis=-1, keepdims=True))
            p = jnp.exp(s - m_new)
            alpha = jnp.exp(m - m_new)
            return m_new, alpha * l + jnp.sum(p, axis=-1, keepdims=True), alpha * acc + _dot(p, vb)

        init = (jnp.full((TQ, 1), NEG_INF, F32), jnp.zeros((TQ, 1), F32), jnp.zeros((TQ, HP), F32))
        m, l, acc = step(qi, lax.fori_loop(0, qi, step, init), diag=True)
        o_ref[...] = acc / l
        lse_ref[...] = jnp.broadcast_to(m + jnp.log(l), (TQ, HP))

    qspec = pl.BlockSpec((TQ, HP), lambda b, h, i: (b * nq + i, h))
    kspec = pl.BlockSpec((S, HP), lambda b, h, i: (b, h))
    return pl.pallas_call(
        body, name="attn_fwd", grid=(B, HEADS, nq), in_specs=[qspec, kspec, kspec], out_specs=[qspec, qspec],
        out_shape=[jax.ShapeDtypeStruct((B * S, QW), F32)] * 2,
        compiler_params=_cparams(("parallel", "parallel", "arbitrary")))(q, k, v)


def _attn_dq(q, k, v, o, do, lse, B, S):
    nq = S // TQ

    def body(q_ref, k_ref, v_ref, o_ref, do_ref, lse_ref, dq_ref):
        qi = pl.program_id(2)
        qb, dob = q_ref[...], do_ref[...]
        delta = jnp.sum(o_ref[...] * dob, axis=-1, keepdims=True)
        lse_b = lse_ref[...][:, :1]

        def step(kj, dq, diag=False):
            off = pl.multiple_of(kj * TQ, TQ)
            kb, vb = k_ref[pl.ds(off, TQ), :], v_ref[pl.ds(off, TQ), :]
            p = jnp.exp(_dot_nt(qb, kb) - lse_b)
            if diag:
                p = jnp.where(_diag_mask(), p, 0.0)
            ds = p * (_dot_nt(dob, vb) - delta)
            return dq + _dot(ds, kb)

        dq_ref[...] = step(qi, lax.fori_loop(0, qi, step, jnp.zeros((TQ, HP), F32)), diag=True)

    qspec = pl.BlockSpec((TQ, HP), lambda b, h, i: (b * nq + i, h))
    kspec = pl.BlockSpec((S, HP), lambda b, h, i: (b, h))
    return pl.pallas_call(
        body, name="attn_dq", grid=(B, HEADS, nq), in_specs=[qspec, kspec, kspec, qspec, qspec, qspec],
        out_specs=qspec, out_shape=jax.ShapeDtypeStruct((B * S, QW), F32),
        compiler_params=_cparams(("parallel", "parallel", "arbitrary")))(q, k, v, o, do, lse)


def _attn_dkv(q, k, v, o, do, lse, B, S):
    nq = S // TQ

    def body(q_ref, k_ref, v_ref, o_ref, do_ref, lse_ref, dk_ref, dv_ref):
        kj = pl.program_id(2)
        kb, vb = k_ref[...], v_ref[...]

        def step(qi, carry, diag=False):
            dk, dv = carry
            off = pl.multiple_of(qi * TQ, TQ)
            qb, dob, ob = q_ref[pl.ds(off, TQ), :], do_ref[pl.ds(off, TQ), :], o_ref[pl.ds(off, TQ), :]
            lse_b = lse_ref[pl.ds(off, TQ), :][:, :1]
            p = jnp.exp(_dot_nt(qb, kb) - lse_b)
            if diag:
                p = jnp.where(_diag_mask(), p, 0.0)
            delta = jnp.sum(ob * dob, axis=-1, keepdims=True)
            ds = p * (_dot_nt(dob, vb) - delta)
            return dk + _dot_tn(ds, qb), dv + _dot_tn(p, dob)

        init = step(kj, (jnp.zeros((TQ, HP), F32), jnp.zeros((TQ, HP), F32)), diag=True)
        dk, dv = lax.fori_loop(kj + 1, nq, step, init)
        dk_ref[...] = dk.astype(BF16)
        dv_ref[...] = dv.astype(BF16)

    kspec = pl.BlockSpec((TQ, HP), lambda b, h, j: (b * nq + j, h))
    qspec = pl.BlockSpec((S, HP), lambda b, h, j: (b, h))
    return pl.pallas_call(
        body, name="attn_dkv", grid=(B, HEADS, nq), in_specs=[qspec, kspec, kspec, qspec, qspec, qspec],
        out_specs=[kspec, kspec], out_shape=[jax.ShapeDtypeStruct((B * S, QW), BF16)] * 2,
        compiler_params=_cparams(("parallel", "parallel", "arbitrary")))(q, k, v, o, do, lse)


def _all_gather(name, xs):
    n = len(xs)

    def body(*refs):
        x_refs, out_refs = refs[:n], refs[n:2 * n]
        send_sems, recv_sems, local_sems = refs[2 * n:]
        x_, y_, c_ = lax.axis_index("x"), lax.axis_index("y"), lax.axis_index("c")
        me, sibling = (x_, y_, c_), (x_, y_, 1 - c_)
        chips = [(1 - x_, y_), (x_, 1 - y_), (1 - x_, 1 - y_)]

        def copy(a, k, block, to, own=False):
            slot = out_refs[a].at[4 * block[0] + 2 * block[1] + block[2]]
            return pltpu.make_async_remote_copy(
                src_ref=x_refs[a] if own else slot, dst_ref=slot,
                send_sem=send_sems.at[7 * a + k], recv_sem=recv_sems.at[7 * a + k], device_id=to, device_id_type=MESH)

        mine = [pltpu.make_async_copy(x_refs[a], out_refs[a].at[4 * x_ + 2 * y_ + c_], local_sems.at[a]) for a in range(n)]
        for cp in mine:
            cp.start()
        first = [copy(a, 1 + j, me, (*chip, c_), own=True) for j, chip in enumerate(chips) for a in range(n)]
        first += [copy(a, 0, me, sibling, own=True) for a in range(n)]
        for cp in first:
            cp.start()
        passed = []
        for j, chip in enumerate(chips):
            for a in range(n):
                copy(a, 1 + j, (*chip, c_), me).wait_recv()
                passed.append(copy(a, 4 + j, (*chip, c_), sibling))
                passed[-1].start()
        for a in range(n):
            copy(a, 0, sibling, me).wait_recv()
            for j, chip in enumerate(chips):
                copy(a, 4 + j, (*chip, 1 - c_), me).wait_recv()
        for cp in first + passed:
            cp.wait_send()
        for cp in mine:
            cp.wait()

    return pl.pallas_call(
        body, name=name, out_shape=[jax.ShapeDtypeStruct((N_DEV,) + x.shape, x.dtype) for x in xs],
        in_specs=[pl.BlockSpec(memory_space=pl.ANY)] * n, out_specs=[pl.BlockSpec(memory_space=pl.ANY)] * n,
        scratch_shapes=[pltpu.SemaphoreType.DMA((7 * n,)), pltpu.SemaphoreType.DMA((7 * n,)), pltpu.SemaphoreType.DMA((n,))],
    )(*xs)


def _all_to_all(name, xs):
    n = len(xs)

    def body(*refs):
        x_refs, out_refs = refs[:n], refs[n:2 * n]
        send_sems, recv_sems, local_sems = refs[2 * n:]
        x_, y_, c_ = lax.axis_index("x"), lax.axis_index("y"), lax.axis_index("c")
        me = 4 * x_ + 2 * y_ + c_
        mine = [pltpu.make_async_copy(x_refs[a].at[me], out_refs[a].at[me], local_sems.at[a]) for a in range(n)]
        for cp in mine:
            cp.start()
        sends, recvs = [], []
        for k in range(1, N_DEV):
            px, py, pc = (x_ + (k >> 2)) % 2, (y_ + ((k >> 1) & 1)) % 2, (c_ + (k & 1)) % 2
            peer = 4 * px + 2 * py + pc
            for a in range(n):
                sem = dict(send_sem=send_sems.at[7 * a + k - 1], recv_sem=recv_sems.at[7 * a + k - 1],
                           device_id=(px, py, pc), device_id_type=MESH)
                sends.append(pltpu.make_async_remote_copy(src_ref=x_refs[a].at[peer], dst_ref=out_refs[a].at[me], **sem))
                recvs.append(pltpu.make_async_remote_copy(src_ref=x_refs[a].at[me], dst_ref=out_refs[a].at[peer], **sem))
        for cp in sends:
            cp.start()
        for cp in recvs:
            cp.wait_recv()
        for cp in sends:
            cp.wait_send()
        for cp in mine:
            cp.wait()

    return pl.pallas_call(
        body, name=name, out_shape=[jax.ShapeDtypeStruct(x.shape, x.dtype) for x in xs],
        in_specs=[pl.BlockSpec(memory_space=pl.ANY)] * n, out_specs=[pl.BlockSpec(memory_space=pl.ANY)] * n,
        scratch_shapes=[pltpu.SemaphoreType.DMA((7 * n,)), pltpu.SemaphoreType.DMA((7 * n,)), pltpu.SemaphoreType.DMA((n,))],
    )(*xs)


def _adamw(name, parts, w, m, v, row_off=0):
    R, C = w.shape
    assert parts.shape[2] == C and parts.shape[1] >= row_off + R
    tr = max(t for t in range(8, 513, 8) if R % t == 0 and row_off % t == 0) if R % 8 == 0 else R
    assert row_off % tr == 0
    ob = row_off // tr

    def body(p_ref, w_ref, m_ref, v_ref, g_ref, d_ref, nm_ref, nv_ref):
        g = p_ref[0].astype(F32)
        for j in range(1, N_DEV):
            g = g + p_ref[j].astype(F32)
        m_new = ADAM_B1 * m_ref[...] + (1.0 - ADAM_B1) * g
        v_new = ADAM_B2 * v_ref[...] + (1.0 - ADAM_B2) * (g * g)
        m_hat = m_new / (1.0 - ADAM_B1 ** ADAM_STEP)
        v_hat = v_new / (1.0 - ADAM_B2 ** ADAM_STEP)
        g_ref[...] = g
        d_ref[...] = -ADAM_LR * (m_hat / (jnp.sqrt(v_hat) + ADAM_EPS) + ADAM_WD * w_ref[...])
        nm_ref[...] = m_new
        nv_ref[...] = v_new

    spec = pl.BlockSpec((tr, C), lambda i: (i, 0))
    return pl.pallas_call(
        body, name=name, grid=(R // tr,), in_specs=[pl.BlockSpec((N_DEV, tr, C), lambda i: (0, i + ob, 0)), spec, spec, spec],
        out_specs=[spec] * 4, out_shape=[jax.ShapeDtypeStruct((R, C), F32)] * 4,
        compiler_params=_cparams(("parallel",)))(parts, w, m, v)


def _rot_cols(w):
    h = w.shape[-1] // 2
    return jnp.concatenate([-w[..., h:], w[..., :h]], axis=-1)


def _unrot_cols(d):
    h = d.shape[-1] // 2
    return jnp.concatenate([d[..., h:], -d[..., :h]], axis=-1)


def _split_in(w):
    out, o = [], 0
    for n in IN_SIZES:
        out.append(w[..., o:o + n])
        o += n
    return out


def _layer_weights(G):
    L = DEPTH
    cq, ckv, kr, pin, su, sv, cvb, cvc, cvx, gate = _split_in(G['w_in'])
    W1 = jnp.concatenate([cq, ckv, kr, _rot_cols(kr), jnp.zeros((L, D, 64), BF16), pin, su, sv, cvb, cvc, cvx, gate], axis=-1)
    wq = G['w_uq'].reshape(L, Q_LORA, HEADS, QK_NOPE + QK_ROPE)
    nope, rope = wq[..., :QK_NOPE], wq[..., QK_NOPE:]
    z32 = jnp.zeros((L, Q_LORA, HEADS, HP - QK_NOPE - QK_ROPE), BF16)
    wq_a = jnp.concatenate([nope, rope, z32], axis=-1).reshape(L, Q_LORA, QW)
    wq_b = jnp.concatenate([jnp.zeros_like(nope), _rot_cols(rope), z32], axis=-1).reshape(L, Q_LORA, QW)
    wkv = G['w_ukv'].reshape(L, KV_LORA, HEADS, QK_NOPE + V_HEAD)
    z64 = jnp.zeros((L, KV_LORA, HEADS, HP - QK_NOPE), BF16)
    wkn = jnp.concatenate([wkv[..., :QK_NOPE], z64], axis=-1).reshape(L, KV_LORA, QW)
    wv = jnp.concatenate([wkv[..., QK_NOPE:], z64], axis=-1).reshape(L, KV_LORA, QW)
    wa = G['w_br_a'].reshape(L, HEADS, V_HEAD, D)
    wa = jnp.concatenate([wa, jnp.zeros((L, HEADS, HP - V_HEAD, D), BF16)], axis=2).reshape(L, QW, D)
    return dict(
        W1=W1, W1s=W1[..., :C_GATE], W1g=W1[..., C_GATE:], Wq=jnp.concatenate([wq_a, wq_b], axis=-1), Wkv=jnp.concatenate([wkn, wv], axis=-1), Wa=wa,
        Wb=G['w_br_b'], Wc=G['w_br_c'], Wd=G['w_br_d'], Wout=G['w_out'],
        Wgu=jnp.concatenate([G['w_ffn_gate'], G['w_ffn_up']], axis=-1), Wdn=G['w_ffn_down'],
        conv=G['conv_w'].reshape(L, 3, BR).astype(F32))


def _tables(S):
    inv = ROPE_THETA ** (-jnp.arange(0, QK_ROPE, 2, dtype=F32) / QK_ROPE)
    ang = jnp.arange(S, dtype=F32)[:, None] * inv[None, :]
    cos, sin = jnp.cos(ang), jnp.sin(ang)
    scale = (QK_NOPE + QK_ROPE) ** -0.5
    one, zero = jnp.ones((S, QK_NOPE), F32), jnp.zeros((S, QK_NOPE), F32)
    z32 = jnp.zeros((S, HP - QK_NOPE - QK_ROPE), F32)
    cq = jnp.tile(jnp.concatenate([one, cos, cos, z32], axis=1) * scale, (1, HEADS))
    sq = jnp.tile(jnp.concatenate([zero, sin, sin, z32], axis=1) * scale, (1, HEADS))
    ck = jnp.concatenate([cos, cos, sin, sin, jnp.zeros((S, HP - 2 * QK_ROPE), F32)], axis=1)
    return cq, sq, ck


def _place_k():
    p = np.zeros((HP, QW), np.float32)
    for r in range(2 * QK_ROPE):
        for h in range(HEADS):
            p[r, h * HP + QK_NOPE + r % QK_ROPE] = 1.0
    return jnp.asarray(p, BF16)


def _layer_fwd(x, W, P, tabs, B, S):
    cq_t, sq_t, ck_t, place = tabs
    g1, g_cq, g_ckv = P['g_pre_mix'], P['g_cq'], P['g_ckv']
    h = _rw("pre_mix", lambda x, g: _rms(x, g), [x], [g1], [(D, BF16)], tile=512)[0]
    P1 = _mm("in_proj", h, W['W1'], tm=512, tn=512)

    def q_prep(c, cq, sq, g, w):
        qq = _dot(_rms(c, g), w)
        return qq[:, :QW] * cq + qq[:, QW:] * sq
    q = _rw("q_prep", q_prep, [win(P1, 256, 0), tab(cq_t), tab(sq_t)], [g_cq, W['Wq']], [(QW, BF16)])[0]

    def kv_prep(c, seg, ck, g, w, place):
        kv = _dot(_rms(c, g), w)
        return kv[:, :QW] + _dot(seg * ck, place), kv[:, QW:]
    k, v = _rw("kv_prep", kv_prep, [win(P1, 128, 2), win(P1, 128, 3), tab(ck_t)], [g_ckv, W['Wkv'], place],
               [(QW, BF16), (QW, BF16)])
    o, lse = _attn_fwd(q, k, v, B, S)

    def pool_f(z, wbd, scale):
        return _dot(_pooled(z), wbd) * scale
    bo = _rw("pool_fwd", pool_f, [win(P1, BR, 2)], [P['pool_bd'], P['pool_scale']], [(BR, BF16)], tile=S)[0]

    def sgu_f(u, sv, g, wm, bias):
        return u * _sgu_mix(_rms(sv, g), wm, bias)
    co = _rw("sgu_fwd", sgu_f, [win(P1, BR, 3), win(P1, BR, 4)], [P['g_sgu_v'], P['sgu_wm'], P['sgu_bias']],
             [(BR, BF16)], tile=SGU_BLOCK)[0]

    def conv_f(b, c, xi, w):
        z = c * xi
        return b * (w[0:1] * _shift_down(z, 2) + w[1:2] * _shift_down(z, 1) + w[2:3] * z)
    do_ = _rw("conv_fwd", conv_f, [win(P1, BR, 5), win(P1, BR, 6), win(P1, BR, 7)], [W['conv']], [(BR, BF16)], tile=S)[0]

    def merge_f(a, b, c, d, l0, l1, l2, l3, wa, wb, wc, wd):
        return (_sigmoid(l0) * _dot(a, wa) + _sigmoid(l1) * _dot(b, wb) + _sigmoid(l2) * _dot(c, wc)
                + _sigmoid(l3) * _dot(d, wd))
    merged = _rw("merge_fwd", merge_f, [o, bo, co, do_] + [win(P1, D, 2 + i) for i in range(4)],
                 [W['Wa'], W['Wb'], W['Wc'], W['Wd']], [(D, BF16)])[0]
    om = _mm("out_proj", merged, W['Wout'])
    x1 = _rw("post_mix", lambda x, o, g: x + _rms(o, g), [x, om], [P['g_post_mix']], [(D, F32)], tile=512)[0]
    h2 = _rw("pre_ffn", lambda x, g: _rms(x, g), [x1], [P['g_pre_ffn']], [(D, BF16)], tile=512)[0]
    ab = _mm("ffn_up", h2, W['Wgu'])

    def act_f(a, b):
        return a * _sigmoid(a) * b
    act = _rw("ffn_act", act_f, [win(ab, D_FF, 0), win(ab, D_FF, 1)], [], [(D_FF, BF16)])[0]
    f = _mm("ffn_down", act, W['Wdn'])
    x2 = _rw("post_ffn", lambda x, f, g: x + _rms(f, g), [x1, f], [P['g_post_ffn']], [(D, F32)], tile=512)[0]
    saved = dict(x=x, h=h, P1=P1, q=q, k=k, v=v, o=o, lse=lse, bo=bo, co=co, do=do_, merged=merged, om=om, x1=x1,
                 h2=h2, ab=ab, act=act, f=f)
    return x2, saved


def _sgu_mix(vn, wm, bias):
    outs = [_dot(wm[g], vn) for g in range(4)]
    return _group_select(outs) + bias


def _layer_bwd(dx2, sv, W, P, tabs, B, S):
    cq_t, sq_t, ck_t, place = tabs
    P1 = sv['P1']
    grads = {}

    def post_ffn_b(f, dy, g):
        return _rms_bwd(f, g, dy)
    df, grads['g_post_ffn'] = _rw("post_ffn_bwd", post_ffn_b, [sv['f'], dx2], [P['g_post_ffn']], [(D, BF16)], [(1, D)], tile=512)
    dact = _mm("ffn_down_dx", df, W['Wdn'], tb=True)
    grads['Wdn'] = _mm_t("ffn_down_dw", sv['act'], df)

    def act_b(a, b, d):
        s = _sigmoid(a)
        return jnp.concatenate([(d * b * (s + a * s * (1.0 - s))).astype(BF16), (d * a * s).astype(BF16)], axis=1)
    dab = _rw("ffn_act_bwd", act_b, [win(sv['ab'], D_FF, 0), win(sv['ab'], D_FF, 1), dact], [], [(2 * D_FF, BF16)])[0]
    dh2 = _mm("ffn_up_dx", dab, W['Wgu'], tb=True)
    grads['Wgu'] = _mm_t("ffn_up_dw", sv['h2'], dab)

    def pre_ffn_b(x, dh, dy, g):
        dx, dg = _rms_bwd(x, g, dh)
        return dy + dx, dg
    dx1, grads['g_pre_ffn'] = _rw("pre_ffn_bwd", pre_ffn_b, [sv['x1'], dh2, dx2], [P['g_pre_ffn']], [(D, F32)], [(1, D)], tile=512)

    def post_mix_b(o, dy, g):
        return _rms_bwd(o, g, dy)
    dom, grads['g_post_mix'] = _rw("post_mix_bwd", post_mix_b, [sv['om'], dx1], [P['g_post_mix']], [(D, BF16)], [(1, D)], tile=512)
    dmerged = _mm("out_proj_dx", dom, W['Wout'], tb=True)
    grads['Wout'] = _mm_t("out_proj_dw", sv['merged'], dom)

    def merge_b(a, b, c, d, l0, l1, l2, l3, dm, wa, wb, wc, wd):
        outs_dl, outs_dy, outs_dbr = [], [], []
        for br, l, w in ((a, l0, wa), (b, l1, wb), (c, l2, wc), (d, l3, wd)):
            s = _sigmoid(l)
            y = _dot(br, w)
            dy = (dm * s).astype(BF16)
            outs_dl.append((dm * y * s * (1.0 - s)).astype(BF16))
            outs_dy.append(dy)
            outs_dbr.append(_dot_nt(dy, w))
        return (jnp.concatenate(outs_dl, axis=1), *outs_dy, *outs_dbr)
    mb = _rw("merge_bwd", merge_b,
             [sv['o'], sv['bo'], sv['co'], sv['do']] + [win(P1, D, 2 + i) for i in range(4)] + [dmerged],
             [W['Wa'], W['Wb'], W['Wc'], W['Wd']],
             [(4 * D, BF16)] + [(D, BF16)] * 4 + [(QW, F32), (BR, F32), (BR, F32), (BR, F32)])
    dl, dy, (d_o, d_bo, d_co, d_do) = mb[0], mb[1:5], mb[5:9]
    grads['Wa'] = _mm_t("br_a_dw", sv['o'], dy[0])
    grads['Wb'] = _mm_t("br_b_dw", sv['bo'], dy[1])
    grads['Wc'] = _mm_t("br_c_dw", sv['co'], dy[2])
    grads['Wd'] = _mm_t("br_d_dw", sv['do'], dy[3])

    def conv_b(b, c, xi, dout, w):
        z = c * xi
        z1, z2 = _shift_down(z, 1), _shift_down(z, 2)
        y = w[0:1] * z2 + w[1:2] * z1 + w[2:3] * z
        dyv = dout * b
        dz = w[2:3] * dyv + w[1:2] * _shift_up(dyv, 1) + w[0:1] * _shift_up(dyv, 2)
        return (dout * y, dz * xi, dz * c, jnp.sum(dyv * z2, axis=0, keepdims=True),
                jnp.sum(dyv * z1, axis=0, keepdims=True), jnp.sum(dyv * z, axis=0, keepdims=True))
    dcvb, dcvc, dcvx, dw0, dw1, dw2 = _rw("conv_bwd", conv_b, [win(P1, BR, 5), win(P1, BR, 6), win(P1, BR, 7), d_do],
                                          [W['conv']], [(BR, BF16)] * 3, [(1, BR)] * 3, tile=S)
    grads['conv'] = jnp.concatenate([dw0, dw1, dw2], axis=0)

    def sgu_b(u, s_v, dout, g, wm, bias):
        vn = _rms(s_v, g)
        mixed = _sgu_mix(vn, wm, bias)
        dmix = dout * u
        dvn = _group_select([_dot_tn(wm[gi], dmix) for gi in range(4)])
        dsv, dg = _rms_bwd(s_v, g, dvn)
        lane = lax.broadcasted_iota(jnp.int32, dmix.shape, 1) // GROUP
        dwm = [_dot_nt(jnp.where(lane == gi, dmix, 0.0), vn) for gi in range(4)]
        return (dout * mixed, dsv, dg, *dwm, dmix)
    dsu, dsv_, grads['g_sgu_v'], wm0, wm1, wm2, wm3, grads['sgu_bias'] = _rw(
        "sgu_bwd", sgu_b, [win(P1, BR, 3), win(P1, BR, 4), d_co], [P['g_sgu_v'], P['sgu_wm'], P['sgu_bias']],
        [(BR, BF16)] * 2, [(1, BR)] + [(SGU_BLOCK, SGU_BLOCK)] * 4 + [(SGU_BLOCK, BR)], tile=SGU_BLOCK)
    grads['sgu_wm'] = jnp.stack([wm0, wm1, wm2, wm3])

    def pool_b(z, dout, wbd, scale):
        pooled = _pooled(z)
        mixed = _dot(pooled, wbd)
        dmix = dout * scale
        dz = _pooled_bwd(_dot_nt(dmix, wbd))
        return dz, _dot_tn(pooled, dmix), jnp.sum(dout * mixed, axis=0, keepdims=True)
    dpin, grads['pool_bd'], grads['pool_scale'] = _rw(
        "pool_bwd", pool_b, [win(P1, BR, 2), d_bo], [P['pool_bd'], P['pool_scale']], [(BR, BF16)], [(BR, BR), (1, BR)], tile=S)

    dq = _attn_dq(sv['q'], sv['k'], sv['v'], sv['o'], d_o, sv['lse'], B, S)
    dk, dv = _attn_dkv(sv['q'], sv['k'], sv['v'], sv['o'], d_o, sv['lse'], B, S)

    def q_prep_b(c, dq, cq, sq, g, w):
        dqq = jnp.concatenate([dq * cq, dq * sq], axis=1).astype(BF16)
        dc, dg = _rms_bwd(c, g, _dot_nt(dqq, w))
        return dc, _rms(c, g), dqq, dg
    dcq, cqn, dqq, grads['g_cq'] = _rw("q_prep_bwd", q_prep_b, [win(P1, 256, 0), dq, tab(cq_t), tab(sq_t)],
                                       [P['g_cq'], W['Wq']], [(Q_LORA, BF16), (Q_LORA, BF16), (2 * QW, BF16)], [(1, Q_LORA)])
    grads['Wq'] = _mm_t("uq_dw", cqn, dqq)

    def kv_prep_b(c, dk, dv, ck, g, w, place):
        dkv = jnp.concatenate([dk, dv], axis=1)
        dc, dg = _rms_bwd(c, g, _dot_nt(dkv, w))
        return dc, _dot_nt(dk, place) * ck, _rms(c, g), dkv, dg
    dckv, dseg, kvn, dkv, grads['g_ckv'] = _rw(
        "kv_prep_bwd", kv_prep_b, [win(P1, 128, 2), dk, dv, tab(ck_t)], [P['g_ckv'], W['Wkv'], place],
        [(KV_LORA, BF16), (HP, BF16), (KV_LORA, BF16), (2 * QW, BF16)], [(1, KV_LORA)])
    grads['Wkv'] = _mm_t("ukv_dw", kvn, dkv)

    dps = jnp.concatenate([dcq, dckv, dseg, dpin, dsu, dsv_, dcvb, dcvc, dcvx], axis=1)
    dh = _mm("in_proj_dx", dps, W['W1s'], tb=True, a2=dl, b2=W['W1g'])
    grads['W1s'] = _mm_t("in_proj_s_dw", sv['h'], dps)
    grads['W1g'] = _mm_t("in_proj_g_dw", sv['h'], dl)

    def pre_mix_b(x, dh, dy, g):
        dx, dg = _rms_bwd(x, g, dh)
        return dy + dx, dg
    dx, grads['g_pre_mix'] = _rw("pre_mix_bwd", pre_mix_b, [sv['x'], dh, dx1], [P['g_pre_mix']], [(D, F32)], [(1, D)], tile=512)
    return dx, grads


def _param_grads(gl):
    st = {k: jnp.stack([g[k] for g in gl]) for k in gl[0]}
    L = DEPTH
    w1 = st['W1s']
    seg = w1[..., C_KR:C_KR + 128]
    dkr = seg[..., :QK_ROPE] + _unrot_cols(seg[..., QK_ROPE:2 * QK_ROPE])
    out = {}
    out['w_in'] = jnp.concatenate([w1[..., :C_KR], dkr, w1[..., C_PIN:], st['W1g']], axis=-1)
    qa = st['Wq'][..., :QW].reshape(L, Q_LORA, HEADS, HP)
    qb = st['Wq'][..., QW:].reshape(L, Q_LORA, HEADS, HP)
    drope = qa[..., QK_NOPE:QK_NOPE + QK_ROPE] + _unrot_cols(qb[..., QK_NOPE:QK_NOPE + QK_ROPE])
    out['w_uq'] = jnp.concatenate([qa[..., :QK_NOPE], drope], axis=-1).reshape(L, Q_LORA, HEADS * (QK_NOPE + QK_ROPE))
    kn = st['Wkv'][..., :QW].reshape(L, KV_LORA, HEADS, HP)[..., :QK_NOPE]
    vv = st['Wkv'][..., QW:].reshape(L, KV_LORA, HEADS, HP)[..., :V_HEAD]
    out['w_ukv'] = jnp.concatenate([kn, vv], axis=-1).reshape(L, KV_LORA, HEADS * (QK_NOPE + V_HEAD))
    out['w_br_a'] = st['Wa'].reshape(L, HEADS, HP, D)[:, :, :V_HEAD].reshape(L, HEADS * V_HEAD, D)
    out['w_br_b'], out['w_br_c'], out['w_br_d'], out['w_out'] = st['Wb'], st['Wc'], st['Wd'], st['Wout']
    out['w_ffn_gate'], out['w_ffn_up'], out['w_ffn_down'] = st['Wgu'][..., :D_FF], st['Wgu'][..., D_FF:], st['Wdn']
    out['conv_w'] = st['conv'].reshape(L, 3, 1, BR)
    for n in ('g_pre_mix', 'g_cq', 'g_ckv', 'g_sgu_v', 'g_post_mix', 'g_pre_ffn', 'g_post_ffn', 'pool_scale'):
        out[n] = st[n].reshape(L, -1)
    bd = st['pool_bd'].reshape(L, 4, GROUP, 4, GROUP)
    out['pool_w'] = jnp.stack([bd[:, g, :, g, :] for g in range(4)], axis=1)
    out['sgu_w'] = st['sgu_wm'] * _sgu_mask()[None, None]
    out['sgu_b'] = st['sgu_bias'].reshape(L, SGU_BLOCK, 4, GROUP).sum(-1).transpose(0, 2, 1)
    return out


def _sgu_mask():
    pc = np.arange(SGU_BLOCK) // CHUNK
    return jnp.asarray(pc[:, None] >= pc[None, :], F32)


def _rows(a, lead):
    return a.reshape(a.shape[:lead] + (-1, a.shape[-1]))


def _pad_to(a, rows, cols):
    pad = [(0, 0)] * (a.ndim - 2) + [(0, rows - a.shape[-2]), (0, cols - a.shape[-1])]
    return jnp.pad(a, pad)


GROUPS = (('w_in',), ('w_ffn_gate', 'w_ffn_up'), ('w_out', 'w_ffn_down'),
          ('w_uq', 'w_ukv', 'w_br_a', 'w_br_b', 'w_br_c', 'w_br_d', 'conv_w'))
NARROW = {'w_uq': (DEPTH * Q_LORA, 128), 'conv_w': (16, 128)}


def _group(t, lead):
    bufs = []
    for names in GROUPS:
        ps = [_rows(t[n], lead) for n in names]
        ps = [_pad_to(p, *NARROW[n]) if n in NARROW else p for n, p in zip(names, ps)]
        bufs.append(ps[0] if len(ps) == 1 else jnp.concatenate(ps, axis=-2))
    return bufs


def _group_rows(shapes):
    where = {}
    for b, names in enumerate(GROUPS):
        off = 0
        for n in names:
            r, c = NARROW.get(n, shapes[n])
            where[n] = (b, off)
            off += r
    return where


def _shard_split(a, axis):
    n = a.shape[axis]
    a = a.reshape(a.shape[:axis] + (N_DEV, n // N_DEV) + a.shape[axis + 1:])
    return jnp.moveaxis(a, axis, 0)


def _shard_join(a, axis):
    a = jnp.moveaxis(a, 0, axis)
    return a.reshape(a.shape[:axis] + (a.shape[axis] * a.shape[axis + 1],) + a.shape[axis + 2:])


def _as2d(a):
    return a.reshape(-1, a.shape[-1])


def kernel(x, w_in, g_pre_mix, g_cq, g_ckv, w_uq, w_ukv, pool_w, pool_scale, g_sgu_v, sgu_w, sgu_b, conv_w, w_br_a, w_br_b, w_br_c, w_br_d, w_out, g_post_mix, g_pre_ffn, w_ffn_gate, w_ffn_up, w_ffn_down, g_post_ffn, loss_target, m_w_in, m_g_pre_mix, m_g_cq, m_g_ckv, m_w_uq, m_w_ukv, m_pool_w, m_pool_scale, m_g_sgu_v, m_sgu_w, m_sgu_b, m_conv_w, m_w_br_a, m_w_br_b, m_w_br_c, m_w_br_d, m_w_out, m_g_post_mix, m_g_pre_ffn, m_w_ffn_gate, m_w_ffn_up, m_w_ffn_down, m_g_post_ffn, v_w_in, v_g_pre_mix, v_g_cq, v_g_ckv, v_w_uq, v_w_ukv, v_pool_w, v_pool_scale, v_g_sgu_v, v_sgu_w, v_sgu_b, v_conv_w, v_w_br_a, v_w_br_b, v_w_br_c, v_w_br_d, v_w_out, v_g_post_mix, v_g_pre_ffn, v_w_ffn_gate, v_w_ffn_up, v_w_ffn_down, v_g_post_ffn):
    args = dict(locals())
    w = {n: args[n] for n in WEIGHTS}
    m = {n: args['m_' + n] for n in WEIGHTS}
    v = {n: args['v_' + n] for n in WEIGHTS}
    B, S, _ = x.shape
    T = B * S
    L = DEPTH

    names = list(SHARDED)
    shapes2 = {n: _as2d(w[n]).shape for n in names}
    where = _group_rows(shapes2)
    gathered = _all_gather("gather_weights", _group({n: w[n].astype(BF16) for n in names}, 0))
    G = {}
    for n in names:
        b, off = where[n]
        r, c = shapes2[n]
        G[n] = _shard_join(gathered[b][:, off:off + r, :c].reshape((N_DEV,) + w[n].shape), SHARDED[n])
    W = _layer_weights(G)

    eye = jnp.eye(4, dtype=F32)
    pool_bd = (pool_w[:, :, :, None, :] * eye[None, :, None, :, None]).reshape(L, BR, BR).astype(BF16)
    sgu_wm = (sgu_w * _sgu_mask()[None, None]).astype(BF16)
    sgu_bias = jnp.repeat(sgu_b.transpose(0, 2, 1), GROUP, axis=2)
    tabs = _tables(S) + (_place_k(),)

    def layer_params(l):
        P = {n: w[n][l][None, :] for n in ('g_pre_mix', 'g_cq', 'g_ckv', 'g_sgu_v', 'g_post_mix', 'g_pre_ffn',
                                            'g_post_ffn', 'pool_scale')}
        P.update(pool_bd=pool_bd[l], sgu_wm=sgu_wm[l], sgu_bias=sgu_bias[l])
        return P

    xt = x.reshape(T, D)
    saved = []
    for l in range(L):
        xt, sv = _layer_fwd(xt, {k: a[l] for k, a in W.items()}, layer_params(l), tabs, B, S)
        saved.append(sv)

    def loss_f(y, t):
        e = y - t
        return e * (1.0 / D), jnp.sum(e * e, axis=0, keepdims=True)
    dy, sq = _rw("loss", loss_f, [xt, loss_target.reshape(T, D)], [], [(D, F32)], [(1, D)], tile=512)
    loss = lax.psum(0.5 * jnp.sum(sq) / D, ("x", "y", "c"))

    gl = [None] * L
    for l in reversed(range(L)):
        dy, gl[l] = _layer_bwd(dy, saved[l], {k: a[l] for k, a in W.items()}, layer_params(l), tabs, B, S)
    grad_x = dy.reshape(B, S, D)
    pg = _param_grads(gl)

    got = _all_to_all("scatter_grads", _group({n: _shard_split(pg[n], SHARDED[n]).astype(BF16) for n in names}, 1))
    small = jnp.concatenate([pg[n].reshape(-1, 128) for n in REPLICATED], axis=0)
    small = _pad_to(small, -(-small.shape[0] // 8) * 8, 128)
    got_small = _all_gather("gather_small_grads", [small])[0]

    res = {}
    for n in names:
        b, off = where[n]
        r, c = shapes2[n]
        if n in NARROW:
            res[n] = _adamw("adamw_" + n, got[b][:, off:off + r, :c], _as2d(w[n]), _as2d(m[n]), _as2d(v[n]))
        else:
            res[n] = _adamw("adamw_" + n, got[b], _as2d(w[n]), _as2d(m[n]), _as2d(v[n]), row_off=off)
    off = 0
    for n in REPLICATED:
        r = int(np.prod(w[n].shape)) // 128
        shp2 = _as2d(w[n]).shape
        res[n] = _adamw("adamw_" + n, got_small[:, off:off + r].reshape((N_DEV,) + shp2), _as2d(w[n]), _as2d(m[n]), _as2d(v[n]))
        off += r
    outs = [loss, grad_x]
    for k in range(4):
        outs += [res[n][k].reshape(w[n].shape) for n in WEIGHTS]
    return tuple(outs)
```

```python
import numpy as np
import jax
import jax.numpy as jnp
from jax import lax
from jax.experimental import pallas as pl
from jax.experimental.pallas import tpu as pltpu

F32, BF16 = jnp.float32, jnp.bfloat16
MESH = pl.DeviceIdType.MESH
N_DEV = 8

D = 1024
DEPTH = 4
CHUNK = 64
EPS = 1e-6
NEG_INF = -1e30
HEADS, QK_NOPE, QK_ROPE, V_HEAD = 8, 64, 32, 64
Q_LORA, KV_LORA = 256, 128
ROPE_THETA = 10000.0
POOL_WINDOWS = (2, 4, 8, 16)
GROUP = 64
BR = 256
SGU_BLOCK = 128
D_FF = 2816
IN_SIZES = (256, 128, 32, 256, 256, 256, 256, 256, 256, 4096)
HP = 128
QW = HEADS * HP
C_CQ, C_CKV, C_KR, C_PIN, C_SU, C_SV, C_CVB, C_CVC, C_CVX, C_GATE = 0, 256, 384, 512, 768, 1024, 1280, 1536, 1792, 2048
W1_COLS = 6144

ADAM_LR, ADAM_B1, ADAM_B2, ADAM_EPS, ADAM_WD, ADAM_STEP = 0.001, 0.9, 0.999, 1e-08, 0.01, 10

WEIGHTS = ['w_in', 'g_pre_mix', 'g_cq', 'g_ckv', 'w_uq', 'w_ukv', 'pool_w', 'pool_scale', 'g_sgu_v', 'sgu_w', 'sgu_b',
           'conv_w', 'w_br_a', 'w_br_b', 'w_br_c', 'w_br_d', 'w_out', 'g_post_mix', 'g_pre_ffn', 'w_ffn_gate',
           'w_ffn_up', 'w_ffn_down', 'g_post_ffn']
SHARDED = {'w_in': 2, 'w_uq': 2, 'w_ukv': 2, 'conv_w': 3, 'w_br_a': 2, 'w_br_b': 2, 'w_br_c': 2, 'w_br_d': 2,
           'w_out': 1, 'w_ffn_gate': 2, 'w_ffn_up': 2, 'w_ffn_down': 1}
REPLICATED = [n for n in WEIGHTS if n not in SHARDED]

VMEM_LIMIT = 56 * 1024 * 1024
TQ = 256


def _cparams(sem=None):
    return pltpu.CompilerParams(vmem_limit_bytes=VMEM_LIMIT, dimension_semantics=sem)


def _tile(n, cap, align=128):
    if n <= cap:
        return n
    for t in range(cap - cap % align, 0, -align):
        if n % t == 0:
            return t
    return n


def win(arr, width, idx):
    return ('win', arr, width, idx)


def tab(arr):
    return ('tab', arr)


def _rw(name, fn, ins, consts, outs, accs=(), tile=256):
    first = ins[0][1] if isinstance(ins[0], tuple) else ins[0]
    T = first.shape[0]
    tile = min(tile, T)
    grid = (T // tile,)
    arrays, in_specs = [], []
    for x in ins:
        if isinstance(x, tuple) and x[0] == 'win':
            _, a, w, k = x
            arrays.append(a)
            in_specs.append(pl.BlockSpec((tile, w), lambda i, k=k: (i, k)))
        elif isinstance(x, tuple) and x[0] == 'tab':
            a = x[1]
            nb = a.shape[0] // tile
            arrays.append(a)
            in_specs.append(pl.BlockSpec((tile, a.shape[1]), lambda i, nb=nb: (i % nb, 0)))
        else:
            arrays.append(x)
            in_specs.append(pl.BlockSpec((tile, x.shape[1]), lambda i: (i, 0)))
    for c in consts:
        arrays.append(c)
        in_specs.append(pl.BlockSpec(c.shape, lambda i, n=c.ndim: (0,) * n))
    out_shape = [jax.ShapeDtypeStruct((T, f), dt) for f, dt in outs]
    out_specs = [pl.BlockSpec((tile, f), lambda i: (i, 0)) for f, _ in outs]
    for shp in accs:
        out_shape.append(jax.ShapeDtypeStruct(shp, F32))
        out_specs.append(pl.BlockSpec(shp, lambda i, n=len(shp): (0,) * n))
    ni, nc, no = len(ins), len(consts), len(outs)

    def body(*refs):
        vals = fn(*[r[...] for r in refs[:ni + nc]])
        if not isinstance(vals, (tuple, list)):
            vals = (vals,)
        o_refs = refs[ni + nc:]
        for r, v in zip(o_refs[:no], vals[:no]):
            r[...] = v.astype(r.dtype)
        i = pl.program_id(0)
        for r, v in zip(o_refs[no:], vals[no:]):
            @pl.when(i == 0)
            def _(r=r, v=v):
                r[...] = v

            @pl.when(i > 0)
            def _(r=r, v=v):
                r[...] += v

    res = pl.pallas_call(body, name=name, grid=grid, in_specs=in_specs, out_specs=out_specs, out_shape=out_shape,
                         compiler_params=_cparams(("arbitrary",)))(*arrays)
    return res


def _mm(name, a, b, tb=False, out_dtype=F32, tm=512, tn=3072, a2=None, b2=None):
    M, K = a.shape
    N = b.shape[0] if tb else b.shape[1]
    tm, tn = _tile(M, tm, 8), _tile(N, tn)
    dims = (((1,), (1,)), ((), ())) if tb else (((1,), (0,)), ((), ()))
    pairs = [(a, b)] + ([(a2, b2)] if a2 is not None else [])

    def body(*refs):
        o_ref = refs[-1]
        acc = None
        for i in range(len(pairs)):
            p = lax.dot_general(refs[2 * i][...].astype(BF16), refs[2 * i + 1][...].astype(BF16), dims,
                                preferred_element_type=F32)
            acc = p if acc is None else acc + p
        o_ref[...] = acc.astype(o_ref.dtype)

    in_specs, arrays = [], []
    for x, y in pairs:
        k = x.shape[1]
        in_specs.append(pl.BlockSpec((tm, k), lambda j, i: (i, 0)))
        in_specs.append(pl.BlockSpec((tn, k), lambda j, i: (j, 0)) if tb else pl.BlockSpec((k, tn), lambda j, i: (0, j)))
        arrays += [x, y]
    return pl.pallas_call(
        body, name=name, grid=(N // tn, M // tm), in_specs=in_specs,
        out_specs=pl.BlockSpec((tm, tn), lambda j, i: (i, j)),
        out_shape=jax.ShapeDtypeStruct((M, N), out_dtype),
        compiler_params=_cparams(("parallel", "parallel")))(*arrays)


def _mm_t(name, a, g, tk=1408, tn=1408, tt=2048):
    T, K = a.shape
    N = g.shape[1]
    tk, tn, tt = _tile(K, tk), _tile(N, tn), _tile(T, tt, 8)

    def body(a_ref, g_ref, o_ref):
        p = lax.dot_general(a_ref[...].astype(BF16), g_ref[...].astype(BF16), (((0,), (0,)), ((), ())),
                            preferred_element_type=F32)
        t = pl.program_id(2)

        @pl.when(t == 0)
        def _():
            o_ref[...] = p

        @pl.when(t > 0)
        def _():
            o_ref[...] += p

    return pl.pallas_call(
        body, name=name, grid=(K // tk, N // tn, T // tt),
        in_specs=[pl.BlockSpec((tt, tk), lambda i, j, t: (t, i)), pl.BlockSpec((tt, tn), lambda i, j, t: (t, j))],
        out_specs=pl.BlockSpec((tk, tn), lambda i, j, t: (i, j)),
        out_shape=jax.ShapeDtypeStruct((K, N), F32),
        compiler_params=_cparams(("parallel", "parallel", "arbitrary")))(a, g)


def _dot(a, b):
    return jnp.dot(a.astype(BF16), b.astype(BF16), preferred_element_type=F32)


def _dot_nt(a, b):
    return lax.dot_general(a.astype(BF16), b.astype(BF16), (((1,), (1,)), ((), ())), preferred_element_type=F32)


def _dot_tn(a, b):
    return lax.dot_general(a.astype(BF16), b.astype(BF16), (((0,), (0,)), ((), ())), preferred_element_type=F32)


def _rms(x, g):
    r = lax.rsqrt(jnp.mean(x * x, axis=-1, keepdims=True) + EPS)
    return x * r * g


def _rms_bwd(x, g, dy):
    r = lax.rsqrt(jnp.mean(x * x, axis=-1, keepdims=True) + EPS)
    xh = x * r
    dxh = dy * g
    dx = r * (dxh - xh * jnp.mean(dxh * xh, axis=-1, keepdims=True))
    return dx, jnp.sum(dy * xh, axis=0, keepdims=True)


def _sigmoid(x):
    return 1.0 / (1.0 + jnp.exp(-x))


def _shift_down(x, k):
    n = x.shape[0]
    t = lax.broadcasted_iota(jnp.int32, x.shape, 0)
    return jnp.where(t >= k, pltpu.roll(x, k, 0), 0.0)


def _shift_up(x, k):
    n = x.shape[0]
    t = lax.broadcasted_iota(jnp.int32, x.shape, 0)
    return jnp.where(t < n - k, pltpu.roll(x, n - k, 0), 0.0)


def _group_select(vals):
    lane = lax.broadcasted_iota(jnp.int32, vals[0].shape, 1) // GROUP
    out = vals[-1]
    for g in range(len(vals) - 2, -1, -1):
        out = jnp.where(lane == g, vals[g], out)
    return out


def _pool_count(shape):
    t = lax.broadcasted_iota(jnp.int32, shape, 0)
    lane = lax.broadcasted_iota(jnp.int32, shape, 1) // GROUP
    w = jnp.where(lane == 0, POOL_WINDOWS[0], jnp.where(lane == 1, POOL_WINDOWS[1],
                  jnp.where(lane == 2, POOL_WINDOWS[2], POOL_WINDOWS[3])))
    return jnp.minimum(t + 1, w).astype(F32)


def _pooled(z):
    s = [z]
    for k in (1, 2, 4, 8):
        s.append(s[-1] + _shift_down(s[-1], k))
    return _group_select(s[1:]) / _pool_count(z.shape) - z


def _pooled_bwd(dp):
    u = dp / _pool_count(dp.shape)
    s = [u]
    for k in (1, 2, 4, 8):
        s.append(s[-1] + _shift_up(s[-1], k))
    return _group_select(s[1:]) - dp


def _diag_mask():
    r = lax.broadcasted_iota(jnp.int32, (TQ, TQ), 0) // CHUNK
    c = lax.broadcasted_iota(jnp.int32, (TQ, TQ), 1) // CHUNK
    return r >= c


def _attn_fwd(q, k, v, B, S):
    nq = S // TQ

    def body(q_ref, k_ref, v_ref, o_ref, lse_ref):
        mask = _diag_mask()
        for i in range(nq):
            r0, r1 = i * TQ, (i + 1) * TQ
            qb = q_ref[r0:r1, :]
            sd = jnp.where(mask, _dot_nt(qb, k_ref[r0:r1, :]), NEG_INF)
            m = jnp.max(sd, axis=-1, keepdims=True)
            if i:
                so = _dot_nt(qb, k_ref[0:r0, :])
                m = jnp.maximum(m, jnp.max(so, axis=-1, keepdims=True))
            pd = jnp.exp(sd - m)
            l = jnp.sum(pd, axis=-1, keepdims=True)
            acc = _dot(pd, v_ref[r0:r1, :])
            if i:
                po = jnp.exp(so - m)
                l = l + jnp.sum(po, axis=-1, keepdims=True)
                acc = acc + _dot(po, v_ref[0:r0, :])
            o_ref[r0:r1, :] = acc / l
            lse_ref[r0:r1, :] = jnp.broadcast_to(m + jnp.log(l), (TQ, HP))

    spec = pl.BlockSpec((S, HP), lambda b, h: (b, h))
    return pl.pallas_call(
        body, name="attn_fwd", grid=(B, HEADS), in_specs=[spec] * 3, out_specs=[spec] * 2,
        out_shape=[jax.ShapeDtypeStruct((B * S, QW), F32)] * 2,
        compiler_params=_cparams(("parallel", "parallel")))(q, k, v)


def _attn_bwd(q, k, v, o, do, lse, B, S):
    nq = S // TQ

    def body(q_ref, k_ref, v_ref, o_ref, do_ref, lse_ref, dq_ref, dk_ref, dv_ref, dk_acc, dv_acc):
        mask = _diag_mask()
        dk_acc[...] = jnp.zeros_like(dk_acc)
        dv_acc[...] = jnp.zeros_like(dv_acc)
        for i in range(nq):
            r0, r1 = i * TQ, (i + 1) * TQ
            qb, dob = q_ref[r0:r1, :], do_ref[r0:r1, :]
            delta = jnp.sum(o_ref[r0:r1, :] * dob, axis=-1, keepdims=True)
            lse_b = lse_ref[r0:r1, :][:, :1]
            dq = None
            for c0, c1, diag in ([(0, r0, False)] if i else []) + [(r0, r1, True)]:
                kb, vb = k_ref[c0:c1, :], v_ref[c0:c1, :]
                p = jnp.exp(_dot_nt(qb, kb) - lse_b)
                if diag:
                    p = jnp.where(mask, p, 0.0)
                ds = p * (_dot_nt(dob, vb) - delta)
                part = _dot(ds, kb)
                dq = part if dq is None else dq + part
                dk_acc[c0:c1, :] += _dot_tn(ds, qb)
                dv_acc[c0:c1, :] += _dot_tn(p, dob)
            dq_ref[r0:r1, :] = dq
        dk_ref[...] = dk_acc[...].astype(BF16)
        dv_ref[...] = dv_acc[...].astype(BF16)

    spec = pl.BlockSpec((S, HP), lambda b, h: (b, h))
    return pl.pallas_call(
        body, name="attn_bwd", grid=(B, HEADS), in_specs=[spec] * 6, out_specs=[spec] * 3,
        out_shape=[jax.ShapeDtypeStruct((B * S, QW), F32)] + [jax.ShapeDtypeStruct((B * S, QW), BF16)] * 2,
        scratch_shapes=[pltpu.VMEM((S, HP), F32)] * 2,
        compiler_params=_cparams(("parallel", "parallel")))(q, k, v, o, do, lse)


def _all_gather(name, xs):
    n = len(xs)

    def body(*refs):
        x_refs, out_refs = refs[:n], refs[n:2 * n]
        send_sems, recv_sems, local_sems = refs[2 * n:]
        x_, y_, c_ = lax.axis_index("x"), lax.axis_index("y"), lax.axis_index("c")
        me, sibling = (x_, y_, c_), (x_, y_, 1 - c_)
        chips = [(1 - x_, y_), (x_, 1 - y_), (1 - x_, 1 - y_)]

        def copy(a, k, block, to, own=False):
            slot = out_refs[a].at[4 * block[0] + 2 * block[1] + block[2]]
            return pltpu.make_async_remote_copy(
                src_ref=x_refs[a] if own else slot, dst_ref=slot,
                send_sem=send_sems.at[7 * a + k], recv_sem=recv_sems.at[7 * a + k], device_id=to, device_id_type=MESH)

        mine = [pltpu.make_async_copy(x_refs[a], out_refs[a].at[4 * x_ + 2 * y_ + c_], local_sems.at[a]) for a in range(n)]
        for cp in mine:
            cp.start()
        first = [copy(a, 1 + j, me, (*chip, c_), own=True) for j, chip in enumerate(chips) for a in range(n)]
        first += [copy(a, 0, me, sibling, own=True) for a in range(n)]
        for cp in first:
            cp.start()
        passed = []
        for j, chip in enumerate(chips):
            for a in range(n):
                copy(a, 1 + j, (*chip, c_), me).wait_recv()
                passed.append(copy(a, 4 + j, (*chip, c_), sibling))
                passed[-1].start()
        for a in range(n):
            copy(a, 0, sibling, me).wait_recv()
            for j, chip in enumerate(chips):
                copy(a, 4 + j, (*chip, 1 - c_), me).wait_recv()
        for cp in first + passed:
            cp.wait_send()
        for cp in mine:
            cp.wait()

    return pl.pallas_call(
        body, name=name, out_shape=[jax.ShapeDtypeStruct((N_DEV,) + x.shape, x.dtype) for x in xs],
        in_specs=[pl.BlockSpec(memory_space=pl.ANY)] * n, out_specs=[pl.BlockSpec(memory_space=pl.ANY)] * n,
        scratch_shapes=[pltpu.SemaphoreType.DMA((7 * n,)), pltpu.SemaphoreType.DMA((7 * n,)), pltpu.SemaphoreType.DMA((n,))],
    )(*xs)


def _all_to_all(name, xs):
    n = len(xs)

    def body(*refs):
        x_refs, out_refs = refs[:n], refs[n:2 * n]
        send_sems, recv_sems, local_sems = refs[2 * n:]
        x_, y_, c_ = lax.axis_index("x"), lax.axis_index("y"), lax.axis_index("c")
        me = 4 * x_ + 2 * y_ + c_
        mine = [pltpu.make_async_copy(x_refs[a].at[me], out_refs[a].at[me], local_sems.at[a]) for a in range(n)]
        for cp in mine:
            cp.start()
        sends, recvs = [], []
        for k in range(1, N_DEV):
            px, py, pc = (x_ + (k >> 2)) % 2, (y_ + ((k >> 1) & 1)) % 2, (c_ + (k & 1)) % 2
            peer = 4 * px + 2 * py + pc
            for a in range(n):
                sem = dict(send_sem=send_sems.at[7 * a + k - 1], recv_sem=recv_sems.at[7 * a + k - 1],
                           device_id=(px, py, pc), device_id_type=MESH)
                sends.append(pltpu.make_async_remote_copy(src_ref=x_refs[a].at[peer], dst_ref=out_refs[a].at[me], **sem))
                recvs.append(pltpu.make_async_remote_copy(src_ref=x_refs[a].at[me], dst_ref=out_refs[a].at[peer], **sem))
        for cp in sends:
            cp.start()
        for cp in recvs:
            cp.wait_recv()
        for cp in sends:
            cp.wait_send()
        for cp in mine:
            cp.wait()

    return pl.pallas_call(
        body, name=name, out_shape=[jax.ShapeDtypeStruct(x.shape, x.dtype) for x in xs],
        in_specs=[pl.BlockSpec(memory_space=pl.ANY)] * n, out_specs=[pl.BlockSpec(memory_space=pl.ANY)] * n,
        scratch_shapes=[pltpu.SemaphoreType.DMA((7 * n,)), pltpu.SemaphoreType.DMA((7 * n,)), pltpu.SemaphoreType.DMA((n,))],
    )(*xs)


def _adamw(name, parts, w, m, v, row_off=0):
    R, C = w.shape
    assert parts.shape[2] == C and parts.shape[1] >= row_off + R
    tr = max(t for t in range(8, 513, 8) if R % t == 0 and row_off % t == 0) if R % 8 == 0 else R
    assert row_off % tr == 0
    ob = row_off // tr

    def body(p_ref, w_ref, m_ref, v_ref, g_ref, d_ref, nm_ref, nv_ref):
        g = p_ref[0].astype(F32)
        for j in range(1, N_DEV):
            g = g + p_ref[j].astype(F32)
        m_new = ADAM_B1 * m_ref[...] + (1.0 - ADAM_B1) * g
        v_new = ADAM_B2 * v_ref[...] + (1.0 - ADAM_B2) * (g * g)
        m_hat = m_new / (1.0 - ADAM_B1 ** ADAM_STEP)
        v_hat = v_new / (1.0 - ADAM_B2 ** ADAM_STEP)
        g_ref[...] = g
        d_ref[...] = -ADAM_LR * (m_hat / (jnp.sqrt(v_hat) + ADAM_EPS) + ADAM_WD * w_ref[...])
        nm_ref[...] = m_new
        nv_ref[...] = v_new

    spec = pl.BlockSpec((tr, C), lambda i: (i, 0))
    return pl.pallas_call(
        body, name=name, grid=(R // tr,), in_specs=[pl.BlockSpec((N_DEV, tr, C), lambda i: (0, i + ob, 0)), spec, spec, spec],
        out_specs=[spec] * 4, out_shape=[jax.ShapeDtypeStruct((R, C), F32)] * 4,
        compiler_params=_cparams(("parallel",)))(parts, w, m, v)


def _rot_cols(w):
    h = w.shape[-1] // 2
    return jnp.concatenate([-w[..., h:], w[..., :h]], axis=-1)


def _unrot_cols(d):
    h = d.shape[-1] // 2
    return jnp.concatenate([d[..., h:], -d[..., :h]], axis=-1)


def _split_in(w):
    out, o = [], 0
    for n in IN_SIZES:
        out.append(w[..., o:o + n])
        o += n
    return out


def _layer_weights(G):
    L = DEPTH
    cq, ckv, kr, pin, su, sv, cvb, cvc, cvx, gate = _split_in(G['w_in'])
    W1 = jnp.concatenate([cq, ckv, kr, _rot_cols(kr), jnp.zeros((L, D, 64), BF16), pin, su, sv, cvb, cvc, cvx, gate], axis=-1)
    wq = G['w_uq'].reshape(L, Q_LORA, HEADS, QK_NOPE + QK_ROPE)
    nope, rope = wq[..., :QK_NOPE], wq[..., QK_NOPE:]
    z32 = jnp.zeros((L, Q_LORA, HEADS, HP - QK_NOPE - QK_ROPE), BF16)
    wq_a = jnp.concatenate([nope, rope, z32], axis=-1).reshape(L, Q_LORA, QW)
    wq_b = jnp.concatenate([jnp.zeros_like(nope), _rot_cols(rope), z32], axis=-1).reshape(L, Q_LORA, QW)
    wkv = G['w_ukv'].reshape(L, KV_LORA, HEADS, QK_NOPE + V_HEAD)
    z64 = jnp.zeros((L, KV_LORA, HEADS, HP - QK_NOPE), BF16)
    wkn = jnp.concatenate([wkv[..., :QK_NOPE], z64], axis=-1).reshape(L, KV_LORA, QW)
    wv = jnp.concatenate([wkv[..., QK_NOPE:], z64], axis=-1).reshape(L, KV_LORA, QW)
    wa = G['w_br_a'].reshape(L, HEADS, V_HEAD, D)
    wa = jnp.concatenate([wa, jnp.zeros((L, HEADS, HP - V_HEAD, D), BF16)], axis=2).reshape(L, QW, D)
    return dict(
        W1=W1, W1s=W1[..., :C_GATE], W1g=W1[..., C_GATE:], Wq=jnp.concatenate([wq_a, wq_b], axis=-1), Wkv=jnp.concatenate([wkn, wv], axis=-1), Wa=wa,
        Wb=G['w_br_b'], Wc=G['w_br_c'], Wd=G['w_br_d'], Wout=G['w_out'],
        Wgu=jnp.concatenate([G['w_ffn_gate'], G['w_ffn_up']], axis=-1), Wdn=G['w_ffn_down'],
        conv=G['conv_w'].reshape(L, 3, BR).astype(F32))


def _tables(S):
    inv = ROPE_THETA ** (-jnp.arange(0, QK_ROPE, 2, dtype=F32) / QK_ROPE)
    ang = jnp.arange(S, dtype=F32)[:, None] * inv[None, :]
    cos, sin = jnp.cos(ang), jnp.sin(ang)
    scale = (QK_NOPE + QK_ROPE) ** -0.5
    one, zero = jnp.ones((S, QK_NOPE), F32), jnp.zeros((S, QK_NOPE), F32)
    z32 = jnp.zeros((S, HP - QK_NOPE - QK_ROPE), F32)
    cq = jnp.tile(jnp.concatenate([one, cos, cos, z32], axis=1) * scale, (1, HEADS))
    sq = jnp.tile(jnp.concatenate([zero, sin, sin, z32], axis=1) * scale, (1, HEADS))
    ck = jnp.concatenate([cos, cos, sin, sin, jnp.zeros((S, HP - 2 * QK_ROPE), F32)], axis=1)
    return cq, sq, ck


def _place_k():
    p = np.zeros((HP, QW), np.float32)
    for r in range(2 * QK_ROPE):
        for h in range(HEADS):
            p[r, h * HP + QK_NOPE + r % QK_ROPE] = 1.0
    return jnp.asarray(p, BF16)


def _layer_fwd(x, W, P, tabs, B, S):
    cq_t, sq_t, ck_t, place = tabs
    g1, g_cq, g_ckv = P['g_pre_mix'], P['g_cq'], P['g_ckv']
    h = _rw("pre_mix", lambda x, g: _rms(x, g), [x], [g1], [(D, BF16)], tile=512)[0]
    P1 = _mm("in_proj", h, W['W1'])

    def q_prep(c, cq, sq, g, w):
        qq = _dot(_rms(c, g), w)
        return qq[:, :QW] * cq + qq[:, QW:] * sq
    q = _rw("q_prep", q_prep, [win(P1, 256, 0), tab(cq_t), tab(sq_t)], [g_cq, W['Wq']], [(QW, BF16)])[0]

    def kv_prep(c, seg, ck, g, w, place):
        kv = _dot(_rms(c, g), w)
        return kv[:, :QW] + _dot(seg * ck, place), kv[:, QW:]
    k, v = _rw("kv_prep", kv_prep, [win(P1, 128, 2), win(P1, 128, 3), tab(ck_t)], [g_ckv, W['Wkv'], place],
               [(QW, BF16), (QW, BF16)])
    o, lse = _attn_fwd(q, k, v, B, S)

    def pool_f(z, wbd, scale):
        return _dot(_pooled(z), wbd) * scale
    bo = _rw("pool_fwd", pool_f, [win(P1, BR, 2)], [P['pool_bd'], P['pool_scale']], [(BR, BF16)], tile=S)[0]

    def sgu_f(u, sv, g, wm, bias):
        return u * _sgu_mix(_rms(sv, g), wm, bias)
    co = _rw("sgu_fwd", sgu_f, [win(P1, BR, 3), win(P1, BR, 4)], [P['g_sgu_v'], P['sgu_wm'], P['sgu_bias']],
             [(BR, BF16)], tile=SGU_BLOCK)[0]

    def conv_f(b, c, xi, w):
        z = c * xi
        return b * (w[0:1] * _shift_down(z, 2) + w[1:2] * _shift_down(z, 1) + w[2:3] * z)
    do_ = _rw("conv_fwd", conv_f, [win(P1, BR, 5), win(P1, BR, 6), win(P1, BR, 7)], [W['conv']], [(BR, BF16)], tile=S)[0]

    def merge_f(a, b, c, d, l0, l1, l2, l3, wa, wb, wc, wd):
        return (_sigmoid(l0) * _dot(a, wa) + _sigmoid(l1) * _dot(b, wb) + _sigmoid(l2) * _dot(c, wc)
                + _sigmoid(l3) * _dot(d, wd))
    merged = _rw("merge_fwd", merge_f, [o, bo, co, do_] + [win(P1, D, 2 + i) for i in range(4)],
                 [W['Wa'], W['Wb'], W['Wc'], W['Wd']], [(D, BF16)])[0]
    om = _mm("out_proj", merged, W['Wout'])
    x1 = _rw("post_mix", lambda x, o, g: x + _rms(o, g), [x, om], [P['g_post_mix']], [(D, F32)], tile=512)[0]
    h2 = _rw("pre_ffn", lambda x, g: _rms(x, g), [x1], [P['g_pre_ffn']], [(D, BF16)], tile=512)[0]
    ab = _mm("ffn_up", h2, W['Wgu'])

    def act_f(a, b):
        return a * _sigmoid(a) * b
    act = _rw("ffn_act", act_f, [win(ab, D_FF, 0), win(ab, D_FF, 1)], [], [(D_FF, BF16)])[0]
    f = _mm("ffn_down", act, W['Wdn'])
    x2 = _rw("post_ffn", lambda x, f, g: x + _rms(f, g), [x1, f], [P['g_post_ffn']], [(D, F32)], tile=512)[0]
    saved = dict(x=x, h=h, P1=P1, q=q, k=k, v=v, o=o, lse=lse, bo=bo, co=co, do=do_, merged=merged, om=om, x1=x1,
                 h2=h2, ab=ab, act=act, f=f)
    return x2, saved


def _sgu_mix(vn, wm, bias):
    outs = [_dot(wm[g], vn) for g in range(4)]
    return _group_select(outs) + bias


def _layer_bwd(dx2, sv, W, P, tabs, B, S):
    cq_t, sq_t, ck_t, place = tabs
    P1 = sv['P1']
    grads = {}

    def post_ffn_b(f, dy, g):
        return _rms_bwd(f, g, dy)
    df, grads['g_post_ffn'] = _rw("post_ffn_bwd", post_ffn_b, [sv['f'], dx2], [P['g_post_ffn']], [(D, BF16)], [(1, D)], tile=512)
    dact = _mm("ffn_down_dx", df, W['Wdn'], tb=True)
    grads['Wdn'] = _mm_t("ffn_down_dw", sv['act'], df)

    def act_b(a, b, d):
        s = _sigmoid(a)
        return jnp.concatenate([(d * b * (s + a * s * (1.0 - s))).astype(BF16), (d * a * s).astype(BF16)], axis=1)
    dab = _rw("ffn_act_bwd", act_b, [win(sv['ab'], D_FF, 0), win(sv['ab'], D_FF, 1), dact], [], [(2 * D_FF, BF16)])[0]
    dh2 = _mm("ffn_up_dx", dab, W['Wgu'], tb=True)
    grads['Wgu'] = _mm_t("ffn_up_dw", sv['h2'], dab)

    def pre_ffn_b(x, dh, dy, g):
        dx, dg = _rms_bwd(x, g, dh)
        return dy + dx, dg
    dx1, grads['g_pre_ffn'] = _rw("pre_ffn_bwd", pre_ffn_b, [sv['x1'], dh2, dx2], [P['g_pre_ffn']], [(D, F32)], [(1, D)], tile=512)

    def post_mix_b(o, dy, g):
        return _rms_bwd(o, g, dy)
    dom, grads['g_post_mix'] = _rw("post_mix_bwd", post_mix_b, [sv['om'], dx1], [P['g_post_mix']], [(D, BF16)], [(1, D)], tile=512)
    dmerged = _mm("out_proj_dx", dom, W['Wout'], tb=True)
    grads['Wout'] = _mm_t("out_proj_dw", sv['merged'], dom)

    def merge_b(a, b, c, d, l0, l1, l2, l3, dm, wa, wb, wc, wd):
        outs_dl, outs_dy, outs_dbr = [], [], []
        for br, l, w in ((a, l0, wa), (b, l1, wb), (c, l2, wc), (d, l3, wd)):
            s = _sigmoid(l)
            y = _dot(br, w)
            dy = (dm * s).astype(BF16)
            outs_dl.append((dm * y * s * (1.0 - s)).astype(BF16))
            outs_dy.append(dy)
            outs_dbr.append(_dot_nt(dy, w))
        return (jnp.concatenate(outs_dl, axis=1), *outs_dy, *outs_dbr)
    mb = _rw("merge_bwd", merge_b,
             [sv['o'], sv['bo'], sv['co'], sv['do']] + [win(P1, D, 2 + i) for i in range(4)] + [dmerged],
             [W['Wa'], W['Wb'], W['Wc'], W['Wd']],
             [(4 * D, BF16)] + [(D, BF16)] * 4 + [(QW, F32), (BR, F32), (BR, F32), (BR, F32)])
    dl, dy, (d_o, d_bo, d_co, d_do) = mb[0], mb[1:5], mb[5:9]
    grads['Wa'] = _mm_t("br_a_dw", sv['o'], dy[0])
    grads['Wb'] = _mm_t("br_b_dw", sv['bo'], dy[1])
    grads['Wc'] = _mm_t("br_c_dw", sv['co'], dy[2])
    grads['Wd'] = _mm_t("br_d_dw", sv['do'], dy[3])

    def conv_b(b, c, xi, dout, w):
        z = c * xi
        z1, z2 = _shift_down(z, 1), _shift_down(z, 2)
        y = w[0:1] * z2 + w[1:2] * z1 + w[2:3] * z
        dyv = dout * b
        dz = w[2:3] * dyv + w[1:2] * _shift_up(dyv, 1) + w[0:1] * _shift_up(dyv, 2)
        return (dout * y, dz * xi, dz * c, jnp.sum(dyv * z2, axis=0, keepdims=True),
                jnp.sum(dyv * z1, axis=0, keepdims=True), jnp.sum(dyv * z, axis=0, keepdims=True))
    dcvb, dcvc, dcvx, dw0, dw1, dw2 = _rw("conv_bwd", conv_b, [win(P1, BR, 5), win(P1, BR, 6), win(P1, BR, 7), d_do],
                                          [W['conv']], [(BR, BF16)] * 3, [(1, BR)] * 3, tile=S)
    grads['conv'] = jnp.concatenate([dw0, dw1, dw2], axis=0)

    def sgu_b(u, s_v, dout, g, wm, bias):
        vn = _rms(s_v, g)
        mixed = _sgu_mix(vn, wm, bias)
        dmix = dout * u
        dvn = _group_select([_dot_tn(wm[gi], dmix) for gi in range(4)])
        dsv, dg = _rms_bwd(s_v, g, dvn)
        lane = lax.broadcasted_iota(jnp.int32, dmix.shape, 1) // GROUP
        dwm = [_dot_nt(jnp.where(lane == gi, dmix, 0.0), vn) for gi in range(4)]
        return (dout * mixed, dsv, dg, *dwm, dmix)
    dsu, dsv_, grads['g_sgu_v'], wm0, wm1, wm2, wm3, grads['sgu_bias'] = _rw(
        "sgu_bwd", sgu_b, [win(P1, BR, 3), win(P1, BR, 4), d_co], [P['g_sgu_v'], P['sgu_wm'], P['sgu_bias']],
        [(BR, BF16)] * 2, [(1, BR)] + [(SGU_BLOCK, SGU_BLOCK)] * 4 + [(SGU_BLOCK, BR)], tile=SGU_BLOCK)
    grads['sgu_wm'] = jnp.stack([wm0, wm1, wm2, wm3])

    def pool_b(z, dout, wbd, scale):
        pooled = _pooled(z)
        mixed = _dot(pooled, wbd)
        dmix = dout * scale
        dz = _pooled_bwd(_dot_nt(dmix, wbd))
        return dz, _dot_tn(pooled, dmix), jnp.sum(dout * mixed, axis=0, keepdims=True)
    dpin, grads['pool_bd'], grads['pool_scale'] = _rw(
        "pool_bwd", pool_b, [win(P1, BR, 2), d_bo], [P['pool_bd'], P['pool_scale']], [(BR, BF16)], [(BR, BR), (1, BR)], tile=S)

    dq, dk, dv = _attn_bwd(sv['q'], sv['k'], sv['v'], sv['o'], d_o, sv['lse'], B, S)

    def q_prep_b(c, dq, cq, sq, g, w):
        dqq = jnp.concatenate([dq * cq, dq * sq], axis=1).astype(BF16)
        dc, dg = _rms_bwd(c, g, _dot_nt(dqq, w))
        return dc, _rms(c, g), dqq, dg
    dcq, cqn, dqq, grads['g_cq'] = _rw("q_prep_bwd", q_prep_b, [win(P1, 256, 0), dq, tab(cq_t), tab(sq_t)],
                                       [P['g_cq'], W['Wq']], [(Q_LORA, BF16), (Q_LORA, BF16), (2 * QW, BF16)], [(1, Q_LORA)])
    grads['Wq'] = _mm_t("uq_dw", cqn, dqq)

    def kv_prep_b(c, dk, dv, ck, g, w, place):
        dkv = jnp.concatenate([dk, dv], axis=1)
        dc, dg = _rms_bwd(c, g, _dot_nt(dkv, w))
        return dc, _dot_nt(dk, place) * ck, _rms(c, g), dkv, dg
    dckv, dseg, kvn, dkv, grads['g_ckv'] = _rw(
        "kv_prep_bwd", kv_prep_b, [win(P1, 128, 2), dk, dv, tab(ck_t)], [P['g_ckv'], W['Wkv'], place],
        [(KV_LORA, BF16), (HP, BF16), (KV_LORA, BF16), (2 * QW, BF16)], [(1, KV_LORA)])
    grads['Wkv'] = _mm_t("ukv_dw", kvn, dkv)

    dps = jnp.concatenate([dcq, dckv, dseg, dpin, dsu, dsv_, dcvb, dcvc, dcvx], axis=1)
    dh = _mm("in_proj_dx", dps, W['W1s'], tb=True, a2=dl, b2=W['W1g'], tm=256)
    grads['W1s'] = _mm_t("in_proj_s_dw", sv['h'], dps)
    grads['W1g'] = _mm_t("in_proj_g_dw", sv['h'], dl)

    def pre_mix_b(x, dh, dy, g):
        dx, dg = _rms_bwd(x, g, dh)
        return dy + dx, dg
    dx, grads['g_pre_mix'] = _rw("pre_mix_bwd", pre_mix_b, [sv['x'], dh, dx1], [P['g_pre_mix']], [(D, F32)], [(1, D)], tile=512)
    return dx, grads


def _param_grads(gl):
    st = {k: jnp.stack([g[k] for g in gl]) for k in gl[0]}
    L = DEPTH
    w1 = st['W1s']
    seg = w1[..., C_KR:C_KR + 128]
    dkr = seg[..., :QK_ROPE] + _unrot_cols(seg[..., QK_ROPE:2 * QK_ROPE])
    out = {}
    out['w_in'] = jnp.concatenate([w1[..., :C_KR], dkr, w1[..., C_PIN:], st['W1g']], axis=-1)
    qa = st['Wq'][..., :QW].reshape(L, Q_LORA, HEADS, HP)
    qb = st['Wq'][..., QW:].reshape(L, Q_LORA, HEADS, HP)
    drope = qa[..., QK_NOPE:QK_NOPE + QK_ROPE] + _unrot_cols(qb[..., QK_NOPE:QK_NOPE + QK_ROPE])
    out['w_uq'] = jnp.concatenate([qa[..., :QK_NOPE], drope], axis=-1).reshape(L, Q_LORA, HEADS * (QK_NOPE + QK_ROPE))
    kn = st['Wkv'][..., :QW].reshape(L, KV_LORA, HEADS, HP)[..., :QK_NOPE]
    vv = st['Wkv'][..., QW:].reshape(L, KV_LORA, HEADS, HP)[..., :V_HEAD]
    out['w_ukv'] = jnp.concatenate([kn, vv], axis=-1).reshape(L, KV_LORA, HEADS * (QK_NOPE + V_HEAD))
    out['w_br_a'] = st['Wa'].reshape(L, HEADS, HP, D)[:, :, :V_HEAD].reshape(L, HEADS * V_HEAD, D)
    out['w_br_b'], out['w_br_c'], out['w_br_d'], out['w_out'] = st['Wb'], st['Wc'], st['Wd'], st['Wout']
    out['w_ffn_gate'], out['w_ffn_up'], out['w_ffn_down'] = st['Wgu'][..., :D_FF], st['Wgu'][..., D_FF:], st['Wdn']
    out['conv_w'] = st['conv'].reshape(L, 3, 1, BR)
    for n in ('g_pre_mix', 'g_cq', 'g_ckv', 'g_sgu_v', 'g_post_mix', 'g_pre_ffn', 'g_post_ffn', 'pool_scale'):
        out[n] = st[n].reshape(L, -1)
    bd = st['pool_bd'].reshape(L, 4, GROUP, 4, GROUP)
    out['pool_w'] = jnp.stack([bd[:, g, :, g, :] for g in range(4)], axis=1)
    out['sgu_w'] = st['sgu_wm'] * _sgu_mask()[None, None]
    out['sgu_b'] = st['sgu_bias'].reshape(L, SGU_BLOCK, 4, GROUP).sum(-1).transpose(0, 2, 1)
    return out


def _sgu_mask():
    pc = np.arange(SGU_BLOCK) // CHUNK
    return jnp.asarray(pc[:, None] >= pc[None, :], F32)


def _rows(a, lead):
    return a.reshape(a.shape[:lead] + (-1, a.shape[-1]))


def _pad_to(a, rows, cols):
    pad = [(0, 0)] * (a.ndim - 2) + [(0, rows - a.shape[-2]), (0, cols - a.shape[-1])]
    return jnp.pad(a, pad)


GROUPS = (('w_in',), ('w_ffn_gate', 'w_ffn_up'), ('w_out', 'w_ffn_down'),
          ('w_uq', 'w_ukv', 'w_br_a', 'w_br_b', 'w_br_c', 'w_br_d', 'conv_w'))
NARROW = {'w_uq': (DEPTH * Q_LORA, 128), 'conv_w': (16, 128)}


def _group(t, lead):
    bufs = []
    for names in GROUPS:
        ps = [_rows(t[n], lead) for n in names]
        ps = [_pad_to(p, *NARROW[n]) if n in NARROW else p for n, p in zip(names, ps)]
        bufs.append(ps[0] if len(ps) == 1 else jnp.concatenate(ps, axis=-2))
    return bufs


def _group_rows(shapes):
    where = {}
    for b, names in enumerate(GROUPS):
        off = 0
        for n in names:
            r, c = NARROW.get(n, shapes[n])
            where[n] = (b, off)
            off += r
    return where


def _shard_split(a, axis):
    n = a.shape[axis]
    a = a.reshape(a.shape[:axis] + (N_DEV, n // N_DEV) + a.shape[axis + 1:])
    return jnp.moveaxis(a, axis, 0)


def _shard_join(a, axis):
    a = jnp.moveaxis(a, 0, axis)
    return a.reshape(a.shape[:axis] + (a.shape[axis] * a.shape[axis + 1],) + a.shape[axis + 2:])


def _as2d(a):
    return a.reshape(-1, a.shape[-1])


def kernel(x, w_in, g_pre_mix, g_cq, g_ckv, w_uq, w_ukv, pool_w, pool_scale, g_sgu_v, sgu_w, sgu_b, conv_w, w_br_a, w_br_b, w_br_c, w_br_d, w_out, g_post_mix, g_pre_ffn, w_ffn_gate, w_ffn_up, w_ffn_down, g_post_ffn, loss_target, m_w_in, m_g_pre_mix, m_g_cq, m_g_ckv, m_w_uq, m_w_ukv, m_pool_w, m_pool_scale, m_g_sgu_v, m_sgu_w, m_sgu_b, m_conv_w, m_w_br_a, m_w_br_b, m_w_br_c, m_w_br_d, m_w_out, m_g_post_mix, m_g_pre_ffn, m_w_ffn_gate, m_w_ffn_up, m_w_ffn_down, m_g_post_ffn, v_w_in, v_g_pre_mix, v_g_cq, v_g_ckv, v_w_uq, v_w_ukv, v_pool_w, v_pool_scale, v_g_sgu_v, v_sgu_w, v_sgu_b, v_conv_w, v_w_br_a, v_w_br_b, v_w_br_c, v_w_br_d, v_w_out, v_g_post_mix, v_g_pre_ffn, v_w_ffn_gate, v_w_ffn_up, v_w_ffn_down, v_g_post_ffn):
    args = dict(locals())
    w = {n: args[n] for n in WEIGHTS}
    m = {n: args['m_' + n] for n in WEIGHTS}
    v = {n: args['v_' + n] for n in WEIGHTS}
    B, S, _ = x.shape
    T = B * S
    L = DEPTH

    names = list(SHARDED)
    shapes2 = {n: _as2d(w[n]).shape for n in names}
    where = _group_rows(shapes2)
    gathered = _all_gather("gather_weights", _group({n: w[n].astype(BF16) for n in names}, 0))
    G = {}
    for n in names:
        b, off = where[n]
        r, c = shapes2[n]
        G[n] = _shard_join(gathered[b][:, off:off + r, :c].reshape((N_DEV,) + w[n].shape), SHARDED[n])
    W = _layer_weights(G)

    eye = jnp.eye(4, dtype=F32)
    pool_bd = (pool_w[:, :, :, None, :] * eye[None, :, None, :, None]).reshape(L, BR, BR).astype(BF16)
    sgu_wm = (sgu_w * _sgu_mask()[None, None]).astype(BF16)
    sgu_bias = jnp.repeat(sgu_b.transpose(0, 2, 1), GROUP, axis=2)
    tabs = _tables(S) + (_place_k(),)

    def layer_params(l):
        P = {n: w[n][l][None, :] for n in ('g_pre_mix', 'g_cq', 'g_ckv', 'g_sgu_v', 'g_post_mix', 'g_pre_ffn',
                                            'g_post_ffn', 'pool_scale')}
        P.update(pool_bd=pool_bd[l], sgu_wm=sgu_wm[l], sgu_bias=sgu_bias[l])
        return P

    xt = x.reshape(T, D)
    saved = []
    for l in range(L):
        xt, sv = _layer_fwd(xt, {k: a[l] for k, a in W.items()}, layer_params(l), tabs, B, S)
        saved.append(sv)

    def loss_f(y, t):
        e = y - t
        return e * (1.0 / D), jnp.sum(e * e, axis=0, keepdims=True)
    dy, sq = _rw("loss", loss_f, [xt, loss_target.reshape(T, D)], [], [(D, F32)], [(1, D)], tile=512)
    loss = lax.psum(0.5 * jnp.sum(sq) / D, ("x", "y", "c"))

    gl = [None] * L
    for l in reversed(range(L)):
        dy, gl[l] = _layer_bwd(dy, saved[l], {k: a[l] for k, a in W.items()}, layer_params(l), tabs, B, S)
    grad_x = dy.reshape(B, S, D)
    pg = _param_grads(gl)

    got = _all_to_all("scatter_grads", _group({n: _shard_split(pg[n], SHARDED[n]).astype(BF16) for n in names}, 1))
    small = jnp.concatenate([pg[n].reshape(-1, 128) for n in REPLICATED], axis=0)
    small = _pad_to(small, -(-small.shape[0] // 8) * 8, 128)
    got_small = _all_gather("gather_small_grads", [small])[0]

    res = {}
    for n in names:
        b, off = where[n]
        r, c = shapes2[n]
        if n in NARROW:
            res[n] = _adamw("adamw_" + n, got[b][:, off:off + r, :c], _as2d(w[n]), _as2d(m[n]), _as2d(v[n]))
        else:
            res[n] = _adamw("adamw_" + n, got[b], _as2d(w[n]), _as2d(m[n]), _as2d(v[n]), row_off=off)
    off = 0
    for n in REPLICATED:
        r = int(np.prod(w[n].shape)) // 128
        shp2 = _as2d(w[n]).shape
        res[n] = _adamw("adamw_" + n, got_small[:, off:off + r].reshape((N_DEV,) + shp2), _as2d(w[n]), _as2d(m[n]), _as2d(v[n]))
        off += r
    outs = [loss, grad_x]
    for k in range(4):
        outs += [res[n][k].reshape(w[n].shape) for n in WEIGHTS]
    return tuple(outs)
```

```python
import functools

import numpy as np
import jax
import jax.numpy as jnp
from jax import lax
from jax.experimental import pallas as pl
from jax.experimental.pallas import tpu as pltpu

F32, BF16 = jnp.float32, jnp.bfloat16
MESH = pl.DeviceIdType.MESH
N_DEV = 8

D = 1024
DEPTH = 4
CHUNK = 64
EPS = 1e-6
NEG_INF = -1e30
HEADS, QK_NOPE, QK_ROPE, V_HEAD = 8, 64, 32, 64
Q_LORA, KV_LORA = 256, 128
ROPE_THETA = 10000.0
POOL_WINDOWS = (2, 4, 8, 16)
GROUP = 64
BR = 256
SGU_BLOCK = 128
D_FF = 2816
IN_SIZES = (256, 128, 32, 256, 256, 256, 256, 256, 256, 4096)
HP = 128
QW = HEADS * HP
C_CQ, C_CKV, C_KR, C_PIN, C_SU, C_SV, C_CVB, C_CVC, C_CVX, C_GATE = 0, 256, 384, 512, 768, 1024, 1280, 1536, 1792, 2048

ADAM_LR, ADAM_B1, ADAM_B2, ADAM_EPS, ADAM_WD, ADAM_STEP = 0.001, 0.9, 0.999, 1e-08, 0.01, 10

WEIGHTS = ['w_in', 'g_pre_mix', 'g_cq', 'g_ckv', 'w_uq', 'w_ukv', 'pool_w', 'pool_scale', 'g_sgu_v', 'sgu_w', 'sgu_b',
           'conv_w', 'w_br_a', 'w_br_b', 'w_br_c', 'w_br_d', 'w_out', 'g_post_mix', 'g_pre_ffn', 'w_ffn_gate',
           'w_ffn_up', 'w_ffn_down', 'g_post_ffn']
SHARDED = {'w_in': 2, 'w_uq': 2, 'w_ukv': 2, 'conv_w': 3, 'w_br_a': 2, 'w_br_b': 2, 'w_br_c': 2, 'w_br_d': 2,
           'w_out': 1, 'w_ffn_gate': 2, 'w_ffn_up': 2, 'w_ffn_down': 1}
REPLICATED = [n for n in WEIGHTS if n not in SHARDED]

VMEM_LIMIT = 56 * 1024 * 1024
TQ = 256


def _cparams(sem=None):
    return pltpu.CompilerParams(vmem_limit_bytes=VMEM_LIMIT, dimension_semantics=sem)


def _tile(n, cap, align=128):
    if n <= cap:
        return n
    for t in range(cap - cap % align, 0, -align):
        if n % t == 0:
            return t
    return n


def _peers():
    x_, y_, c_ = lax.axis_index("x"), lax.axis_index("y"), lax.axis_index("c")
    out = []
    for k in range(1, N_DEV):
        px, py, pc = (x_ + (k >> 2)) % 2, (y_ + ((k >> 1) & 1)) % 2, (c_ + (k & 1)) % 2
        out.append((k, (px, py, pc), 4 * px + 2 * py + pc))
    return 4 * x_ + 2 * y_ + c_, out


class _Exchange:
    def __init__(self, n):
        self.n = n
        self.scratch = [pltpu.SemaphoreType.DMA((7 * n,)), pltpu.SemaphoreType.DMA((7 * n,)),
                        pltpu.SemaphoreType.DMA((n,))]
        self.results = None

    def start(self, cin, cout, sems):
        local, sends, _ = self.copies(cin, cout, sems)
        for cp in local + sends:
            cp.start()

    def wait(self, cin, cout, sems):
        local, sends, recvs = self.copies(cin, cout, sems)
        for cp in recvs:
            cp.wait_recv()
        for cp in sends:
            cp.wait_send()
        for cp in local:
            cp.wait()


class _Gather(_Exchange):
    def __init__(self, xs):
        super().__init__(len(xs))
        self.inputs = list(xs)
        self.out_shape = [jax.ShapeDtypeStruct((N_DEV,) + x.shape, x.dtype) for x in xs]
        self.aliases = {}

    def copies(self, cin, cout, sems):
        send_sems, recv_sems, local_sems = sems
        me, peers = _peers()
        local, sends, recvs = [], [], []
        for a in range(self.n):
            local.append(pltpu.make_async_copy(cin[a], cout[a].at[me], local_sems.at[a]))
            for k, dev, flat in peers:
                sem = dict(send_sem=send_sems.at[7 * a + k - 1], recv_sem=recv_sems.at[7 * a + k - 1],
                           device_id=dev, device_id_type=MESH)
                sends.append(pltpu.make_async_remote_copy(src_ref=cin[a], dst_ref=cout[a].at[me], **sem))
                recvs.append(pltpu.make_async_remote_copy(src_ref=cin[a], dst_ref=cout[a].at[flat], **sem))
        return local, sends, recvs


class _Scatter(_Exchange):
    def __init__(self, xs, recv, row0):
        super().__init__(len(xs))
        self.inputs = list(xs) + list(recv)
        self.out_shape = [jax.ShapeDtypeStruct(r.shape, r.dtype) for r in recv]
        self.aliases = {self.n + a: a for a in range(self.n)}
        self.row0 = list(row0)

    def copies(self, cin, cout, sems):
        send_sems, recv_sems, local_sems = sems
        me, peers = _peers()
        local, sends, recvs = [], [], []
        for a in range(self.n):
            rows = pl.ds(self.row0[a], self.inputs[a].shape[1])
            local.append(pltpu.make_async_copy(cin[a].at[me], cout[a].at[me, rows], local_sems.at[a]))
            for k, dev, flat in peers:
                sem = dict(send_sem=send_sems.at[7 * a + k - 1], recv_sem=recv_sems.at[7 * a + k - 1],
                           device_id=dev, device_id_type=MESH)
                sends.append(pltpu.make_async_remote_copy(src_ref=cin[a].at[flat], dst_ref=cout[a].at[me, rows], **sem))
                recvs.append(pltpu.make_async_remote_copy(src_ref=cin[a].at[me], dst_ref=cout[a].at[flat, rows], **sem))
        return local, sends, recvs


def _call(name, body, grid, in_specs, out_specs, out_shape, arrays, scratch=(), sem=None, comm=None):
    if comm is None:
        res = pl.pallas_call(body, name=name, grid=grid, in_specs=list(in_specs), out_specs=list(out_specs),
                             out_shape=list(out_shape), scratch_shapes=list(scratch), compiler_params=_cparams(sem))(*arrays)
        return list(res)
    ni, no, ns = len(in_specs), len(out_specs), len(scratch)
    nci, nco = len(comm.inputs), len(comm.out_shape)
    hbm = pl.BlockSpec(memory_space=pl.ANY)

    def wrapped(*refs):
        ins, cin = refs[:ni], refs[ni:ni + nci]
        o0 = ni + nci
        outs, cout = refs[o0:o0 + no], refs[o0 + no:o0 + no + nco]
        s0 = o0 + no + nco
        scr, sems = refs[s0:s0 + ns], refs[s0 + ns:]
        ids = [pl.program_id(d) for d in range(len(grid))]
        first = functools.reduce(jnp.logical_and, [i == 0 for i in ids])
        last = functools.reduce(jnp.logical_and, [i == g - 1 for i, g in zip(ids, grid)])

        @pl.when(first)
        def _():
            comm.start(cin, cout, sems)

        body(*ins, *outs, *scr)

        @pl.when(last)
        def _():
            comm.wait(cin, cout, sems)

    res = pl.pallas_call(
        wrapped, name=name, grid=grid, in_specs=list(in_specs) + [hbm] * nci, out_specs=list(out_specs) + [hbm] * nco,
        out_shape=list(out_shape) + list(comm.out_shape), scratch_shapes=list(scratch) + comm.scratch,
        input_output_aliases={ni + i: no + j for i, j in comm.aliases.items()},
        compiler_params=_cparams(("arbitrary",) * len(grid)))(*arrays, *comm.inputs)
    comm.results = list(res[no:])
    return list(res[:no])


def win(arr, width, idx):
    return ('win', arr, width, idx)


def tab(arr):
    return ('tab', arr)


def _rw(name, fn, ins, consts, outs, accs=(), tile=256, comm=None):
    first = ins[0][1] if isinstance(ins[0], tuple) else ins[0]
    T = first.shape[0]
    tile = min(tile, T)
    grid = (T // tile,)
    arrays, in_specs = [], []
    for x in ins:
        if isinstance(x, tuple) and x[0] == 'win':
            _, a, w, k = x
            arrays.append(a)
            in_specs.append(pl.BlockSpec((tile, w), lambda i, k=k: (i, k)))
        elif isinstance(x, tuple) and x[0] == 'tab':
            a = x[1]
            nb = a.shape[0] // tile
            arrays.append(a)
            in_specs.append(pl.BlockSpec((tile, a.shape[1]), lambda i, nb=nb: (i % nb, 0)))
        else:
            arrays.append(x)
            in_specs.append(pl.BlockSpec((tile, x.shape[1]), lambda i: (i, 0)))
    for c in consts:
        arrays.append(c)
        in_specs.append(pl.BlockSpec(c.shape, lambda i, n=c.ndim: (0,) * n))
    out_shape = [jax.ShapeDtypeStruct((T, f), dt) for f, dt in outs]
    out_specs = [pl.BlockSpec((tile, f), lambda i: (i, 0)) for f, _ in outs]
    for shp in accs:
        out_shape.append(jax.ShapeDtypeStruct(shp, F32))
        out_specs.append(pl.BlockSpec(shp, lambda i, n=len(shp): (0,) * n))
    ni, nc, no = len(ins), len(consts), len(outs)

    def body(*refs):
        vals_in = [r[...] for r in refs[:ni]]
        vals_in = [v.astype(F32) if v.dtype == BF16 else v for v in vals_in]
        vals = fn(*vals_in, *[r[...] for r in refs[ni:ni + nc]])
        if not isinstance(vals, (tuple, list)):
            vals = (vals,)
        o_refs = refs[ni + nc:]
        for r, v in zip(o_refs[:no], vals[:no]):
            r[...] = v.astype(r.dtype)
        i = pl.program_id(0)
        for r, v in zip(o_refs[no:], vals[no:]):
            @pl.when(i == 0)
            def _(r=r, v=v):
                r[...] = v

            @pl.when(i > 0)
            def _(r=r, v=v):
                r[...] += v

    return _call(name, body, grid, in_specs, out_specs, out_shape, arrays, sem=("arbitrary",), comm=comm)


def _mm(name, a, b, tb=False, out_dtype=F32, tm=512, tn=3072, a2=None, b2=None, comm=None):
    M, K = a.shape
    N = b.shape[0] if tb else b.shape[1]
    tm, tn = _tile(M, tm, 8), _tile(N, tn)
    dims = (((1,), (1,)), ((), ())) if tb else (((1,), (0,)), ((), ()))
    pairs = [(a, b)] + ([(a2, b2)] if a2 is not None else [])

    def body(*refs):
        o_ref = refs[-1]
        acc = None
        for i in range(len(pairs)):
            p = lax.dot_general(refs[2 * i][...].astype(BF16), refs[2 * i + 1][...].astype(BF16), dims,
                                preferred_element_type=F32)
            acc = p if acc is None else acc + p
        o_ref[...] = acc.astype(o_ref.dtype)

    in_specs, arrays = [], []
    for x, y in pairs:
        k = x.shape[1]
        in_specs.append(pl.BlockSpec((tm, k), lambda j, i: (i, 0)))
        in_specs.append(pl.BlockSpec((tn, k), lambda j, i: (j, 0)) if tb else pl.BlockSpec((k, tn), lambda j, i: (0, j)))
        arrays += [x, y]
    return _call(name, body, (N // tn, M // tm), in_specs, [pl.BlockSpec((tm, tn), lambda j, i: (i, j))],
                 [jax.ShapeDtypeStruct((M, N), out_dtype)], arrays, sem=("parallel", "parallel"), comm=comm)[0]


def _mm_t(name, a, g, tk=1408, tn=1408, tt=2048):
    T, K = a.shape
    N = g.shape[1]
    tk, tn, tt = _tile(K, tk), _tile(N, tn), _tile(T, tt, 8)

    def body(a_ref, g_ref, o_ref):
        p = lax.dot_general(a_ref[...].astype(BF16), g_ref[...].astype(BF16), (((0,), (0,)), ((), ())),
                            preferred_element_type=F32)
        t = pl.program_id(2)

        @pl.when(t == 0)
        def _():
            o_ref[...] = p

        @pl.when(t > 0)
        def _():
            o_ref[...] += p

    return _call(name, body, (K // tk, N // tn, T // tt),
                 [pl.BlockSpec((tt, tk), lambda i, j, t: (t, i)), pl.BlockSpec((tt, tn), lambda i, j, t: (t, j))],
                 [pl.BlockSpec((tk, tn), lambda i, j, t: (i, j))], [jax.ShapeDtypeStruct((K, N), F32)], [a, g],
                 sem=("parallel", "parallel", "arbitrary"))[0]


def _dot(a, b):
    return jnp.dot(a.astype(BF16), b.astype(BF16), preferred_element_type=F32)


def _dot_nt(a, b):
    return lax.dot_general(a.astype(BF16), b.astype(BF16), (((1,), (1,)), ((), ())), preferred_element_type=F32)


def _dot_tn(a, b):
    return lax.dot_general(a.astype(BF16), b.astype(BF16), (((0,), (0,)), ((), ())), preferred_element_type=F32)


def _sigmoid(x):
    return 1.0 / (1.0 + jnp.exp(-x))


def _ffn_up(h, wg, wu, comm=None):
    M, K = h.shape
    N = wg.shape[1]
    tm, tn = _tile(M, 512, 8), _tile(N, 1408)

    def body(h_ref, wg_ref, wu_ref, a_ref, b_ref, act_ref):
        hb = h_ref[...]
        a = jnp.dot(hb, wg_ref[...], preferred_element_type=F32)
        b = jnp.dot(hb, wu_ref[...], preferred_element_type=F32)
        a_ref[...] = a.astype(BF16)
        b_ref[...] = b.astype(BF16)
        act_ref[...] = (a * _sigmoid(a) * b).astype(BF16)

    wspec = pl.BlockSpec((K, tn), lambda j, i: (0, j))
    ospec = pl.BlockSpec((tm, tn), lambda j, i: (i, j))
    return _call("ffn_up", body, (N // tn, M // tm), [pl.BlockSpec((tm, K), lambda j, i: (i, 0)), wspec, wspec],
                 [ospec] * 3, [jax.ShapeDtypeStruct((M, N), BF16)] * 3, [h, wg, wu], sem=("parallel", "parallel"),
                 comm=comm)


def _ffn_down_dx(df, wdn, a, b):
    M, K = df.shape
    N = wdn.shape[0]
    tm, tn = _tile(M, 512, 8), _tile(N, 1408)

    def body(df_ref, w_ref, a_ref, b_ref, da_ref, db_ref):
        d = _dot_nt(df_ref[...], w_ref[...])
        av, bv = a_ref[...].astype(F32), b_ref[...].astype(F32)
        s = _sigmoid(av)
        da_ref[...] = (d * bv * (s + av * s * (1.0 - s))).astype(BF16)
        db_ref[...] = (d * av * s).astype(BF16)

    tspec = pl.BlockSpec((tm, tn), lambda j, i: (i, j))
    return _call("ffn_down_dx", body, (N // tn, M // tm),
                 [pl.BlockSpec((tm, K), lambda j, i: (i, 0)), pl.BlockSpec((tn, K), lambda j, i: (j, 0)), tspec, tspec],
                 [tspec] * 2, [jax.ShapeDtypeStruct((M, N), BF16)] * 2, [df, wdn, a, b], sem=("parallel", "parallel"))


def _rms(x, g):
    r = lax.rsqrt(jnp.mean(x * x, axis=-1, keepdims=True) + EPS)
    return x * r * g


def _rms_bwd(x, g, dy):
    r = lax.rsqrt(jnp.mean(x * x, axis=-1, keepdims=True) + EPS)
    xh = x * r
    dxh = dy * g
    dx = r * (dxh - xh * jnp.mean(dxh * xh, axis=-1, keepdims=True))
    return dx, jnp.sum(dy * xh, axis=0, keepdims=True)


def _shift_down(x, k):
    t = lax.broadcasted_iota(jnp.int32, x.shape, 0)
    return jnp.where(t >= k, pltpu.roll(x, k, 0), 0.0)


def _shift_up(x, k):
    n = x.shape[0]
    t = lax.broadcasted_iota(jnp.int32, x.shape, 0)
    return jnp.where(t < n - k, pltpu.roll(x, n - k, 0), 0.0)


def _group_select(vals):
    lane = lax.broadcasted_iota(jnp.int32, vals[0].shape, 1) // GROUP
    out = vals[-1]
    for g in range(len(vals) - 2, -1, -1):
        out = jnp.where(lane == g, vals[g], out)
    return out


def _pool_count(shape):
    t = lax.broadcasted_iota(jnp.int32, shape, 0)
    lane = lax.broadcasted_iota(jnp.int32, shape, 1) // GROUP
    w = jnp.where(lane == 0, POOL_WINDOWS[0], jnp.where(lane == 1, POOL_WINDOWS[1],
                  jnp.where(lane == 2, POOL_WINDOWS[2], POOL_WINDOWS[3])))
    return jnp.minimum(t + 1, w).astype(F32)


def _pooled(z):
    s = [z]
    for k in (1, 2, 4, 8):
        s.append(s[-1] + _shift_down(s[-1], k))
    return _group_select(s[1:]) / _pool_count(z.shape) - z


def _pooled_bwd(dp):
    u = dp / _pool_count(dp.shape)
    s = [u]
    for k in (1, 2, 4, 8):
        s.append(s[-1] + _shift_up(s[-1], k))
    return _group_select(s[1:]) - dp


def _diag_mask():
    r = lax.broadcasted_iota(jnp.int32, (TQ, TQ), 0) // CHUNK
    c = lax.broadcasted_iota(jnp.int32, (TQ, TQ), 1) // CHUNK
    return r >= c


def _attn_fwd(q, k, v, B, S, comm=None):
    nq = S // TQ

    def body(q_ref, k_ref, v_ref, o_ref, lse_ref):
        mask = _diag_mask()
        for i in range(nq):
            r0, r1 = i * TQ, (i + 1) * TQ
            qb = q_ref[r0:r1, :]
            sd = jnp.where(mask, _dot_nt(qb, k_ref[r0:r1, :]), NEG_INF)
            m = jnp.max(sd, axis=-1, keepdims=True)
            if i:
                so = _dot_nt(qb, k_ref[0:r0, :])
                m = jnp.maximum(m, jnp.max(so, axis=-1, keepdims=True))
            pd = jnp.exp(sd - m)
            l = jnp.sum(pd, axis=-1, keepdims=True)
            acc = _dot(pd, v_ref[r0:r1, :])
            if i:
                po = jnp.exp(so - m)
                l = l + jnp.sum(po, axis=-1, keepdims=True)
                acc = acc + _dot(po, v_ref[0:r0, :])
            o_ref[r0:r1, :] = acc / l
            lse_ref[r0:r1, :] = jnp.broadcast_to(m + jnp.log(l), (TQ, HP))

    spec = pl.BlockSpec((S, HP), lambda b, h: (b, h))
    return _call("attn_fwd", body, (B, HEADS), [spec] * 3, [spec] * 2, [jax.ShapeDtypeStruct((B * S, QW), F32)] * 2,
                 [q, k, v], sem=("parallel", "parallel"), comm=comm)


def _attn_bwd(q, k, v, o, do, lse, B, S, comm=None):
    nq = S // TQ

    def body(q_ref, k_ref, v_ref, o_ref, do_ref, lse_ref, dq_ref, dk_ref, dv_ref, dk_acc, dv_acc):
        mask = _diag_mask()
        dk_acc[...] = jnp.zeros_like(dk_acc)
        dv_acc[...] = jnp.zeros_like(dv_acc)
        for i in range(nq):
            r0, r1 = i * TQ, (i + 1) * TQ
            qb, dob = q_ref[r0:r1, :], do_ref[r0:r1, :]
            delta = jnp.sum(o_ref[r0:r1, :] * dob, axis=-1, keepdims=True)
            lse_b = lse_ref[r0:r1, :][:, :1]
            dq = None
            for c0, c1, diag in ([(0, r0, False)] if i else []) + [(r0, r1, True)]:
                kb, vb = k_ref[c0:c1, :], v_ref[c0:c1, :]
                p = jnp.exp(_dot_nt(qb, kb) - lse_b)
                if diag:
                    p = jnp.where(mask, p, 0.0)
                ds = p * (_dot_nt(dob, vb) - delta)
                part = _dot(ds, kb)
                dq = part if dq is None else dq + part
                dk_acc[c0:c1, :] += _dot_tn(ds, qb)
                dv_acc[c0:c1, :] += _dot_tn(p, dob)
            dq_ref[r0:r1, :] = dq
        dk_ref[...] = dk_acc[...].astype(BF16)
        dv_ref[...] = dv_acc[...].astype(BF16)

    spec = pl.BlockSpec((S, HP), lambda b, h: (b, h))
    return _call("attn_bwd", body, (B, HEADS), [spec] * 6, [spec] * 3,
                 [jax.ShapeDtypeStruct((B * S, QW), F32)] + [jax.ShapeDtypeStruct((B * S, QW), BF16)] * 2,
                 [q, k, v, o, do, lse], scratch=[pltpu.VMEM((S, HP), F32)] * 2, sem=("parallel", "parallel"), comm=comm)


def _all_gather(name, xs):
    n = len(xs)

    def body(*refs):
        x_refs, out_refs = refs[:n], refs[n:2 * n]
        send_sems, recv_sems, local_sems = refs[2 * n:]
        x_, y_, c_ = lax.axis_index("x"), lax.axis_index("y"), lax.axis_index("c")
        me, sibling = (x_, y_, c_), (x_, y_, 1 - c_)
        chips = [(1 - x_, y_), (x_, 1 - y_), (1 - x_, 1 - y_)]

        def copy(a, k, block, to, own=False):
            slot = out_refs[a].at[4 * block[0] + 2 * block[1] + block[2]]
            return pltpu.make_async_remote_copy(
                src_ref=x_refs[a] if own else slot, dst_ref=slot,
                send_sem=send_sems.at[7 * a + k], recv_sem=recv_sems.at[7 * a + k], device_id=to, device_id_type=MESH)

        mine = [pltpu.make_async_copy(x_refs[a], out_refs[a].at[4 * x_ + 2 * y_ + c_], local_sems.at[a]) for a in range(n)]
        for cp in mine:
            cp.start()
        first = [copy(a, 1 + j, me, (*chip, c_), own=True) for j, chip in enumerate(chips) for a in range(n)]
        first += [copy(a, 0, me, sibling, own=True) for a in range(n)]
        for cp in first:
            cp.start()
        passed = []
        for j, chip in enumerate(chips):
            for a in range(n):
                copy(a, 1 + j, (*chip, c_), me).wait_recv()
                passed.append(copy(a, 4 + j, (*chip, c_), sibling))
                passed[-1].start()
        for a in range(n):
            copy(a, 0, sibling, me).wait_recv()
            for j, chip in enumerate(chips):
                copy(a, 4 + j, (*chip, 1 - c_), me).wait_recv()
        for cp in first + passed:
            cp.wait_send()
        for cp in mine:
            cp.wait()

    return pl.pallas_call(
        body, name=name, out_shape=[jax.ShapeDtypeStruct((N_DEV,) + x.shape, x.dtype) for x in xs],
        in_specs=[pl.BlockSpec(memory_space=pl.ANY)] * n, out_specs=[pl.BlockSpec(memory_space=pl.ANY)] * n,
        scratch_shapes=[pltpu.SemaphoreType.DMA((7 * n,)), pltpu.SemaphoreType.DMA((7 * n,)), pltpu.SemaphoreType.DMA((n,))],
    )(*xs)


def _exchange_alone(name, comm):
    def body(x_ref, o_ref):
        o_ref[...] = x_ref[...]

    spec = pl.BlockSpec((8, 128), lambda i: (0, 0))
    _call(name, body, (1,), [spec], [spec], [jax.ShapeDtypeStruct((8, 128), F32)], [jnp.zeros((8, 128), F32)], comm=comm)
    return comm.results


def _adamw(name, parts, w, m, v, row_off=0, layers=1, stride=0):
    R, C = w.shape
    r = R // layers
    assert parts.shape[2] == C and parts.shape[1] >= (layers - 1) * stride + row_off + r
    if r % 8 == 0:
        tr = max(t for t in range(8, 513, 8) if r % t == 0 and row_off % t == 0 and stride % t == 0)
    else:
        assert layers == 1
        tr = r
    nb = r // tr

    def body(p_ref, w_ref, m_ref, v_ref, g_ref, d_ref, nm_ref, nv_ref):
        g = p_ref[0].astype(F32)
        for j in range(1, N_DEV):
            g = g + p_ref[j].astype(F32)
        m_new = ADAM_B1 * m_ref[...] + (1.0 - ADAM_B1) * g
        v_new = ADAM_B2 * v_ref[...] + (1.0 - ADAM_B2) * (g * g)
        m_hat = m_new / (1.0 - ADAM_B1 ** ADAM_STEP)
        v_hat = v_new / (1.0 - ADAM_B2 ** ADAM_STEP)
        g_ref[...] = g
        d_ref[...] = -ADAM_LR * (m_hat / (jnp.sqrt(v_hat) + ADAM_EPS) + ADAM_WD * w_ref[...])
        nm_ref[...] = m_new
        nv_ref[...] = v_new

    spec = pl.BlockSpec((tr, C), lambda l, i: (l * nb + i, 0))
    pspec = pl.BlockSpec((N_DEV, tr, C), lambda l, i: (0, (l * stride + row_off) // tr + i, 0))
    return _call(name, body, (layers, nb), [pspec, spec, spec, spec], [spec] * 4,
                 [jax.ShapeDtypeStruct((R, C), F32)] * 4, [parts, w, m, v], sem=("parallel", "parallel"))


def _rot_cols(w):
    h = w.shape[-1] // 2
    return jnp.concatenate([-w[..., h:], w[..., :h]], axis=-1)


def _unrot_cols(d):
    h = d.shape[-1] // 2
    return jnp.concatenate([d[..., h:], -d[..., :h]], axis=-1)


def _split_in(w):
    out, o = [], 0
    for n in IN_SIZES:
        out.append(w[..., o:o + n])
        o += n
    return out


def _layer_weights(G):
    L = G['w_in'].shape[0]
    cq, ckv, kr, pin, su, sv, cvb, cvc, cvx, gate = _split_in(G['w_in'])
    W1s = jnp.concatenate([cq, ckv, kr, _rot_cols(kr), jnp.zeros((L, D, 64), BF16), pin, su, sv, cvb, cvc, cvx], axis=-1)
    wq = G['w_uq'].reshape(L, Q_LORA, HEADS, QK_NOPE + QK_ROPE)
    nope, rope = wq[..., :QK_NOPE], wq[..., QK_NOPE:]
    z32 = jnp.zeros((L, Q_LORA, HEADS, HP - QK_NOPE - QK_ROPE), BF16)
    wq_a = jnp.concatenate([nope, rope, z32], axis=-1).reshape(L, Q_LORA, QW)
    wq_b = jnp.concatenate([jnp.zeros_like(nope), _rot_cols(rope), z32], axis=-1).reshape(L, Q_LORA, QW)
    wkv = G['w_ukv'].reshape(L, KV_LORA, HEADS, QK_NOPE + V_HEAD)
    z64 = jnp.zeros((L, KV_LORA, HEADS, HP - QK_NOPE), BF16)
    wkn = jnp.concatenate([wkv[..., :QK_NOPE], z64], axis=-1).reshape(L, KV_LORA, QW)
    wv = jnp.concatenate([wkv[..., QK_NOPE:], z64], axis=-1).reshape(L, KV_LORA, QW)
    wa = G['w_br_a'].reshape(L, HEADS, V_HEAD, D)
    wa = jnp.concatenate([wa, jnp.zeros((L, HEADS, HP - V_HEAD, D), BF16)], axis=2).reshape(L, QW, D)
    return dict(
        W1=jnp.concatenate([W1s, gate], axis=-1), W1s=W1s, W1g=gate, Wq=jnp.concatenate([wq_a, wq_b], axis=-1),
        Wkv=jnp.concatenate([wkn, wv], axis=-1), Wa=wa, Wb=G['w_br_b'], Wc=G['w_br_c'], Wd=G['w_br_d'], Wout=G['w_out'],
        Wg=G['w_ffn_gate'], Wu=G['w_ffn_up'], Wdn=G['w_ffn_down'], conv=G['conv_w'].reshape(L, 3, BR).astype(F32))


def _tables(S):
    inv = ROPE_THETA ** (-jnp.arange(0, QK_ROPE, 2, dtype=F32) / QK_ROPE)
    ang = jnp.arange(S, dtype=F32)[:, None] * inv[None, :]
    cos, sin = jnp.cos(ang), jnp.sin(ang)
    scale = (QK_NOPE + QK_ROPE) ** -0.5
    one, zero = jnp.ones((S, QK_NOPE), F32), jnp.zeros((S, QK_NOPE), F32)
    z32 = jnp.zeros((S, HP - QK_NOPE - QK_ROPE), F32)
    cq = jnp.tile(jnp.concatenate([one, cos, cos, z32], axis=1) * scale, (1, HEADS))
    sq = jnp.tile(jnp.concatenate([zero, sin, sin, z32], axis=1) * scale, (1, HEADS))
    ck = jnp.concatenate([cos, cos, sin, sin, jnp.zeros((S, HP - 2 * QK_ROPE), F32)], axis=1)
    return cq, sq, ck


def _place_k():
    p = np.zeros((HP, QW), np.float32)
    for r in range(2 * QK_ROPE):
        for h in range(HEADS):
            p[r, h * HP + QK_NOPE + r % QK_ROPE] = 1.0
    return jnp.asarray(p, BF16)


def _layer_fwd(x, W, P, tabs, B, S, ride=(None, None, None)):
    cq_t, sq_t, ck_t, place = tabs
    g1, g_cq, g_ckv = P['g_pre_mix'], P['g_cq'], P['g_ckv']
    h = _rw("pre_mix", lambda x, g: _rms(x, g), [x], [g1], [(D, BF16)], tile=512)[0]
    P1 = _mm("in_proj", h, W['W1'], out_dtype=BF16, comm=ride[1])

    def q_prep(c, cq, sq, g, w):
        qq = _dot(_rms(c, g), w)
        return qq[:, :QW] * cq + qq[:, QW:] * sq
    q = _rw("q_prep", q_prep, [win(P1, 256, 0), tab(cq_t), tab(sq_t)], [g_cq, W['Wq']], [(QW, BF16)])[0]

    def kv_prep(c, seg, ck, g, w, place):
        kv = _dot(_rms(c, g), w)
        return kv[:, :QW] + _dot(seg * ck, place), kv[:, QW:]
    k, v = _rw("kv_prep", kv_prep, [win(P1, 128, 2), win(P1, 128, 3), tab(ck_t)], [g_ckv, W['Wkv'], place],
               [(QW, BF16), (QW, BF16)])
    o, lse = _attn_fwd(q, k, v, B, S, comm=ride[0])

    def pool_f(z, wbd, scale):
        return _dot(_pooled(z), wbd) * scale
    bo = _rw("pool_fwd", pool_f, [win(P1, BR, 2)], [P['pool_bd'], P['pool_scale']], [(BR, BF16)], tile=S)[0]

    def sgu_f(u, sv, g, wm, bias):
        return u * _sgu_mix(_rms(sv, g), wm, bias)
    co = _rw("sgu_fwd", sgu_f, [win(P1, BR, 3), win(P1, BR, 4)], [P['g_sgu_v'], P['sgu_wm'], P['sgu_bias']],
             [(BR, BF16)], tile=SGU_BLOCK)[0]

    def conv_f(b, c, xi, w):
        z = c * xi
        return b * (w[0:1] * _shift_down(z, 2) + w[1:2] * _shift_down(z, 1) + w[2:3] * z)
    do_ = _rw("conv_fwd", conv_f, [win(P1, BR, 5), win(P1, BR, 6), win(P1, BR, 7)], [W['conv']], [(BR, BF16)], tile=S)[0]

    def merge_f(a, b, c, d, l0, l1, l2, l3, wa, wb, wc, wd):
        return (_sigmoid(l0) * _dot(a, wa) + _sigmoid(l1) * _dot(b, wb) + _sigmoid(l2) * _dot(c, wc)
                + _sigmoid(l3) * _dot(d, wd))
    merged = _rw("merge_fwd", merge_f, [o, bo, co, do_] + [win(P1, D, 2 + i) for i in range(4)],
                 [W['Wa'], W['Wb'], W['Wc'], W['Wd']], [(D, BF16)])[0]
    om = _mm("out_proj", merged, W['Wout'])
    x1 = _rw("post_mix", lambda x, o, g: x + _rms(o, g), [x, om], [P['g_post_mix']], [(D, F32)], tile=512)[0]
    h2 = _rw("pre_ffn", lambda x, g: _rms(x, g), [x1], [P['g_pre_ffn']], [(D, BF16)], tile=512)[0]
    fa, fb, act = _ffn_up(h2, W['Wg'], W['Wu'], comm=ride[2])
    f = _mm("ffn_down", act, W['Wdn'])
    x2 = _rw("post_ffn", lambda x, f, g: x + _rms(f, g), [x1, f], [P['g_post_ffn']], [(D, F32)], tile=512)[0]
    saved = dict(x=x, h=h, P1=P1, q=q, k=k, v=v, o=o, lse=lse, bo=bo, co=co, do=do_, merged=merged, om=om, x1=x1,
                 h2=h2, fa=fa, fb=fb, act=act, f=f)
    return x2, saved


def _sgu_mix(vn, wm, bias):
    outs = [_dot(wm[g], vn) for g in range(4)]
    return _group_select(outs) + bias


def _layer_bwd(dx2, sv, W, P, tabs, B, S, ride=(None, None, None)):
    cq_t, sq_t, ck_t, place = tabs
    P1 = sv['P1']
    grads = {}

    def post_ffn_b(f, dy, g):
        return _rms_bwd(f, g, dy)
    df, grads['g_post_ffn'] = _rw("post_ffn_bwd", post_ffn_b, [sv['f'], dx2], [P['g_post_ffn']], [(D, BF16)], [(1, D)], tile=512)
    da, db = _ffn_down_dx(df, W['Wdn'], sv['fa'], sv['fb'])
    grads['Wdn'] = _mm_t("ffn_down_dw", sv['act'], df)
    dh2 = _mm("ffn_up_dx", da, W['Wg'], tb=True, a2=db, b2=W['Wu'])
    grads['Wg'] = _mm_t("ffn_gate_dw", sv['h2'], da)
    grads['Wu'] = _mm_t("ffn_up_dw", sv['h2'], db)

    def pre_ffn_b(x, dh, dy, g):
        dx, dg = _rms_bwd(x, g, dh)
        return dy + dx, dg
    dx1, grads['g_pre_ffn'] = _rw("pre_ffn_bwd", pre_ffn_b, [sv['x1'], dh2, dx2], [P['g_pre_ffn']], [(D, F32)], [(1, D)], tile=512)

    def post_mix_b(o, dy, g):
        return _rms_bwd(o, g, dy)
    dom, grads['g_post_mix'] = _rw("post_mix_bwd", post_mix_b, [sv['om'], dx1], [P['g_post_mix']], [(D, BF16)], [(1, D)], tile=512)
    dmerged = _mm("out_proj_dx", dom, W['Wout'], tb=True)
    grads['Wout'] = _mm_t("out_proj_dw", sv['merged'], dom)

    def merge_b(a, b, c, d, l0, l1, l2, l3, dm, wa, wb, wc, wd):
        outs_dl, outs_dy, outs_dbr = [], [], []
        for br, l, w in ((a, l0, wa), (b, l1, wb), (c, l2, wc), (d, l3, wd)):
            s = _sigmoid(l)
            y = _dot(br, w)
            dy = (dm * s).astype(BF16)
            outs_dl.append((dm * y * s * (1.0 - s)).astype(BF16))
            outs_dy.append(dy)
            outs_dbr.append(_dot_nt(dy, w))
        return (jnp.concatenate(outs_dl, axis=1), *outs_dy, *outs_dbr)
    mb = _rw("merge_bwd", merge_b,
             [sv['o'], sv['bo'], sv['co'], sv['do']] + [win(P1, D, 2 + i) for i in range(4)] + [dmerged],
             [W['Wa'], W['Wb'], W['Wc'], W['Wd']],
             [(4 * D, BF16)] + [(D, BF16)] * 4 + [(QW, F32), (BR, F32), (BR, F32), (BR, F32)], comm=ride[1])
    dl, dy, (d_o, d_bo, d_co, d_do) = mb[0], mb[1:5], mb[5:9]
    grads['Wa'] = _mm_t("br_a_dw", sv['o'], dy[0])
    grads['Wb'] = _mm_t("br_b_dw", sv['bo'], dy[1])
    grads['Wc'] = _mm_t("br_c_dw", sv['co'], dy[2])
    grads['Wd'] = _mm_t("br_d_dw", sv['do'], dy[3])

    def conv_b(b, c, xi, dout, w):
        z = c * xi
        z1, z2 = _shift_down(z, 1), _shift_down(z, 2)
        y = w[0:1] * z2 + w[1:2] * z1 + w[2:3] * z
        dyv = dout * b
        dz = w[2:3] * dyv + w[1:2] * _shift_up(dyv, 1) + w[0:1] * _shift_up(dyv, 2)
        return (dout * y, dz * xi, dz * c, jnp.sum(dyv * z2, axis=0, keepdims=True),
                jnp.sum(dyv * z1, axis=0, keepdims=True), jnp.sum(dyv * z, axis=0, keepdims=True))
    dcvb, dcvc, dcvx, dw0, dw1, dw2 = _rw("conv_bwd", conv_b, [win(P1, BR, 5), win(P1, BR, 6), win(P1, BR, 7), d_do],
                                          [W['conv']], [(BR, BF16)] * 3, [(1, BR)] * 3, tile=S)
    grads['conv'] = jnp.concatenate([dw0, dw1, dw2], axis=0)

    def sgu_b(u, s_v, dout, g, wm, bias):
        vn = _rms(s_v, g)
        mixed = _sgu_mix(vn, wm, bias)
        dmix = dout * u
        dvn = _group_select([_dot_tn(wm[gi], dmix) for gi in range(4)])
        dsv, dg = _rms_bwd(s_v, g, dvn)
        lane = lax.broadcasted_iota(jnp.int32, dmix.shape, 1) // GROUP
        dwm = [_dot_nt(jnp.where(lane == gi, dmix, 0.0), vn) for gi in range(4)]
        return (dout * mixed, dsv, dg, *dwm, dmix)
    dsu, dsv_, grads['g_sgu_v'], wm0, wm1, wm2, wm3, grads['sgu_bias'] = _rw(
        "sgu_bwd", sgu_b, [win(P1, BR, 3), win(P1, BR, 4), d_co], [P['g_sgu_v'], P['sgu_wm'], P['sgu_bias']],
        [(BR, BF16)] * 2, [(1, BR)] + [(SGU_BLOCK, SGU_BLOCK)] * 4 + [(SGU_BLOCK, BR)], tile=SGU_BLOCK)
    grads['sgu_wm'] = jnp.stack([wm0, wm1, wm2, wm3])

    def pool_b(z, dout, wbd, scale):
        pooled = _pooled(z)
        mixed = _dot(pooled, wbd)
        dmix = dout * scale
        dz = _pooled_bwd(_dot_nt(dmix, wbd))
        return dz, _dot_tn(pooled, dmix), jnp.sum(dout * mixed, axis=0, keepdims=True)
    dpin, grads['pool_bd'], grads['pool_scale'] = _rw(
        "pool_bwd", pool_b, [win(P1, BR, 2), d_bo], [P['pool_bd'], P['pool_scale']], [(BR, BF16)], [(BR, BR), (1, BR)], tile=S)

    dq, dk, dv = _attn_bwd(sv['q'], sv['k'], sv['v'], sv['o'], d_o, sv['lse'], B, S, comm=ride[0])

    def q_prep_b(c, dq, cq, sq, g, w):
        dqq = jnp.concatenate([dq * cq, dq * sq], axis=1).astype(BF16)
        dc, dg = _rms_bwd(c, g, _dot_nt(dqq, w))
        return dc, _rms(c, g), dqq, dg
    dcq, cqn, dqq, grads['g_cq'] = _rw("q_prep_bwd", q_prep_b, [win(P1, 256, 0), dq, tab(cq_t), tab(sq_t)],
                                       [P['g_cq'], W['Wq']], [(Q_LORA, BF16), (Q_LORA, BF16), (2 * QW, BF16)], [(1, Q_LORA)])
    grads['Wq'] = _mm_t("uq_dw", cqn, dqq)

    def kv_prep_b(c, dk, dv, ck, g, w, place):
        dkv = jnp.concatenate([dk, dv], axis=1)
        dc, dg = _rms_bwd(c, g, _dot_nt(dkv, w))
        return dc, _dot_nt(dk, place) * ck, _rms(c, g), dkv, dg
    dckv, dseg, kvn, dkv, grads['g_ckv'] = _rw(
        "kv_prep_bwd", kv_prep_b, [win(P1, 128, 2), dk, dv, tab(ck_t)], [P['g_ckv'], W['Wkv'], place],
        [(KV_LORA, BF16), (HP, BF16), (KV_LORA, BF16), (2 * QW, BF16)], [(1, KV_LORA)])
    grads['Wkv'] = _mm_t("ukv_dw", kvn, dkv)

    dps = jnp.concatenate([dcq, dckv, dseg, dpin, dsu, dsv_, dcvb, dcvc, dcvx], axis=1)
    dh = _mm("in_proj_dx", dps, W['W1s'], tb=True, a2=dl, b2=W['W1g'], tm=256, comm=ride[2])
    grads['W1s'] = _mm_t("in_proj_s_dw", sv['h'], dps)
    grads['W1g'] = _mm_t("in_proj_g_dw", sv['h'], dl)

    def pre_mix_b(x, dh, dy, g):
        dx, dg = _rms_bwd(x, g, dh)
        return dy + dx, dg
    dx, grads['g_pre_mix'] = _rw("pre_mix_bwd", pre_mix_b, [sv['x'], dh, dx1], [P['g_pre_mix']], [(D, F32)], [(1, D)], tile=512)
    return dx, grads


def _param_grads(gl):
    st = {k: jnp.stack([g[k] for g in gl]) for k in gl[0]}
    L = len(gl)
    w1 = st['W1s']
    seg = w1[..., C_KR:C_KR + 128]
    dkr = seg[..., :QK_ROPE] + _unrot_cols(seg[..., QK_ROPE:2 * QK_ROPE])
    out = {}
    out['w_in'] = jnp.concatenate([w1[..., :C_KR], dkr, w1[..., C_PIN:], st['W1g']], axis=-1)
    qa = st['Wq'][..., :QW].reshape(L, Q_LORA, HEADS, HP)
    qb = st['Wq'][..., QW:].reshape(L, Q_LORA, HEADS, HP)
    drope = qa[..., QK_NOPE:QK_NOPE + QK_ROPE] + _unrot_cols(qb[..., QK_NOPE:QK_NOPE + QK_ROPE])
    out['w_uq'] = jnp.concatenate([qa[..., :QK_NOPE], drope], axis=-1).reshape(L, Q_LORA, HEADS * (QK_NOPE + QK_ROPE))
    kn = st['Wkv'][..., :QW].reshape(L, KV_LORA, HEADS, HP)[..., :QK_NOPE]
    vv = st['Wkv'][..., QW:].reshape(L, KV_LORA, HEADS, HP)[..., :V_HEAD]
    out['w_ukv'] = jnp.concatenate([kn, vv], axis=-1).reshape(L, KV_LORA, HEADS * (QK_NOPE + V_HEAD))
    out['w_br_a'] = st['Wa'].reshape(L, HEADS, HP, D)[:, :, :V_HEAD].reshape(L, HEADS * V_HEAD, D)
    out['w_br_b'], out['w_br_c'], out['w_br_d'], out['w_out'] = st['Wb'], st['Wc'], st['Wd'], st['Wout']
    out['w_ffn_gate'], out['w_ffn_up'], out['w_ffn_down'] = st['Wg'], st['Wu'], st['Wdn']
    out['conv_w'] = st['conv'].reshape(L, 3, 1, BR)
    for n in ('g_pre_mix', 'g_cq', 'g_ckv', 'g_sgu_v', 'g_post_mix', 'g_pre_ffn', 'g_post_ffn', 'pool_scale'):
        out[n] = st[n].reshape(L, -1)
    bd = st['pool_bd'].reshape(L, 4, GROUP, 4, GROUP)
    out['pool_w'] = jnp.stack([bd[:, g, :, g, :] for g in range(4)], axis=1)
    out['sgu_w'] = st['sgu_wm'] * _sgu_mask()[None, None]
    out['sgu_b'] = st['sgu_bias'].reshape(L, SGU_BLOCK, 4, GROUP).sum(-1).transpose(0, 2, 1)
    return out


def _sgu_mask():
    pc = np.arange(SGU_BLOCK) // CHUNK
    return jnp.asarray(pc[:, None] >= pc[None, :], F32)


def _rows(a, lead):
    return a.reshape(a.shape[:lead] + (-1, a.shape[-1]))


def _pad_to(a, rows, cols):
    pad = [(0, 0)] * (a.ndim - 2) + [(0, rows - a.shape[-2]), (0, cols - a.shape[-1])]
    return jnp.pad(a, pad)


GROUPS = (('w_in',), ('w_ffn_gate', 'w_ffn_up'), ('w_out', 'w_ffn_down'),
          ('w_uq', 'w_ukv', 'w_br_a', 'w_br_b', 'w_br_c', 'w_br_d', 'conv_w'))
PADDED = {'w_uq': (Q_LORA, 128), 'conv_w': (128, 128), 'w_ffn_down': (384, D)}
NARROW = ('w_uq', 'conv_w')


def _group(t, lead):
    bufs = []
    for names in GROUPS:
        ps = [_rows(t[n], lead) for n in names]
        ps = [_pad_to(p, *PADDED[n]) if n in PADDED else p for n, p in zip(names, ps)]
        bufs.append(ps[0] if len(ps) == 1 else jnp.concatenate(ps, axis=-2))
    return bufs


def _group_rows(shapes):
    where, rows = {}, []
    for b, names in enumerate(GROUPS):
        off = 0
        for n in names:
            where[n] = (b, off)
            off += PADDED.get(n, shapes[n])[0]
        rows.append(off)
    return where, rows


def _shard_split(a, axis):
    n = a.shape[axis]
    a = a.reshape(a.shape[:axis] + (N_DEV, n // N_DEV) + a.shape[axis + 1:])
    return jnp.moveaxis(a, axis, 0)


def _shard_join(a, axis):
    a = jnp.moveaxis(a, 0, axis)
    return a.reshape(a.shape[:axis] + (a.shape[axis] * a.shape[axis + 1],) + a.shape[axis + 2:])


def _as2d(a):
    return a.reshape(-1, a.shape[-1])


RIDE_GROUPS = ((0, 3), (1,), (2,))


def kernel(x, w_in, g_pre_mix, g_cq, g_ckv, w_uq, w_ukv, pool_w, pool_scale, g_sgu_v, sgu_w, sgu_b, conv_w, w_br_a, w_br_b, w_br_c, w_br_d, w_out, g_post_mix, g_pre_ffn, w_ffn_gate, w_ffn_up, w_ffn_down, g_post_ffn, loss_target, m_w_in, m_g_pre_mix, m_g_cq, m_g_ckv, m_w_uq, m_w_ukv, m_pool_w, m_pool_scale, m_g_sgu_v, m_sgu_w, m_sgu_b, m_conv_w, m_w_br_a, m_w_br_b, m_w_br_c, m_w_br_d, m_w_out, m_g_post_mix, m_g_pre_ffn, m_w_ffn_gate, m_w_ffn_up, m_w_ffn_down, m_g_post_ffn, v_w_in, v_g_pre_mix, v_g_cq, v_g_ckv, v_w_uq, v_w_ukv, v_pool_w, v_pool_scale, v_g_sgu_v, v_sgu_w, v_sgu_b, v_conv_w, v_w_br_a, v_w_br_b, v_w_br_c, v_w_br_d, v_w_out, v_g_post_mix, v_g_pre_ffn, v_w_ffn_gate, v_w_ffn_up, v_w_ffn_down, v_g_post_ffn):
    args = dict(locals())
    w = {n: args[n] for n in WEIGHTS}
    m = {n: args['m_' + n] for n in WEIGHTS}
    v = {n: args['v_' + n] for n in WEIGHTS}
    B, S, _ = x.shape
    T = B * S
    L = DEPTH
    names = list(SHARDED)
    shapes1 = {n: _as2d(w[n][0]).shape for n in names}
    where, group_rows = _group_rows(shapes1)

    def shard_bufs(l):
        return _group({n: w[n][l:l + 1].astype(BF16) for n in names}, 0)

    def layer_weights(gathered):
        G = {}
        for n in names:
            b, off = where[n]
            r, c = shapes1[n]
            G[n] = _shard_join(gathered[b][:, off:off + r, :c].reshape((N_DEV, 1) + w[n].shape[1:]), SHARDED[n])
        return {k: a[0] for k, a in _layer_weights(G).items()}

    eye = jnp.eye(4, dtype=F32)
    pool_bd = (pool_w[:, :, :, None, :] * eye[None, :, None, :, None]).reshape(L, BR, BR).astype(BF16)
    sgu_wm = (sgu_w * _sgu_mask()[None, None]).astype(BF16)
    sgu_bias = jnp.repeat(sgu_b.transpose(0, 2, 1), GROUP, axis=2)
    tabs = _tables(S) + (_place_k(),)

    def layer_params(l):
        P = {n: w[n][l][None, :] for n in ('g_pre_mix', 'g_cq', 'g_ckv', 'g_sgu_v', 'g_post_mix', 'g_pre_ffn',
                                            'g_post_ffn', 'pool_scale')}
        P.update(pool_bd=pool_bd[l], sgu_wm=sgu_wm[l], sgu_bias=sgu_bias[l])
        return P

    xt = x.reshape(T, D)
    saved, Ws = [], []
    gathered = _all_gather("gather_weights", shard_bufs(0))
    for l in range(L):
        Ws.append(layer_weights(gathered))
        ride = (None, None, None)
        if l + 1 < L:
            nxt = shard_bufs(l + 1)
            ride = tuple(_Gather([nxt[b] for b in bs]) for bs in RIDE_GROUPS)
        xt, sv = _layer_fwd(xt, Ws[l], layer_params(l), tabs, B, S, ride)
        saved.append(sv)
        if l + 1 < L:
            gathered = [None] * len(GROUPS)
            for bs, r in zip(RIDE_GROUPS, ride):
                for b, res in zip(bs, r.results):
                    gathered[b] = res

    def loss_f(y, t):
        e = y - t
        return e * (1.0 / D), jnp.sum(e * e, axis=0, keepdims=True)
    dy, sq = _rw("loss", loss_f, [xt, loss_target.reshape(T, D)], [], [(D, F32)], [(1, D)], tile=512)
    loss = lax.psum(0.5 * jnp.sum(sq) / D, ("x", "y", "c"))

    recv = [lax.empty((N_DEV, L * r, PADDED.get(ns[0], shapes1[ns[0]])[1]), BF16) for ns, r in zip(GROUPS, group_rows)]

    def send_bufs(pg):
        return _group({n: _shard_split(pg[n], SHARDED[n]).astype(BF16) for n in names}, 1)

    small, pending = [None] * L, None
    for l in reversed(range(L)):
        ride = (None, None, None)
        if pending is not None:
            ride = tuple(_Scatter([pending[b] for b in bs], [recv[b] for b in bs], [(l + 1) * group_rows[b] for b in bs])
                         for bs in RIDE_GROUPS)
        dy, gl = _layer_bwd(dy, saved[l], Ws[l], layer_params(l), tabs, B, S, ride)
        if pending is not None:
            for bs, r in zip(RIDE_GROUPS, ride):
                for b, res in zip(bs, r.results):
                    recv[b] = res
        pg = _param_grads([gl])
        pending = send_bufs(pg)
        rep = jnp.concatenate([pg[n].reshape(-1, 128) for n in REPLICATED], axis=0)
        small[l] = _pad_to(rep, -(-rep.shape[0] // 8) * 8, 128)
    grad_x = dy.reshape(B, S, D)
    recv = _exchange_alone("scatter_grads", _Scatter(pending, recv, [0] * len(GROUPS)))

    small_rows = small[0].shape[0]
    small = jnp.concatenate(small, axis=0)
    got_small = _all_gather("gather_small_grads", [small])[0].reshape(N_DEV, L, small_rows, 128)

    res = {}
    for n in names:
        b, off = where[n]
        r, c = shapes1[n]
        if n in NARROW:
            parts = recv[b].reshape(N_DEV, L, group_rows[b], -1)[:, :, off:off + r, :c].reshape(N_DEV, L * r, c)
            res[n] = _adamw("adamw_" + n, parts, _as2d(w[n]), _as2d(m[n]), _as2d(v[n]))
        else:
            res[n] = _adamw("adamw_" + n, recv[b], _as2d(w[n]), _as2d(m[n]), _as2d(v[n]), row_off=off, layers=L,
                            stride=group_rows[b])
    off = 0
    for n in REPLICATED:
        r = int(np.prod(w[n].shape[1:])) // 128
        shp2 = _as2d(w[n]).shape
        res[n] = _adamw("adamw_" + n, got_small[:, :, off:off + r].reshape((N_DEV,) + shp2), _as2d(w[n]), _as2d(m[n]), _as2d(v[n]))
        off += r
    outs = [loss, grad_x]
    for k in range(4):
        outs += [res[n][k].reshape(w[n].shape) for n in WEIGHTS]
    return tuple(outs)
```

```python
import functools

import numpy as np
import jax
import jax.numpy as jnp
from jax import lax
from jax.experimental import pallas as pl
from jax.experimental.pallas import tpu as pltpu

F32, BF16 = jnp.float32, jnp.bfloat16
MESH = pl.DeviceIdType.MESH
N_DEV = 8

D = 1024
DEPTH = 4
CHUNK = 64
EPS = 1e-6
NEG_INF = -1e30
HEADS, QK_NOPE, QK_ROPE, V_HEAD = 8, 64, 32, 64
Q_LORA, KV_LORA = 256, 128
ROPE_THETA = 10000.0
POOL_WINDOWS = (2, 4, 8, 16)
GROUP = 64
BR = 256
SGU_BLOCK = 128
D_FF = 2816
IN_SIZES = (256, 128, 32, 256, 256, 256, 256, 256, 256, 4096)
HP = 128
QW = HEADS * HP
C_CQ, C_CKV, C_KR, C_PIN, C_SU, C_SV, C_CVB, C_CVC, C_CVX, C_GATE = 0, 256, 384, 512, 768, 1024, 1280, 1536, 1792, 2048

ADAM_LR, ADAM_B1, ADAM_B2, ADAM_EPS, ADAM_WD, ADAM_STEP = 0.001, 0.9, 0.999, 1e-08, 0.01, 10

WEIGHTS = ['w_in', 'g_pre_mix', 'g_cq', 'g_ckv', 'w_uq', 'w_ukv', 'pool_w', 'pool_scale', 'g_sgu_v', 'sgu_w', 'sgu_b',
           'conv_w', 'w_br_a', 'w_br_b', 'w_br_c', 'w_br_d', 'w_out', 'g_post_mix', 'g_pre_ffn', 'w_ffn_gate',
           'w_ffn_up', 'w_ffn_down', 'g_post_ffn']
SHARDED = {'w_in': 2, 'w_uq': 2, 'w_ukv': 2, 'conv_w': 3, 'w_br_a': 2, 'w_br_b': 2, 'w_br_c': 2, 'w_br_d': 2,
           'w_out': 1, 'w_ffn_gate': 2, 'w_ffn_up': 2, 'w_ffn_down': 1}
REPLICATED = [n for n in WEIGHTS if n not in SHARDED]

VMEM_LIMIT = 56 * 1024 * 1024
TQ = 256


def _cparams(sem=None):
    return pltpu.CompilerParams(vmem_limit_bytes=VMEM_LIMIT, dimension_semantics=sem)


def _tile(n, cap, align=128):
    if n <= cap:
        return n
    for t in range(cap - cap % align, 0, -align):
        if n % t == 0:
            return t
    return n


def _peers():
    x_, y_, c_ = lax.axis_index("x"), lax.axis_index("y"), lax.axis_index("c")
    out = []
    for k in range(1, N_DEV):
        px, py, pc = (x_ + (k >> 2)) % 2, (y_ + ((k >> 1) & 1)) % 2, (c_ + (k & 1)) % 2
        out.append((k, (px, py, pc), 4 * px + 2 * py + pc))
    return 4 * x_ + 2 * y_ + c_, out


class _Exchange:
    def __init__(self, n):
        self.n = n
        self.scratch = [pltpu.SemaphoreType.DMA((7 * n,)), pltpu.SemaphoreType.DMA((7 * n,)),
                        pltpu.SemaphoreType.DMA((n,))]
        self.results = None

    def start(self, cin, cout, sems):
        local, sends, _ = self.copies(cin, cout, sems)
        for cp in local + sends:
            cp.start()

    def wait(self, cin, cout, sems):
        local, sends, recvs = self.copies(cin, cout, sems)
        for cp in recvs:
            cp.wait_recv()
        for cp in sends:
            cp.wait_send()
        for cp in local:
            cp.wait()


class _Gather(_Exchange):
    def __init__(self, xs):
        super().__init__(len(xs))
        self.inputs = list(xs)
        self.out_shape = [jax.ShapeDtypeStruct((N_DEV,) + x.shape, x.dtype) for x in xs]
        self.aliases = {}

    def copies(self, cin, cout, sems):
        send_sems, recv_sems, local_sems = sems
        me, peers = _peers()
        local, sends, recvs = [], [], []
        for a in range(self.n):
            local.append(pltpu.make_async_copy(cin[a], cout[a].at[me], local_sems.at[a]))
            for k, dev, flat in peers:
                sem = dict(send_sem=send_sems.at[7 * a + k - 1], recv_sem=recv_sems.at[7 * a + k - 1],
                           device_id=dev, device_id_type=MESH)
                sends.append(pltpu.make_async_remote_copy(src_ref=cin[a], dst_ref=cout[a].at[me], **sem))
                recvs.append(pltpu.make_async_remote_copy(src_ref=cin[a], dst_ref=cout[a].at[flat], **sem))
        return local, sends, recvs


class _Scatter(_Exchange):
    def __init__(self, xs, recv, row0):
        super().__init__(len(xs))
        self.inputs = list(xs) + list(recv)
        self.out_shape = [jax.ShapeDtypeStruct(r.shape, r.dtype) for r in recv]
        self.aliases = {self.n + a: a for a in range(self.n)}
        self.row0 = list(row0)

    def copies(self, cin, cout, sems):
        send_sems, recv_sems, local_sems = sems
        me, peers = _peers()
        local, sends, recvs = [], [], []
        for a in range(self.n):
            rows = pl.ds(self.row0[a], self.inputs[a].shape[1])
            local.append(pltpu.make_async_copy(cin[a].at[me], cout[a].at[me, rows], local_sems.at[a]))
            for k, dev, flat in peers:
                sem = dict(send_sem=send_sems.at[7 * a + k - 1], recv_sem=recv_sems.at[7 * a + k - 1],
                           device_id=dev, device_id_type=MESH)
                sends.append(pltpu.make_async_remote_copy(src_ref=cin[a].at[flat], dst_ref=cout[a].at[me, rows], **sem))
                recvs.append(pltpu.make_async_remote_copy(src_ref=cin[a].at[me], dst_ref=cout[a].at[flat, rows], **sem))
        return local, sends, recvs


def _call(name, body, grid, in_specs, out_specs, out_shape, arrays, scratch=(), sem=None, comm=None):
    if comm is None:
        res = pl.pallas_call(body, name=name, grid=grid, in_specs=list(in_specs), out_specs=list(out_specs),
                             out_shape=list(out_shape), scratch_shapes=list(scratch), compiler_params=_cparams(sem))(*arrays)
        return list(res)
    ni, no, ns = len(in_specs), len(out_specs), len(scratch)
    nci, nco = len(comm.inputs), len(comm.out_shape)
    hbm = pl.BlockSpec(memory_space=pl.ANY)

    def wrapped(*refs):
        ins, cin = refs[:ni], refs[ni:ni + nci]
        o0 = ni + nci
        outs, cout = refs[o0:o0 + no], refs[o0 + no:o0 + no + nco]
        s0 = o0 + no + nco
        scr, sems = refs[s0:s0 + ns], refs[s0 + ns:]
        ids = [pl.program_id(d) for d in range(len(grid))]
        first = functools.reduce(jnp.logical_and, [i == 0 for i in ids])
        last = functools.reduce(jnp.logical_and, [i == g - 1 for i, g in zip(ids, grid)])

        @pl.when(first)
        def _():
            comm.start(cin, cout, sems)

        body(*ins, *outs, *scr)

        @pl.when(last)
        def _():
            comm.wait(cin, cout, sems)

    res = pl.pallas_call(
        wrapped, name=name, grid=grid, in_specs=list(in_specs) + [hbm] * nci, out_specs=list(out_specs) + [hbm] * nco,
        out_shape=list(out_shape) + list(comm.out_shape), scratch_shapes=list(scratch) + comm.scratch,
        input_output_aliases={ni + i: no + j for i, j in comm.aliases.items()},
        compiler_params=_cparams(("arbitrary",) * len(grid)))(*arrays, *comm.inputs)
    comm.results = list(res[no:])
    return list(res[:no])


def win(arr, width, idx):
    return ('win', arr, width, idx)


def tab(arr):
    return ('tab', arr)


def _rw(name, fn, ins, consts, outs, accs=(), tile=256, comm=None):
    first = ins[0][1] if isinstance(ins[0], tuple) else ins[0]
    T = first.shape[0]
    tile = min(tile, T)
    grid = (T // tile,)
    arrays, in_specs = [], []
    for x in ins:
        if isinstance(x, tuple) and x[0] == 'win':
            _, a, w, k = x
            arrays.append(a)
            in_specs.append(pl.BlockSpec((tile, w), lambda i, k=k: (i, k)))
        elif isinstance(x, tuple) and x[0] == 'tab':
            a = x[1]
            nb = a.shape[0] // tile
            arrays.append(a)
            in_specs.append(pl.BlockSpec((tile, a.shape[1]), lambda i, nb=nb: (i % nb, 0)))
        else:
            arrays.append(x)
            in_specs.append(pl.BlockSpec((tile, x.shape[1]), lambda i: (i, 0)))
    for c in consts:
        arrays.append(c)
        in_specs.append(pl.BlockSpec(c.shape, lambda i, n=c.ndim: (0,) * n))
    out_shape = [jax.ShapeDtypeStruct((T, f), dt) for f, dt in outs]
    out_specs = [pl.BlockSpec((tile, f), lambda i: (i, 0)) for f, _ in outs]
    for shp in accs:
        out_shape.append(jax.ShapeDtypeStruct(shp, F32))
        out_specs.append(pl.BlockSpec(shp, lambda i, n=len(shp): (0,) * n))
    ni, nc, no = len(ins), len(consts), len(outs)

    def body(*refs):
        vals_in = [r[...] for r in refs[:ni]]
        vals_in = [v.astype(F32) if v.dtype == BF16 else v for v in vals_in]
        vals = fn(*vals_in, *[r[...] for r in refs[ni:ni + nc]])
        if not isinstance(vals, (tuple, list)):
            vals = (vals,)
        o_refs = refs[ni + nc:]
        for r, v in zip(o_refs[:no], vals[:no]):
            r[...] = v.astype(r.dtype)
        i = pl.program_id(0)
        for r, v in zip(o_refs[no:], vals[no:]):
            @pl.when(i == 0)
            def _(r=r, v=v):
                r[...] = v

            @pl.when(i > 0)
            def _(r=r, v=v):
                r[...] += v

    return _call(name, body, grid, in_specs, out_specs, out_shape, arrays, sem=("arbitrary",), comm=comm)


def _mm(name, a, b, tb=False, out_dtype=F32, tm=512, tn=3072, a2=None, b2=None, comm=None):
    M, K = a.shape
    N = b.shape[0] if tb else b.shape[1]
    tm, tn = _tile(M, tm, 8), _tile(N, tn)
    dims = (((1,), (1,)), ((), ())) if tb else (((1,), (0,)), ((), ()))
    pairs = [(a, b)] + ([(a2, b2)] if a2 is not None else [])

    def body(*refs):
        o_ref = refs[-1]
        acc = None
        for i in range(len(pairs)):
            p = lax.dot_general(refs[2 * i][...].astype(BF16), refs[2 * i + 1][...].astype(BF16), dims,
                                preferred_element_type=F32)
            acc = p if acc is None else acc + p
        o_ref[...] = acc.astype(o_ref.dtype)

    in_specs, arrays = [], []
    for x, y in pairs:
        k = x.shape[1]
        in_specs.append(pl.BlockSpec((tm, k), lambda j, i: (i, 0)))
        in_specs.append(pl.BlockSpec((tn, k), lambda j, i: (j, 0)) if tb else pl.BlockSpec((k, tn), lambda j, i: (0, j)))
        arrays += [x, y]
    return _call(name, body, (N // tn, M // tm), in_specs, [pl.BlockSpec((tm, tn), lambda j, i: (i, j))],
                 [jax.ShapeDtypeStruct((M, N), out_dtype)], arrays, sem=("parallel", "parallel"), comm=comm)[0]


def _mm_t(name, a, g, tk=1408, tn=1408, tt=2048):
    T, K = a.shape
    N = g.shape[1]
    tk, tn, tt = _tile(K, tk), _tile(N, tn), _tile(T, tt, 8)

    def body(a_ref, g_ref, o_ref):
        p = lax.dot_general(a_ref[...].astype(BF16), g_ref[...].astype(BF16), (((0,), (0,)), ((), ())),
                            preferred_element_type=F32)
        t = pl.program_id(2)

        @pl.when(t == 0)
        def _():
            o_ref[...] = p

        @pl.when(t > 0)
        def _():
            o_ref[...] += p

    return _call(name, body, (K // tk, N // tn, T // tt),
                 [pl.BlockSpec((tt, tk), lambda i, j, t: (t, i)), pl.BlockSpec((tt, tn), lambda i, j, t: (t, j))],
                 [pl.BlockSpec((tk, tn), lambda i, j, t: (i, j))], [jax.ShapeDtypeStruct((K, N), F32)], [a, g],
                 sem=("parallel", "parallel", "arbitrary"))[0]


def _dot(a, b):
    return jnp.dot(a.astype(BF16), b.astype(BF16), preferred_element_type=F32)


def _dot_nt(a, b):
    return lax.dot_general(a.astype(BF16), b.astype(BF16), (((1,), (1,)), ((), ())), preferred_element_type=F32)


def _dot_tn(a, b):
    return lax.dot_general(a.astype(BF16), b.astype(BF16), (((0,), (0,)), ((), ())), preferred_element_type=F32)


def _sigmoid(x):
    return 0.5 * jnp.tanh(0.5 * x) + 0.5


def _ffn_up(h, wg, wu, comm=None):
    M, K = h.shape
    N = wg.shape[1]
    tm, tn = _tile(M, 512, 8), _tile(N, 1408)

    def body(h_ref, wg_ref, wu_ref, a_ref, b_ref, act_ref):
        hb = h_ref[...]
        a = jnp.dot(hb, wg_ref[...], preferred_element_type=F32)
        b = jnp.dot(hb, wu_ref[...], preferred_element_type=F32)
        a_ref[...] = a.astype(BF16)
        b_ref[...] = b.astype(BF16)
        act_ref[...] = (a * _sigmoid(a) * b).astype(BF16)

    wspec = pl.BlockSpec((K, tn), lambda j, i: (0, j))
    ospec = pl.BlockSpec((tm, tn), lambda j, i: (i, j))
    return _call("ffn_up", body, (N // tn, M // tm), [pl.BlockSpec((tm, K), lambda j, i: (i, 0)), wspec, wspec],
                 [ospec] * 3, [jax.ShapeDtypeStruct((M, N), BF16)] * 3, [h, wg, wu], sem=("parallel", "parallel"),
                 comm=comm)


def _ffn_down_dx(df, wdn, a, b, comm=None):
    M, K = df.shape
    N = wdn.shape[0]
    tm, tn = _tile(M, 512, 8), _tile(N, 1408)

    def body(df_ref, w_ref, a_ref, b_ref, da_ref, db_ref):
        d = _dot_nt(df_ref[...], w_ref[...])
        av, bv = a_ref[...].astype(F32), b_ref[...].astype(F32)
        s = _sigmoid(av)
        da_ref[...] = (d * bv * (s + av * s * (1.0 - s))).astype(BF16)
        db_ref[...] = (d * av * s).astype(BF16)

    tspec = pl.BlockSpec((tm, tn), lambda j, i: (i, j))
    return _call("ffn_down_dx", body, (N // tn, M // tm),
                 [pl.BlockSpec((tm, K), lambda j, i: (i, 0)), pl.BlockSpec((tn, K), lambda j, i: (j, 0)), tspec, tspec],
                 [tspec] * 2, [jax.ShapeDtypeStruct((M, N), BF16)] * 2, [df, wdn, a, b], sem=("parallel", "parallel"),
                 comm=comm)


def _mm_norm(name, a, b, x, g, g_next=None):
    M, K = a.shape
    N = b.shape[1]
    tm = _tile(M, 512, 8)
    more = g_next is not None

    def body(*refs):
        a_ref, b_ref, x_ref, g_ref = refs[:4]
        y_ref, xn_ref = refs[4 + more], refs[5 + more]
        y = jnp.dot(a_ref[...].astype(BF16), b_ref[...], preferred_element_type=F32)
        y_ref[...] = y
        xn = x_ref[...] + _rms(y, g_ref[...])
        xn_ref[...] = xn
        if more:
            refs[6 + more][...] = _rms(xn, refs[4][...]).astype(BF16)

    row = lambda n: pl.BlockSpec((tm, n), lambda i: (i, 0))
    whole = lambda z: pl.BlockSpec(z.shape, lambda i: (0, 0))
    arrays = [a, b, x, g] + ([g_next] if more else [])
    in_specs = [row(K), whole(b), row(N), whole(g)] + ([whole(g_next)] if more else [])
    out_shape = [jax.ShapeDtypeStruct((M, N), F32)] * 2 + ([jax.ShapeDtypeStruct((M, N), BF16)] if more else [])
    return _call(name, body, (M // tm,), in_specs, [row(N)] * (2 + more), out_shape, arrays, sem=("parallel",))


def _rms(x, g):
    r = lax.rsqrt(jnp.mean(x * x, axis=-1, keepdims=True) + EPS)
    return x * r * g


def _rms_bwd(x, g, dy):
    r = lax.rsqrt(jnp.mean(x * x, axis=-1, keepdims=True) + EPS)
    xh = x * r
    dxh = dy * g
    dx = r * (dxh - xh * jnp.mean(dxh * xh, axis=-1, keepdims=True))
    return dx, jnp.sum(dy * xh, axis=0, keepdims=True)


def _shift_down(x, k):
    t = lax.broadcasted_iota(jnp.int32, x.shape, 0)
    return jnp.where(t >= k, pltpu.roll(x, k, 0), 0.0)


def _shift_up(x, k):
    n = x.shape[0]
    t = lax.broadcasted_iota(jnp.int32, x.shape, 0)
    return jnp.where(t < n - k, pltpu.roll(x, n - k, 0), 0.0)


def _group_select(vals):
    lane = lax.broadcasted_iota(jnp.int32, vals[0].shape, 1) // GROUP
    out = vals[-1]
    for g in range(len(vals) - 2, -1, -1):
        out = jnp.where(lane == g, vals[g], out)
    return out


def _pool_count(shape):
    t = lax.broadcasted_iota(jnp.int32, shape, 0)
    lane = lax.broadcasted_iota(jnp.int32, shape, 1) // GROUP
    w = jnp.where(lane == 0, POOL_WINDOWS[0], jnp.where(lane == 1, POOL_WINDOWS[1],
                  jnp.where(lane == 2, POOL_WINDOWS[2], POOL_WINDOWS[3])))
    return jnp.minimum(t + 1, w).astype(F32)


def _pooled(z):
    s = [z]
    for k in (1, 2, 4, 8):
        s.append(s[-1] + _shift_down(s[-1], k))
    return _group_select(s[1:]) / _pool_count(z.shape) - z


def _pooled_bwd(dp):
    u = dp / _pool_count(dp.shape)
    s = [u]
    for k in (1, 2, 4, 8):
        s.append(s[-1] + _shift_up(s[-1], k))
    return _group_select(s[1:]) - dp


def _diag_mask():
    r = lax.broadcasted_iota(jnp.int32, (TQ, TQ), 0) // CHUNK
    c = lax.broadcasted_iota(jnp.int32, (TQ, TQ), 1) // CHUNK
    return r >= c


def _attn_fwd(q, k, v, B, S, comm=None):
    nq = S // TQ

    def body(q_ref, k_ref, v_ref, o_ref, lse_ref):
        mask = _diag_mask()
        for i in range(nq):
            r0, r1 = i * TQ, (i + 1) * TQ
            qb = q_ref[r0:r1, :]
            sd = jnp.where(mask, _dot_nt(qb, k_ref[r0:r1, :]), NEG_INF)
            m = jnp.max(sd, axis=-1, keepdims=True)
            if i:
                so = _dot_nt(qb, k_ref[0:r0, :])
                m = jnp.maximum(m, jnp.max(so, axis=-1, keepdims=True))
            pd = jnp.exp(sd - m)
            l = jnp.sum(pd, axis=-1, keepdims=True)
            acc = _dot(pd, v_ref[r0:r1, :])
            if i:
                po = jnp.exp(so - m)
                l = l + jnp.sum(po, axis=-1, keepdims=True)
                acc = acc + _dot(po, v_ref[0:r0, :])
            o_ref[r0:r1, :] = acc / l
            lse_ref[r0:r1, :] = jnp.broadcast_to(m + jnp.log(l), (TQ, HP))

    spec = pl.BlockSpec((S, HP), lambda b, h: (b, h))
    return _call("attn_fwd", body, (B, HEADS), [spec] * 3, [spec] * 2, [jax.ShapeDtypeStruct((B * S, QW), F32)] * 2,
                 [q, k, v], sem=("parallel", "parallel"), comm=comm)


def _attn_bwd(q, k, v, o, do, lse, B, S, comm=None):
    nq = S // TQ

    def body(q_ref, k_ref, v_ref, o_ref, do_ref, lse_ref, dq_ref, dk_ref, dv_ref, dk_acc, dv_acc):
        mask = _diag_mask()
        dk_acc[...] = jnp.zeros_like(dk_acc)
        dv_acc[...] = jnp.zeros_like(dv_acc)
        for i in range(nq):
            r0, r1 = i * TQ, (i + 1) * TQ
            qb, dob = q_ref[r0:r1, :], do_ref[r0:r1, :]
            delta = jnp.sum(o_ref[r0:r1, :] * dob, axis=-1, keepdims=True)
            lse_b = lse_ref[r0:r1, :][:, :1]
            dq = None
            for c0, c1, diag in ([(0, r0, False)] if i else []) + [(r0, r1, True)]:
                kb, vb = k_ref[c0:c1, :], v_ref[c0:c1, :]
                p = jnp.exp(_dot_nt(qb, kb) - lse_b)
                if diag:
                    p = jnp.where(mask, p, 0.0)
                ds = p * (_dot_nt(dob, vb) - delta)
                part = _dot(ds, kb)
                dq = part if dq is None else dq + part
                dk_acc[c0:c1, :] += _dot_tn(ds, qb)
                dv_acc[c0:c1, :] += _dot_tn(p, dob)
            dq_ref[r0:r1, :] = dq
        dk_ref[...] = dk_acc[...].astype(BF16)
        dv_ref[...] = dv_acc[...].astype(BF16)

    spec = pl.BlockSpec((S, HP), lambda b, h: (b, h))
    return _call("attn_bwd", body, (B, HEADS), [spec] * 6, [spec] * 3,
                 [jax.ShapeDtypeStruct((B * S, QW), F32)] + [jax.ShapeDtypeStruct((B * S, QW), BF16)] * 2,
                 [q, k, v, o, do, lse], scratch=[pltpu.VMEM((S, HP), F32)] * 2, sem=("parallel", "parallel"), comm=comm)


def _all_gather(name, xs):
    n = len(xs)

    def body(*refs):
        x_refs, out_refs = refs[:n], refs[n:2 * n]
        send_sems, recv_sems, local_sems = refs[2 * n:]
        x_, y_, c_ = lax.axis_index("x"), lax.axis_index("y"), lax.axis_index("c")
        me, sibling = (x_, y_, c_), (x_, y_, 1 - c_)
        chips = [(1 - x_, y_), (x_, 1 - y_), (1 - x_, 1 - y_)]

        def copy(a, k, block, to, own=False):
            slot = out_refs[a].at[4 * block[0] + 2 * block[1] + block[2]]
            return pltpu.make_async_remote_copy(
                src_ref=x_refs[a] if own else slot, dst_ref=slot,
                send_sem=send_sems.at[7 * a + k], recv_sem=recv_sems.at[7 * a + k], device_id=to, device_id_type=MESH)

        mine = [pltpu.make_async_copy(x_refs[a], out_refs[a].at[4 * x_ + 2 * y_ + c_], local_sems.at[a]) for a in range(n)]
        for cp in mine:
            cp.start()
        first = [copy(a, 1 + j, me, (*chip, c_), own=True) for j, chip in enumerate(chips) for a in range(n)]
        first += [copy(a, 0, me, sibling, own=True) for a in range(n)]
        for cp in first:
            cp.start()
        passed = []
        for j, chip in enumerate(chips):
            for a in range(n):
                copy(a, 1 + j, (*chip, c_), me).wait_recv()
                passed.append(copy(a, 4 + j, (*chip, c_), sibling))
                passed[-1].start()
        for a in range(n):
            copy(a, 0, sibling, me).wait_recv()
            for j, chip in enumerate(chips):
                copy(a, 4 + j, (*chip, 1 - c_), me).wait_recv()
        for cp in first + passed:
            cp.wait_send()
        for cp in mine:
            cp.wait()

    return pl.pallas_call(
        body, name=name, out_shape=[jax.ShapeDtypeStruct((N_DEV,) + x.shape, x.dtype) for x in xs],
        in_specs=[pl.BlockSpec(memory_space=pl.ANY)] * n, out_specs=[pl.BlockSpec(memory_space=pl.ANY)] * n,
        scratch_shapes=[pltpu.SemaphoreType.DMA((7 * n,)), pltpu.SemaphoreType.DMA((7 * n,)), pltpu.SemaphoreType.DMA((n,))],
    )(*xs)


def _exchange_alone(name, comm):
    def body(x_ref, o_ref):
        o_ref[...] = x_ref[...]

    spec = pl.BlockSpec((8, 128), lambda i: (0, 0))
    _call(name, body, (1,), [spec], [spec], [jax.ShapeDtypeStruct((8, 128), F32)], [jnp.zeros((8, 128), F32)], comm=comm)
    return comm.results


def _adamw(name, parts, w, m, v, row_off=0, layers=1, stride=0):
    R, C = w.shape
    r = R // layers
    assert parts.shape[2] == C and parts.shape[1] >= (layers - 1) * stride + row_off + r
    if r % 8 == 0:
        tr = max(t for t in range(8, 513, 8) if r % t == 0 and row_off % t == 0 and stride % t == 0)
    else:
        assert layers == 1
        tr = r
    nb = r // tr

    def body(p_ref, w_ref, m_ref, v_ref, g_ref, d_ref, nm_ref, nv_ref):
        g = p_ref[0].astype(F32)
        for j in range(1, N_DEV):
            g = g + p_ref[j].astype(F32)
        m_new = ADAM_B1 * m_ref[...] + (1.0 - ADAM_B1) * g
        v_new = ADAM_B2 * v_ref[...] + (1.0 - ADAM_B2) * (g * g)
        m_hat = m_new / (1.0 - ADAM_B1 ** ADAM_STEP)
        v_hat = v_new / (1.0 - ADAM_B2 ** ADAM_STEP)
        g_ref[...] = g
        d_ref[...] = -ADAM_LR * (m_hat / (jnp.sqrt(v_hat) + ADAM_EPS) + ADAM_WD * w_ref[...])
        nm_ref[...] = m_new
        nv_ref[...] = v_new

    spec = pl.BlockSpec((tr, C), lambda l, i: (l * nb + i, 0))
    pspec = pl.BlockSpec((N_DEV, tr, C), lambda l, i: (0, (l * stride + row_off) // tr + i, 0))
    return _call(name, body, (layers, nb), [pspec, spec, spec, spec], [spec] * 4,
                 [jax.ShapeDtypeStruct((R, C), F32)] * 4, [parts, w, m, v], sem=("parallel", "parallel"))


def _rot_cols(w):
    h = w.shape[-1] // 2
    return jnp.concatenate([-w[..., h:], w[..., :h]], axis=-1)


def _unrot_cols(d):
    h = d.shape[-1] // 2
    return jnp.concatenate([d[..., h:], -d[..., :h]], axis=-1)


def _split_in(w):
    out, o = [], 0
    for n in IN_SIZES:
        out.append(w[..., o:o + n])
        o += n
    return out


def _layer_weights(G):
    L = G['w_in'].shape[0]
    cq, ckv, kr, pin, su, sv, cvb, cvc, cvx, gate = _split_in(G['w_in'])
    W1s = jnp.concatenate([cq, ckv, kr, _rot_cols(kr), jnp.zeros((L, D, 64), BF16), pin, su, sv, cvb, cvc, cvx], axis=-1)
    wq = G['w_uq'].reshape(L, Q_LORA, HEADS, QK_NOPE + QK_ROPE)
    nope, rope = wq[..., :QK_NOPE], wq[..., QK_NOPE:]
    z32 = jnp.zeros((L, Q_LORA, HEADS, HP - QK_NOPE - QK_ROPE), BF16)
    wq_a = jnp.concatenate([nope, rope, z32], axis=-1).reshape(L, Q_LORA, QW)
    wq_b = jnp.concatenate([jnp.zeros_like(nope), _rot_cols(rope), z32], axis=-1).reshape(L, Q_LORA, QW)
    wkv = G['w_ukv'].reshape(L, KV_LORA, HEADS, QK_NOPE + V_HEAD)
    z64 = jnp.zeros((L, KV_LORA, HEADS, HP - QK_NOPE), BF16)
    wkn = jnp.concatenate([wkv[..., :QK_NOPE], z64], axis=-1).reshape(L, KV_LORA, QW)
    wv = jnp.concatenate([wkv[..., QK_NOPE:], z64], axis=-1).reshape(L, KV_LORA, QW)
    wa = G['w_br_a'].reshape(L, HEADS, V_HEAD, D)
    wa = jnp.concatenate([wa, jnp.zeros((L, HEADS, HP - V_HEAD, D), BF16)], axis=2).reshape(L, QW, D)
    return dict(
        W1=jnp.concatenate([W1s, gate], axis=-1), W1s=W1s, W1g=gate, Wq=jnp.concatenate([wq_a, wq_b], axis=-1),
        Wkv=jnp.concatenate([wkn, wv], axis=-1), Wa=wa, Wb=G['w_br_b'], Wc=G['w_br_c'], Wd=G['w_br_d'],
        conv=G['conv_w'].reshape(L, 3, BR).astype(F32))


def _tables(S):
    inv = ROPE_THETA ** (-jnp.arange(0, QK_ROPE, 2, dtype=F32) / QK_ROPE)
    ang = jnp.arange(S, dtype=F32)[:, None] * inv[None, :]
    cos, sin = jnp.cos(ang), jnp.sin(ang)
    scale = (QK_NOPE + QK_ROPE) ** -0.5
    one, zero = jnp.ones((S, QK_NOPE), F32), jnp.zeros((S, QK_NOPE), F32)
    z32 = jnp.zeros((S, HP - QK_NOPE - QK_ROPE), F32)
    cq = jnp.concatenate([one, cos, cos, z32], axis=1) * scale
    sq = jnp.concatenate([zero, sin, sin, z32], axis=1) * scale
    ck = jnp.concatenate([cos, cos, sin, sin, jnp.zeros((S, HP - 2 * QK_ROPE), F32)], axis=1)
    return cq, sq, ck


def _place_k():
    p = np.zeros((HP, QW), np.float32)
    for r in range(2 * QK_ROPE):
        for h in range(HEADS):
            p[r, h * HP + QK_NOPE + r % QK_ROPE] = 1.0
    return jnp.asarray(p, BF16)


def _layer_fwd(x, h, W, late, P, tabs, B, S, ride, g_next):
    cq_t, sq_t, ck_t, place = tabs
    g_cq, g_ckv = P['g_cq'], P['g_ckv']
    P1 = _mm("in_proj", h, W['W1'], out_dtype=BF16, comm=ride.get('in_proj'))

    def q_prep(c, cq, sq, g, w):
        qq = _dot(_rms(c, g), w)
        return qq[:, :QW] * jnp.tile(cq, (1, HEADS)) + qq[:, QW:] * jnp.tile(sq, (1, HEADS))
    q = _rw("q_prep", q_prep, [win(P1, 256, 0), tab(cq_t), tab(sq_t)], [g_cq, W['Wq']], [(QW, BF16)])[0]

    def kv_prep(c, seg, ck, g, w, place):
        kv = _dot(_rms(c, g), w)
        return kv[:, :QW] + _dot(seg * ck, place), kv[:, QW:]
    k, v = _rw("kv_prep", kv_prep, [win(P1, 128, 2), win(P1, 128, 3), tab(ck_t)], [g_ckv, W['Wkv'], place],
               [(QW, BF16), (QW, BF16)])
    o, lse = _attn_fwd(q, k, v, B, S, comm=ride.get('attn_fwd'))

    def pool_f(z, wbd, scale):
        return _dot(_pooled(z), wbd) * scale
    bo = _rw("pool_fwd", pool_f, [win(P1, BR, 2)], [P['pool_bd'], P['pool_scale']], [(BR, BF16)], tile=S)[0]

    def sgu_f(u, sv, g, wm, bias):
        return u * _sgu_mix(_rms(sv, g), wm, bias)
    co = _rw("sgu_fwd", sgu_f, [win(P1, BR, 3), win(P1, BR, 4)], [P['g_sgu_v'], P['sgu_wm'], P['sgu_bias']],
             [(BR, BF16)], tile=SGU_BLOCK)[0]

    def conv_f(b, c, xi, w):
        z = c * xi
        return b * (w[0:1] * _shift_down(z, 2) + w[1:2] * _shift_down(z, 1) + w[2:3] * z)
    do_ = _rw("conv_fwd", conv_f, [win(P1, BR, 5), win(P1, BR, 6), win(P1, BR, 7)], [W['conv']], [(BR, BF16)], tile=S)[0]

    def merge_f(a, b, c, d, l0, l1, l2, l3, wa, wb, wc, wd):
        return (_sigmoid(l0) * _dot(a, wa) + _sigmoid(l1) * _dot(b, wb) + _sigmoid(l2) * _dot(c, wc)
                + _sigmoid(l3) * _dot(d, wd))
    merged = _rw("merge_fwd", merge_f, [o, bo, co, do_] + [win(P1, D, 2 + i) for i in range(4)],
                 [W['Wa'], W['Wb'], W['Wc'], W['Wd']], [(D, BF16)])[0]
    W = dict(W, **late())
    om, x1, h2 = _mm_norm("out_proj", merged, W['Wout'], x, P['g_post_mix'], P['g_pre_ffn'])
    fa, fb, act = _ffn_up(h2, W['Wg'], W['Wu'], comm=ride.get('ffn_up'))
    f, x2, *h_next = _mm_norm("ffn_down", act, W['Wdn'], x1, P['g_post_ffn'], g_next)
    saved = dict(x=x, h=h, P1=P1, q=q, k=k, v=v, o=o, lse=lse, bo=bo, co=co, do=do_, merged=merged, om=om, x1=x1,
                 h2=h2, fa=fa, fb=fb, act=act, f=f)
    return x2, (h_next[0] if h_next else None), saved, W


def _sgu_mix(vn, wm, bias):
    outs = [_dot(wm[g], vn) for g in range(4)]
    return _group_select(outs) + bias


def _layer_bwd(dx2, df, sv, W, P, tabs, B, S, ride, own_ride, prev):
    cq_t, sq_t, ck_t, place = tabs
    P1 = sv['P1']
    grads = {}
    da, db = _ffn_down_dx(df, W['Wdn'], sv['fa'], sv['fb'], comm=ride.get('ffn_down_dx'))
    grads['Wdn'] = _mm_t("ffn_down_dw", sv['act'], df)
    dh2 = _mm("ffn_up_dx", da, W['Wg'], tb=True, a2=db, b2=W['Wu'], comm=ride.get('ffn_up_dx'))
    grads['Wg'] = _mm_t("ffn_gate_dw", sv['h2'], da)
    grads['Wu'] = _mm_t("ffn_up_dw", sv['h2'], db)

    def mid_norms_b(x1, dh, dy, om, g_ffn, g_mix):
        dx, dg_ffn = _rms_bwd(x1, g_ffn, dh)
        dx1 = dy + dx
        dom, dg_mix = _rms_bwd(om, g_mix, dx1)
        return dx1, dom, dg_ffn, dg_mix
    dx1, dom, grads['g_pre_ffn'], grads['g_post_mix'] = _rw(
        "mid_norms_bwd", mid_norms_b, [sv['x1'], dh2, dx2, sv['om']], [P['g_pre_ffn'], P['g_post_mix']],
        [(D, F32), (D, BF16)], [(1, D), (1, D)], tile=512)
    dmerged = _mm("out_proj_dx", dom, W['Wout'], tb=True)
    grads['Wout'] = _mm_t("out_proj_dw", sv['merged'], dom)
    ride = dict(ride, **own_ride(grads))

    def merge_b(a, b, c, d, l0, l1, l2, l3, dm, wa, wb, wc, wd):
        outs_dl, outs_dy, outs_dbr = [], [], []
        for br, l, w in ((a, l0, wa), (b, l1, wb), (c, l2, wc), (d, l3, wd)):
            s = _sigmoid(l)
            y = _dot(br, w)
            dy = (dm * s).astype(BF16)
            outs_dl.append((dm * y * s * (1.0 - s)).astype(BF16))
            outs_dy.append(dy)
            outs_dbr.append(_dot_nt(dy, w))
        return (jnp.concatenate(outs_dl, axis=1), *outs_dy, *outs_dbr)
    mb = _rw("merge_bwd", merge_b,
             [sv['o'], sv['bo'], sv['co'], sv['do']] + [win(P1, D, 2 + i) for i in range(4)] + [dmerged],
             [W['Wa'], W['Wb'], W['Wc'], W['Wd']],
             [(4 * D, BF16)] + [(D, BF16)] * 4 + [(QW, F32), (BR, F32), (BR, F32), (BR, F32)], comm=ride.get('merge_bwd'))
    dl, dy, (d_o, d_bo, d_co, d_do) = mb[0], mb[1:5], mb[5:9]
    grads['Wa'] = _mm_t("br_a_dw", sv['o'], dy[0])
    grads['Wb'] = _mm_t("br_b_dw", sv['bo'], dy[1])
    grads['Wc'] = _mm_t("br_c_dw", sv['co'], dy[2])
    grads['Wd'] = _mm_t("br_d_dw", sv['do'], dy[3])

    def conv_b(b, c, xi, dout, w):
        z = c * xi
        z1, z2 = _shift_down(z, 1), _shift_down(z, 2)
        y = w[0:1] * z2 + w[1:2] * z1 + w[2:3] * z
        dyv = dout * b
        dz = w[2:3] * dyv + w[1:2] * _shift_up(dyv, 1) + w[0:1] * _shift_up(dyv, 2)
        return (dout * y, dz * xi, dz * c, jnp.sum(dyv * z2, axis=0, keepdims=True),
                jnp.sum(dyv * z1, axis=0, keepdims=True), jnp.sum(dyv * z, axis=0, keepdims=True))
    dcvb, dcvc, dcvx, dw0, dw1, dw2 = _rw("conv_bwd", conv_b, [win(P1, BR, 5), win(P1, BR, 6), win(P1, BR, 7), d_do],
                                          [W['conv']], [(BR, BF16)] * 3, [(1, BR)] * 3, tile=S)
    grads['conv'] = jnp.concatenate([dw0, dw1, dw2], axis=0)

    def sgu_b(u, s_v, dout, g, wm, bias):
        vn = _rms(s_v, g)
        mixed = _sgu_mix(vn, wm, bias)
        dmix = dout * u
        dvn = _group_select([_dot_tn(wm[gi], dmix) for gi in range(4)])
        dsv, dg = _rms_bwd(s_v, g, dvn)
        lane = lax.broadcasted_iota(jnp.int32, dmix.shape, 1) // GROUP
        dwm = [_dot_nt(jnp.where(lane == gi, dmix, 0.0), vn) for gi in range(4)]
        return (dout * mixed, dsv, dg, *dwm, dmix)
    dsu, dsv_, grads['g_sgu_v'], wm0, wm1, wm2, wm3, grads['sgu_bias'] = _rw(
        "sgu_bwd", sgu_b, [win(P1, BR, 3), win(P1, BR, 4), d_co], [P['g_sgu_v'], P['sgu_wm'], P['sgu_bias']],
        [(BR, BF16)] * 2, [(1, BR)] + [(SGU_BLOCK, SGU_BLOCK)] * 4 + [(SGU_BLOCK, BR)], tile=SGU_BLOCK)
    grads['sgu_wm'] = jnp.stack([wm0, wm1, wm2, wm3])

    def pool_b(z, dout, wbd, scale):
        pooled = _pooled(z)
        mixed = _dot(pooled, wbd)
        dmix = dout * scale
        dz = _pooled_bwd(_dot_nt(dmix, wbd))
        return dz, _dot_tn(pooled, dmix), jnp.sum(dout * mixed, axis=0, keepdims=True)
    dpin, grads['pool_bd'], grads['pool_scale'] = _rw(
        "pool_bwd", pool_b, [win(P1, BR, 2), d_bo], [P['pool_bd'], P['pool_scale']], [(BR, BF16)], [(BR, BR), (1, BR)], tile=S)

    dq, dk, dv = _attn_bwd(sv['q'], sv['k'], sv['v'], sv['o'], d_o, sv['lse'], B, S, comm=ride.get('attn_bwd'))

    def q_prep_b(c, dq, cq, sq, g, w):
        dqq = jnp.concatenate([dq * jnp.tile(cq, (1, HEADS)), dq * jnp.tile(sq, (1, HEADS))], axis=1).astype(BF16)
        dc, dg = _rms_bwd(c, g, _dot_nt(dqq, w))
        return dc, _rms(c, g), dqq, dg
    dcq, cqn, dqq, grads['g_cq'] = _rw("q_prep_bwd", q_prep_b, [win(P1, 256, 0), dq, tab(cq_t), tab(sq_t)],
                                       [P['g_cq'], W['Wq']], [(Q_LORA, BF16), (Q_LORA, BF16), (2 * QW, BF16)], [(1, Q_LORA)])
    grads['Wq'] = _mm_t("uq_dw", cqn, dqq)

    def kv_prep_b(c, dk, dv, ck, g, w, place):
        dkv = jnp.concatenate([dk, dv], axis=1)
        dc, dg = _rms_bwd(c, g, _dot_nt(dkv, w))
        return dc, _dot_nt(dk, place) * ck, _rms(c, g), dkv, dg
    dckv, dseg, kvn, dkv, grads['g_ckv'] = _rw(
        "kv_prep_bwd", kv_prep_b, [win(P1, 128, 2), dk, dv, tab(ck_t)], [P['g_ckv'], W['Wkv'], place],
        [(KV_LORA, BF16), (HP, BF16), (KV_LORA, BF16), (2 * QW, BF16)], [(1, KV_LORA)])
    grads['Wkv'] = _mm_t("ukv_dw", kvn, dkv)

    dps = jnp.concatenate([dcq, dckv, dseg, dpin, dsu, dsv_, dcvb, dcvc, dcvx], axis=1)
    dh = _mm("in_proj_dx", dps, W['W1s'], tb=True, a2=dl, b2=W['W1g'], tm=256)
    grads['W1s'] = _mm_t("in_proj_s_dw", sv['h'], dps)
    grads['W1g'] = _mm_t("in_proj_g_dw", sv['h'], dl)

    if prev is None:
        def pre_mix_b(x, dh, dy, g):
            dx, dg = _rms_bwd(x, g, dh)
            return dy + dx, dg
        dx, grads['g_pre_mix'] = _rw("pre_mix_bwd", pre_mix_b, [sv['x'], dh, dx1], [P['g_pre_mix']], [(D, F32)], [(1, D)], tile=512)
        return dx, None, None, grads, ride

    def edge_norms_b(x, dh, dy, f, g_mix, g_ffn):
        dx, dg_mix = _rms_bwd(x, g_mix, dh)
        dx = dy + dx
        df, dg_ffn = _rms_bwd(f, g_ffn, dx)
        return dx, df, dg_mix, dg_ffn
    dx, df_prev, grads['g_pre_mix'], dg_prev = _rw(
        "edge_norms_bwd", edge_norms_b, [sv['x'], dh, dx1, prev[0]], [P['g_pre_mix'], prev[1]],
        [(D, F32), (D, BF16)], [(1, D), (1, D)], tile=512)
    return dx, df_prev, dg_prev, grads, ride


def _param_grads(gl):
    st = {k: jnp.stack([g[k] for g in gl]) for k in gl[0]}
    L = len(gl)
    w1 = st['W1s']
    seg = w1[..., C_KR:C_KR + 128]
    dkr = seg[..., :QK_ROPE] + _unrot_cols(seg[..., QK_ROPE:2 * QK_ROPE])
    out = {}
    out['w_in'] = jnp.concatenate([w1[..., :C_KR], dkr, w1[..., C_PIN:], st['W1g']], axis=-1)
    qa = st['Wq'][..., :QW].reshape(L, Q_LORA, HEADS, HP)
    qb = st['Wq'][..., QW:].reshape(L, Q_LORA, HEADS, HP)
    drope = qa[..., QK_NOPE:QK_NOPE + QK_ROPE] + _unrot_cols(qb[..., QK_NOPE:QK_NOPE + QK_ROPE])
    out['w_uq'] = jnp.concatenate([qa[..., :QK_NOPE], drope], axis=-1).reshape(L, Q_LORA, HEADS * (QK_NOPE + QK_ROPE))
    kn = st['Wkv'][..., :QW].reshape(L, KV_LORA, HEADS, HP)[..., :QK_NOPE]
    vv = st['Wkv'][..., QW:].reshape(L, KV_LORA, HEADS, HP)[..., :V_HEAD]
    out['w_ukv'] = jnp.concatenate([kn, vv], axis=-1).reshape(L, KV_LORA, HEADS * (QK_NOPE + V_HEAD))
    out['w_br_a'] = st['Wa'].reshape(L, HEADS, HP, D)[:, :, :V_HEAD].reshape(L, HEADS * V_HEAD, D)
    out['w_br_b'], out['w_br_c'], out['w_br_d'] = st['Wb'], st['Wc'], st['Wd']
    out['conv_w'] = st['conv'].reshape(L, 3, 1, BR)
    for n in ('g_pre_mix', 'g_cq', 'g_ckv', 'g_sgu_v', 'g_post_mix', 'g_pre_ffn', 'g_post_ffn', 'pool_scale'):
        out[n] = st[n].reshape(L, -1)
    bd = st['pool_bd'].reshape(L, 4, GROUP, 4, GROUP)
    out['pool_w'] = jnp.stack([bd[:, g, :, g, :] for g in range(4)], axis=1)
    out['sgu_w'] = st['sgu_wm'] * _sgu_mask()[None, None]
    out['sgu_b'] = st['sgu_bias'].reshape(L, SGU_BLOCK, 4, GROUP).sum(-1).transpose(0, 2, 1)
    return out


def _sgu_mask():
    pc = np.arange(SGU_BLOCK) // CHUNK
    return jnp.asarray(pc[:, None] >= pc[None, :], F32)


def _rows(a, lead):
    return a.reshape(a.shape[:lead] + (-1, a.shape[-1]))


def _pad_to(a, rows, cols):
    pad = [(0, 0)] * (a.ndim - 2) + [(0, rows - a.shape[-2]), (0, cols - a.shape[-1])]
    return jnp.pad(a, pad)


GROUPS = (('w_in',), ('w_ffn_gate', 'w_ffn_up'), ('w_out', 'w_ffn_down'),
          ('w_uq', 'w_ukv', 'w_br_a', 'w_br_b', 'w_br_c', 'w_br_d', 'conv_w'))
PADDED = {'w_uq': (Q_LORA, 128), 'conv_w': (128, 128), 'w_ffn_down': (384, D)}
NARROW = ('w_uq', 'conv_w')


def _group(t, lead, groups):
    bufs = {}
    for b in groups:
        ps = [_rows(t[n], lead) for n in GROUPS[b]]
        ps = [_pad_to(p, *PADDED[n]) if n in PADDED else p for n, p in zip(GROUPS[b], ps)]
        bufs[b] = ps[0] if len(ps) == 1 else jnp.concatenate(ps, axis=-2)
    return bufs


def _group_rows(shapes):
    where, rows = {}, []
    for b, names in enumerate(GROUPS):
        off = 0
        for n in names:
            where[n] = (b, off)
            off += PADDED.get(n, shapes[n])[0]
        rows.append(off)
    return where, rows


def _shard_split(a, axis):
    n = a.shape[axis]
    a = a.reshape(a.shape[:axis] + (N_DEV, n // N_DEV) + a.shape[axis + 1:])
    return jnp.moveaxis(a, axis, 0)


def _shard_join(a, axis):
    a = jnp.moveaxis(a, 0, axis)
    return a.reshape(a.shape[:axis] + (a.shape[axis] * a.shape[axis + 1],) + a.shape[axis + 2:])


def _as2d(a):
    return a.reshape(-1, a.shape[-1])


def kernel(x, w_in, g_pre_mix, g_cq, g_ckv, w_uq, w_ukv, pool_w, pool_scale, g_sgu_v, sgu_w, sgu_b, conv_w, w_br_a, w_br_b, w_br_c, w_br_d, w_out, g_post_mix, g_pre_ffn, w_ffn_gate, w_ffn_up, w_ffn_down, g_post_ffn, loss_target, m_w_in, m_g_pre_mix, m_g_cq, m_g_ckv, m_w_uq, m_w_ukv, m_pool_w, m_pool_scale, m_g_sgu_v, m_sgu_w, m_sgu_b, m_conv_w, m_w_br_a, m_w_br_b, m_w_br_c, m_w_br_d, m_w_out, m_g_post_mix, m_g_pre_ffn, m_w_ffn_gate, m_w_ffn_up, m_w_ffn_down, m_g_post_ffn, v_w_in, v_g_pre_mix, v_g_cq, v_g_ckv, v_w_uq, v_w_ukv, v_pool_w, v_pool_scale, v_g_sgu_v, v_sgu_w, v_sgu_b, v_conv_w, v_w_br_a, v_w_br_b, v_w_br_c, v_w_br_d, v_w_out, v_g_post_mix, v_g_pre_ffn, v_w_ffn_gate, v_w_ffn_up, v_w_ffn_down, v_g_post_ffn):
    args = dict(locals())
    w = {n: args[n] for n in WEIGHTS}
    m = {n: args['m_' + n] for n in WEIGHTS}
    v = {n: args['v_' + n] for n in WEIGHTS}
    B, S, _ = x.shape
    T = B * S
    L = DEPTH
    names = list(SHARDED)
    shapes1 = {n: _as2d(w[n][0]).shape for n in names}
    where, group_rows = _group_rows(shapes1)

    def shard_bufs(l, groups):
        return _group({n: w[n][l:l + 1].astype(BF16) for b in groups for n in GROUPS[b]}, 0, groups)

    def joined(gathered):
        G = {}
        for b, buf in gathered.items():
            for n in GROUPS[b]:
                off = where[n][1]
                r, c = shapes1[n]
                G[n] = _shard_join(buf[:, off:off + r, :c].reshape((N_DEV, 1) + w[n].shape[1:]), SHARDED[n])
        return G

    eye = jnp.eye(4, dtype=F32)
    pool_bd = (pool_w[:, :, :, None, :] * eye[None, :, None, :, None]).reshape(L, BR, BR).astype(BF16)
    sgu_wm = (sgu_w * _sgu_mask()[None, None]).astype(BF16)
    sgu_bias = jnp.repeat(sgu_b.transpose(0, 2, 1), GROUP, axis=2)
    tabs = _tables(S) + (_place_k(),)

    def layer_params(l):
        P = {n: w[n][l][None, :] for n in ('g_pre_mix', 'g_cq', 'g_ckv', 'g_sgu_v', 'g_post_mix', 'g_pre_ffn',
                                            'g_post_ffn', 'pool_scale')}
        P.update(pool_bd=pool_bd[l], sgu_wm=sgu_wm[l], sgu_bias=sgu_bias[l])
        return P

    xt = x.reshape(T, D)
    h = _rw("pre_mix", lambda x, g: _rms(x, g), [xt], [w['g_pre_mix'][0][None, :]], [(D, BF16)], tile=512)[0]
    saved, Ws = [], []
    first = shard_bufs(0, (0, 3))
    early = dict(zip((0, 3), _all_gather("gather_weights", [first[0], first[3]])))
    for l in range(L):
        own = shard_bufs(l, (1, 2))
        nxt = shard_bufs(l + 1, (0, 3)) if l + 1 < L else None
        ride = {'in_proj': _Gather([own[1]]), 'attn_fwd': _Gather([own[2]] + ([nxt[3]] if nxt else []))}
        if nxt:
            ride['ffn_up'] = _Gather([nxt[0]])

        def late(ride=ride):
            G = joined({1: ride['in_proj'].results[0], 2: ride['attn_fwd'].results[0]})
            return dict(Wout=G['w_out'][0], Wg=G['w_ffn_gate'][0], Wu=G['w_ffn_up'][0], Wdn=G['w_ffn_down'][0])

        We = {k: a[0] for k, a in _layer_weights(joined(early)).items()}
        g_next = w['g_pre_mix'][l + 1][None, :] if nxt else None
        xt, h, sv, W = _layer_fwd(xt, h, We, late, layer_params(l), tabs, B, S, ride, g_next)
        saved.append(sv)
        Ws.append(W)
        if nxt:
            early = {0: ride['ffn_up'].results[0], 3: ride['attn_fwd'].results[1]}

    def loss_f(y, t, f, g):
        e = y - t
        dy = e * (1.0 / D)
        df, dg = _rms_bwd(f, g, dy)
        return dy, df, jnp.sum(e * e, axis=0, keepdims=True), dg
    dy, df, sq, dg_ffn = _rw("loss", loss_f, [xt, loss_target.reshape(T, D), saved[-1]['f']], [w['g_post_ffn'][L - 1][None, :]],
                             [(D, F32), (D, BF16)], [(1, D), (1, D)], tile=512)
    loss = lax.psum(0.5 * jnp.sum(sq) / D, ("x", "y", "c"))

    recv = [lax.empty((N_DEV, L * r, PADDED.get(ns[0], shapes1[ns[0]])[1]), BF16) for ns, r in zip(GROUPS, group_rows)]

    def send_bufs(pg, groups):
        return _group({n: _shard_split(pg[n], SHARDED[n]).astype(BF16) for b in groups for n in GROUPS[b]}, 1, groups)

    small, pending = [None] * L, None
    for l in reversed(range(L)):
        ride = {}
        if pending is not None:
            ride = {'ffn_down_dx': _Scatter([pending[0]], [recv[0]], [(l + 1) * group_rows[0]]),
                    'ffn_up_dx': _Scatter([pending[3]], [recv[3]], [(l + 1) * group_rows[3]])}

        def own_ride(grads, l=l):
            bufs = send_bufs({'w_ffn_gate': grads['Wg'][None], 'w_ffn_up': grads['Wu'][None],
                              'w_ffn_down': grads['Wdn'][None], 'w_out': grads['Wout'][None]}, (1, 2))
            return {'merge_bwd': _Scatter([bufs[1]], [recv[1]], [l * group_rows[1]]),
                    'attn_bwd': _Scatter([bufs[2]], [recv[2]], [l * group_rows[2]])}

        prev = (saved[l - 1]['f'], w['g_post_ffn'][l - 1][None, :]) if l else None
        dy, df_prev, dg_prev, gl, ride = _layer_bwd(dy, df, saved[l], Ws[l], layer_params(l), tabs, B, S, ride, own_ride, prev)
        gl['g_post_ffn'] = dg_ffn
        df, dg_ffn = df_prev, dg_prev
        for b, name in ((0, 'ffn_down_dx'), (3, 'ffn_up_dx'), (1, 'merge_bwd'), (2, 'attn_bwd')):
            if name in ride:
                recv[b] = ride[name].results[0]
        pg = _param_grads([gl])
        pending = send_bufs(pg, (0, 3))
        rep = jnp.concatenate([pg[n].reshape(-1, 128) for n in REPLICATED], axis=0)
        small[l] = _pad_to(rep, -(-rep.shape[0] // 8) * 8, 128)
    grad_x = dy.reshape(B, S, D)
    recv[0], recv[3] = _exchange_alone("scatter_grads", _Scatter([pending[0], pending[3]], [recv[0], recv[3]], [0, 0]))

    small_rows = small[0].shape[0]
    small = jnp.concatenate(small, axis=0)
    got_small = _all_gather("gather_small_grads", [small])[0].reshape(N_DEV, L, small_rows, 128)

    res = {}
    for n in names:
        b, off = where[n]
        r, c = shapes1[n]
        if n in NARROW:
            parts = recv[b].reshape(N_DEV, L, group_rows[b], -1)[:, :, off:off + r, :c].reshape(N_DEV, L * r, c)
            res[n] = _adamw("adamw_" + n, parts, _as2d(w[n]), _as2d(m[n]), _as2d(v[n]))
        else:
            res[n] = _adamw("adamw_" + n, recv[b], _as2d(w[n]), _as2d(m[n]), _as2d(v[n]), row_off=off, layers=L,
                            stride=group_rows[b])
    off = 0
    for n in REPLICATED:
        r = int(np.prod(w[n].shape[1:])) // 128
        shp2 = _as2d(w[n]).shape
        res[n] = _adamw("adamw_" + n, got_small[:, :, off:off + r].reshape((N_DEV,) + shp2), _as2d(w[n]), _as2d(m[n]), _as2d(v[n]))
        off += r
    outs = [loss, grad_x]
    for k in range(4):
        outs += [res[n][k].reshape(w[n].shape) for n in WEIGHTS]
    return tuple(outs)
```

```python
import functools

import numpy as np
import jax
import jax.numpy as jnp
from jax import lax
from jax.experimental import pallas as pl
from jax.experimental.pallas import tpu as pltpu

F32, BF16 = jnp.float32, jnp.bfloat16
MESH = pl.DeviceIdType.MESH
N_DEV = 8

D = 1024
DEPTH = 4
CHUNK = 64
EPS = 1e-6
NEG_INF = -1e30
HEADS, QK_NOPE, QK_ROPE, V_HEAD = 8, 64, 32, 64
Q_LORA, KV_LORA = 256, 128
ROPE_THETA = 10000.0
POOL_WINDOWS = (2, 4, 8, 16)
GROUP = 64
BR = 256
SGU_BLOCK = 128
D_FF = 2816
IN_SIZES = (256, 128, 32, 256, 256, 256, 256, 256, 256, 4096)
HP = 128
QW = HEADS * HP
C_CQ, C_CKV, C_KR, C_PIN, C_SU, C_SV, C_CVB, C_CVC, C_CVX, C_GATE = 0, 256, 384, 512, 768, 1024, 1280, 1536, 1792, 2048

ADAM_LR, ADAM_B1, ADAM_B2, ADAM_EPS, ADAM_WD, ADAM_STEP = 0.001, 0.9, 0.999, 1e-08, 0.01, 10

WEIGHTS = ['w_in', 'g_pre_mix', 'g_cq', 'g_ckv', 'w_uq', 'w_ukv', 'pool_w', 'pool_scale', 'g_sgu_v', 'sgu_w', 'sgu_b',
           'conv_w', 'w_br_a', 'w_br_b', 'w_br_c', 'w_br_d', 'w_out', 'g_post_mix', 'g_pre_ffn', 'w_ffn_gate',
           'w_ffn_up', 'w_ffn_down', 'g_post_ffn']
SHARDED = {'w_in': 2, 'w_uq': 2, 'w_ukv': 2, 'conv_w': 3, 'w_br_a': 2, 'w_br_b': 2, 'w_br_c': 2, 'w_br_d': 2,
           'w_out': 1, 'w_ffn_gate': 2, 'w_ffn_up': 2, 'w_ffn_down': 1}
REPLICATED = [n for n in WEIGHTS if n not in SHARDED]

VMEM_LIMIT = 56 * 1024 * 1024
TQ = 256


def _cparams(sem=None):
    return pltpu.CompilerParams(vmem_limit_bytes=VMEM_LIMIT, dimension_semantics=sem)


def _tile(n, cap, align=128):
    if n <= cap:
        return n
    for t in range(cap - cap % align, 0, -align):
        if n % t == 0:
            return t
    return n


def _peers():
    x_, y_, c_ = lax.axis_index("x"), lax.axis_index("y"), lax.axis_index("c")
    out = []
    for k in range(1, N_DEV):
        px, py, pc = (x_ + (k >> 2)) % 2, (y_ + ((k >> 1) & 1)) % 2, (c_ + (k & 1)) % 2
        out.append((k, (px, py, pc), 4 * px + 2 * py + pc))
    return 4 * x_ + 2 * y_ + c_, out


class _Exchange:
    def __init__(self, n):
        self.n = n
        self.scratch = [pltpu.SemaphoreType.DMA((7 * n,)), pltpu.SemaphoreType.DMA((7 * n,)),
                        pltpu.SemaphoreType.DMA((n,))]
        self.results = None

    def start(self, cin, cout, sems):
        local, sends, _ = self.copies(cin, cout, sems)
        for cp in local + sends:
            cp.start()

    def wait(self, cin, cout, sems):
        local, sends, recvs = self.copies(cin, cout, sems)
        for cp in recvs:
            cp.wait_recv()
        for cp in sends:
            cp.wait_send()
        for cp in local:
            cp.wait()


class _Gather(_Exchange):
    def __init__(self, xs):
        super().__init__(len(xs))
        self.inputs = list(xs)
        self.out_shape = [jax.ShapeDtypeStruct((N_DEV,) + x.shape, x.dtype) for x in xs]
        self.aliases = {}

    def copies(self, cin, cout, sems):
        send_sems, recv_sems, local_sems = sems
        me, peers = _peers()
        local, sends, recvs = [], [], []
        for a in range(self.n):
            local.append(pltpu.make_async_copy(cin[a], cout[a].at[me], local_sems.at[a]))
            for k, dev, flat in peers:
                sem = dict(send_sem=send_sems.at[7 * a + k - 1], recv_sem=recv_sems.at[7 * a + k - 1],
                           device_id=dev, device_id_type=MESH)
                sends.append(pltpu.make_async_remote_copy(src_ref=cin[a], dst_ref=cout[a].at[me], **sem))
                recvs.append(pltpu.make_async_remote_copy(src_ref=cin[a], dst_ref=cout[a].at[flat], **sem))
        return local, sends, recvs


class _Scatter(_Exchange):
    def __init__(self, xs, recv, row0):
        super().__init__(len(xs))
        self.inputs = list(xs) + list(recv)
        self.out_shape = [jax.ShapeDtypeStruct(r.shape, r.dtype) for r in recv]
        self.aliases = {self.n + a: a for a in range(self.n)}
        self.row0 = list(row0)

    def copies(self, cin, cout, sems):
        send_sems, recv_sems, local_sems = sems
        me, peers = _peers()
        local, sends, recvs = [], [], []
        for a in range(self.n):
            rows = pl.ds(self.row0[a], self.inputs[a].shape[1])
            local.append(pltpu.make_async_copy(cin[a].at[me], cout[a].at[me, rows], local_sems.at[a]))
            for k, dev, flat in peers:
                sem = dict(send_sem=send_sems.at[7 * a + k - 1], recv_sem=recv_sems.at[7 * a + k - 1],
                           device_id=dev, device_id_type=MESH)
                sends.append(pltpu.make_async_remote_copy(src_ref=cin[a].at[flat], dst_ref=cout[a].at[me, rows], **sem))
                recvs.append(pltpu.make_async_remote_copy(src_ref=cin[a].at[me], dst_ref=cout[a].at[flat, rows], **sem))
        return local, sends, recvs


def _call(name, body, grid, in_specs, out_specs, out_shape, arrays, scratch=(), sem=None, comm=None):
    if comm is None:
        res = pl.pallas_call(body, name=name, grid=grid, in_specs=list(in_specs), out_specs=list(out_specs),
                             out_shape=list(out_shape), scratch_shapes=list(scratch), compiler_params=_cparams(sem))(*arrays)
        return list(res)
    ni, no, ns = len(in_specs), len(out_specs), len(scratch)
    nci, nco = len(comm.inputs), len(comm.out_shape)
    hbm = pl.BlockSpec(memory_space=pl.ANY)

    def wrapped(*refs):
        ins, cin = refs[:ni], refs[ni:ni + nci]
        o0 = ni + nci
        outs, cout = refs[o0:o0 + no], refs[o0 + no:o0 + no + nco]
        s0 = o0 + no + nco
        scr, sems = refs[s0:s0 + ns], refs[s0 + ns:]
        ids = [pl.program_id(d) for d in range(len(grid))]
        first = functools.reduce(jnp.logical_and, [i == 0 for i in ids])
        last = functools.reduce(jnp.logical_and, [i == g - 1 for i, g in zip(ids, grid)])

        @pl.when(first)
        def _():
            comm.start(cin, cout, sems)

        body(*ins, *outs, *scr)

        @pl.when(last)
        def _():
            comm.wait(cin, cout, sems)

    res = pl.pallas_call(
        wrapped, name=name, grid=grid, in_specs=list(in_specs) + [hbm] * nci, out_specs=list(out_specs) + [hbm] * nco,
        out_shape=list(out_shape) + list(comm.out_shape), scratch_shapes=list(scratch) + comm.scratch,
        input_output_aliases={ni + i: no + j for i, j in comm.aliases.items()},
        compiler_params=_cparams(("arbitrary",) * len(grid)))(*arrays, *comm.inputs)
    comm.results = list(res[no:])
    return list(res[:no])


def win(arr, width, idx):
    return ('win', arr, width, idx)


def tab(arr):
    return ('tab', arr)


def _rw(name, fn, ins, consts, outs, accs=(), tile=256, comm=None):
    first = ins[0][1] if isinstance(ins[0], tuple) else ins[0]
    T = first.shape[0]
    tile = min(tile, T)
    grid = (T // tile,)
    arrays, in_specs = [], []
    for x in ins:
        if isinstance(x, tuple) and x[0] == 'win':
            _, a, w, k = x
            arrays.append(a)
            in_specs.append(pl.BlockSpec((tile, w), lambda i, k=k: (i, k)))
        elif isinstance(x, tuple) and x[0] == 'tab':
            a = x[1]
            nb = a.shape[0] // tile
            arrays.append(a)
            in_specs.append(pl.BlockSpec((tile, a.shape[1]), lambda i, nb=nb: (i % nb, 0)))
        else:
            arrays.append(x)
            in_specs.append(pl.BlockSpec((tile, x.shape[1]), lambda i: (i, 0)))
    for c in consts:
        arrays.append(c)
        in_specs.append(pl.BlockSpec(c.shape, lambda i, n=c.ndim: (0,) * n))
    out_shape = [jax.ShapeDtypeStruct((T, f), dt) for f, dt in outs]
    out_specs = [pl.BlockSpec((tile, f), lambda i: (i, 0)) for f, _ in outs]
    for shp in accs:
        out_shape.append(jax.ShapeDtypeStruct(shp, F32))
        out_specs.append(pl.BlockSpec(shp, lambda i, n=len(shp): (0,) * n))
    ni, nc, no = len(ins), len(consts), len(outs)

    def body(*refs):
        vals_in = [r[...] for r in refs[:ni]]
        vals_in = [v.astype(F32) if v.dtype == BF16 else v for v in vals_in]
        vals = fn(*vals_in, *[r[...] for r in refs[ni:ni + nc]])
        if not isinstance(vals, (tuple, list)):
            vals = (vals,)
        o_refs = refs[ni + nc:]
        for r, v in zip(o_refs[:no], vals[:no]):
            r[...] = v.astype(r.dtype)
        i = pl.program_id(0)
        for r, v in zip(o_refs[no:], vals[no:]):
            @pl.when(i == 0)
            def _(r=r, v=v):
                r[...] = v

            @pl.when(i > 0)
            def _(r=r, v=v):
                r[...] += v

    return _call(name, body, grid, in_specs, out_specs, out_shape, arrays, sem=("arbitrary",), comm=comm)


def _mm(name, a, b, tb=False, out_dtype=F32, tm=512, tn=3072, a2=None, b2=None, comm=None):
    M, K = a.shape
    N = b.shape[0] if tb else b.shape[1]
    tm, tn = _tile(M, tm, 8), _tile(N, tn)
    dims = (((1,), (1,)), ((), ())) if tb else (((1,), (0,)), ((), ()))
    pairs = [(a, b)] + ([(a2, b2)] if a2 is not None else [])

    def body(*refs):
        o_ref = refs[-1]
        acc = None
        for i in range(len(pairs)):
            p = lax.dot_general(refs[2 * i][...].astype(BF16), refs[2 * i + 1][...].astype(BF16), dims,
                                preferred_element_type=F32)
            acc = p if acc is None else acc + p
        o_ref[...] = acc.astype(o_ref.dtype)

    in_specs, arrays = [], []
    for x, y in pairs:
        k = x.shape[1]
        in_specs.append(pl.BlockSpec((tm, k), lambda j, i: (i, 0)))
        in_specs.append(pl.BlockSpec((tn, k), lambda j, i: (j, 0)) if tb else pl.BlockSpec((k, tn), lambda j, i: (0, j)))
        arrays += [x, y]
    return _call(name, body, (N // tn, M // tm), in_specs, [pl.BlockSpec((tm, tn), lambda j, i: (i, j))],
                 [jax.ShapeDtypeStruct((M, N), out_dtype)], arrays, sem=("parallel", "parallel"), comm=comm)[0]


def _mm_t(name, a, g, tk=1408, tn=1408, tt=2048):
    T, K = a.shape
    N = g.shape[1]
    tk, tn, tt = _tile(K, tk), _tile(N, tn), _tile(T, tt, 8)
    nt = T // tt

    def body(a_ref, g_ref, o_ref, acc_ref):
        p = lax.dot_general(a_ref[...].astype(BF16), g_ref[...].astype(BF16), (((0,), (0,)), ((), ())),
                            preferred_element_type=F32)
        t = pl.program_id(2)

        @pl.when(t == 0)
        def _():
            acc_ref[...] = p

        @pl.when(t > 0)
        def _():
            acc_ref[...] += p

        @pl.when(t == nt - 1)
        def _():
            o_ref[...] = acc_ref[...].astype(BF16)

    return _call(name, body, (K // tk, N // tn, nt),
                 [pl.BlockSpec((tt, tk), lambda i, j, t: (t, i)), pl.BlockSpec((tt, tn), lambda i, j, t: (t, j))],
                 [pl.BlockSpec((tk, tn), lambda i, j, t: (i, j))], [jax.ShapeDtypeStruct((K, N), BF16)], [a, g],
                 scratch=[pltpu.VMEM((tk, tn), F32)], sem=("parallel", "parallel", "arbitrary"))[0]


def _dot(a, b):
    return jnp.dot(a.astype(BF16), b.astype(BF16), preferred_element_type=F32)


def _dot_nt(a, b):
    return lax.dot_general(a.astype(BF16), b.astype(BF16), (((1,), (1,)), ((), ())), preferred_element_type=F32)


def _dot_tn(a, b):
    return lax.dot_general(a.astype(BF16), b.astype(BF16), (((0,), (0,)), ((), ())), preferred_element_type=F32)


def _sigmoid(x):
    return 0.5 * jnp.tanh(0.5 * x) + 0.5


def _ffn_up(h, wg, wu, comm=None):
    M, K = h.shape
    N = wg.shape[1]
    tm, tn = _tile(M, 512, 8), _tile(N, 1408)

    def body(h_ref, wg_ref, wu_ref, a_ref, b_ref, act_ref):
        hb = h_ref[...]
        a = jnp.dot(hb, wg_ref[...], preferred_element_type=F32)
        b = jnp.dot(hb, wu_ref[...], preferred_element_type=F32)
        a_ref[...] = a.astype(BF16)
        b_ref[...] = b.astype(BF16)
        act_ref[...] = (a * _sigmoid(a) * b).astype(BF16)

    wspec = pl.BlockSpec((K, tn), lambda j, i: (0, j))
    ospec = pl.BlockSpec((tm, tn), lambda j, i: (i, j))
    return _call("ffn_up", body, (N // tn, M // tm), [pl.BlockSpec((tm, K), lambda j, i: (i, 0)), wspec, wspec],
                 [ospec] * 3, [jax.ShapeDtypeStruct((M, N), BF16)] * 3, [h, wg, wu], sem=("parallel", "parallel"),
                 comm=comm)


def _ffn_down_dx(df, wdn, a, b, comm=None):
    M, K = df.shape
    N = wdn.shape[0]
    tm, tn = _tile(M, 512, 8), _tile(N, 1408)

    def body(df_ref, w_ref, a_ref, b_ref, da_ref, db_ref):
        d = _dot_nt(df_ref[...], w_ref[...])
        av, bv = a_ref[...].astype(F32), b_ref[...].astype(F32)
        s = _sigmoid(av)
        da_ref[...] = (d * bv * (s + av * s * (1.0 - s))).astype(BF16)
        db_ref[...] = (d * av * s).astype(BF16)

    tspec = pl.BlockSpec((tm, tn), lambda j, i: (i, j))
    return _call("ffn_down_dx", body, (N // tn, M // tm),
                 [pl.BlockSpec((tm, K), lambda j, i: (i, 0)), pl.BlockSpec((tn, K), lambda j, i: (j, 0)), tspec, tspec],
                 [tspec] * 2, [jax.ShapeDtypeStruct((M, N), BF16)] * 2, [df, wdn, a, b], sem=("parallel", "parallel"),
                 comm=comm)


def _mm_norm(name, a, b, x, g, g_next=None, comm=None):
    M, K = a.shape
    N = b.shape[1]
    tm = _tile(M, 512, 8)
    more = g_next is not None

    def body(*refs):
        a_ref, b_ref, x_ref, g_ref = refs[:4]
        y_ref, xn_ref = refs[4 + more], refs[5 + more]
        y = jnp.dot(a_ref[...].astype(BF16), b_ref[...], preferred_element_type=F32)
        y_ref[...] = y
        xn = x_ref[...] + _rms(y, g_ref[...])
        xn_ref[...] = xn
        if more:
            refs[6 + more][...] = _rms(xn, refs[4][...]).astype(BF16)

    row = lambda n: pl.BlockSpec((tm, n), lambda i: (i, 0))
    whole = lambda z: pl.BlockSpec(z.shape, lambda i: (0, 0))
    arrays = [a, b, x, g] + ([g_next] if more else [])
    in_specs = [row(K), whole(b), row(N), whole(g)] + ([whole(g_next)] if more else [])
    out_shape = [jax.ShapeDtypeStruct((M, N), F32)] * 2 + ([jax.ShapeDtypeStruct((M, N), BF16)] if more else [])
    return _call(name, body, (M // tm,), in_specs, [row(N)] * (2 + more), out_shape, arrays, sem=("parallel",), comm=comm)


def _rms(x, g):
    r = lax.rsqrt(jnp.mean(x * x, axis=-1, keepdims=True) + EPS)
    return x * r * g


def _rms_bwd(x, g, dy):
    r = lax.rsqrt(jnp.mean(x * x, axis=-1, keepdims=True) + EPS)
    xh = x * r
    dxh = dy * g
    dx = r * (dxh - xh * jnp.mean(dxh * xh, axis=-1, keepdims=True))
    return dx, jnp.sum(dy * xh, axis=0, keepdims=True)


def _shift_down(x, k):
    t = lax.broadcasted_iota(jnp.int32, x.shape, 0)
    return jnp.where(t >= k, pltpu.roll(x, k, 0), 0.0)


def _shift_up(x, k):
    n = x.shape[0]
    t = lax.broadcasted_iota(jnp.int32, x.shape, 0)
    return jnp.where(t < n - k, pltpu.roll(x, n - k, 0), 0.0)


def _group_select(vals):
    lane = lax.broadcasted_iota(jnp.int32, vals[0].shape, 1) // GROUP
    out = vals[-1]
    for g in range(len(vals) - 2, -1, -1):
        out = jnp.where(lane == g, vals[g], out)
    return out


def _pool_count(shape):
    t = lax.broadcasted_iota(jnp.int32, shape, 0)
    lane = lax.broadcasted_iota(jnp.int32, shape, 1) // GROUP
    w = jnp.where(lane == 0, POOL_WINDOWS[0], jnp.where(lane == 1, POOL_WINDOWS[1],
                  jnp.where(lane == 2, POOL_WINDOWS[2], POOL_WINDOWS[3])))
    return jnp.minimum(t + 1, w).astype(F32)


def _pooled(z):
    s = [z]
    for k in (1, 2, 4, 8):
        s.append(s[-1] + _shift_down(s[-1], k))
    return _group_select(s[1:]) / _pool_count(z.shape) - z


def _pooled_bwd(dp):
    u = dp / _pool_count(dp.shape)
    s = [u]
    for k in (1, 2, 4, 8):
        s.append(s[-1] + _shift_up(s[-1], k))
    return _group_select(s[1:]) - dp


def _diag_mask():
    r = lax.broadcasted_iota(jnp.int32, (TQ, TQ), 0) // CHUNK
    c = lax.broadcasted_iota(jnp.int32, (TQ, TQ), 1) // CHUNK
    return r >= c


def _attn_fwd(q, k, v, B, S, comm=None):
    nq = S // TQ

    def body(q_ref, k_ref, v_ref, o_ref, lse_ref):
        mask = _diag_mask()
        for i in range(nq):
            r0, r1 = i * TQ, (i + 1) * TQ
            qb = q_ref[r0:r1, :]
            sd = jnp.where(mask, _dot_nt(qb, k_ref[r0:r1, :]), NEG_INF)
            m = jnp.max(sd, axis=-1, keepdims=True)
            if i:
                so = _dot_nt(qb, k_ref[0:r0, :])
                m = jnp.maximum(m, jnp.max(so, axis=-1, keepdims=True))
            pd = jnp.exp(sd - m)
            l = jnp.sum(pd, axis=-1, keepdims=True)
            acc = _dot(pd, v_ref[r0:r1, :])
            if i:
                po = jnp.exp(so - m)
                l = l + jnp.sum(po, axis=-1, keepdims=True)
                acc = acc + _dot(po, v_ref[0:r0, :])
            o_ref[r0:r1, :] = acc / l
            lse_ref[r0:r1, :] = jnp.broadcast_to(m + jnp.log(l), (TQ, HP))

    spec = pl.BlockSpec((S, HP), lambda b, h: (b, h))
    return _call("attn_fwd", body, (B, HEADS), [spec] * 3, [spec] * 2, [jax.ShapeDtypeStruct((B * S, QW), F32)] * 2,
                 [q, k, v], sem=("parallel", "parallel"), comm=comm)


def _attn_bwd(q, k, v, o, do, lse, B, S, comm=None):
    nq = S // TQ

    def body(q_ref, k_ref, v_ref, o_ref, do_ref, lse_ref, dq_ref, dk_ref, dv_ref, dk_acc, dv_acc):
        mask = _diag_mask()
        dk_acc[...] = jnp.zeros_like(dk_acc)
        dv_acc[...] = jnp.zeros_like(dv_acc)
        for i in range(nq):
            r0, r1 = i * TQ, (i + 1) * TQ
            qb, dob = q_ref[r0:r1, :], do_ref[r0:r1, :]
            delta = jnp.sum(o_ref[r0:r1, :] * dob, axis=-1, keepdims=True)
            lse_b = lse_ref[r0:r1, :][:, :1]
            dq = None
            for c0, c1, diag in ([(0, r0, False)] if i else []) + [(r0, r1, True)]:
                kb, vb = k_ref[c0:c1, :], v_ref[c0:c1, :]
                p = jnp.exp(_dot_nt(qb, kb) - lse_b)
                if diag:
                    p = jnp.where(mask, p, 0.0)
                ds = p * (_dot_nt(dob, vb) - delta)
                part = _dot(ds, kb)
                dq = part if dq is None else dq + part
                dk_acc[c0:c1, :] += _dot_tn(ds, qb)
                dv_acc[c0:c1, :] += _dot_tn(p, dob)
            dq_ref[r0:r1, :] = dq
        dk_ref[...] = dk_acc[...].astype(BF16)
        dv_ref[...] = dv_acc[...].astype(BF16)

    spec = pl.BlockSpec((S, HP), lambda b, h: (b, h))
    return _call("attn_bwd", body, (B, HEADS), [spec] * 6, [spec] * 3,
                 [jax.ShapeDtypeStruct((B * S, QW), F32)] + [jax.ShapeDtypeStruct((B * S, QW), BF16)] * 2,
                 [q, k, v, o, do, lse], scratch=[pltpu.VMEM((S, HP), F32)] * 2, sem=("parallel", "parallel"), comm=comm)


def _all_gather(name, xs):
    n = len(xs)

    def body(*refs):
        x_refs, out_refs = refs[:n], refs[n:2 * n]
        send_sems, recv_sems, local_sems = refs[2 * n:]
        x_, y_, c_ = lax.axis_index("x"), lax.axis_index("y"), lax.axis_index("c")
        me, sibling = (x_, y_, c_), (x_, y_, 1 - c_)
        chips = [(1 - x_, y_), (x_, 1 - y_), (1 - x_, 1 - y_)]

        def copy(a, k, block, to, own=False):
            slot = out_refs[a].at[4 * block[0] + 2 * block[1] + block[2]]
            return pltpu.make_async_remote_copy(
                src_ref=x_refs[a] if own else slot, dst_ref=slot,
                send_sem=send_sems.at[7 * a + k], recv_sem=recv_sems.at[7 * a + k], device_id=to, device_id_type=MESH)

        mine = [pltpu.make_async_copy(x_refs[a], out_refs[a].at[4 * x_ + 2 * y_ + c_], local_sems.at[a]) for a in range(n)]
        for cp in mine:
            cp.start()
        first = [copy(a, 1 + j, me, (*chip, c_), own=True) for j, chip in enumerate(chips) for a in range(n)]
        first += [copy(a, 0, me, sibling, own=True) for a in range(n)]
        for cp in first:
            cp.start()
        passed = []
        for j, chip in enumerate(chips):
            for a in range(n):
                copy(a, 1 + j, (*chip, c_), me).wait_recv()
                passed.append(copy(a, 4 + j, (*chip, c_), sibling))
                passed[-1].start()
        for a in range(n):
            copy(a, 0, sibling, me).wait_recv()
            for j, chip in enumerate(chips):
                copy(a, 4 + j, (*chip, 1 - c_), me).wait_recv()
        for cp in first + passed:
            cp.wait_send()
        for cp in mine:
            cp.wait()

    return pl.pallas_call(
        body, name=name, out_shape=[jax.ShapeDtypeStruct((N_DEV,) + x.shape, x.dtype) for x in xs],
        in_specs=[pl.BlockSpec(memory_space=pl.ANY)] * n, out_specs=[pl.BlockSpec(memory_space=pl.ANY)] * n,
        scratch_shapes=[pltpu.SemaphoreType.DMA((7 * n,)), pltpu.SemaphoreType.DMA((7 * n,)), pltpu.SemaphoreType.DMA((n,))],
    )(*xs)


def _adamw(name, parts, w, m, v, row_off=0, layers=1, stride=0, comm=None):
    R, C = w.shape
    r = R // layers
    assert parts.shape[2] == C and parts.shape[1] >= (layers - 1) * stride + row_off + r
    if r % 8 == 0:
        tr = max(t for t in range(8, 513, 8) if r % t == 0 and row_off % t == 0 and stride % t == 0)
    else:
        assert layers == 1
        tr = r
    nb = r // tr

    def body(p_ref, w_ref, m_ref, v_ref, g_ref, d_ref, nm_ref, nv_ref):
        g = p_ref[0].astype(F32)
        for j in range(1, N_DEV):
            g = g + p_ref[j].astype(F32)
        m_new = ADAM_B1 * m_ref[...] + (1.0 - ADAM_B1) * g
        v_new = ADAM_B2 * v_ref[...] + (1.0 - ADAM_B2) * (g * g)
        m_hat = m_new / (1.0 - ADAM_B1 ** ADAM_STEP)
        v_hat = v_new / (1.0 - ADAM_B2 ** ADAM_STEP)
        g_ref[...] = g
        d_ref[...] = -ADAM_LR * (m_hat / (jnp.sqrt(v_hat) + ADAM_EPS) + ADAM_WD * w_ref[...])
        nm_ref[...] = m_new
        nv_ref[...] = v_new

    spec = pl.BlockSpec((tr, C), lambda l, i: (l * nb + i, 0))
    pspec = pl.BlockSpec((N_DEV, tr, C), lambda l, i: (0, (l * stride + row_off) // tr + i, 0))
    return _call(name, body, (layers, nb), [pspec, spec, spec, spec], [spec] * 4,
                 [jax.ShapeDtypeStruct((R, C), F32)] * 4, [parts, w, m, v], sem=("parallel", "parallel"), comm=comm)


def _rot_cols(w):
    h = w.shape[-1] // 2
    return jnp.concatenate([-w[..., h:], w[..., :h]], axis=-1)


def _unrot_cols(d):
    h = d.shape[-1] // 2
    return jnp.concatenate([d[..., h:], -d[..., :h]], axis=-1)


def _split_in(w):
    out, o = [], 0
    for n in IN_SIZES:
        out.append(w[..., o:o + n])
        o += n
    return out


def _layer_weights(G):
    L = G['w_in'].shape[0]
    cq, ckv, kr, pin, su, sv, cvb, cvc, cvx, gate = _split_in(G['w_in'])
    W1s = jnp.concatenate([cq, ckv, kr, _rot_cols(kr), jnp.zeros((L, D, 64), BF16), pin, su, sv, cvb, cvc, cvx], axis=-1)
    wq = G['w_uq'].reshape(L, Q_LORA, HEADS, QK_NOPE + QK_ROPE)
    nope, rope = wq[..., :QK_NOPE], wq[..., QK_NOPE:]
    z32 = jnp.zeros((L, Q_LORA, HEADS, HP - QK_NOPE - QK_ROPE), BF16)
    wq_a = jnp.concatenate([nope, rope, z32], axis=-1).reshape(L, Q_LORA, QW)
    wq_b = jnp.concatenate([jnp.zeros_like(nope), _rot_cols(rope), z32], axis=-1).reshape(L, Q_LORA, QW)
    wkv = G['w_ukv'].reshape(L, KV_LORA, HEADS, QK_NOPE + V_HEAD)
    z64 = jnp.zeros((L, KV_LORA, HEADS, HP - QK_NOPE), BF16)
    wkn = jnp.concatenate([wkv[..., :QK_NOPE], z64], axis=-1).reshape(L, KV_LORA, QW)
    wv = jnp.concatenate([wkv[..., QK_NOPE:], z64], axis=-1).reshape(L, KV_LORA, QW)
    wa = G['w_br_a'].reshape(L, HEADS, V_HEAD, D)
    wa = jnp.concatenate([wa, jnp.zeros((L, HEADS, HP - V_HEAD, D), BF16)], axis=2).reshape(L, QW, D)
    return dict(
        W1=jnp.concatenate([W1s, gate], axis=-1), W1s=W1s, W1g=gate, Wq=jnp.concatenate([wq_a, wq_b], axis=-1),
        Wkv=jnp.concatenate([wkn, wv], axis=-1), Wa=wa, Wb=G['w_br_b'], Wc=G['w_br_c'], Wd=G['w_br_d'],
        conv=G['conv_w'].reshape(L, 3, BR).astype(F32))


def _tables(S):
    inv = ROPE_THETA ** (-jnp.arange(0, QK_ROPE, 2, dtype=F32) / QK_ROPE)
    ang = jnp.arange(S, dtype=F32)[:, None] * inv[None, :]
    cos, sin = jnp.cos(ang), jnp.sin(ang)
    scale = (QK_NOPE + QK_ROPE) ** -0.5
    one, zero = jnp.ones((S, QK_NOPE), F32), jnp.zeros((S, QK_NOPE), F32)
    z32 = jnp.zeros((S, HP - QK_NOPE - QK_ROPE), F32)
    cq = jnp.concatenate([one, cos, cos, z32], axis=1) * scale
    sq = jnp.concatenate([zero, sin, sin, z32], axis=1) * scale
    ck = jnp.concatenate([cos, cos, sin, sin, jnp.zeros((S, HP - 2 * QK_ROPE), F32)], axis=1)
    return cq, sq, ck


def _place_k():
    p = np.zeros((HP, QW), np.float32)
    for r in range(2 * QK_ROPE):
        for h in range(HEADS):
            p[r, h * HP + QK_NOPE + r % QK_ROPE] = 1.0
    return jnp.asarray(p, BF16)


def _layer_fwd(x, h, W, late, P, tabs, B, S, ride, g_next):
    cq_t, sq_t, ck_t, place = tabs
    g_cq, g_ckv = P['g_cq'], P['g_ckv']
    P1 = _mm("in_proj", h, W['W1'], out_dtype=BF16, comm=ride.get('in_proj'))

    def q_prep(c, cq, sq, g, w):
        qq = _dot(_rms(c, g), w)
        return qq[:, :QW] * jnp.tile(cq, (1, HEADS)) + qq[:, QW:] * jnp.tile(sq, (1, HEADS))
    q = _rw("q_prep", q_prep, [win(P1, 256, 0), tab(cq_t), tab(sq_t)], [g_cq, W['Wq']], [(QW, BF16)])[0]

    def kv_prep(c, seg, ck, g, w, place):
        kv = _dot(_rms(c, g), w)
        return kv[:, :QW] + _dot(seg * ck, place), kv[:, QW:]
    k, v = _rw("kv_prep", kv_prep, [win(P1, 128, 2), win(P1, 128, 3), tab(ck_t)], [g_ckv, W['Wkv'], place],
               [(QW, BF16), (QW, BF16)])
    o, lse = _attn_fwd(q, k, v, B, S, comm=ride.get('attn_fwd'))

    def pool_f(z, wbd, scale):
        return _dot(_pooled(z), wbd) * scale
    bo = _rw("pool_fwd", pool_f, [win(P1, BR, 2)], [P['pool_bd'], P['pool_scale']], [(BR, BF16)], tile=S)[0]

    def sgu_f(u, sv, g, wm, bias):
        return u * _sgu_mix(_rms(sv, g), wm, bias)
    co = _rw("sgu_fwd", sgu_f, [win(P1, BR, 3), win(P1, BR, 4)], [P['g_sgu_v'], P['sgu_wm'], P['sgu_bias']],
             [(BR, BF16)], tile=SGU_BLOCK)[0]

    def conv_f(b, c, xi, w):
        z = c * xi
        return b * (w[0:1] * _shift_down(z, 2) + w[1:2] * _shift_down(z, 1) + w[2:3] * z)
    do_ = _rw("conv_fwd", conv_f, [win(P1, BR, 5), win(P1, BR, 6), win(P1, BR, 7)], [W['conv']], [(BR, BF16)], tile=S)[0]

    def merge_f(a, b, c, d, l0, l1, l2, l3, wa, wb, wc, wd):
        return (_sigmoid(l0) * _dot(a, wa) + _sigmoid(l1) * _dot(b, wb) + _sigmoid(l2) * _dot(c, wc)
                + _sigmoid(l3) * _dot(d, wd))
    merged = _rw("merge_fwd", merge_f, [o, bo, co, do_] + [win(P1, D, 2 + i) for i in range(4)],
                 [W['Wa'], W['Wb'], W['Wc'], W['Wd']], [(D, BF16)])[0]
    W = dict(W, **late())
    om, x1, h2 = _mm_norm("out_proj", merged, W['Wout'], x, P['g_post_mix'], P['g_pre_ffn'])
    fa, fb, act = _ffn_up(h2, W['Wg'], W['Wu'], comm=ride.get('ffn_up'))
    f, x2, *h_next = _mm_norm("ffn_down", act, W['Wdn'], x1, P['g_post_ffn'], g_next, comm=ride.get('ffn_down'))
    saved = dict(x=x, h=h, P1=P1, q=q, k=k, v=v, o=o, lse=lse, bo=bo, co=co, do=do_, merged=merged, om=om, x1=x1,
                 h2=h2, fa=fa, fb=fb, act=act, f=f)
    return x2, (h_next[0] if h_next else None), saved, W


def _sgu_mix(vn, wm, bias):
    outs = [_dot(wm[g], vn) for g in range(4)]
    return _group_select(outs) + bias


def _layer_bwd(dx2, df, sv, W, P, tabs, B, S, ride, own_ride, prev):
    cq_t, sq_t, ck_t, place = tabs
    P1 = sv['P1']
    grads = {}
    da, db = _ffn_down_dx(df, W['Wdn'], sv['fa'], sv['fb'], comm=ride.get('ffn_down_dx'))
    grads['Wdn'] = _mm_t("ffn_down_dw", sv['act'], df)
    dh2 = _mm("ffn_up_dx", da, W['Wg'], tb=True, a2=db, b2=W['Wu'])
    grads['Wg'] = _mm_t("ffn_gate_dw", sv['h2'], da)
    grads['Wu'] = _mm_t("ffn_up_dw", sv['h2'], db)

    def mid_norms_b(x1, dh, dy, om, g_ffn, g_mix):
        dx, dg_ffn = _rms_bwd(x1, g_ffn, dh)
        dx1 = dy + dx
        dom, dg_mix = _rms_bwd(om, g_mix, dx1)
        return dx1, dom, dg_ffn, dg_mix
    dx1, dom, grads['g_pre_ffn'], grads['g_post_mix'] = _rw(
        "mid_norms_bwd", mid_norms_b, [sv['x1'], dh2, dx2, sv['om']], [P['g_pre_ffn'], P['g_post_mix']],
        [(D, F32), (D, BF16)], [(1, D), (1, D)], tile=512)
    dmerged = _mm("out_proj_dx", dom, W['Wout'], tb=True)
    grads['Wout'] = _mm_t("out_proj_dw", sv['merged'], dom)
    ride = dict(ride, **own_ride(grads))

    def merge_b(a, b, c, d, l0, l1, l2, l3, dm, wa, wb, wc, wd):
        outs_dl, outs_dy, outs_dbr = [], [], []
        for br, l, w in ((a, l0, wa), (b, l1, wb), (c, l2, wc), (d, l3, wd)):
            s = _sigmoid(l)
            y = _dot(br, w)
            dy = (dm * s).astype(BF16)
            outs_dl.append((dm * y * s * (1.0 - s)).astype(BF16))
            outs_dy.append(dy)
            outs_dbr.append(_dot_nt(dy, w))
        return (jnp.concatenate(outs_dl, axis=1), *outs_dy, *outs_dbr)
    mb = _rw("merge_bwd", merge_b,
             [sv['o'], sv['bo'], sv['co'], sv['do']] + [win(P1, D, 2 + i) for i in range(4)] + [dmerged],
             [W['Wa'], W['Wb'], W['Wc'], W['Wd']],
             [(4 * D, BF16)] + [(D, BF16)] * 4 + [(QW, F32), (BR, F32), (BR, F32), (BR, F32)], comm=ride.get('merge_bwd'))
    dl, dy, (d_o, d_bo, d_co, d_do) = mb[0], mb[1:5], mb[5:9]
    grads['Wa'] = _mm_t("br_a_dw", sv['o'], dy[0])
    grads['Wb'] = _mm_t("br_b_dw", sv['bo'], dy[1])
    grads['Wc'] = _mm_t("br_c_dw", sv['co'], dy[2])
    grads['Wd'] = _mm_t("br_d_dw", sv['do'], dy[3])

    def conv_b(b, c, xi, dout, w):
        z = c * xi
        z1, z2 = _shift_down(z, 1), _shift_down(z, 2)
        y = w[0:1] * z2 + w[1:2] * z1 + w[2:3] * z
        dyv = dout * b
        dz = w[2:3] * dyv + w[1:2] * _shift_up(dyv, 1) + w[0:1] * _shift_up(dyv, 2)
        return (dout * y, dz * xi, dz * c, jnp.sum(dyv * z2, axis=0, keepdims=True),
                jnp.sum(dyv * z1, axis=0, keepdims=True), jnp.sum(dyv * z, axis=0, keepdims=True))
    dcvb, dcvc, dcvx, dw0, dw1, dw2 = _rw("conv_bwd", conv_b, [win(P1, BR, 5), win(P1, BR, 6), win(P1, BR, 7), d_do],
                                          [W['conv']], [(BR, BF16)] * 3, [(1, BR)] * 3, tile=S)
    grads['conv'] = jnp.concatenate([dw0, dw1, dw2], axis=0)

    def sgu_b(u, s_v, dout, g, wm, bias):
        vn = _rms(s_v, g)
        mixed = _sgu_mix(vn, wm, bias)
        dmix = dout * u
        dvn = _group_select([_dot_tn(wm[gi], dmix) for gi in range(4)])
        dsv, dg = _rms_bwd(s_v, g, dvn)
        lane = lax.broadcasted_iota(jnp.int32, dmix.shape, 1) // GROUP
        dwm = [_dot_nt(jnp.where(lane == gi, dmix, 0.0), vn) for gi in range(4)]
        return (dout * mixed, dsv, dg, *dwm, dmix)
    dsu, dsv_, grads['g_sgu_v'], wm0, wm1, wm2, wm3, grads['sgu_bias'] = _rw(
        "sgu_bwd", sgu_b, [win(P1, BR, 3), win(P1, BR, 4), d_co], [P['g_sgu_v'], P['sgu_wm'], P['sgu_bias']],
        [(BR, BF16)] * 2, [(1, BR)] + [(SGU_BLOCK, SGU_BLOCK)] * 4 + [(SGU_BLOCK, BR)], tile=SGU_BLOCK)
    grads['sgu_wm'] = jnp.stack([wm0, wm1, wm2, wm3])

    def pool_b(z, dout, wbd, scale):
        pooled = _pooled(z)
        mixed = _dot(pooled, wbd)
        dmix = dout * scale
        dz = _pooled_bwd(_dot_nt(dmix, wbd))
        return dz, _dot_tn(pooled, dmix), jnp.sum(dout * mixed, axis=0, keepdims=True)
    dpin, grads['pool_bd'], grads['pool_scale'] = _rw(
        "pool_bwd", pool_b, [win(P1, BR, 2), d_bo], [P['pool_bd'], P['pool_scale']], [(BR, BF16)], [(BR, BR), (1, BR)], tile=S)

    dq, dk, dv = _attn_bwd(sv['q'], sv['k'], sv['v'], sv['o'], d_o, sv['lse'], B, S, comm=ride.get('attn_bwd'))

    def q_prep_b(c, dq, cq, sq, g, w):
        dqq = jnp.concatenate([dq * jnp.tile(cq, (1, HEADS)), dq * jnp.tile(sq, (1, HEADS))], axis=1).astype(BF16)
        dc, dg = _rms_bwd(c, g, _dot_nt(dqq, w))
        return dc, _rms(c, g), dqq, dg
    dcq, cqn, dqq, grads['g_cq'] = _rw("q_prep_bwd", q_prep_b, [win(P1, 256, 0), dq, tab(cq_t), tab(sq_t)],
                                       [P['g_cq'], W['Wq']], [(Q_LORA, BF16), (Q_LORA, BF16), (2 * QW, BF16)], [(1, Q_LORA)])
    grads['Wq'] = _mm_t("uq_dw", cqn, dqq)

    def kv_prep_b(c, dk, dv, ck, g, w, place):
        dkv = jnp.concatenate([dk, dv], axis=1)
        dc, dg = _rms_bwd(c, g, _dot_nt(dkv, w))
        return dc, _dot_nt(dk, place) * ck, _rms(c, g), dkv, dg
    dckv, dseg, kvn, dkv, grads['g_ckv'] = _rw(
        "kv_prep_bwd", kv_prep_b, [win(P1, 128, 2), dk, dv, tab(ck_t)], [P['g_ckv'], W['Wkv'], place],
        [(KV_LORA, BF16), (HP, BF16), (KV_LORA, BF16), (2 * QW, BF16)], [(1, KV_LORA)])
    grads['Wkv'] = _mm_t("ukv_dw", kvn, dkv)

    dps = jnp.concatenate([dcq, dckv, dseg, dpin, dsu, dsv_, dcvb, dcvc, dcvx], axis=1)
    dh = _mm("in_proj_dx", dps, W['W1s'], tb=True, a2=dl, b2=W['W1g'], tm=256, comm=ride.get('in_proj_dx'))
    grads['W1s'] = _mm_t("in_proj_s_dw", sv['h'], dps)
    grads['W1g'] = _mm_t("in_proj_g_dw", sv['h'], dl)

    if prev is None:
        def pre_mix_b(x, dh, dy, g):
            dx, dg = _rms_bwd(x, g, dh)
            return dy + dx, dg
        dx, grads['g_pre_mix'] = _rw("pre_mix_bwd", pre_mix_b, [sv['x'], dh, dx1], [P['g_pre_mix']], [(D, F32)], [(1, D)], tile=512)
        return dx, None, None, grads, ride

    def edge_norms_b(x, dh, dy, f, g_mix, g_ffn):
        dx, dg_mix = _rms_bwd(x, g_mix, dh)
        dx = dy + dx
        df, dg_ffn = _rms_bwd(f, g_ffn, dx)
        return dx, df, dg_mix, dg_ffn
    dx, df_prev, grads['g_pre_mix'], dg_prev = _rw(
        "edge_norms_bwd", edge_norms_b, [sv['x'], dh, dx1, prev[0]], [P['g_pre_mix'], prev[1]],
        [(D, F32), (D, BF16)], [(1, D), (1, D)], tile=512)
    return dx, df_prev, dg_prev, grads, ride


def _param_grads(gl):
    st = {k: jnp.stack([g[k] for g in gl]) for k in gl[0]}
    L = len(gl)
    w1 = st['W1s']
    seg = w1[..., C_KR:C_KR + 128]
    dkr = seg[..., :QK_ROPE] + _unrot_cols(seg[..., QK_ROPE:2 * QK_ROPE])
    out = {}
    out['w_in'] = jnp.concatenate([w1[..., :C_KR], dkr, w1[..., C_PIN:], st['W1g']], axis=-1)
    qa = st['Wq'][..., :QW].reshape(L, Q_LORA, HEADS, HP)
    qb = st['Wq'][..., QW:].reshape(L, Q_LORA, HEADS, HP)
    drope = qa[..., QK_NOPE:QK_NOPE + QK_ROPE] + _unrot_cols(qb[..., QK_NOPE:QK_NOPE + QK_ROPE])
    out['w_uq'] = jnp.concatenate([qa[..., :QK_NOPE], drope], axis=-1).reshape(L, Q_LORA, HEADS * (QK_NOPE + QK_ROPE))
    kn = st['Wkv'][..., :QW].reshape(L, KV_LORA, HEADS, HP)[..., :QK_NOPE]
    vv = st['Wkv'][..., QW:].reshape(L, KV_LORA, HEADS, HP)[..., :V_HEAD]
    out['w_ukv'] = jnp.concatenate([kn, vv], axis=-1).reshape(L, KV_LORA, HEADS * (QK_NOPE + V_HEAD))
    out['w_br_a'] = st['Wa'].reshape(L, HEADS, HP, D)[:, :, :V_HEAD].reshape(L, HEADS * V_HEAD, D)
    out['w_br_b'], out['w_br_c'], out['w_br_d'] = st['Wb'], st['Wc'], st['Wd']
    out['conv_w'] = st['conv'].reshape(L, 3, 1, BR)
    for n in ('g_pre_mix', 'g_cq', 'g_ckv', 'g_sgu_v', 'g_post_mix', 'g_pre_ffn', 'g_post_ffn', 'pool_scale'):
        out[n] = st[n].reshape(L, -1)
    bd = st['pool_bd'].reshape(L, 4, GROUP, 4, GROUP)
    out['pool_w'] = jnp.stack([bd[:, g, :, g, :] for g in range(4)], axis=1)
    out['sgu_w'] = st['sgu_wm'] * _sgu_mask()[None, None]
    out['sgu_b'] = st['sgu_bias'].reshape(L, SGU_BLOCK, 4, GROUP).sum(-1).transpose(0, 2, 1)
    return out


def _sgu_mask():
    pc = np.arange(SGU_BLOCK) // CHUNK
    return jnp.asarray(pc[:, None] >= pc[None, :], F32)


def _rows(a, lead):
    return a.reshape(a.shape[:lead] + (-1, a.shape[-1]))


def _pad_to(a, rows, cols):
    pad = [(0, 0)] * (a.ndim - 2) + [(0, rows - a.shape[-2]), (0, cols - a.shape[-1])]
    return jnp.pad(a, pad)


GROUPS = (('w_in',), ('w_ffn_gate',), ('w_ffn_up',), ('w_out', 'w_ffn_down'),
          ('w_uq', 'w_ukv', 'w_br_a', 'w_br_b', 'w_br_c', 'w_br_d', 'conv_w'))
PADDED = {'w_uq': (Q_LORA, 128), 'conv_w': (128, 128), 'w_ffn_down': (384, D)}
NARROW = ('w_uq', 'conv_w')


def _group(t, lead, groups):
    bufs = {}
    for b in groups:
        ps = [_rows(t[n], lead) for n in GROUPS[b]]
        ps = [_pad_to(p, *PADDED[n]) if n in PADDED else p for n, p in zip(GROUPS[b], ps)]
        bufs[b] = ps[0] if len(ps) == 1 else jnp.concatenate(ps, axis=-2)
    return bufs


def _group_rows(shapes):
    where, rows = {}, []
    for b, names in enumerate(GROUPS):
        off = 0
        for n in names:
            where[n] = (b, off)
            off += PADDED.get(n, shapes[n])[0]
        rows.append(off)
    return where, rows


def _shard_split(a, axis):
    n = a.shape[axis]
    a = a.reshape(a.shape[:axis] + (N_DEV, n // N_DEV) + a.shape[axis + 1:])
    return jnp.moveaxis(a, axis, 0)


def _shard_join(a, axis):
    a = jnp.moveaxis(a, 0, axis)
    return a.reshape(a.shape[:axis] + (a.shape[axis] * a.shape[axis + 1],) + a.shape[axis + 2:])


def _as2d(a):
    return a.reshape(-1, a.shape[-1])


def kernel(x, w_in, g_pre_mix, g_cq, g_ckv, w_uq, w_ukv, pool_w, pool_scale, g_sgu_v, sgu_w, sgu_b, conv_w, w_br_a, w_br_b, w_br_c, w_br_d, w_out, g_post_mix, g_pre_ffn, w_ffn_gate, w_ffn_up, w_ffn_down, g_post_ffn, loss_target, m_w_in, m_g_pre_mix, m_g_cq, m_g_ckv, m_w_uq, m_w_ukv, m_pool_w, m_pool_scale, m_g_sgu_v, m_sgu_w, m_sgu_b, m_conv_w, m_w_br_a, m_w_br_b, m_w_br_c, m_w_br_d, m_w_out, m_g_post_mix, m_g_pre_ffn, m_w_ffn_gate, m_w_ffn_up, m_w_ffn_down, m_g_post_ffn, v_w_in, v_g_pre_mix, v_g_cq, v_g_ckv, v_w_uq, v_w_ukv, v_pool_w, v_pool_scale, v_g_sgu_v, v_sgu_w, v_sgu_b, v_conv_w, v_w_br_a, v_w_br_b, v_w_br_c, v_w_br_d, v_w_out, v_g_post_mix, v_g_pre_ffn, v_w_ffn_gate, v_w_ffn_up, v_w_ffn_down, v_g_post_ffn):
    args = dict(locals())
    w = {n: args[n] for n in WEIGHTS}
    m = {n: args['m_' + n] for n in WEIGHTS}
    v = {n: args['v_' + n] for n in WEIGHTS}
    B, S, _ = x.shape
    T = B * S
    L = DEPTH
    names = list(SHARDED)
    shapes1 = {n: _as2d(w[n][0]).shape for n in names}
    where, group_rows = _group_rows(shapes1)

    def shard_bufs(l, groups):
        return _group({n: w[n][l:l + 1].astype(BF16) for b in groups for n in GROUPS[b]}, 0, groups)

    def joined(gathered):
        G = {}
        for b, buf in gathered.items():
            for n in GROUPS[b]:
                off = where[n][1]
                r, c = shapes1[n]
                G[n] = _shard_join(buf[:, off:off + r, :c].reshape((N_DEV, 1) + w[n].shape[1:]), SHARDED[n])
        return G

    eye = jnp.eye(4, dtype=F32)
    pool_bd = (pool_w[:, :, :, None, :] * eye[None, :, None, :, None]).reshape(L, BR, BR).astype(BF16)
    sgu_wm = (sgu_w * _sgu_mask()[None, None]).astype(BF16)
    sgu_bias = jnp.repeat(sgu_b.transpose(0, 2, 1), GROUP, axis=2)
    tabs = _tables(S) + (_place_k(),)

    def layer_params(l):
        P = {n: w[n][l][None, :] for n in ('g_pre_mix', 'g_cq', 'g_ckv', 'g_sgu_v', 'g_post_mix', 'g_pre_ffn',
                                            'g_post_ffn', 'pool_scale')}
        P.update(pool_bd=pool_bd[l], sgu_wm=sgu_wm[l], sgu_bias=sgu_bias[l])
        return P

    xt = x.reshape(T, D)
    h = _rw("pre_mix", lambda x, g: _rms(x, g), [xt], [w['g_pre_mix'][0][None, :]], [(D, BF16)], tile=512)[0]
    saved, Ws = [], []
    first = shard_bufs(0, (0, 4))
    early = dict(zip((0, 4), _all_gather("gather_weights", [first[0], first[4]])))
    for l in range(L):
        own = shard_bufs(l, (1, 2, 3))
        nxt = shard_bufs(l + 1, (0, 4)) if l + 1 < L else None
        ride = {'in_proj': _Gather([own[1]]), 'attn_fwd': _Gather([own[3], own[2]])}
        if nxt:
            ride.update(ffn_up=_Gather([nxt[0]]), ffn_down=_Gather([nxt[4]]))

        def late(ride=ride):
            G = joined({1: ride['in_proj'].results[0], 3: ride['attn_fwd'].results[0], 2: ride['attn_fwd'].results[1]})
            return dict(Wout=G['w_out'][0], Wg=G['w_ffn_gate'][0], Wu=G['w_ffn_up'][0], Wdn=G['w_ffn_down'][0])

        We = {k: a[0] for k, a in _layer_weights(joined(early)).items()}
        g_next = w['g_pre_mix'][l + 1][None, :] if nxt else None
        xt, h, sv, W = _layer_fwd(xt, h, We, late, layer_params(l), tabs, B, S, ride, g_next)
        saved.append(sv)
        Ws.append(W)
        if nxt:
            early = {0: ride['ffn_up'].results[0], 4: ride['ffn_down'].results[0]}

    def loss_f(y, t, f, g):
        e = y - t
        dy = e * (1.0 / D)
        df, dg = _rms_bwd(f, g, dy)
        return dy, df, jnp.sum(e * e, axis=0, keepdims=True), dg
    dy, df, sq, dg_ffn = _rw("loss", loss_f, [xt, loss_target.reshape(T, D), saved[-1]['f']], [w['g_post_ffn'][L - 1][None, :]],
                             [(D, F32), (D, BF16)], [(1, D), (1, D)], tile=512)
    loss = lax.psum(0.5 * jnp.sum(sq) / D, ("x", "y", "c"))

    recv = [lax.empty((N_DEV, L * r, PADDED.get(ns[0], shapes1[ns[0]])[1]), BF16) for ns, r in zip(GROUPS, group_rows)]

    def send_bufs(pg, groups):
        return _group({n: _shard_split(pg[n], SHARDED[n]).astype(BF16) for b in groups for n in GROUPS[b]}, 1, groups)

    small, pending = [None] * L, None
    for l in reversed(range(L)):
        ride = {}
        if pending is not None:
            ride = {'ffn_down_dx': _Scatter([pending[4]], [recv[4]], [(l + 1) * group_rows[4]]),
                    'in_proj_dx': _Scatter([pending[0]], [recv[0]], [(l + 1) * group_rows[0]])}

        def own_ride(grads, l=l):
            bufs = send_bufs({'w_ffn_gate': grads['Wg'][None], 'w_ffn_up': grads['Wu'][None],
                              'w_ffn_down': grads['Wdn'][None], 'w_out': grads['Wout'][None]}, (1, 2, 3))
            return {'merge_bwd': _Scatter([bufs[1]], [recv[1]], [l * group_rows[1]]),
                    'attn_bwd': _Scatter([bufs[3], bufs[2]], [recv[3], recv[2]], [l * group_rows[3], l * group_rows[2]])}

        prev = (saved[l - 1]['f'], w['g_post_ffn'][l - 1][None, :]) if l else None
        dy, df_prev, dg_prev, gl, ride = _layer_bwd(dy, df, saved[l], Ws[l], layer_params(l), tabs, B, S, ride, own_ride, prev)
        gl['g_post_ffn'] = dg_ffn
        df, dg_ffn = df_prev, dg_prev
        for name, bs in (('ffn_down_dx', (4,)), ('in_proj_dx', (0,)), ('merge_bwd', (1,)), ('attn_bwd', (3, 2))):
            if name in ride:
                for b, res_b in zip(bs, ride[name].results):
                    recv[b] = res_b
        pg = _param_grads([gl])
        pending = send_bufs(pg, (0, 4))
        rep = jnp.concatenate([pg[n].reshape(-1, 128) for n in REPLICATED], axis=0)
        small[l] = _pad_to(rep, -(-rep.shape[0] // 8) * 8, 128)
    grad_x = dy.reshape(B, S, D)

    small_rows = small[0].shape[0]
    last = {'w_ffn_down': _Scatter([pending[0], pending[4]], [recv[0], recv[4]], [0, 0]),
            'w_ffn_gate': _Gather([jnp.concatenate(small, axis=0)])}

    res = {}
    for n in sorted(names, key=lambda n: n not in last):
        b, off = where[n]
        r, c = shapes1[n]
        if n in NARROW:
            parts = recv[b].reshape(N_DEV, L, group_rows[b], -1)[:, :, off:off + r, :c].reshape(N_DEV, L * r, c)
            res[n] = _adamw("adamw_" + n, parts, _as2d(w[n]), _as2d(m[n]), _as2d(v[n]))
        else:
            res[n] = _adamw("adamw_" + n, recv[b], _as2d(w[n]), _as2d(m[n]), _as2d(v[n]), row_off=off, layers=L,
                            stride=group_rows[b], comm=last.get(n))
        if n == 'w_ffn_down':
            recv[0], recv[4] = last[n].results
    got_small = last['w_ffn_gate'].results[0].reshape(N_DEV, L, small_rows, 128)
    off = 0
    for n in REPLICATED:
        r = int(np.prod(w[n].shape[1:])) // 128
        shp2 = _as2d(w[n]).shape
        res[n] = _adamw("adamw_" + n, got_small[:, :, off:off + r].reshape((N_DEV,) + shp2), _as2d(w[n]), _as2d(m[n]), _as2d(v[n]))
        off += r
    outs = [loss, grad_x]
    for k in range(4):
        outs += [res[n][k].reshape(w[n].shape) for n in WEIGHTS]
    return tuple(outs)
```

```python
import functools

import numpy as np
import jax
import jax.numpy as jnp
from jax import lax
from jax.experimental import pallas as pl
from jax.experimental.pallas import tpu as pltpu

F32, BF16 = jnp.float32, jnp.bfloat16
MESH = pl.DeviceIdType.MESH
N_DEV = 8

D = 1024
DEPTH = 4
CHUNK = 64
EPS = 1e-6
NEG_INF = -1e30
HEADS, QK_NOPE, QK_ROPE, V_HEAD = 8, 64, 32, 64
Q_LORA, KV_LORA = 256, 128
ROPE_THETA = 10000.0
POOL_WINDOWS = (2, 4, 8, 16)
GROUP = 64
BR = 256
SGU_BLOCK = 128
SGU_TILE = 4 * SGU_BLOCK
D_FF = 2816
IN_SIZES = (256, 128, 32, 256, 256, 256, 256, 256, 256, 4096)
HP = 128
QW = HEADS * HP
C_CQ, C_CKV, C_KR, C_PIN, C_SU, C_SV, C_CVB, C_CVC, C_CVX, C_GATE = 0, 256, 384, 512, 768, 1024, 1280, 1536, 1792, 2048

ADAM_LR, ADAM_B1, ADAM_B2, ADAM_EPS, ADAM_WD, ADAM_STEP = 0.001, 0.9, 0.999, 1e-08, 0.01, 10

WEIGHTS = ['w_in', 'g_pre_mix', 'g_cq', 'g_ckv', 'w_uq', 'w_ukv', 'pool_w', 'pool_scale', 'g_sgu_v', 'sgu_w', 'sgu_b',
           'conv_w', 'w_br_a', 'w_br_b', 'w_br_c', 'w_br_d', 'w_out', 'g_post_mix', 'g_pre_ffn', 'w_ffn_gate',
           'w_ffn_up', 'w_ffn_down', 'g_post_ffn']
SHARDED = {'w_in': 2, 'w_uq': 2, 'w_ukv': 2, 'conv_w': 3, 'w_br_a': 2, 'w_br_b': 2, 'w_br_c': 2, 'w_br_d': 2,
           'w_out': 1, 'w_ffn_gate': 2, 'w_ffn_up': 2, 'w_ffn_down': 1}
REPLICATED = [n for n in WEIGHTS if n not in SHARDED]

VMEM_LIMIT = 56 * 1024 * 1024
TQ = 256


def _cparams(sem=None):
    return pltpu.CompilerParams(vmem_limit_bytes=VMEM_LIMIT, dimension_semantics=sem)


def _tile(n, cap, align=128):
    if n <= cap:
        return n
    for t in range(cap - cap % align, 0, -align):
        if n % t == 0:
            return t
    return n


def _peers():
    x_, y_, c_ = lax.axis_index("x"), lax.axis_index("y"), lax.axis_index("c")
    out = []
    for k in range(1, N_DEV):
        px, py, pc = (x_ + (k >> 2)) % 2, (y_ + ((k >> 1) & 1)) % 2, (c_ + (k & 1)) % 2
        out.append((k, (px, py, pc), 4 * px + 2 * py + pc))
    return 4 * x_ + 2 * y_ + c_, out


class _Exchange:
    def __init__(self, n):
        self.n = n
        self.scratch = [pltpu.SemaphoreType.DMA((7 * n,)), pltpu.SemaphoreType.DMA((7 * n,)),
                        pltpu.SemaphoreType.DMA((n,))]
        self.results = None

    def start(self, cin, cout, sems):
        local, sends, _ = self.copies(cin, cout, sems)
        for cp in local + sends:
            cp.start()

    def wait(self, cin, cout, sems):
        local, sends, recvs = self.copies(cin, cout, sems)
        for cp in recvs:
            cp.wait_recv()
        for cp in sends:
            cp.wait_send()
        for cp in local:
            cp.wait()


class _Gather(_Exchange):
    def __init__(self, xs):
        super().__init__(len(xs))
        self.inputs = list(xs)
        self.out_shape = [jax.ShapeDtypeStruct((N_DEV,) + x.shape, x.dtype) for x in xs]
        self.aliases = {}

    def copies(self, cin, cout, sems):
        send_sems, recv_sems, local_sems = sems
        me, peers = _peers()
        local, sends, recvs = [], [], []
        for a in range(self.n):
            local.append(pltpu.make_async_copy(cin[a], cout[a].at[me], local_sems.at[a]))
            for k, dev, flat in peers:
                sem = dict(send_sem=send_sems.at[7 * a + k - 1], recv_sem=recv_sems.at[7 * a + k - 1],
                           device_id=dev, device_id_type=MESH)
                sends.append(pltpu.make_async_remote_copy(src_ref=cin[a], dst_ref=cout[a].at[me], **sem))
                recvs.append(pltpu.make_async_remote_copy(src_ref=cin[a], dst_ref=cout[a].at[flat], **sem))
        return local, sends, recvs


class _Scatter(_Exchange):
    def __init__(self, xs, recv, row0):
        super().__init__(len(xs))
        self.inputs = list(xs) + list(recv)
        self.out_shape = [jax.ShapeDtypeStruct(r.shape, r.dtype) for r in recv]
        self.aliases = {self.n + a: a for a in range(self.n)}
        self.row0 = list(row0)

    def copies(self, cin, cout, sems):
        send_sems, recv_sems, local_sems = sems
        me, peers = _peers()
        local, sends, recvs = [], [], []
        for a in range(self.n):
            rows = pl.ds(self.row0[a], self.inputs[a].shape[1])
            local.append(pltpu.make_async_copy(cin[a].at[me], cout[a].at[me, rows], local_sems.at[a]))
            for k, dev, flat in peers:
                sem = dict(send_sem=send_sems.at[7 * a + k - 1], recv_sem=recv_sems.at[7 * a + k - 1],
                           device_id=dev, device_id_type=MESH)
                sends.append(pltpu.make_async_remote_copy(src_ref=cin[a].at[flat], dst_ref=cout[a].at[me, rows], **sem))
                recvs.append(pltpu.make_async_remote_copy(src_ref=cin[a].at[me], dst_ref=cout[a].at[flat, rows], **sem))
        return local, sends, recvs


def _call(name, body, grid, in_specs, out_specs, out_shape, arrays, scratch=(), sem=None, comm=None):
    if comm is None:
        res = pl.pallas_call(body, name=name, grid=grid, in_specs=list(in_specs), out_specs=list(out_specs),
                             out_shape=list(out_shape), scratch_shapes=list(scratch), compiler_params=_cparams(sem))(*arrays)
        return list(res)
    ni, no, ns = len(in_specs), len(out_specs), len(scratch)
    nci, nco = len(comm.inputs), len(comm.out_shape)
    hbm = pl.BlockSpec(memory_space=pl.ANY)

    def wrapped(*refs):
        ins, cin = refs[:ni], refs[ni:ni + nci]
        o0 = ni + nci
        outs, cout = refs[o0:o0 + no], refs[o0 + no:o0 + no + nco]
        s0 = o0 + no + nco
        scr, sems = refs[s0:s0 + ns], refs[s0 + ns:]
        ids = [pl.program_id(d) for d in range(len(grid))]
        first = functools.reduce(jnp.logical_and, [i == 0 for i in ids])
        last = functools.reduce(jnp.logical_and, [i == g - 1 for i, g in zip(ids, grid)])

        @pl.when(first)
        def _():
            comm.start(cin, cout, sems)

        body(*ins, *outs, *scr)

        @pl.when(last)
        def _():
            comm.wait(cin, cout, sems)

    res = pl.pallas_call(
        wrapped, name=name, grid=grid, in_specs=list(in_specs) + [hbm] * nci, out_specs=list(out_specs) + [hbm] * nco,
        out_shape=list(out_shape) + list(comm.out_shape), scratch_shapes=list(scratch) + comm.scratch,
        input_output_aliases={ni + i: no + j for i, j in comm.aliases.items()},
        compiler_params=_cparams(("arbitrary",) * len(grid)))(*arrays, *comm.inputs)
    comm.results = list(res[no:])
    return list(res[:no])


def win(arr, width, idx):
    return ('win', arr, width, idx)


def tab(arr):
    return ('tab', arr)


def _rw(name, fn, ins, consts, outs, accs=(), tile=256, comm=None):
    first = ins[0][1] if isinstance(ins[0], tuple) else ins[0]
    T = first.shape[0]
    tile = min(tile, T)
    grid = (T // tile,)
    arrays, in_specs = [], []
    for x in ins:
        if isinstance(x, tuple) and x[0] == 'win':
            _, a, w, k = x
            arrays.append(a)
            in_specs.append(pl.BlockSpec((tile, w), lambda i, k=k: (i, k)))
        elif isinstance(x, tuple) and x[0] == 'tab':
            a = x[1]
            nb = a.shape[0] // tile
            arrays.append(a)
            in_specs.append(pl.BlockSpec((tile, a.shape[1]), lambda i, nb=nb: (i % nb, 0)))
        else:
            arrays.append(x)
            in_specs.append(pl.BlockSpec((tile, x.shape[1]), lambda i: (i, 0)))
    for c in consts:
        arrays.append(c)
        in_specs.append(pl.BlockSpec(c.shape, lambda i, n=c.ndim: (0,) * n))
    out_shape = [jax.ShapeDtypeStruct((T, f), dt) for f, dt in outs]
    out_specs = [pl.BlockSpec((tile, f), lambda i: (i, 0)) for f, _ in outs]
    for shp in accs:
        out_shape.append(jax.ShapeDtypeStruct(shp, F32))
        out_specs.append(pl.BlockSpec(shp, lambda i, n=len(shp): (0,) * n))
    ni, nc, no = len(ins), len(consts), len(outs)

    def body(*refs):
        vals_in = [r[...] for r in refs[:ni]]
        vals_in = [v.astype(F32) if v.dtype == BF16 else v for v in vals_in]
        vals = fn(*vals_in, *[r[...] for r in refs[ni:ni + nc]])
        if not isinstance(vals, (tuple, list)):
            vals = (vals,)
        o_refs = refs[ni + nc:]
        for r, v in zip(o_refs[:no], vals[:no]):
            r[...] = v.astype(r.dtype)
        i = pl.program_id(0)
        for r, v in zip(o_refs[no:], vals[no:]):
            @pl.when(i == 0)
            def _(r=r, v=v):
                r[...] = v

            @pl.when(i > 0)
            def _(r=r, v=v):
                r[...] += v

    return _call(name, body, grid, in_specs, out_specs, out_shape, arrays, sem=("arbitrary",), comm=comm)


def _mm(name, a, b, tb=False, out_dtype=F32, tm=512, tn=3072, a2=None, b2=None, comm=None):
    M, K = a.shape
    N = b.shape[0] if tb else b.shape[1]
    tm, tn = _tile(M, tm, 8), _tile(N, tn)
    dims = (((1,), (1,)), ((), ())) if tb else (((1,), (0,)), ((), ()))
    pairs = [(a, b)] + ([(a2, b2)] if a2 is not None else [])

    def body(*refs):
        o_ref = refs[-1]
        acc = None
        for i in range(len(pairs)):
            p = lax.dot_general(refs[2 * i][...].astype(BF16), refs[2 * i + 1][...].astype(BF16), dims,
                                preferred_element_type=F32)
            acc = p if acc is None else acc + p
        o_ref[...] = acc.astype(o_ref.dtype)

    in_specs, arrays = [], []
    for x, y in pairs:
        k = x.shape[1]
        in_specs.append(pl.BlockSpec((tm, k), lambda j, i: (i, 0)))
        in_specs.append(pl.BlockSpec((tn, k), lambda j, i: (j, 0)) if tb else pl.BlockSpec((k, tn), lambda j, i: (0, j)))
        arrays += [x, y]
    return _call(name, body, (N // tn, M // tm), in_specs, [pl.BlockSpec((tm, tn), lambda j, i: (i, j))],
                 [jax.ShapeDtypeStruct((M, N), out_dtype)], arrays, sem=("parallel", "parallel"), comm=comm)[0]


def _mm_t(name, a, g, tk=1408, tn=1408, tt=2048):
    T, K = a.shape
    N = g.shape[1]
    tk, tn, tt = _tile(K, tk), _tile(N, tn), _tile(T, tt, 8)
    nt = T // tt

    def body(a_ref, g_ref, o_ref, acc_ref):
        p = lax.dot_general(a_ref[...].astype(BF16), g_ref[...].astype(BF16), (((0,), (0,)), ((), ())),
                            preferred_element_type=F32)
        t = pl.program_id(2)

        @pl.when(t == 0)
        def _():
            acc_ref[...] = p

        @pl.when(t > 0)
        def _():
            acc_ref[...] += p

        @pl.when(t == nt - 1)
        def _():
            o_ref[...] = acc_ref[...].astype(BF16)

    return _call(name, body, (K // tk, N // tn, nt),
                 [pl.BlockSpec((tt, tk), lambda i, j, t: (t, i)), pl.BlockSpec((tt, tn), lambda i, j, t: (t, j))],
                 [pl.BlockSpec((tk, tn), lambda i, j, t: (i, j))], [jax.ShapeDtypeStruct((K, N), BF16)], [a, g],
                 scratch=[pltpu.VMEM((tk, tn), F32)], sem=("parallel", "parallel", "arbitrary"))[0]


def _dot(a, b):
    return jnp.dot(a.astype(BF16), b.astype(BF16), preferred_element_type=F32)


def _dot_nt(a, b):
    return lax.dot_general(a.astype(BF16), b.astype(BF16), (((1,), (1,)), ((), ())), preferred_element_type=F32)


def _dot_tn(a, b):
    return lax.dot_general(a.astype(BF16), b.astype(BF16), (((0,), (0,)), ((), ())), preferred_element_type=F32)


def _sigmoid(x):
    return 0.5 * jnp.tanh(0.5 * x) + 0.5


def _ffn_up(h, wg, wu, comm=None):
    M, K = h.shape
    N = wg.shape[1]
    tm, tn = _tile(M, 512, 8), _tile(N, 1408)

    def body(h_ref, wg_ref, wu_ref, a_ref, b_ref, act_ref):
        hb = h_ref[...]
        a = jnp.dot(hb, wg_ref[...], preferred_element_type=F32)
        b = jnp.dot(hb, wu_ref[...], preferred_element_type=F32)
        a_ref[...] = a.astype(BF16)
        b_ref[...] = b.astype(BF16)
        act_ref[...] = (a * _sigmoid(a) * b).astype(BF16)

    wspec = pl.BlockSpec((K, tn), lambda j, i: (0, j))
    ospec = pl.BlockSpec((tm, tn), lambda j, i: (i, j))
    return _call("ffn_up", body, (N // tn, M // tm), [pl.BlockSpec((tm, K), lambda j, i: (i, 0)), wspec, wspec],
                 [ospec] * 3, [jax.ShapeDtypeStruct((M, N), BF16)] * 3, [h, wg, wu], sem=("parallel", "parallel"),
                 comm=comm)


def _ffn_down_dx(df, wdn, a, b, comm=None):
    M, K = df.shape
    N = wdn.shape[0]
    tm, tn = _tile(M, 512, 8), _tile(N, 1408)

    def body(df_ref, w_ref, a_ref, b_ref, da_ref, db_ref):
        d = _dot_nt(df_ref[...], w_ref[...])
        av, bv = a_ref[...].astype(F32), b_ref[...].astype(F32)
        s = _sigmoid(av)
        da_ref[...] = (d * bv * (s + av * s * (1.0 - s))).astype(BF16)
        db_ref[...] = (d * av * s).astype(BF16)

    tspec = pl.BlockSpec((tm, tn), lambda j, i: (i, j))
    return _call("ffn_down_dx", body, (N // tn, M // tm),
                 [pl.BlockSpec((tm, K), lambda j, i: (i, 0)), pl.BlockSpec((tn, K), lambda j, i: (j, 0)), tspec, tspec],
                 [tspec] * 2, [jax.ShapeDtypeStruct((M, N), BF16)] * 2, [df, wdn, a, b], sem=("parallel", "parallel"),
                 comm=comm)


def _mm_norm(name, a, b, x, g, g_next=None, comm=None):
    M, K = a.shape
    N = b.shape[1]
    tm = _tile(M, 512, 8)
    more = g_next is not None

    def body(*refs):
        a_ref, b_ref, x_ref, g_ref = refs[:4]
        y_ref, xn_ref = refs[4 + more], refs[5 + more]
        y = jnp.dot(a_ref[...].astype(BF16), b_ref[...], preferred_element_type=F32)
        y_ref[...] = y
        xn = x_ref[...] + _rms(y, g_ref[...])
        xn_ref[...] = xn
        if more:
            refs[6 + more][...] = _rms(xn, refs[4][...]).astype(BF16)

    row = lambda n: pl.BlockSpec((tm, n), lambda i: (i, 0))
    whole = lambda z: pl.BlockSpec(z.shape, lambda i: (0, 0))
    arrays = [a, b, x, g] + ([g_next] if more else [])
    in_specs = [row(K), whole(b), row(N), whole(g)] + ([whole(g_next)] if more else [])
    out_shape = [jax.ShapeDtypeStruct((M, N), F32)] * 2 + ([jax.ShapeDtypeStruct((M, N), BF16)] if more else [])
    return _call(name, body, (M // tm,), in_specs, [row(N)] * (2 + more), out_shape, arrays, sem=("parallel",), comm=comm)


def _rms(x, g):
    r = lax.rsqrt(jnp.mean(x * x, axis=-1, keepdims=True) + EPS)
    return x * r * g


def _rms_bwd(x, g, dy):
    r = lax.rsqrt(jnp.mean(x * x, axis=-1, keepdims=True) + EPS)
    xh = x * r
    dxh = dy * g
    dx = r * (dxh - xh * jnp.mean(dxh * xh, axis=-1, keepdims=True))
    return dx, jnp.sum(dy * xh, axis=0, keepdims=True)


def _shift_down(x, k):
    t = lax.broadcasted_iota(jnp.int32, x.shape, 0)
    return jnp.where(t >= k, pltpu.roll(x, k, 0), 0.0)


def _shift_up(x, k):
    n = x.shape[0]
    t = lax.broadcasted_iota(jnp.int32, x.shape, 0)
    return jnp.where(t < n - k, pltpu.roll(x, n - k, 0), 0.0)


def _group_select(vals):
    lane = lax.broadcasted_iota(jnp.int32, vals[0].shape, 1) // GROUP
    out = vals[-1]
    for g in range(len(vals) - 2, -1, -1):
        out = jnp.where(lane == g, vals[g], out)
    return out


def _pool_count(shape):
    t = lax.broadcasted_iota(jnp.int32, shape, 0)
    lane = lax.broadcasted_iota(jnp.int32, shape, 1) // GROUP
    w = jnp.where(lane == 0, POOL_WINDOWS[0], jnp.where(lane == 1, POOL_WINDOWS[1],
                  jnp.where(lane == 2, POOL_WINDOWS[2], POOL_WINDOWS[3])))
    return jnp.minimum(t + 1, w).astype(F32)


def _pooled(z):
    s = [z]
    for k in (1, 2, 4, 8):
        s.append(s[-1] + _shift_down(s[-1], k))
    return _group_select(s[1:]) / _pool_count(z.shape) - z


def _pooled_bwd(dp):
    u = dp / _pool_count(dp.shape)
    s = [u]
    for k in (1, 2, 4, 8):
        s.append(s[-1] + _shift_up(s[-1], k))
    return _group_select(s[1:]) - dp


def _diag_mask():
    r = lax.broadcasted_iota(jnp.int32, (TQ, TQ), 0) // CHUNK
    c = lax.broadcasted_iota(jnp.int32, (TQ, TQ), 1) // CHUNK
    return r >= c


def _attn_fwd(q, k, v, B, S, comm=None):
    nq = S // TQ

    def body(q_ref, k_ref, v_ref, o_ref, lse_ref):
        mask = _diag_mask()
        for i in range(nq):
            r0, r1 = i * TQ, (i + 1) * TQ
            qb = q_ref[r0:r1, :]
            sd = jnp.where(mask, _dot_nt(qb, k_ref[r0:r1, :]), NEG_INF)
            m = jnp.max(sd, axis=-1, keepdims=True)
            if i:
                so = _dot_nt(qb, k_ref[0:r0, :])
                m = jnp.maximum(m, jnp.max(so, axis=-1, keepdims=True))
            pd = jnp.exp(sd - m)
            l = jnp.sum(pd, axis=-1, keepdims=True)
            acc = _dot(pd, v_ref[r0:r1, :])
            if i:
                po = jnp.exp(so - m)
                l = l + jnp.sum(po, axis=-1, keepdims=True)
                acc = acc + _dot(po, v_ref[0:r0, :])
            o_ref[r0:r1, :] = acc / l
            lse_ref[r0:r1, :] = jnp.broadcast_to(m + jnp.log(l), (TQ, HP))

    spec = pl.BlockSpec((S, HP), lambda b, h: (b, h))
    return _call("attn_fwd", body, (B, HEADS), [spec] * 3, [spec] * 2, [jax.ShapeDtypeStruct((B * S, QW), F32)] * 2,
                 [q, k, v], sem=("parallel", "parallel"), comm=comm)


def _attn_bwd(q, k, v, o, do, lse, B, S, comm=None):
    nq = S // TQ

    def body(q_ref, k_ref, v_ref, o_ref, do_ref, lse_ref, dq_ref, dk_ref, dv_ref, dk_acc, dv_acc):
        mask = _diag_mask()
        dk_acc[...] = jnp.zeros_like(dk_acc)
        dv_acc[...] = jnp.zeros_like(dv_acc)
        for i in range(nq):
            r0, r1 = i * TQ, (i + 1) * TQ
            qb, dob = q_ref[r0:r1, :], do_ref[r0:r1, :]
            delta = jnp.sum(o_ref[r0:r1, :] * dob, axis=-1, keepdims=True)
            lse_b = lse_ref[r0:r1, :][:, :1]
            dq = None
            for c0, c1, diag in ([(0, r0, False)] if i else []) + [(r0, r1, True)]:
                kb, vb = k_ref[c0:c1, :], v_ref[c0:c1, :]
                p = jnp.exp(_dot_nt(qb, kb) - lse_b)
                if diag:
                    p = jnp.where(mask, p, 0.0)
                ds = p * (_dot_nt(dob, vb) - delta)
                part = _dot(ds, kb)
                dq = part if dq is None else dq + part
                dk_acc[c0:c1, :] += _dot_tn(ds, qb)
                dv_acc[c0:c1, :] += _dot_tn(p, dob)
            dq_ref[r0:r1, :] = dq
        dk_ref[...] = dk_acc[...].astype(BF16)
        dv_ref[...] = dv_acc[...].astype(BF16)

    spec = pl.BlockSpec((S, HP), lambda b, h: (b, h))
    return _call("attn_bwd", body, (B, HEADS), [spec] * 6, [spec] * 3,
                 [jax.ShapeDtypeStruct((B * S, QW), F32)] + [jax.ShapeDtypeStruct((B * S, QW), BF16)] * 2,
                 [q, k, v, o, do, lse], scratch=[pltpu.VMEM((S, HP), F32)] * 2, sem=("parallel", "parallel"), comm=comm)


def _all_gather(name, xs):
    n = len(xs)

    def body(*refs):
        x_refs, out_refs = refs[:n], refs[n:2 * n]
        send_sems, recv_sems, local_sems = refs[2 * n:]
        x_, y_, c_ = lax.axis_index("x"), lax.axis_index("y"), lax.axis_index("c")
        me, sibling = (x_, y_, c_), (x_, y_, 1 - c_)
        chips = [(1 - x_, y_), (x_, 1 - y_), (1 - x_, 1 - y_)]

        def copy(a, k, block, to, own=False):
            slot = out_refs[a].at[4 * block[0] + 2 * block[1] + block[2]]
            return pltpu.make_async_remote_copy(
                src_ref=x_refs[a] if own else slot, dst_ref=slot,
                send_sem=send_sems.at[7 * a + k], recv_sem=recv_sems.at[7 * a + k], device_id=to, device_id_type=MESH)

        mine = [pltpu.make_async_copy(x_refs[a], out_refs[a].at[4 * x_ + 2 * y_ + c_], local_sems.at[a]) for a in range(n)]
        for cp in mine:
            cp.start()
        first = [copy(a, 1 + j, me, (*chip, c_), own=True) for j, chip in enumerate(chips) for a in range(n)]
        first += [copy(a, 0, me, sibling, own=True) for a in range(n)]
        for cp in first:
            cp.start()
        passed = []
        for j, chip in enumerate(chips):
            for a in range(n):
                copy(a, 1 + j, (*chip, c_), me).wait_recv()
                passed.append(copy(a, 4 + j, (*chip, c_), sibling))
                passed[-1].start()
        for a in range(n):
            copy(a, 0, sibling, me).wait_recv()
            for j, chip in enumerate(chips):
                copy(a, 4 + j, (*chip, 1 - c_), me).wait_recv()
        for cp in first + passed:
            cp.wait_send()
        for cp in mine:
            cp.wait()

    return pl.pallas_call(
        body, name=name, out_shape=[jax.ShapeDtypeStruct((N_DEV,) + x.shape, x.dtype) for x in xs],
        in_specs=[pl.BlockSpec(memory_space=pl.ANY)] * n, out_specs=[pl.BlockSpec(memory_space=pl.ANY)] * n,
        scratch_shapes=[pltpu.SemaphoreType.DMA((7 * n,)), pltpu.SemaphoreType.DMA((7 * n,)), pltpu.SemaphoreType.DMA((n,))],
    )(*xs)


def _adamw(name, parts, w, m, v, row_off=0, layers=1, stride=0, comm=None):
    R, C = w.shape
    r = R // layers
    assert parts.shape[2] == C and parts.shape[1] >= (layers - 1) * stride + row_off + r
    if r % 8 == 0:
        tr = max(t for t in range(8, 513, 8) if r % t == 0 and row_off % t == 0 and stride % t == 0)
    else:
        assert layers == 1
        tr = r
    nb = r // tr

    def body(p_ref, w_ref, m_ref, v_ref, g_ref, d_ref, nm_ref, nv_ref):
        g = p_ref[0].astype(F32)
        for j in range(1, N_DEV):
            g = g + p_ref[j].astype(F32)
        m_new = ADAM_B1 * m_ref[...] + (1.0 - ADAM_B1) * g
        v_new = ADAM_B2 * v_ref[...] + (1.0 - ADAM_B2) * (g * g)
        m_hat = m_new / (1.0 - ADAM_B1 ** ADAM_STEP)
        v_hat = v_new / (1.0 - ADAM_B2 ** ADAM_STEP)
        g_ref[...] = g
        d_ref[...] = -ADAM_LR * (m_hat / (jnp.sqrt(v_hat) + ADAM_EPS) + ADAM_WD * w_ref[...])
        nm_ref[...] = m_new
        nv_ref[...] = v_new

    spec = pl.BlockSpec((tr, C), lambda l, i: (l * nb + i, 0))
    pspec = pl.BlockSpec((N_DEV, tr, C), lambda l, i: (0, (l * stride + row_off) // tr + i, 0))
    return _call(name, body, (layers, nb), [pspec, spec, spec, spec], [spec] * 4,
                 [jax.ShapeDtypeStruct((R, C), F32)] * 4, [parts, w, m, v], sem=("parallel", "parallel"), comm=comm)


def _rot_cols(w):
    h = w.shape[-1] // 2
    return jnp.concatenate([-w[..., h:], w[..., :h]], axis=-1)


def _unrot_cols(d):
    h = d.shape[-1] // 2
    return jnp.concatenate([d[..., h:], -d[..., :h]], axis=-1)


IN_WIDTH = sum(IN_SIZES)
IN_SHARD = IN_WIDTH // N_DEV
IN_GATE = IN_WIDTH - IN_SIZES[-1]


def _w_in_cols(buf, a, b):
    out = []
    for j in range(a // IN_SHARD, (b - 1) // IN_SHARD + 1):
        lo, hi = max(a, IN_SHARD * j) - IN_SHARD * j, min(b, IN_SHARD * (j + 1)) - IN_SHARD * j
        out.append(buf[j, :, lo:hi])
    return out


def _w1(buf):
    kr0 = IN_SIZES[0] + IN_SIZES[1]
    kr = jnp.concatenate(_w_in_cols(buf, kr0, kr0 + QK_ROPE), axis=-1)
    w1s = jnp.concatenate(_w_in_cols(buf, 0, kr0) + [kr, _rot_cols(kr), jnp.zeros((D, 64), BF16)]
                          + _w_in_cols(buf, kr0 + QK_ROPE, IN_GATE), axis=-1)
    return w1s, jnp.concatenate(_w_in_cols(buf, IN_GATE, IN_WIDTH), axis=-1)


def _w_in_grad_shards(d1s, d1g):
    kr0 = IN_SIZES[0] + IN_SIZES[1]
    seg = d1s[:, C_KR:C_KR + 128]
    dkr = seg[:, :QK_ROPE] + _unrot_cols(seg[:, QK_ROPE:2 * QK_ROPE])
    runs = [(0, d1s, 0), (kr0, dkr, 0), (kr0 + QK_ROPE, d1s, C_PIN), (IN_GATE, d1g, 0), (IN_WIDTH, None, 0)]
    blocks = []
    for j in range(N_DEV):
        a, b = IN_SHARD * j, IN_SHARD * (j + 1)
        pieces = []
        for (c0, src, s0), (c1, _, _) in zip(runs[:-1], runs[1:]):
            lo, hi = max(a, c0), min(b, c1)
            if lo < hi:
                pieces.append(src[:, s0 + lo - c0:s0 + hi - c0])
        blocks.append(jnp.concatenate(pieces, axis=-1))
    return jnp.stack(blocks)


def _layer_weights(G):
    L = G['w_uq'].shape[0]
    wq = G['w_uq'].reshape(L, Q_LORA, HEADS, QK_NOPE + QK_ROPE)
    nope, rope = wq[..., :QK_NOPE], wq[..., QK_NOPE:]
    z32 = jnp.zeros((L, Q_LORA, HEADS, HP - QK_NOPE - QK_ROPE), BF16)
    wq_a = jnp.concatenate([nope, rope, z32], axis=-1).reshape(L, Q_LORA, QW)
    wq_b = jnp.concatenate([jnp.zeros_like(nope), _rot_cols(rope), z32], axis=-1).reshape(L, Q_LORA, QW)
    wkv = G['w_ukv'].reshape(L, KV_LORA, HEADS, QK_NOPE + V_HEAD)
    z64 = jnp.zeros((L, KV_LORA, HEADS, HP - QK_NOPE), BF16)
    wkn = jnp.concatenate([wkv[..., :QK_NOPE], z64], axis=-1).reshape(L, KV_LORA, QW)
    wv = jnp.concatenate([wkv[..., QK_NOPE:], z64], axis=-1).reshape(L, KV_LORA, QW)
    wa = G['w_br_a'].reshape(L, HEADS, V_HEAD, D)
    wa = jnp.concatenate([wa, jnp.zeros((L, HEADS, HP - V_HEAD, D), BF16)], axis=2).reshape(L, QW, D)
    return dict(
        Wq=jnp.concatenate([wq_a, wq_b], axis=-1),
        Wkv=jnp.concatenate([wkn, wv], axis=-1), Wa=wa, Wb=G['w_br_b'], Wc=G['w_br_c'], Wd=G['w_br_d'],
        conv=G['conv_w'].reshape(L, 3, BR).astype(F32))


def _tables(S):
    inv = ROPE_THETA ** (-jnp.arange(0, QK_ROPE, 2, dtype=F32) / QK_ROPE)
    ang = jnp.arange(S, dtype=F32)[:, None] * inv[None, :]
    cos, sin = jnp.cos(ang), jnp.sin(ang)
    scale = (QK_NOPE + QK_ROPE) ** -0.5
    one, zero = jnp.ones((S, QK_NOPE), F32), jnp.zeros((S, QK_NOPE), F32)
    z32 = jnp.zeros((S, HP - QK_NOPE - QK_ROPE), F32)
    cq = jnp.concatenate([one, cos, cos, z32], axis=1) * scale
    sq = jnp.concatenate([zero, sin, sin, z32], axis=1) * scale
    ck = jnp.concatenate([cos, cos, sin, sin, jnp.zeros((S, HP - 2 * QK_ROPE), F32)], axis=1)
    return cq, sq, ck


def _place_k():
    p = np.zeros((HP, QW), np.float32)
    for r in range(2 * QK_ROPE):
        for h in range(HEADS):
            p[r, h * HP + QK_NOPE + r % QK_ROPE] = 1.0
    return jnp.asarray(p, BF16)


def _layer_fwd(x, h, W, late, P, tabs, B, S, ride, g_next):
    cq_t, sq_t, ck_t, place = tabs
    g_cq, g_ckv = P['g_cq'], P['g_ckv']
    P1 = _mm("in_proj_s", h, W['W1s'], out_dtype=BF16)
    PG = _mm("in_proj_g", h, W['W1g'], out_dtype=BF16, comm=ride.get('in_proj'))

    def q_prep(c, cq, sq, g, w):
        qq = _dot(_rms(c, g), w)
        return qq[:, :QW] * jnp.tile(cq, (1, HEADS)) + qq[:, QW:] * jnp.tile(sq, (1, HEADS))
    q = _rw("q_prep", q_prep, [win(P1, 256, 0), tab(cq_t), tab(sq_t)], [g_cq, W['Wq']], [(QW, BF16)])[0]

    def kv_prep(c, seg, ck, g, w, place):
        kv = _dot(_rms(c, g), w)
        return kv[:, :QW] + _dot(seg * ck, place), kv[:, QW:]
    k, v = _rw("kv_prep", kv_prep, [win(P1, 128, 2), win(P1, 128, 3), tab(ck_t)], [g_ckv, W['Wkv'], place],
               [(QW, BF16), (QW, BF16)])
    o, lse = _attn_fwd(q, k, v, B, S, comm=ride.get('attn_fwd'))

    def pool_f(z, wbd, scale):
        return _dot(_pooled(z), wbd) * scale
    bo = _rw("pool_fwd", pool_f, [win(P1, BR, 2)], [P['pool_bd'], P['pool_scale']], [(BR, BF16)], tile=S)[0]

    def sgu_f(u, sv, g, wm, bias):
        vn = _rms(sv, g)
        blocks = [vn[r:r + SGU_BLOCK] for r in range(0, vn.shape[0], SGU_BLOCK)]
        return u * jnp.concatenate([_sgu_mix(vb, wm, bias) for vb in blocks], axis=0)
    co = _rw("sgu_fwd", sgu_f, [win(P1, BR, 3), win(P1, BR, 4)], [P['g_sgu_v'], P['sgu_wm'], P['sgu_bias']],
             [(BR, BF16)], tile=SGU_TILE)[0]

    def conv_f(b, c, xi, w):
        z = c * xi
        return b * (w[0:1] * _shift_down(z, 2) + w[1:2] * _shift_down(z, 1) + w[2:3] * z)
    do_ = _rw("conv_fwd", conv_f, [win(P1, BR, 5), win(P1, BR, 6), win(P1, BR, 7)], [W['conv']], [(BR, BF16)], tile=S)[0]

    def merge_f(a, b, c, d, l0, l1, l2, l3, wa, wb, wc, wd):
        return (_sigmoid(l0) * _dot(a, wa) + _sigmoid(l1) * _dot(b, wb) + _sigmoid(l2) * _dot(c, wc)
                + _sigmoid(l3) * _dot(d, wd))
    merged = _rw("merge_fwd", merge_f, [o, bo, co, do_] + [win(PG, D, i) for i in range(4)],
                 [W['Wa'], W['Wb'], W['Wc'], W['Wd']], [(D, BF16)])[0]
    W = dict(W, **late())
    om, x1, h2 = _mm_norm("out_proj", merged, W['Wout'], x, P['g_post_mix'], P['g_pre_ffn'])
    fa, fb, act = _ffn_up(h2, W['Wg'], W['Wu'], comm=ride.get('ffn_up'))
    f, x2, *h_next = _mm_norm("ffn_down", act, W['Wdn'], x1, P['g_post_ffn'], g_next, comm=ride.get('ffn_down'))
    saved = dict(x=x, h=h, P1=P1, PG=PG, q=q, k=k, v=v, o=o, lse=lse, bo=bo, co=co, do=do_, merged=merged, om=om, x1=x1,
                 h2=h2, fa=fa, fb=fb, act=act, f=f)
    return x2, (h_next[0] if h_next else None), saved, W


def _sgu_mix(vn, wm, bias):
    outs = [_dot(wm[g], vn) for g in range(4)]
    return _group_select(outs) + bias


def _layer_bwd(dx2, df, sv, W, P, tabs, B, S, ride, own_ride, prev):
    cq_t, sq_t, ck_t, place = tabs
    P1 = sv['P1']
    grads = {}
    da, db = _ffn_down_dx(df, W['Wdn'], sv['fa'], sv['fb'], comm=ride.get('ffn_down_dx'))
    grads['Wdn'] = _mm_t("ffn_down_dw", sv['act'], df)
    dh2 = _mm("ffn_up_dx", da, W['Wg'], tb=True, a2=db, b2=W['Wu'], out_dtype=BF16)
    grads['Wg'] = _mm_t("ffn_gate_dw", sv['h2'], da)
    grads['Wu'] = _mm_t("ffn_up_dw", sv['h2'], db)

    def mid_norms_b(x1, dh, dy, om, g_ffn, g_mix):
        dx, dg_ffn = _rms_bwd(x1, g_ffn, dh)
        dx1 = dy + dx
        dom, dg_mix = _rms_bwd(om, g_mix, dx1)
        return dx1, dom, dg_ffn, dg_mix
    dx1, dom, grads['g_pre_ffn'], grads['g_post_mix'] = _rw(
        "mid_norms_bwd", mid_norms_b, [sv['x1'], dh2, dx2, sv['om']], [P['g_pre_ffn'], P['g_post_mix']],
        [(D, F32), (D, BF16)], [(1, D), (1, D)], tile=512)
    dmerged = _mm("out_proj_dx", dom, W['Wout'], tb=True, out_dtype=BF16)
    grads['Wout'] = _mm_t("out_proj_dw", sv['merged'], dom)
    ride = dict(ride, **own_ride(grads))

    def merge_b(a, b, c, d, l0, l1, l2, l3, dm, wa, wb, wc, wd):
        outs_dl, outs_dy, outs_dbr = [], [], []
        for br, l, w in ((a, l0, wa), (b, l1, wb), (c, l2, wc), (d, l3, wd)):
            s = _sigmoid(l)
            y = _dot(br, w)
            dy = (dm * s).astype(BF16)
            outs_dl.append((dm * y * s * (1.0 - s)).astype(BF16))
            outs_dy.append(dy)
            outs_dbr.append(_dot_nt(dy, w))
        return (jnp.concatenate(outs_dl, axis=1), *outs_dy, *outs_dbr)
    mb = _rw("merge_bwd", merge_b,
             [sv['o'], sv['bo'], sv['co'], sv['do']] + [win(sv['PG'], D, i) for i in range(4)] + [dmerged],
             [W['Wa'], W['Wb'], W['Wc'], W['Wd']],
             [(4 * D, BF16)] + [(D, BF16)] * 4 + [(QW, F32), (BR, F32), (BR, F32), (BR, F32)], comm=ride.get('merge_bwd'))
    dl, dy, (d_o, d_bo, d_co, d_do) = mb[0], mb[1:5], mb[5:9]
    grads['Wa'] = _mm_t("br_a_dw", sv['o'], dy[0])
    grads['Wb'] = _mm_t("br_b_dw", sv['bo'], dy[1])
    grads['Wc'] = _mm_t("br_c_dw", sv['co'], dy[2])
    grads['Wd'] = _mm_t("br_d_dw", sv['do'], dy[3])

    def conv_b(b, c, xi, dout, w):
        z = c * xi
        z1, z2 = _shift_down(z, 1), _shift_down(z, 2)
        y = w[0:1] * z2 + w[1:2] * z1 + w[2:3] * z
        dyv = dout * b
        dz = w[2:3] * dyv + w[1:2] * _shift_up(dyv, 1) + w[0:1] * _shift_up(dyv, 2)
        return (dout * y, dz * xi, dz * c, jnp.sum(dyv * z2, axis=0, keepdims=True),
                jnp.sum(dyv * z1, axis=0, keepdims=True), jnp.sum(dyv * z, axis=0, keepdims=True))
    dcvb, dcvc, dcvx, dw0, dw1, dw2 = _rw("conv_bwd", conv_b, [win(P1, BR, 5), win(P1, BR, 6), win(P1, BR, 7), d_do],
                                          [W['conv']], [(BR, BF16)] * 3, [(1, BR)] * 3, tile=S)
    grads['conv'] = jnp.concatenate([dw0, dw1, dw2], axis=0)

    def sgu_b(u, s_v, dout, g, wm, bias):
        vn = _rms(s_v, g)
        dmix = dout * u
        lane = lax.broadcasted_iota(jnp.int32, (SGU_BLOCK, BR), 1) // GROUP
        mixed, dvn, dwm, dbias = [], [], [0.0] * 4, 0.0
        for r in range(0, vn.shape[0], SGU_BLOCK):
            vb, db = vn[r:r + SGU_BLOCK], dmix[r:r + SGU_BLOCK]
            mixed.append(_sgu_mix(vb, wm, bias))
            dvn.append(_group_select([_dot_tn(wm[gi], db) for gi in range(4)]))
            dwm = [dwm[gi] + _dot_nt(jnp.where(lane == gi, db, 0.0), vb) for gi in range(4)]
            dbias = dbias + db
        dsv, dg = _rms_bwd(s_v, g, jnp.concatenate(dvn, axis=0))
        return (dout * jnp.concatenate(mixed, axis=0), dsv, dg, *dwm, dbias)
    dsu, dsv_, grads['g_sgu_v'], wm0, wm1, wm2, wm3, grads['sgu_bias'] = _rw(
        "sgu_bwd", sgu_b, [win(P1, BR, 3), win(P1, BR, 4), d_co], [P['g_sgu_v'], P['sgu_wm'], P['sgu_bias']],
        [(BR, BF16)] * 2, [(1, BR)] + [(SGU_BLOCK, SGU_BLOCK)] * 4 + [(SGU_BLOCK, BR)], tile=SGU_TILE)
    grads['sgu_wm'] = jnp.stack([wm0, wm1, wm2, wm3])

    def pool_b(z, dout, wbd, scale):
        pooled = _pooled(z)
        mixed = _dot(pooled, wbd)
        dmix = dout * scale
        dz = _pooled_bwd(_dot_nt(dmix, wbd))
        return dz, _dot_tn(pooled, dmix), jnp.sum(dout * mixed, axis=0, keepdims=True)
    dpin, grads['pool_bd'], grads['pool_scale'] = _rw(
        "pool_bwd", pool_b, [win(P1, BR, 2), d_bo], [P['pool_bd'], P['pool_scale']], [(BR, BF16)], [(BR, BR), (1, BR)], tile=S)

    dq, dk, dv = _attn_bwd(sv['q'], sv['k'], sv['v'], sv['o'], d_o, sv['lse'], B, S, comm=ride.get('attn_bwd'))

    def q_prep_b(c, dq, cq, sq, g, w):
        dqq = jnp.concatenate([dq * jnp.tile(cq, (1, HEADS)), dq * jnp.tile(sq, (1, HEADS))], axis=1).astype(BF16)
        dc, dg = _rms_bwd(c, g, _dot_nt(dqq, w))
        return dc, _rms(c, g), dqq, dg
    dcq, cqn, dqq, grads['g_cq'] = _rw("q_prep_bwd", q_prep_b, [win(P1, 256, 0), dq, tab(cq_t), tab(sq_t)],
                                       [P['g_cq'], W['Wq']], [(Q_LORA, BF16), (Q_LORA, BF16), (2 * QW, BF16)], [(1, Q_LORA)])
    grads['Wq'] = _mm_t("uq_dw", cqn, dqq)

    def kv_prep_b(c, dk, dv, ck, g, w, place):
        dkv = jnp.concatenate([dk, dv], axis=1)
        dc, dg = _rms_bwd(c, g, _dot_nt(dkv, w))
        return dc, _dot_nt(dk, place) * ck, _rms(c, g), dkv, dg
    dckv, dseg, kvn, dkv, grads['g_ckv'] = _rw(
        "kv_prep_bwd", kv_prep_b, [win(P1, 128, 2), dk, dv, tab(ck_t)], [P['g_ckv'], W['Wkv'], place],
        [(KV_LORA, BF16), (HP, BF16), (KV_LORA, BF16), (2 * QW, BF16)], [(1, KV_LORA)])
    grads['Wkv'] = _mm_t("ukv_dw", kvn, dkv)

    dps = jnp.concatenate([dcq, dckv, dseg, dpin, dsu, dsv_, dcvb, dcvc, dcvx], axis=1)
    dh = _mm("in_proj_dx", dps, W['W1s'], tb=True, a2=dl, b2=W['W1g'], out_dtype=BF16, tm=256, comm=ride.get('in_proj_dx'))
    grads['W1s'] = _mm_t("in_proj_s_dw", sv['h'], dps)
    grads['W1g'] = _mm_t("in_proj_g_dw", sv['h'], dl)

    if prev is None:
        def pre_mix_b(x, dh, dy, g):
            dx, dg = _rms_bwd(x, g, dh)
            return dy + dx, dg
        dx, grads['g_pre_mix'] = _rw("pre_mix_bwd", pre_mix_b, [sv['x'], dh, dx1], [P['g_pre_mix']], [(D, F32)], [(1, D)], tile=512)
        return dx, None, None, grads, ride

    def edge_norms_b(x, dh, dy, f, g_mix, g_ffn):
        dx, dg_mix = _rms_bwd(x, g_mix, dh)
        dx = dy + dx
        df, dg_ffn = _rms_bwd(f, g_ffn, dx)
        return dx, df, dg_mix, dg_ffn
    dx, df_prev, grads['g_pre_mix'], dg_prev = _rw(
        "edge_norms_bwd", edge_norms_b, [sv['x'], dh, dx1, prev[0]], [P['g_pre_mix'], prev[1]],
        [(D, F32), (D, BF16)], [(1, D), (1, D)], tile=512)
    return dx, df_prev, dg_prev, grads, ride


def _param_grads(gl):
    st = {k: jnp.stack([g[k] for g in gl]) for k in gl[0]}
    L = len(gl)
    out = {}
    qa =st['Wq'][..., :QW].reshape(L, Q_LORA, HEADS, HP)
    qb = st['Wq'][..., QW:].reshape(L, Q_LORA, HEADS, HP)
    drope = qa[..., QK_NOPE:QK_NOPE + QK_ROPE] + _unrot_cols(qb[..., QK_NOPE:QK_NOPE + QK_ROPE])
    out['w_uq'] = jnp.concatenate([qa[..., :QK_NOPE], drope], axis=-1).reshape(L, Q_LORA, HEADS * (QK_NOPE + QK_ROPE))
    kn = st['Wkv'][..., :QW].reshape(L, KV_LORA, HEADS, HP)[..., :QK_NOPE]
    vv = st['Wkv'][..., QW:].reshape(L, KV_LORA, HEADS, HP)[..., :V_HEAD]
    out['w_ukv'] = jnp.concatenate([kn, vv], axis=-1).reshape(L, KV_LORA, HEADS * (QK_NOPE + V_HEAD))
    out['w_br_a'] = st['Wa'].reshape(L, HEADS, HP, D)[:, :, :V_HEAD].reshape(L, HEADS * V_HEAD, D)
    out['w_br_b'], out['w_br_c'], out['w_br_d'] = st['Wb'], st['Wc'], st['Wd']
    out['conv_w'] = st['conv'].reshape(L, 3, 1, BR)
    for n in ('g_pre_mix', 'g_cq', 'g_ckv', 'g_sgu_v', 'g_post_mix', 'g_pre_ffn', 'g_post_ffn', 'pool_scale'):
        out[n] = st[n].reshape(L, -1)
    bd = st['pool_bd'].reshape(L, 4, GROUP, 4, GROUP)
    out['pool_w'] = jnp.stack([bd[:, g, :, g, :] for g in range(4)], axis=1)
    out['sgu_w'] = st['sgu_wm'] * _sgu_mask()[None, None]
    out['sgu_b'] = st['sgu_bias'].reshape(L, SGU_BLOCK, 4, GROUP).sum(-1).transpose(0, 2, 1)
    return out


def _sgu_mask():
    pc = np.arange(SGU_BLOCK) // CHUNK
    return jnp.asarray(pc[:, None] >= pc[None, :], F32)


def _rows(a, lead):
    return a.reshape(a.shape[:lead] + (-1, a.shape[-1]))


def _pad_to(a, rows, cols):
    pad = [(0, 0)] * (a.ndim - 2) + [(0, rows - a.shape[-2]), (0, cols - a.shape[-1])]
    return jnp.pad(a, pad)


GROUPS = (('w_in',), ('w_ffn_gate',), ('w_ffn_up',), ('w_out', 'w_ffn_down'),
          ('w_uq', 'w_ukv', 'w_br_a', 'w_br_b', 'w_br_c', 'w_br_d', 'conv_w'))
PADDED = {'w_uq': (Q_LORA, 128), 'conv_w': (128, 128), 'w_ffn_down': (384, D)}
NARROW = ('w_uq', 'conv_w')


def _group(t, lead, groups):
    bufs = {}
    for b in groups:
        ps = [_rows(t[n], lead) for n in GROUPS[b]]
        ps = [_pad_to(p, *PADDED[n]) if n in PADDED else p for n, p in zip(GROUPS[b], ps)]
        bufs[b] = ps[0] if len(ps) == 1 else jnp.concatenate(ps, axis=-2)
    return bufs


def _group_rows(shapes):
    where, rows = {}, []
    for b, names in enumerate(GROUPS):
        off = 0
        for n in names:
            where[n] = (b, off)
            off += PADDED.get(n, shapes[n])[0]
        rows.append(off)
    return where, rows


def _shard_split(a, axis):
    n = a.shape[axis]
    a = a.reshape(a.shape[:axis] + (N_DEV, n // N_DEV) + a.shape[axis + 1:])
    return jnp.moveaxis(a, axis, 0)


def _shard_join(a, axis):
    a = jnp.moveaxis(a, 0, axis)
    return a.reshape(a.shape[:axis] + (a.shape[axis] * a.shape[axis + 1],) + a.shape[axis + 2:])


def _as2d(a):
    return a.reshape(-1, a.shape[-1])


def kernel(x, w_in, g_pre_mix, g_cq, g_ckv, w_uq, w_ukv, pool_w, pool_scale, g_sgu_v, sgu_w, sgu_b, conv_w, w_br_a, w_br_b, w_br_c, w_br_d, w_out, g_post_mix, g_pre_ffn, w_ffn_gate, w_ffn_up, w_ffn_down, g_post_ffn, loss_target, m_w_in, m_g_pre_mix, m_g_cq, m_g_ckv, m_w_uq, m_w_ukv, m_pool_w, m_pool_scale, m_g_sgu_v, m_sgu_w, m_sgu_b, m_conv_w, m_w_br_a, m_w_br_b, m_w_br_c, m_w_br_d, m_w_out, m_g_post_mix, m_g_pre_ffn, m_w_ffn_gate, m_w_ffn_up, m_w_ffn_down, m_g_post_ffn, v_w_in, v_g_pre_mix, v_g_cq, v_g_ckv, v_w_uq, v_w_ukv, v_pool_w, v_pool_scale, v_g_sgu_v, v_sgu_w, v_sgu_b, v_conv_w, v_w_br_a, v_w_br_b, v_w_br_c, v_w_br_d, v_w_out, v_g_post_mix, v_g_pre_ffn, v_w_ffn_gate, v_w_ffn_up, v_w_ffn_down, v_g_post_ffn):
    args = dict(locals())
    w = {n: args[n] for n in WEIGHTS}
    m = {n: args['m_' + n] for n in WEIGHTS}
    v = {n: args['v_' + n] for n in WEIGHTS}
    B, S, _ = x.shape
    T = B * S
    L = DEPTH
    names = list(SHARDED)
    shapes1 = {n: _as2d(w[n][0]).shape for n in names}
    where, group_rows = _group_rows(shapes1)

    def shard_bufs(l, groups):
        return _group({n: w[n][l:l + 1].astype(BF16) for b in groups for n in GROUPS[b]}, 0, groups)

    def joined(gathered):
        G = {}
        for b, buf in gathered.items():
            for n in GROUPS[b]:
                if n == 'w_in':
                    continue
                off = where[n][1]
                r, c = shapes1[n]
                G[n] = _shard_join(buf[:, off:off + r, :c].reshape((N_DEV, 1) + w[n].shape[1:]), SHARDED[n])
        return G

    eye = jnp.eye(4, dtype=F32)
    pool_bd = (pool_w[:, :, :, None, :] * eye[None, :, None, :, None]).reshape(L, BR, BR).astype(BF16)
    sgu_wm = (sgu_w * _sgu_mask()[None, None]).astype(BF16)
    sgu_bias = jnp.repeat(sgu_b.transpose(0, 2, 1), GROUP, axis=2)
    tabs = _tables(S) + (_place_k(),)

    def layer_params(l):
        P = {n: w[n][l][None, :] for n in ('g_pre_mix', 'g_cq', 'g_ckv', 'g_sgu_v', 'g_post_mix', 'g_pre_ffn',
                                            'g_post_ffn', 'pool_scale')}
        P.update(pool_bd=pool_bd[l], sgu_wm=sgu_wm[l], sgu_bias=sgu_bias[l])
        return P

    xt = x.reshape(T, D)
    h = _rw("pre_mix", lambda x, g: _rms(x, g), [xt], [w['g_pre_mix'][0][None, :]], [(D, BF16)], tile=512)[0]
    saved, Ws = [], []
    first = shard_bufs(0, (0, 4))
    early = dict(zip((0, 4), _all_gather("gather_weights", [first[0], first[4]])))
    for l in range(L):
        own = shard_bufs(l, (1, 2, 3))
        nxt = shard_bufs(l + 1, (0, 4)) if l + 1 < L else None
        ride = {'in_proj': _Gather([own[1]]), 'attn_fwd': _Gather([own[3], own[2]])}
        if nxt:
            ride.update(ffn_up=_Gather([nxt[0]]), ffn_down=_Gather([nxt[4]]))

        def late(ride=ride):
            G = joined({1: ride['in_proj'].results[0], 3: ride['attn_fwd'].results[0], 2: ride['attn_fwd'].results[1]})
            return dict(Wout=G['w_out'][0], Wg=G['w_ffn_gate'][0], Wu=G['w_ffn_up'][0], Wdn=G['w_ffn_down'][0])

        We = {k: a[0] for k, a in _layer_weights(joined(early)).items()}
        We['W1s'], We['W1g'] = _w1(early[0])
        g_next = w['g_pre_mix'][l + 1][None, :] if nxt else None
        xt, h, sv, W = _layer_fwd(xt, h, We, late, layer_params(l), tabs, B, S, ride, g_next)
        saved.append(sv)
        Ws.append(W)
        if nxt:
            early = {0: ride['ffn_up'].results[0], 4: ride['ffn_down'].results[0]}

    def loss_f(y, t, f, g):
        e = y - t
        dy = e * (1.0 / D)
        df, dg = _rms_bwd(f, g, dy)
        return dy, df, jnp.sum(e * e, axis=0, keepdims=True), dg
    dy, df, sq, dg_ffn = _rw("loss", loss_f, [xt, loss_target.reshape(T, D), saved[-1]['f']], [w['g_post_ffn'][L - 1][None, :]],
                             [(D, F32), (D, BF16)], [(1, D), (1, D)], tile=512)
    loss = lax.psum(0.5 * jnp.sum(sq) / D, ("x", "y", "c"))

    recv = [lax.empty((N_DEV, L * r, PADDED.get(ns[0], shapes1[ns[0]])[1]), BF16) for ns, r in zip(GROUPS, group_rows)]

    def send_bufs(pg, groups):
        return _group({n: _shard_split(pg[n], SHARDED[n]).astype(BF16) for b in groups for n in GROUPS[b]}, 1, groups)

    small, pending = [None] * L, None
    for l in reversed(range(L)):
        ride = {}
        if pending is not None:
            ride = {'ffn_down_dx': _Scatter([pending[4]], [recv[4]], [(l + 1) * group_rows[4]]),
                    'in_proj_dx': _Scatter([pending[0]], [recv[0]], [(l + 1) * group_rows[0]])}

        def own_ride(grads, l=l):
            bufs = send_bufs({'w_ffn_gate': grads['Wg'][None], 'w_ffn_up': grads['Wu'][None],
                              'w_ffn_down': grads['Wdn'][None], 'w_out': grads['Wout'][None]}, (1, 2, 3))
            return {'merge_bwd': _Scatter([bufs[1]], [recv[1]], [l * group_rows[1]]),
                    'attn_bwd': _Scatter([bufs[3], bufs[2]], [recv[3], recv[2]], [l * group_rows[3], l * group_rows[2]])}

        prev = (saved[l - 1]['f'], w['g_post_ffn'][l - 1][None, :]) if l else None
        dy, df_prev, dg_prev, gl, ride = _layer_bwd(dy, df, saved[l], Ws[l], layer_params(l), tabs, B, S, ride, own_ride, prev)
        gl['g_post_ffn'] = dg_ffn
        df, dg_ffn = df_prev, dg_prev
        for name, bs in (('ffn_down_dx', (4,)), ('in_proj_dx', (0,)), ('merge_bwd', (1,)), ('attn_bwd', (3, 2))):
            if name in ride:
                for b, res_b in zip(bs, ride[name].results):
                    recv[b] = res_b
        pg = _param_grads([gl])
        pending = send_bufs(pg, (4,))
        pending[0] = _w_in_grad_shards(gl['W1s'], gl['W1g'])
        rep = jnp.concatenate([pg[n].reshape(-1, 128) for n in REPLICATED], axis=0)
        small[l] = _pad_to(rep, -(-rep.shape[0] // 8) * 8, 128)
    grad_x = dy.reshape(B, S, D)

    small_rows = small[0].shape[0]
    got_small = _all_gather("gather_small_grads", [jnp.concatenate(small, axis=0)])[0].reshape(N_DEV, L, small_rows, 128)
    last = {'w_ffn_down': _Scatter([pending[0], pending[4]], [recv[0], recv[4]], [0, 0])}

    res = {}
    for n in sorted(names, key=lambda n: n not in last):
        b, off = where[n]
        r, c = shapes1[n]
        if n in NARROW:
            parts = recv[b].reshape(N_DEV, L, group_rows[b], -1)[:, :, off:off + r, :c].reshape(N_DEV, L * r, c)
            res[n] = _adamw("adamw_" + n, parts, _as2d(w[n]), _as2d(m[n]), _as2d(v[n]))
        else:
            res[n] = _adamw("adamw_" + n, recv[b], _as2d(w[n]), _as2d(m[n]), _as2d(v[n]), row_off=off, layers=L,
                            stride=group_rows[b], comm=last.get(n))
        if n == 'w_ffn_down':
            recv[0], recv[4] = last[n].results
    off = 0
    for n in REPLICATED:
        r = int(np.prod(w[n].shape[1:])) // 128
        shp2 = _as2d(w[n]).shape
        res[n] = _adamw("adamw_" + n, got_small[:, :, off:off + r].reshape((N_DEV,) + shp2), _as2d(w[n]), _as2d(m[n]), _as2d(v[n]))
        off += r
    outs = [loss, grad_x]
    for k in range(4):
        outs += [res[n][k].reshape(w[n].shape) for n in WEIGHTS]
    return tuple(outs)
```

```python
import functools

import numpy as np
import jax
import jax.numpy as jnp
from jax import lax
from jax.experimental import pallas as pl
from jax.experimental.pallas import tpu as pltpu

F32, BF16 = jnp.float32, jnp.bfloat16
MESH = pl.DeviceIdType.MESH
N_DEV = 8

D = 1024
DEPTH = 4
CHUNK = 64
EPS = 1e-6
NEG_INF = -1e30
HEADS, QK_NOPE, QK_ROPE, V_HEAD = 8, 64, 32, 64
Q_LORA, KV_LORA = 256, 128
ROPE_THETA = 10000.0
POOL_WINDOWS = (2, 4, 8, 16)
GROUP = 64
BR = 256
SGU_BLOCK = 128
SGU_TILE = 4 * SGU_BLOCK
D_FF = 2816
IN_SIZES = (256, 128, 32, 256, 256, 256, 256, 256, 256, 4096)
HP = 128
QW = HEADS * HP
C_CQ, C_CKV, C_KR, C_PIN, C_SU, C_SV, C_CVB, C_CVC, C_CVX, C_GATE = 0, 256, 384, 512, 768, 1024, 1280, 1536, 1792, 2048

ADAM_LR, ADAM_B1, ADAM_B2, ADAM_EPS, ADAM_WD, ADAM_STEP = 0.001, 0.9, 0.999, 1e-08, 0.01, 10

WEIGHTS = ['w_in', 'g_pre_mix', 'g_cq', 'g_ckv', 'w_uq', 'w_ukv', 'pool_w', 'pool_scale', 'g_sgu_v', 'sgu_w', 'sgu_b',
           'conv_w', 'w_br_a', 'w_br_b', 'w_br_c', 'w_br_d', 'w_out', 'g_post_mix', 'g_pre_ffn', 'w_ffn_gate',
           'w_ffn_up', 'w_ffn_down', 'g_post_ffn']
SHARDED = {'w_in': 2, 'w_uq': 2, 'w_ukv': 2, 'conv_w': 3, 'w_br_a': 2, 'w_br_b': 2, 'w_br_c': 2, 'w_br_d': 2,
           'w_out': 1, 'w_ffn_gate': 2, 'w_ffn_up': 2, 'w_ffn_down': 1}
REPLICATED = [n for n in WEIGHTS if n not in SHARDED]

VMEM_LIMIT = 56 * 1024 * 1024
TQ = 256


def _cparams(sem=None):
    return pltpu.CompilerParams(vmem_limit_bytes=VMEM_LIMIT, dimension_semantics=sem)


def _tile(n, cap, align=128):
    if n <= cap:
        return n
    for t in range(cap - cap % align, 0, -align):
        if n % t == 0:
            return t
    return n


def _peers():
    x_, y_, c_ = lax.axis_index("x"), lax.axis_index("y"), lax.axis_index("c")
    out = []
    for k in range(1, N_DEV):
        px, py, pc = (x_ + (k >> 2)) % 2, (y_ + ((k >> 1) & 1)) % 2, (c_ + (k & 1)) % 2
        out.append((k, (px, py, pc), 4 * px + 2 * py + pc))
    return 4 * x_ + 2 * y_ + c_, out


class _Exchange:
    def __init__(self, n):
        self.n = n
        self.scratch = [pltpu.SemaphoreType.DMA((7 * n,)), pltpu.SemaphoreType.DMA((7 * n,)),
                        pltpu.SemaphoreType.DMA((n,))]
        self.results = None

    def start(self, cin, cout, sems):
        local, sends, _ = self.copies(cin, cout, sems)
        for cp in local + sends:
            cp.start()

    def wait(self, cin, cout, sems):
        local, sends, recvs = self.copies(cin, cout, sems)
        for cp in recvs:
            cp.wait_recv()
        for cp in sends:
            cp.wait_send()
        for cp in local:
            cp.wait()


class _Gather(_Exchange):
    def __init__(self, xs):
        super().__init__(len(xs))
        self.inputs = list(xs)
        self.out_shape = [jax.ShapeDtypeStruct((N_DEV,) + x.shape, x.dtype) for x in xs]
        self.aliases = {}

    def copies(self, cin, cout, sems):
        send_sems, recv_sems, local_sems = sems
        me, peers = _peers()
        local, sends, recvs = [], [], []
        for a in range(self.n):
            local.append(pltpu.make_async_copy(cin[a], cout[a].at[me], local_sems.at[a]))
            for k, dev, flat in peers:
                sem = dict(send_sem=send_sems.at[7 * a + k - 1], recv_sem=recv_sems.at[7 * a + k - 1],
                           device_id=dev, device_id_type=MESH)
                sends.append(pltpu.make_async_remote_copy(src_ref=cin[a], dst_ref=cout[a].at[me], **sem))
                recvs.append(pltpu.make_async_remote_copy(src_ref=cin[a], dst_ref=cout[a].at[flat], **sem))
        return local, sends, recvs


class _Scatter(_Exchange):
    def __init__(self, xs, recv, row0):
        super().__init__(len(xs))
        self.inputs = list(xs) + list(recv)
        self.out_shape = [jax.ShapeDtypeStruct(r.shape, r.dtype) for r in recv]
        self.aliases = {self.n + a: a for a in range(self.n)}
        self.row0 = list(row0)

    def copies(self, cin, cout, sems):
        send_sems, recv_sems, local_sems = sems
        me, peers = _peers()
        local, sends, recvs = [], [], []
        for a in range(self.n):
            rows = pl.ds(self.row0[a], self.inputs[a].shape[1])
            local.append(pltpu.make_async_copy(cin[a].at[me], cout[a].at[me, rows], local_sems.at[a]))
            for k, dev, flat in peers:
                sem = dict(send_sem=send_sems.at[7 * a + k - 1], recv_sem=recv_sems.at[7 * a + k - 1],
                           device_id=dev, device_id_type=MESH)
                sends.append(pltpu.make_async_remote_copy(src_ref=cin[a].at[flat], dst_ref=cout[a].at[me, rows], **sem))
                recvs.append(pltpu.make_async_remote_copy(src_ref=cin[a].at[me], dst_ref=cout[a].at[flat, rows], **sem))
        return local, sends, recvs


def _call(name, body, grid, in_specs, out_specs, out_shape, arrays, scratch=(), sem=None, comm=None):
    if comm is None:
        res = pl.pallas_call(body, name=name, grid=grid, in_specs=list(in_specs), out_specs=list(out_specs),
                             out_shape=list(out_shape), scratch_shapes=list(scratch), compiler_params=_cparams(sem))(*arrays)
        return list(res)
    ni, no, ns = len(in_specs), len(out_specs), len(scratch)
    nci, nco = len(comm.inputs), len(comm.out_shape)
    hbm = pl.BlockSpec(memory_space=pl.ANY)

    def wrapped(*refs):
        ins, cin = refs[:ni], refs[ni:ni + nci]
        o0 = ni + nci
        outs, cout = refs[o0:o0 + no], refs[o0 + no:o0 + no + nco]
        s0 = o0 + no + nco
        scr, sems = refs[s0:s0 + ns], refs[s0 + ns:]
        ids = [pl.program_id(d) for d in range(len(grid))]
        first = functools.reduce(jnp.logical_and, [i == 0 for i in ids])
        last = functools.reduce(jnp.logical_and, [i == g - 1 for i, g in zip(ids, grid)])

        @pl.when(first)
        def _():
            comm.start(cin, cout, sems)

        body(*ins, *outs, *scr)

        @pl.when(last)
        def _():
            comm.wait(cin, cout, sems)

    res = pl.pallas_call(
        wrapped, name=name, grid=grid, in_specs=list(in_specs) + [hbm] * nci, out_specs=list(out_specs) + [hbm] * nco,
        out_shape=list(out_shape) + list(comm.out_shape), scratch_shapes=list(scratch) + comm.scratch,
        input_output_aliases={ni + i: no + j for i, j in comm.aliases.items()},
        compiler_params=_cparams(("arbitrary",) * len(grid)))(*arrays, *comm.inputs)
    comm.results = list(res[no:])
    return list(res[:no])


def win(arr, width, idx):
    return ('win', arr, width, idx)


def tab(arr):
    return ('tab', arr)


def _rw(name, fn, ins, consts, outs, accs=(), tile=256, comm=None):
    first = ins[0][1] if isinstance(ins[0], tuple) else ins[0]
    T = first.shape[0]
    tile = min(tile, T)
    grid = (T // tile,)
    arrays, in_specs = [], []
    for x in ins:
        if isinstance(x, tuple) and x[0] == 'win':
            _, a, w, k = x
            arrays.append(a)
            in_specs.append(pl.BlockSpec((tile, w), lambda i, k=k: (i, k)))
        elif isinstance(x, tuple) and x[0] == 'tab':
            a = x[1]
            nb = a.shape[0] // tile
            arrays.append(a)
            in_specs.append(pl.BlockSpec((tile, a.shape[1]), lambda i, nb=nb: (i % nb, 0)))
        else:
            arrays.append(x)
            in_specs.append(pl.BlockSpec((tile, x.shape[1]), lambda i: (i, 0)))
    for c in consts:
        arrays.append(c)
        in_specs.append(pl.BlockSpec(c.shape, lambda i, n=c.ndim: (0,) * n))
    out_shape = [jax.ShapeDtypeStruct((T, f), dt) for f, dt in outs]
    out_specs = [pl.BlockSpec((tile, f), lambda i: (i, 0)) for f, _ in outs]
    for shp in accs:
        out_shape.append(jax.ShapeDtypeStruct(shp, F32))
        out_specs.append(pl.BlockSpec(shp, lambda i, n=len(shp): (0,) * n))
    ni, nc, no = len(ins), len(consts), len(outs)

    def body(*refs):
        vals_in = [r[...] for r in refs[:ni]]
        vals_in = [v.astype(F32) if v.dtype == BF16 else v for v in vals_in]
        vals = fn(*vals_in, *[r[...] for r in refs[ni:ni + nc]])
        if not isinstance(vals, (tuple, list)):
            vals = (vals,)
        o_refs = refs[ni + nc:]
        for r, v in zip(o_refs[:no], vals[:no]):
            r[...] = v.astype(r.dtype)
        i = pl.program_id(0)
        for r, v in zip(o_refs[no:], vals[no:]):
            @pl.when(i == 0)
            def _(r=r, v=v):
                r[...] = v

            @pl.when(i > 0)
            def _(r=r, v=v):
                r[...] += v

    return _call(name, body, grid, in_specs, out_specs, out_shape, arrays, sem=("arbitrary",), comm=comm)


def _mm(name, a, b, tb=False, out_dtype=F32, tm=512, tn=3072, a2=None, b2=None, comm=None):
    M, K = a.shape
    N = b.shape[0] if tb else b.shape[1]
    tm, tn = _tile(M, tm, 8), _tile(N, tn)
    dims = (((1,), (1,)), ((), ())) if tb else (((1,), (0,)), ((), ()))
    pairs = [(a, b)] + ([(a2, b2)] if a2 is not None else [])

    def body(*refs):
        o_ref = refs[-1]
        acc = None
        for i in range(len(pairs)):
            p = lax.dot_general(refs[2 * i][...].astype(BF16), refs[2 * i + 1][...].astype(BF16), dims,
                                preferred_element_type=F32)
            acc = p if acc is None else acc + p
        o_ref[...] = acc.astype(o_ref.dtype)

    in_specs, arrays = [], []
    for x, y in pairs:
        k = x.shape[1]
        in_specs.append(pl.BlockSpec((tm, k), lambda j, i: (i, 0)))
        in_specs.append(pl.BlockSpec((tn, k), lambda j, i: (j, 0)) if tb else pl.BlockSpec((k, tn), lambda j, i: (0, j)))
        arrays += [x, y]
    return _call(name, body, (N // tn, M // tm), in_specs, [pl.BlockSpec((tm, tn), lambda j, i: (i, j))],
                 [jax.ShapeDtypeStruct((M, N), out_dtype)], arrays, sem=("parallel", "parallel"), comm=comm)[0]


def _mm_t(name, a, g, tk=1408, tn=1408, tt=2048, comm=None):
    T, K = a.shape
    N = g.shape[1]
    tk, tn, tt = _tile(K, tk), _tile(N, tn), _tile(T, tt, 8)
    nt = T // tt

    def body(a_ref, g_ref, o_ref, acc_ref):
        p = lax.dot_general(a_ref[...].astype(BF16), g_ref[...].astype(BF16), (((0,), (0,)), ((), ())),
                            preferred_element_type=F32)
        t = pl.program_id(2)

        @pl.when(t == 0)
        def _():
            acc_ref[...] = p

        @pl.when(t > 0)
        def _():
            acc_ref[...] += p

        @pl.when(t == nt - 1)
        def _():
            o_ref[...] = acc_ref[...].astype(BF16)

    return _call(name, body, (K // tk, N // tn, nt),
                 [pl.BlockSpec((tt, tk), lambda i, j, t: (t, i)), pl.BlockSpec((tt, tn), lambda i, j, t: (t, j))],
                 [pl.BlockSpec((tk, tn), lambda i, j, t: (i, j))], [jax.ShapeDtypeStruct((K, N), BF16)], [a, g],
                 scratch=[pltpu.VMEM((tk, tn), F32)], sem=("parallel", "parallel", "arbitrary"), comm=comm)[0]


def _dot(a, b):
    return jnp.dot(a.astype(BF16), b.astype(BF16), preferred_element_type=F32)


def _dot_nt(a, b):
    return lax.dot_general(a.astype(BF16), b.astype(BF16), (((1,), (1,)), ((), ())), preferred_element_type=F32)


def _dot_tn(a, b):
    return lax.dot_general(a.astype(BF16), b.astype(BF16), (((0,), (0,)), ((), ())), preferred_element_type=F32)


def _sigmoid(x):
    return 0.5 * jnp.tanh(0.5 * x) + 0.5


def _ffn_up(h, wg, wu, comm=None):
    M, K = h.shape
    N = wg.shape[1]
    tm, tn = _tile(M, 512, 8), _tile(N, 1408)

    def body(h_ref, wg_ref, wu_ref, a_ref, b_ref, act_ref):
        hb = h_ref[...]
        a = jnp.dot(hb, wg_ref[...], preferred_element_type=F32)
        b = jnp.dot(hb, wu_ref[...], preferred_element_type=F32)
        a_ref[...] = a.astype(BF16)
        b_ref[...] = b.astype(BF16)
        act_ref[...] = (a * _sigmoid(a) * b).astype(BF16)

    wspec = pl.BlockSpec((K, tn), lambda j, i: (0, j))
    ospec = pl.BlockSpec((tm, tn), lambda j, i: (i, j))
    return _call("ffn_up", body, (N // tn, M // tm), [pl.BlockSpec((tm, K), lambda j, i: (i, 0)), wspec, wspec],
                 [ospec] * 3, [jax.ShapeDtypeStruct((M, N), BF16)] * 3, [h, wg, wu], sem=("parallel", "parallel"),
                 comm=comm)


def _ffn_down_dx(df, wdn, a, b):
    M, K = df.shape
    N = wdn.shape[0]
    tm, tn = _tile(M, 512, 8), _tile(N, 1408)

    def body(df_ref, w_ref, a_ref, b_ref, da_ref, db_ref):
        d = _dot_nt(df_ref[...], w_ref[...])
        av, bv = a_ref[...].astype(F32), b_ref[...].astype(F32)
        s = _sigmoid(av)
        da_ref[...] = (d * bv * (s + av * s * (1.0 - s))).astype(BF16)
        db_ref[...] = (d * av * s).astype(BF16)

    tspec = pl.BlockSpec((tm, tn), lambda j, i: (i, j))
    return _call("ffn_down_dx", body, (N // tn, M // tm),
                 [pl.BlockSpec((tm, K), lambda j, i: (i, 0)), pl.BlockSpec((tn, K), lambda j, i: (j, 0)), tspec, tspec],
                 [tspec] * 2, [jax.ShapeDtypeStruct((M, N), BF16)] * 2, [df, wdn, a, b], sem=("parallel", "parallel"))


def _mm_norm(name, a, b, x, g, g_next=None, comm=None):
    M, K = a.shape
    N = b.shape[1]
    tm = _tile(M, 512, 8)
    more = g_next is not None

    def body(*refs):
        a_ref, b_ref, x_ref, g_ref = refs[:4]
        y_ref, xn_ref = refs[4 + more], refs[5 + more]
        y = jnp.dot(a_ref[...].astype(BF16), b_ref[...], preferred_element_type=F32)
        y_ref[...] = y
        xn = x_ref[...] + _rms(y, g_ref[...])
        xn_ref[...] = xn
        if more:
            refs[6 + more][...] = _rms(xn, refs[4][...]).astype(BF16)

    row = lambda n: pl.BlockSpec((tm, n), lambda i: (i, 0))
    whole = lambda z: pl.BlockSpec(z.shape, lambda i: (0, 0))
    arrays = [a, b, x, g] + ([g_next] if more else [])
    in_specs = [row(K), whole(b), row(N), whole(g)] + ([whole(g_next)] if more else [])
    out_shape = [jax.ShapeDtypeStruct((M, N), F32)] * 2 + ([jax.ShapeDtypeStruct((M, N), BF16)] if more else [])
    return _call(name, body, (M // tm,), in_specs, [row(N)] * (2 + more), out_shape, arrays, sem=("parallel",), comm=comm)


def _rms(x, g):
    r = lax.rsqrt(jnp.mean(x * x, axis=-1, keepdims=True) + EPS)
    return x * r * g


def _rms_bwd(x, g, dy):
    r = lax.rsqrt(jnp.mean(x * x, axis=-1, keepdims=True) + EPS)
    xh = x * r
    dxh = dy * g
    dx = r * (dxh - xh * jnp.mean(dxh * xh, axis=-1, keepdims=True))
    return dx, jnp.sum(dy * xh, axis=0, keepdims=True)


def _shift_down(x, k):
    t = lax.broadcasted_iota(jnp.int32, x.shape, 0)
    return jnp.where(t >= k, pltpu.roll(x, k, 0), 0.0)


def _shift_up(x, k):
    n = x.shape[0]
    t = lax.broadcasted_iota(jnp.int32, x.shape, 0)
    return jnp.where(t < n - k, pltpu.roll(x, n - k, 0), 0.0)


def _group_select(vals):
    lane = lax.broadcasted_iota(jnp.int32, vals[0].shape, 1) // GROUP
    out = vals[-1]
    for g in range(len(vals) - 2, -1, -1):
        out = jnp.where(lane == g, vals[g], out)
    return out


def _pool_count(shape):
    t = lax.broadcasted_iota(jnp.int32, shape, 0)
    lane = lax.broadcasted_iota(jnp.int32, shape, 1) // GROUP
    w = jnp.where(lane == 0, POOL_WINDOWS[0], jnp.where(lane == 1, POOL_WINDOWS[1],
                  jnp.where(lane == 2, POOL_WINDOWS[2], POOL_WINDOWS[3])))
    return jnp.minimum(t + 1, w).astype(F32)


def _pooled(z):
    s = [z]
    for k in (1, 2, 4, 8):
        s.append(s[-1] + _shift_down(s[-1], k))
    return _group_select(s[1:]) / _pool_count(z.shape) - z


def _pooled_bwd(dp):
    u = dp / _pool_count(dp.shape)
    s = [u]
    for k in (1, 2, 4, 8):
        s.append(s[-1] + _shift_up(s[-1], k))
    return _group_select(s[1:]) - dp


def _diag_mask():
    r = lax.broadcasted_iota(jnp.int32, (TQ, TQ), 0) // CHUNK
    c = lax.broadcasted_iota(jnp.int32, (TQ, TQ), 1) // CHUNK
    return r >= c


def _attn_fwd(q, k, v, B, S, comm=None):
    nq = S // TQ

    def body(q_ref, k_ref, v_ref, o_ref, lse_ref):
        mask = _diag_mask()
        for i in range(nq):
            r0, r1 = i * TQ, (i + 1) * TQ
            qb = q_ref[r0:r1, :]
            sd = jnp.where(mask, _dot_nt(qb, k_ref[r0:r1, :]), NEG_INF)
            m = jnp.max(sd, axis=-1, keepdims=True)
            if i:
                so = _dot_nt(qb, k_ref[0:r0, :])
                m = jnp.maximum(m, jnp.max(so, axis=-1, keepdims=True))
            pd = jnp.exp(sd - m)
            l = jnp.sum(pd, axis=-1, keepdims=True)
            acc = _dot(pd, v_ref[r0:r1, :])
            if i:
                po = jnp.exp(so - m)
                l = l + jnp.sum(po, axis=-1, keepdims=True)
                acc = acc + _dot(po, v_ref[0:r0, :])
            o_ref[r0:r1, :] = acc / l
            lse_ref[r0:r1, :] = jnp.broadcast_to(m + jnp.log(l), (TQ, HP))

    spec = pl.BlockSpec((S, HP), lambda b, h: (b, h))
    return _call("attn_fwd", body, (B, HEADS), [spec] * 3, [spec] * 2, [jax.ShapeDtypeStruct((B * S, QW), F32)] * 2,
                 [q, k, v], sem=("parallel", "parallel"), comm=comm)


def _attn_bwd(q, k, v, o, do, lse, B, S, comm=None):
    nq = S // TQ

    def body(q_ref, k_ref, v_ref, o_ref, do_ref, lse_ref, dq_ref, dk_ref, dv_ref, dk_acc, dv_acc):
        mask = _diag_mask()
        dk_acc[...] = jnp.zeros_like(dk_acc)
        dv_acc[...] = jnp.zeros_like(dv_acc)
        for i in range(nq):
            r0, r1 = i * TQ, (i + 1) * TQ
            qb, dob = q_ref[r0:r1, :], do_ref[r0:r1, :]
            delta = jnp.sum(o_ref[r0:r1, :] * dob, axis=-1, keepdims=True)
            lse_b = lse_ref[r0:r1, :][:, :1]
            dq = None
            for c0, c1, diag in ([(0, r0, False)] if i else []) + [(r0, r1, True)]:
                kb, vb = k_ref[c0:c1, :], v_ref[c0:c1, :]
                p = jnp.exp(_dot_nt(qb, kb) - lse_b)
                if diag:
                    p = jnp.where(mask, p, 0.0)
                ds = p * (_dot_nt(dob, vb) - delta)
                part = _dot(ds, kb)
                dq = part if dq is None else dq + part
                dk_acc[c0:c1, :] += _dot_tn(ds, qb)
                dv_acc[c0:c1, :] += _dot_tn(p, dob)
            dq_ref[r0:r1, :] = dq
        dk_ref[...] = dk_acc[...].astype(BF16)
        dv_ref[...] = dv_acc[...].astype(BF16)

    spec = pl.BlockSpec((S, HP), lambda b, h: (b, h))
    return _call("attn_bwd", body, (B, HEADS), [spec] * 6, [spec] * 3,
                 [jax.ShapeDtypeStruct((B * S, QW), F32)] + [jax.ShapeDtypeStruct((B * S, QW), BF16)] * 2,
                 [q, k, v, o, do, lse], scratch=[pltpu.VMEM((S, HP), F32)] * 2, sem=("parallel", "parallel"), comm=comm)


def _all_gather(name, xs):
    n = len(xs)

    def body(*refs):
        x_refs, out_refs = refs[:n], refs[n:2 * n]
        send_sems, recv_sems, local_sems = refs[2 * n:]
        x_, y_, c_ = lax.axis_index("x"), lax.axis_index("y"), lax.axis_index("c")
        me, sibling = (x_, y_, c_), (x_, y_, 1 - c_)
        chips = [(1 - x_, y_), (x_, 1 - y_), (1 - x_, 1 - y_)]

        def copy(a, k, block, to, own=False):
            slot = out_refs[a].at[4 * block[0] + 2 * block[1] + block[2]]
            return pltpu.make_async_remote_copy(
                src_ref=x_refs[a] if own else slot, dst_ref=slot,
                send_sem=send_sems.at[7 * a + k], recv_sem=recv_sems.at[7 * a + k], device_id=to, device_id_type=MESH)

        mine = [pltpu.make_async_copy(x_refs[a], out_refs[a].at[4 * x_ + 2 * y_ + c_], local_sems.at[a]) for a in range(n)]
        for cp in mine:
            cp.start()
        first = [copy(a, 1 + j, me, (*chip, c_), own=True) for j, chip in enumerate(chips) for a in range(n)]
        first += [copy(a, 0, me, sibling, own=True) for a in range(n)]
        for cp in first:
            cp.start()
        passed = []
        for j, chip in enumerate(chips):
            for a in range(n):
                copy(a, 1 + j, (*chip, c_), me).wait_recv()
                passed.append(copy(a, 4 + j, (*chip, c_), sibling))
                passed[-1].start()
        for a in range(n):
            copy(a, 0, sibling, me).wait_recv()
            for j, chip in enumerate(chips):
                copy(a, 4 + j, (*chip, 1 - c_), me).wait_recv()
        for cp in first + passed:
            cp.wait_send()
        for cp in mine:
            cp.wait()

    return pl.pallas_call(
        body, name=name, out_shape=[jax.ShapeDtypeStruct((N_DEV,) + x.shape, x.dtype) for x in xs],
        in_specs=[pl.BlockSpec(memory_space=pl.ANY)] * n, out_specs=[pl.BlockSpec(memory_space=pl.ANY)] * n,
        scratch_shapes=[pltpu.SemaphoreType.DMA((7 * n,)), pltpu.SemaphoreType.DMA((7 * n,)), pltpu.SemaphoreType.DMA((n,))],
    )(*xs)


def _adamw(name, parts, w, m, v, row_off=0, layers=1, stride=0):
    R, C = w.shape
    r = R // layers
    assert parts.shape[2] == C and parts.shape[1] >= (layers - 1) * stride + row_off + r
    if r % 8 == 0:
        tr = max(t for t in range(8, 513, 8) if r % t == 0 and row_off % t == 0 and stride % t == 0)
    else:
        assert layers == 1
        tr = r
    nb = r // tr

    def body(p_ref, w_ref, m_ref, v_ref, g_ref, d_ref, nm_ref, nv_ref):
        g = p_ref[0].astype(F32)
        for j in range(1, N_DEV):
            g = g + p_ref[j].astype(F32)
        m_new = ADAM_B1 * m_ref[...] + (1.0 - ADAM_B1) * g
        v_new = ADAM_B2 * v_ref[...] + (1.0 - ADAM_B2) * (g * g)
        m_hat = m_new / (1.0 - ADAM_B1 ** ADAM_STEP)
        v_hat = v_new / (1.0 - ADAM_B2 ** ADAM_STEP)
        g_ref[...] = g
        d_ref[...] = -ADAM_LR * (m_hat / (jnp.sqrt(v_hat) + ADAM_EPS) + ADAM_WD * w_ref[...])
        nm_ref[...] = m_new
        nv_ref[...] = v_new

    spec = pl.BlockSpec((tr, C), lambda l, i: (l * nb + i, 0))
    pspec = pl.BlockSpec((N_DEV, tr, C), lambda l, i: (0, (l * stride + row_off) // tr + i, 0))
    return _call(name, body, (layers, nb), [pspec, spec, spec, spec], [spec] * 4,
                 [jax.ShapeDtypeStruct((R, C), F32)] * 4, [parts, w, m, v], sem=("parallel", "parallel"))


def _rot_cols(w):
    h = w.shape[-1] // 2
    return jnp.concatenate([-w[..., h:], w[..., :h]], axis=-1)


def _unrot_cols(d):
    h = d.shape[-1] // 2
    return jnp.concatenate([d[..., h:], -d[..., :h]], axis=-1)


IN_WIDTH = sum(IN_SIZES)
IN_SHARD = IN_WIDTH // N_DEV
IN_GATE = IN_WIDTH - IN_SIZES[-1]


def _w_in_cols(buf, a, b):
    out = []
    for j in range(a // IN_SHARD, (b - 1) // IN_SHARD + 1):
        lo, hi = max(a, IN_SHARD * j) - IN_SHARD * j, min(b, IN_SHARD * (j + 1)) - IN_SHARD * j
        out.append(buf[j, :, lo:hi])
    return out


def _w1(buf):
    kr0 = IN_SIZES[0] + IN_SIZES[1]
    kr = jnp.concatenate(_w_in_cols(buf, kr0, kr0 + QK_ROPE), axis=-1)
    w1s = jnp.concatenate(_w_in_cols(buf, 0, kr0) + [kr, _rot_cols(kr), jnp.zeros((D, 64), BF16)]
                          + _w_in_cols(buf, kr0 + QK_ROPE, IN_GATE), axis=-1)
    return w1s, jnp.concatenate(_w_in_cols(buf, IN_GATE, IN_WIDTH), axis=-1)


def _w_in_grad_shards(d1s, d1g):
    kr0 = IN_SIZES[0] + IN_SIZES[1]
    seg = d1s[:, C_KR:C_KR + 128]
    dkr = seg[:, :QK_ROPE] + _unrot_cols(seg[:, QK_ROPE:2 * QK_ROPE])
    runs = [(0, d1s, 0), (kr0, dkr, 0), (kr0 + QK_ROPE, d1s, C_PIN), (IN_GATE, d1g, 0), (IN_WIDTH, None, 0)]
    blocks = []
    for j in range(N_DEV):
        a, b = IN_SHARD * j, IN_SHARD * (j + 1)
        pieces = []
        for (c0, src, s0), (c1, _, _) in zip(runs[:-1], runs[1:]):
            lo, hi = max(a, c0), min(b, c1)
            if lo < hi:
                pieces.append(src[:, s0 + lo - c0:s0 + hi - c0])
        blocks.append(jnp.concatenate(pieces, axis=-1))
    return jnp.stack(blocks)


def _layer_weights(G):
    L = G['w_uq'].shape[0]
    wq = G['w_uq'].reshape(L, Q_LORA, HEADS, QK_NOPE + QK_ROPE)
    nope, rope = wq[..., :QK_NOPE], wq[..., QK_NOPE:]
    z32 = jnp.zeros((L, Q_LORA, HEADS, HP - QK_NOPE - QK_ROPE), BF16)
    wq_a = jnp.concatenate([nope, rope, z32], axis=-1).reshape(L, Q_LORA, QW)
    wq_b = jnp.concatenate([jnp.zeros_like(nope), _rot_cols(rope), z32], axis=-1).reshape(L, Q_LORA, QW)
    wkv = G['w_ukv'].reshape(L, KV_LORA, HEADS, QK_NOPE + V_HEAD)
    z64 = jnp.zeros((L, KV_LORA, HEADS, HP - QK_NOPE), BF16)
    wkn = jnp.concatenate([wkv[..., :QK_NOPE], z64], axis=-1).reshape(L, KV_LORA, QW)
    wv = jnp.concatenate([wkv[..., QK_NOPE:], z64], axis=-1).reshape(L, KV_LORA, QW)
    wa = G['w_br_a'].reshape(L, HEADS, V_HEAD, D)
    wa = jnp.concatenate([wa, jnp.zeros((L, HEADS, HP - V_HEAD, D), BF16)], axis=2).reshape(L, QW, D)
    return dict(
        Wq=jnp.concatenate([wq_a, wq_b], axis=-1),
        Wkv=jnp.concatenate([wkn, wv], axis=-1), Wa=wa, Wb=G['w_br_b'], Wc=G['w_br_c'], Wd=G['w_br_d'],
        conv=G['conv_w'].reshape(L, 3, BR).astype(F32))


def _tables(S):
    inv = ROPE_THETA ** (-jnp.arange(0, QK_ROPE, 2, dtype=F32) / QK_ROPE)
    ang = jnp.arange(S, dtype=F32)[:, None] * inv[None, :]
    cos, sin = jnp.cos(ang), jnp.sin(ang)
    scale = (QK_NOPE + QK_ROPE) ** -0.5
    one, zero = jnp.ones((S, QK_NOPE), F32), jnp.zeros((S, QK_NOPE), F32)
    z32 = jnp.zeros((S, HP - QK_NOPE - QK_ROPE), F32)
    cq = jnp.concatenate([one, cos, cos, z32], axis=1) * scale
    sq = jnp.concatenate([zero, sin, sin, z32], axis=1) * scale
    ck = jnp.concatenate([cos, cos, sin, sin, jnp.zeros((S, HP - 2 * QK_ROPE), F32)], axis=1)
    return cq, sq, ck


def _place_k():
    p = np.zeros((HP, QW), np.float32)
    for r in range(2 * QK_ROPE):
        for h in range(HEADS):
            p[r, h * HP + QK_NOPE + r % QK_ROPE] = 1.0
    return jnp.asarray(p, BF16)


def _layer_fwd(x, h, W, late, P, tabs, B, S, ride, g_next):
    cq_t, sq_t, ck_t, place = tabs
    g_cq, g_ckv = P['g_cq'], P['g_ckv']
    P1 = _mm("in_proj_s", h, W['W1s'], out_dtype=BF16)
    PG = _mm("in_proj_g", h, W['W1g'], out_dtype=BF16, comm=ride.get('in_proj'))

    def q_prep(c, cq, sq, g, w):
        qq = _dot(_rms(c, g), w)
        return qq[:, :QW] * jnp.tile(cq, (1, HEADS)) + qq[:, QW:] * jnp.tile(sq, (1, HEADS))
    q = _rw("q_prep", q_prep, [win(P1, 256, 0), tab(cq_t), tab(sq_t)], [g_cq, W['Wq']], [(QW, BF16)])[0]

    def kv_prep(c, seg, ck, g, w, place):
        kv = _dot(_rms(c, g), w)
        return kv[:, :QW] + _dot(seg * ck, place), kv[:, QW:]
    k, v = _rw("kv_prep", kv_prep, [win(P1, 128, 2), win(P1, 128, 3), tab(ck_t)], [g_ckv, W['Wkv'], place],
               [(QW, BF16), (QW, BF16)])
    o, lse = _attn_fwd(q, k, v, B, S, comm=ride.get('attn_fwd'))

    def pool_f(z, wbd, scale):
        return _dot(_pooled(z), wbd) * scale
    bo = _rw("pool_fwd", pool_f, [win(P1, BR, 2)], [P['pool_bd'], P['pool_scale']], [(BR, BF16)], tile=S)[0]

    def sgu_f(u, sv, g, wm, bias):
        vn = _rms(sv, g)
        blocks = [vn[r:r + SGU_BLOCK] for r in range(0, vn.shape[0], SGU_BLOCK)]
        return u * jnp.concatenate([_sgu_mix(vb, wm, bias) for vb in blocks], axis=0)
    co = _rw("sgu_fwd", sgu_f, [win(P1, BR, 3), win(P1, BR, 4)], [P['g_sgu_v'], P['sgu_wm'], P['sgu_bias']],
             [(BR, BF16)], tile=SGU_TILE)[0]

    def conv_f(b, c, xi, w):
        z = c * xi
        return b * (w[0:1] * _shift_down(z, 2) + w[1:2] * _shift_down(z, 1) + w[2:3] * z)
    do_ = _rw("conv_fwd", conv_f, [win(P1, BR, 5), win(P1, BR, 6), win(P1, BR, 7)], [W['conv']], [(BR, BF16)], tile=S)[0]

    def merge_f(a, b, c, d, l0, l1, l2, l3, wa, wb, wc, wd):
        return (_sigmoid(l0) * _dot(a, wa) + _sigmoid(l1) * _dot(b, wb) + _sigmoid(l2) * _dot(c, wc)
                + _sigmoid(l3) * _dot(d, wd))
    merged = _rw("merge_fwd", merge_f, [o, bo, co, do_] + [win(PG, D, i) for i in range(4)],
                 [W['Wa'], W['Wb'], W['Wc'], W['Wd']], [(D, BF16)])[0]
    W = dict(W, **late())
    om, x1, h2 = _mm_norm("out_proj", merged, W['Wout'], x, P['g_post_mix'], P['g_pre_ffn'])
    fa, fb, act = _ffn_up(h2, W['Wg'], W['Wu'], comm=ride.get('ffn_up'))
    f, x2, *h_next = _mm_norm("ffn_down", act, W['Wdn'], x1, P['g_post_ffn'], g_next, comm=ride.get('ffn_down'))
    saved = dict(x=x, h=h, P1=P1, PG=PG, q=q, k=k, v=v, o=o, lse=lse, bo=bo, co=co, do=do_, merged=merged, om=om, x1=x1,
                 h2=h2, fa=fa, fb=fb, act=act, f=f)
    return x2, (h_next[0] if h_next else None), saved, W


def _sgu_mix(vn, wm, bias):
    outs = [_dot(wm[g], vn) for g in range(4)]
    return _group_select(outs) + bias


def _layer_bwd(dx2, df, sv, W, P, tabs, B, S, own_ride, prev):
    cq_t, sq_t, ck_t, place = tabs
    P1 = sv['P1']
    grads = {}
    da, db = _ffn_down_dx(df, W['Wdn'], sv['fa'], sv['fb'])
    grads['Wdn'] = _mm_t("ffn_down_dw", sv['act'], df)
    dh2 = _mm("ffn_up_dx", da, W['Wg'], tb=True, a2=db, b2=W['Wu'], out_dtype=BF16)
    grads['Wg'] = _mm_t("ffn_gate_dw", sv['h2'], da)
    grads['Wu'] = _mm_t("ffn_up_dw", sv['h2'], db)

    def mid_norms_b(x1, dh, dy, om, g_ffn, g_mix):
        dx, dg_ffn = _rms_bwd(x1, g_ffn, dh)
        dx1 = dy + dx
        dom, dg_mix = _rms_bwd(om, g_mix, dx1)
        return dx1, dom, dg_ffn, dg_mix
    dx1, dom, grads['g_pre_ffn'], grads['g_post_mix'] = _rw(
        "mid_norms_bwd", mid_norms_b, [sv['x1'], dh2, dx2, sv['om']], [P['g_pre_ffn'], P['g_post_mix']],
        [(D, F32), (D, BF16)], [(1, D), (1, D)], tile=512)
    dmerged = _mm("out_proj_dx", dom, W['Wout'], tb=True, out_dtype=BF16)
    grads['Wout'] = _mm_t("out_proj_dw", sv['merged'], dom)
    ride = own_ride('ffn', grads)

    def merge_b(a, b, c, d, l0, l1, l2, l3, dm, wa, wb, wc, wd):
        outs_dl, outs_dy, outs_dbr = [], [], []
        for br, l, w in ((a, l0, wa), (b, l1, wb), (c, l2, wc), (d, l3, wd)):
            s = _sigmoid(l)
            y = _dot(br, w)
            dy = (dm * s).astype(BF16)
            outs_dl.append((dm * y * s * (1.0 - s)).astype(BF16))
            outs_dy.append(dy)
            outs_dbr.append(_dot_nt(dy, w))
        return (jnp.concatenate(outs_dl, axis=1), *outs_dy, *outs_dbr)
    mb = _rw("merge_bwd", merge_b,
             [sv['o'], sv['bo'], sv['co'], sv['do']] + [win(sv['PG'], D, i) for i in range(4)] + [dmerged],
             [W['Wa'], W['Wb'], W['Wc'], W['Wd']],
             [(4 * D, BF16)] + [(D, BF16)] * 4 + [(QW, F32), (BR, F32), (BR, F32), (BR, F32)], comm=ride.get('merge_bwd'))
    dl, dy, (d_o, d_bo, d_co, d_do) = mb[0], mb[1:5], mb[5:9]
    grads['Wa'] = _mm_t("br_a_dw", sv['o'], dy[0])
    grads['Wb'] = _mm_t("br_b_dw", sv['bo'], dy[1])
    grads['Wc'] = _mm_t("br_c_dw", sv['co'], dy[2])
    grads['Wd'] = _mm_t("br_d_dw", sv['do'], dy[3])

    def conv_b(b, c, xi, dout, w):
        z = c * xi
        z1, z2 = _shift_down(z, 1), _shift_down(z, 2)
        y = w[0:1] * z2 + w[1:2] * z1 + w[2:3] * z
        dyv = dout * b
        dz = w[2:3] * dyv + w[1:2] * _shift_up(dyv, 1) + w[0:1] * _shift_up(dyv, 2)
        return (dout * y, dz * xi, dz * c, jnp.sum(dyv * z2, axis=0, keepdims=True),
                jnp.sum(dyv * z1, axis=0, keepdims=True), jnp.sum(dyv * z, axis=0, keepdims=True))
    dcvb, dcvc, dcvx, dw0, dw1, dw2 = _rw("conv_bwd", conv_b, [win(P1, BR, 5), win(P1, BR, 6), win(P1, BR, 7), d_do],
                                          [W['conv']], [(BR, BF16)] * 3, [(1, BR)] * 3, tile=S)
    grads['conv'] = jnp.concatenate([dw0, dw1, dw2], axis=0)

    def sgu_b(u, s_v, dout, g, wm, bias):
        vn = _rms(s_v, g)
        dmix = dout * u
        lane = lax.broadcasted_iota(jnp.int32, (SGU_BLOCK, BR), 1) // GROUP
        mixed, dvn, dwm, dbias = [], [], [0.0] * 4, 0.0
        for r in range(0, vn.shape[0], SGU_BLOCK):
            vb, db = vn[r:r + SGU_BLOCK], dmix[r:r + SGU_BLOCK]
            mixed.append(_sgu_mix(vb, wm, bias))
            dvn.append(_group_select([_dot_tn(wm[gi], db) for gi in range(4)]))
            dwm = [dwm[gi] + _dot_nt(jnp.where(lane == gi, db, 0.0), vb) for gi in range(4)]
            dbias = dbias + db
        dsv, dg = _rms_bwd(s_v, g, jnp.concatenate(dvn, axis=0))
        return (dout * jnp.concatenate(mixed, axis=0), dsv, dg, *dwm, dbias)
    dsu, dsv_, grads['g_sgu_v'], wm0, wm1, wm2, wm3, grads['sgu_bias'] = _rw(
        "sgu_bwd", sgu_b, [win(P1, BR, 3), win(P1, BR, 4), d_co], [P['g_sgu_v'], P['sgu_wm'], P['sgu_bias']],
        [(BR, BF16)] * 2, [(1, BR)] + [(SGU_BLOCK, SGU_BLOCK)] * 4 + [(SGU_BLOCK, BR)], tile=SGU_TILE)
    grads['sgu_wm'] = jnp.stack([wm0, wm1, wm2, wm3])

    def pool_b(z, dout, wbd, scale):
        pooled = _pooled(z)
        mixed = _dot(pooled, wbd)
        dmix = dout * scale
        dz = _pooled_bwd(_dot_nt(dmix, wbd))
        return dz, _dot_tn(pooled, dmix), jnp.sum(dout * mixed, axis=0, keepdims=True)
    dpin, grads['pool_bd'], grads['pool_scale'] = _rw(
        "pool_bwd", pool_b, [win(P1, BR, 2), d_bo], [P['pool_bd'], P['pool_scale']], [(BR, BF16)], [(BR, BR), (1, BR)], tile=S)

    dq, dk, dv = _attn_bwd(sv['q'], sv['k'], sv['v'], sv['o'], d_o, sv['lse'], B, S, comm=ride.get('attn_bwd'))

    def q_prep_b(c, dq, cq, sq, g, w):
        dqq = jnp.concatenate([dq * jnp.tile(cq, (1, HEADS)), dq * jnp.tile(sq, (1, HEADS))], axis=1).astype(BF16)
        dc, dg = _rms_bwd(c, g, _dot_nt(dqq, w))
        return dc, _rms(c, g), dqq, dg
    dcq, cqn, dqq, grads['g_cq'] = _rw("q_prep_bwd", q_prep_b, [win(P1, 256, 0), dq, tab(cq_t), tab(sq_t)],
                                       [P['g_cq'], W['Wq']], [(Q_LORA, BF16), (Q_LORA, BF16), (2 * QW, BF16)], [(1, Q_LORA)])
    grads['Wq'] = _mm_t("uq_dw", cqn, dqq)

    def kv_prep_b(c, dk, dv, ck, g, w, place):
        dc, dg = _rms_bwd(c, g, _dot_nt(dk, w[:, :QW]) + _dot_nt(dv, w[:, QW:]))
        return dc, _dot_nt(dk, place) * ck, _rms(c, g), dg
    dckv, dseg, kvn, grads['g_ckv'] = _rw(
        "kv_prep_bwd", kv_prep_b, [win(P1, 128, 2), dk, dv, tab(ck_t)], [P['g_ckv'], W['Wkv'], place],
        [(KV_LORA, BF16), (HP, BF16), (KV_LORA, BF16)], [(1, KV_LORA)])
    grads['Wkn'] = _mm_t("uk_dw", kvn, dk)
    grads['Wv'] = _mm_t("uv_dw", kvn, dv)
    ride = dict(ride, **own_ride('mixers', grads))

    dps = jnp.concatenate([dcq, dckv, dseg, dpin, dsu, dsv_, dcvb, dcvc, dcvx], axis=1)
    grads['W1s'] = _mm_t("in_proj_s_dw", sv['h'], dps)
    grads['W1g'] = _mm_t("in_proj_g_dw", sv['h'], dl, comm=ride.get('in_proj_g_dw'))
    ride = dict(ride, **own_ride('w_in', grads))
    dh = _mm("in_proj_dx", dps, W['W1s'], tb=True, a2=dl, b2=W['W1g'], out_dtype=BF16, tm=256, comm=ride.get('in_proj_dx'))

    if prev is None:
        def pre_mix_b(x, dh, dy, g):
            dx, dg = _rms_bwd(x, g, dh)
            return dy + dx, dg
        dx, grads['g_pre_mix'] = _rw("pre_mix_bwd", pre_mix_b, [sv['x'], dh, dx1], [P['g_pre_mix']], [(D, F32)], [(1, D)], tile=512)
        return dx, None, None, grads, ride

    def edge_norms_b(x, dh, dy, f, g_mix, g_ffn):
        dx, dg_mix = _rms_bwd(x, g_mix, dh)
        dx = dy + dx
        df, dg_ffn = _rms_bwd(f, g_ffn, dx)
        return dx, df, dg_mix, dg_ffn
    dx, df_prev, grads['g_pre_mix'], dg_prev = _rw(
        "edge_norms_bwd", edge_norms_b, [sv['x'], dh, dx1, prev[0]], [P['g_pre_mix'], prev[1]],
        [(D, F32), (D, BF16)], [(1, D), (1, D)], tile=512)
    return dx, df_prev, dg_prev, grads, ride


def _mixer_grads(g):
    out = {}
    qa = g['Wq'][:, :QW].reshape(1, Q_LORA, HEADS, HP)
    qb = g['Wq'][:, QW:].reshape(1, Q_LORA, HEADS, HP)
    drope = qa[..., QK_NOPE:QK_NOPE + QK_ROPE] + _unrot_cols(qb[..., QK_NOPE:QK_NOPE + QK_ROPE])
    out['w_uq'] = jnp.concatenate([qa[..., :QK_NOPE], drope], axis=-1).reshape(1, Q_LORA, HEADS * (QK_NOPE + QK_ROPE))
    kn = g['Wkn'].reshape(1, KV_LORA, HEADS, HP)[..., :QK_NOPE]
    vv = g['Wv'].reshape(1, KV_LORA, HEADS, HP)[..., :V_HEAD]
    out['w_ukv'] = jnp.concatenate([kn, vv], axis=-1).reshape(1, KV_LORA, HEADS * (QK_NOPE + V_HEAD))
    out['w_br_a'] = g['Wa'].reshape(1, HEADS, HP, D)[:, :, :V_HEAD].reshape(1, HEADS * V_HEAD, D)
    out['w_br_b'], out['w_br_c'], out['w_br_d'] = g['Wb'][None], g['Wc'][None], g['Wd'][None]
    out['conv_w'] = g['conv'].reshape(1, 3, 1, BR)
    return out


def _small_grads(g):
    out = {}
    for n in ('g_pre_mix', 'g_cq', 'g_ckv', 'g_sgu_v', 'g_post_mix', 'g_pre_ffn', 'g_post_ffn', 'pool_scale'):
        out[n] = g[n].reshape(1, -1)
    bd = g['pool_bd'].reshape(4, GROUP, 4, GROUP)
    out['pool_w'] = jnp.stack([bd[i, :, i, :] for i in range(4)])[None]
    out['sgu_w'] = (g['sgu_wm'] * _sgu_mask()[None])[None]
    out['sgu_b'] = g['sgu_bias'].reshape(SGU_BLOCK, 4, GROUP).sum(-1).T[None]
    return out


def _sgu_mask():
    pc = np.arange(SGU_BLOCK) // CHUNK
    return jnp.asarray(pc[:, None] >= pc[None, :], F32)


def _rows(a, lead):
    return a.reshape(a.shape[:lead] + (-1, a.shape[-1]))


def _pad_to(a, rows, cols):
    pad = [(0, 0)] * (a.ndim - 2) + [(0, rows - a.shape[-2]), (0, cols - a.shape[-1])]
    return jnp.pad(a, pad)


GROUPS = (('w_in',), ('w_ffn_gate',), ('w_ffn_up',), ('w_out', 'w_ffn_down'),
          ('w_uq', 'w_ukv', 'w_br_a', 'w_br_b', 'w_br_c', 'w_br_d', 'conv_w'))
PADDED = {'w_uq': (Q_LORA, 128), 'conv_w': (128, 128), 'w_ffn_down': (384, D)}
NARROW = ('w_uq', 'conv_w')


def _group(t, lead, groups):
    bufs = {}
    for b in groups:
        ps = [_rows(t[n], lead) for n in GROUPS[b]]
        ps = [_pad_to(p, *PADDED[n]) if n in PADDED else p for n, p in zip(GROUPS[b], ps)]
        bufs[b] = ps[0] if len(ps) == 1 else jnp.concatenate(ps, axis=-2)
    return bufs


def _group_rows(shapes):
    where, rows = {}, []
    for b, names in enumerate(GROUPS):
        off = 0
        for n in names:
            where[n] = (b, off)
            off += PADDED.get(n, shapes[n])[0]
        rows.append(off)
    return where, rows


def _shard_split(a, axis):
    n = a.shape[axis]
    a = a.reshape(a.shape[:axis] + (N_DEV, n // N_DEV) + a.shape[axis + 1:])
    return jnp.moveaxis(a, axis, 0)


def _shard_join(a, axis):
    a = jnp.moveaxis(a, 0, axis)
    return a.reshape(a.shape[:axis] + (a.shape[axis] * a.shape[axis + 1],) + a.shape[axis + 2:])


def _as2d(a):
    return a.reshape(-1, a.shape[-1])


def kernel(x, w_in, g_pre_mix, g_cq, g_ckv, w_uq, w_ukv, pool_w, pool_scale, g_sgu_v, sgu_w, sgu_b, conv_w, w_br_a, w_br_b, w_br_c, w_br_d, w_out, g_post_mix, g_pre_ffn, w_ffn_gate, w_ffn_up, w_ffn_down, g_post_ffn, loss_target, m_w_in, m_g_pre_mix, m_g_cq, m_g_ckv, m_w_uq, m_w_ukv, m_pool_w, m_pool_scale, m_g_sgu_v, m_sgu_w, m_sgu_b, m_conv_w, m_w_br_a, m_w_br_b, m_w_br_c, m_w_br_d, m_w_out, m_g_post_mix, m_g_pre_ffn, m_w_ffn_gate, m_w_ffn_up, m_w_ffn_down, m_g_post_ffn, v_w_in, v_g_pre_mix, v_g_cq, v_g_ckv, v_w_uq, v_w_ukv, v_pool_w, v_pool_scale, v_g_sgu_v, v_sgu_w, v_sgu_b, v_conv_w, v_w_br_a, v_w_br_b, v_w_br_c, v_w_br_d, v_w_out, v_g_post_mix, v_g_pre_ffn, v_w_ffn_gate, v_w_ffn_up, v_w_ffn_down, v_g_post_ffn):
    args = dict(locals())
    w = {n: args[n] for n in WEIGHTS}
    m = {n: args['m_' + n] for n in WEIGHTS}
    v = {n: args['v_' + n] for n in WEIGHTS}
    B, S, _ = x.shape
    T = B * S
    L = DEPTH
    names = list(SHARDED)
    shapes1 = {n: _as2d(w[n][0]).shape for n in names}
    where, group_rows = _group_rows(shapes1)

    def shard_bufs(l, groups):
        return _group({n: w[n][l:l + 1].astype(BF16) for b in groups for n in GROUPS[b]}, 0, groups)

    def joined(gathered):
        G = {}
        for b, buf in gathered.items():
            for n in GROUPS[b]:
                if n == 'w_in':
                    continue
                off = where[n][1]
                r, c = shapes1[n]
                G[n] = _shard_join(buf[:, off:off + r, :c].reshape((N_DEV, 1) + w[n].shape[1:]), SHARDED[n])
        return G

    eye = jnp.eye(4, dtype=F32)
    pool_bd = (pool_w[:, :, :, None, :] * eye[None, :, None, :, None]).reshape(L, BR, BR).astype(BF16)
    sgu_wm = (sgu_w * _sgu_mask()[None, None]).astype(BF16)
    sgu_bias = jnp.repeat(sgu_b.transpose(0, 2, 1), GROUP, axis=2)
    tabs = _tables(S) + (_place_k(),)

    def layer_params(l):
        P = {n: w[n][l][None, :] for n in ('g_pre_mix', 'g_cq', 'g_ckv', 'g_sgu_v', 'g_post_mix', 'g_pre_ffn',
                                            'g_post_ffn', 'pool_scale')}
        P.update(pool_bd=pool_bd[l], sgu_wm=sgu_wm[l], sgu_bias=sgu_bias[l])
        return P

    xt = x.reshape(T, D)
    h = _rw("pre_mix", lambda x, g: _rms(x, g), [xt], [w['g_pre_mix'][0][None, :]], [(D, BF16)], tile=512)[0]
    saved, Ws = [], []
    first = shard_bufs(0, (0, 4))
    early = dict(zip((0, 4), _all_gather("gather_weights", [first[0], first[4]])))
    for l in range(L):
        own = shard_bufs(l, (1, 2, 3))
        nxt = shard_bufs(l + 1, (0, 4)) if l + 1 < L else None
        ride = {'in_proj': _Gather([own[1]]), 'attn_fwd': _Gather([own[3], own[2]])}
        if nxt:
            ride.update(ffn_up=_Gather([nxt[0]]), ffn_down=_Gather([nxt[4]]))

        def late(ride=ride):
            G = joined({1: ride['in_proj'].results[0], 3: ride['attn_fwd'].results[0], 2: ride['attn_fwd'].results[1]})
            return dict(Wout=G['w_out'][0], Wg=G['w_ffn_gate'][0], Wu=G['w_ffn_up'][0], Wdn=G['w_ffn_down'][0])

        We = {k: a[0] for k, a in _layer_weights(joined(early)).items()}
        We['W1s'], We['W1g'] = _w1(early[0])
        g_next = w['g_pre_mix'][l + 1][None, :] if nxt else None
        xt, h, sv, W = _layer_fwd(xt, h, We, late, layer_params(l), tabs, B, S, ride, g_next)
        saved.append(sv)
        Ws.append(W)
        if nxt:
            early = {0: ride['ffn_up'].results[0], 4: ride['ffn_down'].results[0]}

    def loss_f(y, t, f, g):
        e = y - t
        dy = e * (1.0 / D)
        df, dg = _rms_bwd(f, g, dy)
        return dy, df, jnp.sum(e * e, axis=0, keepdims=True), dg
    dy, df, sq, dg_ffn = _rw("loss", loss_f, [xt, loss_target.reshape(T, D), saved[-1]['f']], [w['g_post_ffn'][L - 1][None, :]],
                             [(D, F32), (D, BF16)], [(1, D), (1, D)], tile=512)
    loss = lax.psum(0.5 * jnp.sum(sq) / D, ("x", "y", "c"))

    recv = [lax.empty((N_DEV, L * r, PADDED.get(ns[0], shapes1[ns[0]])[1]), BF16) for ns, r in zip(GROUPS, group_rows)]

    def send_bufs(pg, groups):
        return _group({n: _shard_split(pg[n], SHARDED[n]).astype(BF16) for b in groups for n in GROUPS[b]}, 1, groups)

    small = [None] * L
    for l in reversed(range(L)):
        def own_ride(stage, grads, l=l):
            if stage == 'ffn':
                bufs = send_bufs({'w_ffn_gate': grads['Wg'][None], 'w_ffn_up': grads['Wu'][None],
                                  'w_ffn_down': grads['Wdn'][None], 'w_out': grads['Wout'][None]}, (1, 2, 3))
                return {'merge_bwd': _Scatter([bufs[1]], [recv[1]], [l * group_rows[1]]),
                        'attn_bwd': _Scatter([bufs[3], bufs[2]], [recv[3], recv[2]], [l * group_rows[3], l * group_rows[2]])}
            if stage == 'mixers':
                bufs = send_bufs(_mixer_grads(grads), (4,))
                return {'in_proj_g_dw': _Scatter([bufs[4]], [recv[4]], [l * group_rows[4]])}
            return {'in_proj_dx': _Scatter([_w_in_grad_shards(grads['W1s'], grads['W1g'])], [recv[0]], [l * group_rows[0]])}

        prev = (saved[l - 1]['f'], w['g_post_ffn'][l - 1][None, :]) if l else None
        dy, df_prev, dg_prev, gl, ride = _layer_bwd(dy, df, saved[l], Ws[l], layer_params(l), tabs, B, S, own_ride, prev)
        gl['g_post_ffn'] = dg_ffn
        df, dg_ffn = df_prev, dg_prev
        for name, bs in (('in_proj_g_dw', (4,)), ('in_proj_dx', (0,)), ('merge_bwd', (1,)), ('attn_bwd', (3, 2))):
            for b, res_b in zip(bs, ride[name].results):
                recv[b] = res_b
        pg = _small_grads(gl)
        rep = jnp.concatenate([pg[n].reshape(-1, 128) for n in REPLICATED], axis=0)
        small[l] = _pad_to(rep, -(-rep.shape[0] // 8) * 8, 128)
    grad_x = dy.reshape(B, S, D)

    small_rows = small[0].shape[0]
    got_small = _all_gather("gather_small_grads", [jnp.concatenate(small, axis=0)])[0].reshape(N_DEV, L, small_rows, 128)

    res = {}
    for n in names:
        b, off = where[n]
        r, c = shapes1[n]
        if n in NARROW:
            parts = recv[b].reshape(N_DEV, L, group_rows[b], -1)[:, :, off:off + r, :c].reshape(N_DEV, L * r, c)
            res[n] = _adamw("adamw_" + n, parts, _as2d(w[n]), _as2d(m[n]), _as2d(v[n]))
        else:
            res[n] = _adamw("adamw_" + n, recv[b], _as2d(w[n]), _as2d(m[n]), _as2d(v[n]), row_off=off, layers=L,
                            stride=group_rows[b])
    off = 0
    for n in REPLICATED:
        r = int(np.prod(w[n].shape[1:])) // 128
        shp2 = _as2d(w[n]).shape
        res[n] = _adamw("adamw_" + n, got_small[:, :, off:off + r].reshape((N_DEV,) + shp2), _as2d(w[n]), _as2d(m[n]), _as2d(v[n]))
        off += r
    outs = [loss, grad_x]
    for k in range(4):
        outs += [res[n][k].reshape(w[n].shape) for n in WEIGHTS]
    return tuple(outs)
```

```python
import functools

import numpy as np
import jax
import jax.numpy as jnp
from jax import lax
from jax.experimental import pallas as pl
from jax.experimental.pallas import tpu as pltpu

F32, BF16 = jnp.float32, jnp.bfloat16
MESH = pl.DeviceIdType.MESH
N_DEV = 8

D = 1024
DEPTH = 4
CHUNK = 64
EPS = 1e-6
NEG_INF = -1e30
HEADS, QK_NOPE, QK_ROPE, V_HEAD = 8, 64, 32, 64
Q_LORA, KV_LORA = 256, 128
ROPE_THETA = 10000.0
POOL_WINDOWS = (2, 4, 8, 16)
GROUP = 64
BR = 256
SGU_BLOCK = 128
SGU_TILE = 4 * SGU_BLOCK
D_FF = 2816
IN_SIZES = (256, 128, 32, 256, 256, 256, 256, 256, 256, 4096)
HP = 128
QW = HEADS * HP
C_CQ, C_CKV, C_KR, C_PIN, C_SU, C_SV, C_CVB, C_CVC, C_CVX, C_GATE = 0, 256, 384, 512, 768, 1024, 1280, 1536, 1792, 2048

ADAM_LR, ADAM_B1, ADAM_B2, ADAM_EPS, ADAM_WD, ADAM_STEP = 0.001, 0.9, 0.999, 1e-08, 0.01, 10

WEIGHTS = ['w_in', 'g_pre_mix', 'g_cq', 'g_ckv', 'w_uq', 'w_ukv', 'pool_w', 'pool_scale', 'g_sgu_v', 'sgu_w', 'sgu_b',
           'conv_w', 'w_br_a', 'w_br_b', 'w_br_c', 'w_br_d', 'w_out', 'g_post_mix', 'g_pre_ffn', 'w_ffn_gate',
           'w_ffn_up', 'w_ffn_down', 'g_post_ffn']
SHARDED = {'w_in': 2, 'w_uq': 2, 'w_ukv': 2, 'conv_w': 3, 'w_br_a': 2, 'w_br_b': 2, 'w_br_c': 2, 'w_br_d': 2,
           'w_out': 1, 'w_ffn_gate': 2, 'w_ffn_up': 2, 'w_ffn_down': 1}
REPLICATED = [n for n in WEIGHTS if n not in SHARDED]

VMEM_LIMIT = 56 * 1024 * 1024
TQ = 256


def _cparams(sem=None):
    return pltpu.CompilerParams(vmem_limit_bytes=VMEM_LIMIT, dimension_semantics=sem)


def _tile(n, cap, align=128):
    if n <= cap:
        return n
    for t in range(cap - cap % align, 0, -align):
        if n % t == 0:
            return t
    return n


def _peers():
    x_, y_, c_ = lax.axis_index("x"), lax.axis_index("y"), lax.axis_index("c")
    out = []
    for k in range(1, N_DEV):
        px, py, pc = (x_ + (k >> 2)) % 2, (y_ + ((k >> 1) & 1)) % 2, (c_ + (k & 1)) % 2
        out.append((k, (px, py, pc), 4 * px + 2 * py + pc))
    return 4 * x_ + 2 * y_ + c_, out


class _Exchange:
    def __init__(self, n):
        self.n = n
        self.scratch = [pltpu.SemaphoreType.DMA((7 * n,)), pltpu.SemaphoreType.DMA((7 * n,)),
                        pltpu.SemaphoreType.DMA((n,))]
        self.results = None

    def start(self, cin, cout, sems):
        local, sends, _ = self.copies(cin, cout, sems)
        for cp in local + sends:
            cp.start()

    def wait(self, cin, cout, sems):
        local, sends, recvs = self.copies(cin, cout, sems)
        for cp in recvs:
            cp.wait_recv()
        for cp in sends:
            cp.wait_send()
        for cp in local:
            cp.wait()


class _Gather(_Exchange):
    def __init__(self, xs):
        super().__init__(len(xs))
        self.inputs = list(xs)
        self.out_shape = [jax.ShapeDtypeStruct((N_DEV,) + x.shape, x.dtype) for x in xs]
        self.aliases = {}

    def copies(self, cin, cout, sems):
        send_sems, recv_sems, local_sems = sems
        me, peers = _peers()
        local, sends, recvs = [], [], []
        for a in range(self.n):
            local.append(pltpu.make_async_copy(cin[a], cout[a].at[me], local_sems.at[a]))
            for k, dev, flat in peers:
                sem = dict(send_sem=send_sems.at[7 * a + k - 1], recv_sem=recv_sems.at[7 * a + k - 1],
                           device_id=dev, device_id_type=MESH)
                sends.append(pltpu.make_async_remote_copy(src_ref=cin[a], dst_ref=cout[a].at[me], **sem))
                recvs.append(pltpu.make_async_remote_copy(src_ref=cin[a], dst_ref=cout[a].at[flat], **sem))
        return local, sends, recvs


class _Scatter(_Exchange):
    def __init__(self, xs, recv, row0):
        super().__init__(len(xs))
        self.inputs = list(xs) + list(recv)
        self.out_shape = [jax.ShapeDtypeStruct(r.shape, r.dtype) for r in recv]
        self.aliases = {self.n + a: a for a in range(self.n)}
        self.row0 = list(row0)

    def copies(self, cin, cout, sems):
        send_sems, recv_sems, local_sems = sems
        me, peers = _peers()
        local, sends, recvs = [], [], []
        for a in range(self.n):
            rows = pl.ds(self.row0[a], self.inputs[a].shape[1])
            local.append(pltpu.make_async_copy(cin[a].at[me], cout[a].at[me, rows], local_sems.at[a]))
            for k, dev, flat in peers:
                sem = dict(send_sem=send_sems.at[7 * a + k - 1], recv_sem=recv_sems.at[7 * a + k - 1],
                           device_id=dev, device_id_type=MESH)
                sends.append(pltpu.make_async_remote_copy(src_ref=cin[a].at[flat], dst_ref=cout[a].at[me, rows], **sem))
                recvs.append(pltpu.make_async_remote_copy(src_ref=cin[a].at[me], dst_ref=cout[a].at[flat, rows], **sem))
        return local, sends, recvs


def _call(name, body, grid, in_specs, out_specs, out_shape, arrays, scratch=(), sem=None, comm=None):
    if comm is None:
        res = pl.pallas_call(body, name=name, grid=grid, in_specs=list(in_specs), out_specs=list(out_specs),
                             out_shape=list(out_shape), scratch_shapes=list(scratch), compiler_params=_cparams(sem))(*arrays)
        return list(res)
    ni, no, ns = len(in_specs), len(out_specs), len(scratch)
    nci, nco = len(comm.inputs), len(comm.out_shape)
    hbm = pl.BlockSpec(memory_space=pl.ANY)

    def wrapped(*refs):
        ins, cin = refs[:ni], refs[ni:ni + nci]
        o0 = ni + nci
        outs, cout = refs[o0:o0 + no], refs[o0 + no:o0 + no + nco]
        s0 = o0 + no + nco
        scr, sems = refs[s0:s0 + ns], refs[s0 + ns:]
        ids = [pl.program_id(d) for d in range(len(grid))]
        first = functools.reduce(jnp.logical_and, [i == 0 for i in ids])
        last = functools.reduce(jnp.logical_and, [i == g - 1 for i, g in zip(ids, grid)])

        @pl.when(first)
        def _():
            comm.start(cin, cout, sems)

        body(*ins, *outs, *scr)

        @pl.when(last)
        def _():
            comm.wait(cin, cout, sems)

    res = pl.pallas_call(
        wrapped, name=name, grid=grid, in_specs=list(in_specs) + [hbm] * nci, out_specs=list(out_specs) + [hbm] * nco,
        out_shape=list(out_shape) + list(comm.out_shape), scratch_shapes=list(scratch) + comm.scratch,
        input_output_aliases={ni + i: no + j for i, j in comm.aliases.items()},
        compiler_params=_cparams(("arbitrary",) * len(grid)))(*arrays, *comm.inputs)
    comm.results = list(res[no:])
    return list(res[:no])


def win(arr, width, idx):
    return ('win', arr, width, idx)


def tab(arr):
    return ('tab', arr)


def _rw(name, fn, ins, consts, outs, accs=(), tile=256, comm=None):
    first = ins[0][1] if isinstance(ins[0], tuple) else ins[0]
    T = first.shape[0]
    tile = min(tile, T)
    grid = (T // tile,)
    arrays, in_specs = [], []
    for x in ins:
        if isinstance(x, tuple) and x[0] == 'win':
            _, a, w, k = x
            arrays.append(a)
            in_specs.append(pl.BlockSpec((tile, w), lambda i, k=k: (i, k)))
        elif isinstance(x, tuple) and x[0] == 'tab':
            a = x[1]
            nb = a.shape[0] // tile
            arrays.append(a)
            in_specs.append(pl.BlockSpec((tile, a.shape[1]), lambda i, nb=nb: (i % nb, 0)))
        else:
            arrays.append(x)
            in_specs.append(pl.BlockSpec((tile, x.shape[1]), lambda i: (i, 0)))
    for c in consts:
        arrays.append(c)
        in_specs.append(pl.BlockSpec(c.shape, lambda i, n=c.ndim: (0,) * n))
    out_shape = [jax.ShapeDtypeStruct((T, f), dt) for f, dt in outs]
    out_specs = [pl.BlockSpec((tile, f), lambda i: (i, 0)) for f, _ in outs]
    for shp in accs:
        out_shape.append(jax.ShapeDtypeStruct(shp, F32))
        out_specs.append(pl.BlockSpec(shp, lambda i, n=len(shp): (0,) * n))
    ni, nc, no = len(ins), len(consts), len(outs)

    def body(*refs):
        vals_in = [r[...] for r in refs[:ni]]
        vals_in = [v.astype(F32) if v.dtype == BF16 else v for v in vals_in]
        vals = fn(*vals_in, *[r[...] for r in refs[ni:ni + nc]])
        if not isinstance(vals, (tuple, list)):
            vals = (vals,)
        o_refs = refs[ni + nc:]
        for r, v in zip(o_refs[:no], vals[:no]):
            r[...] = v.astype(r.dtype)
        i = pl.program_id(0)
        for r, v in zip(o_refs[no:], vals[no:]):
            @pl.when(i == 0)
            def _(r=r, v=v):
                r[...] = v

            @pl.when(i > 0)
            def _(r=r, v=v):
                r[...] += v

    return _call(name, body, grid, in_specs, out_specs, out_shape, arrays, sem=("arbitrary",), comm=comm)


def _mm(name, a, b, tb=False, out_dtype=F32, tm=512, tn=3072, a2=None, b2=None, comm=None):
    M, K = a.shape
    N = b.shape[0] if tb else b.shape[1]
    tm, tn = _tile(M, tm, 8), _tile(N, tn)
    dims = (((1,), (1,)), ((), ())) if tb else (((1,), (0,)), ((), ()))
    pairs = [(a, b)] + ([(a2, b2)] if a2 is not None else [])

    def body(*refs):
        o_ref = refs[-1]
        acc = None
        for i in range(len(pairs)):
            p = lax.dot_general(refs[2 * i][...].astype(BF16), refs[2 * i + 1][...].astype(BF16), dims,
                                preferred_element_type=F32)
            acc = p if acc is None else acc + p
        o_ref[...] = acc.astype(o_ref.dtype)

    in_specs, arrays = [], []
    for x, y in pairs:
        k = x.shape[1]
        in_specs.append(pl.BlockSpec((tm, k), lambda j, i: (i, 0)))
        in_specs.append(pl.BlockSpec((tn, k), lambda j, i: (j, 0)) if tb else pl.BlockSpec((k, tn), lambda j, i: (0, j)))
        arrays += [x, y]
    return _call(name, body, (N // tn, M // tm), in_specs, [pl.BlockSpec((tm, tn), lambda j, i: (i, j))],
                 [jax.ShapeDtypeStruct((M, N), out_dtype)], arrays, sem=("parallel", "parallel"), comm=comm)[0]


def _mm_t(name, a, g, tk=1408, tn=1408, tt=2048, comm=None):
    T, K = a.shape
    N = g.shape[1]
    tk, tn, tt = _tile(K, tk), _tile(N, tn), _tile(T, tt, 8)
    nt = T // tt

    def body(a_ref, g_ref, o_ref, acc_ref):
        p = lax.dot_general(a_ref[...].astype(BF16), g_ref[...].astype(BF16), (((0,), (0,)), ((), ())),
                            preferred_element_type=F32)
        t = pl.program_id(2)

        @pl.when(t == 0)
        def _():
            acc_ref[...] = p

        @pl.when(t > 0)
        def _():
            acc_ref[...] += p

        @pl.when(t == nt - 1)
        def _():
            o_ref[...] = acc_ref[...].astype(BF16)

    return _call(name, body, (K // tk, N // tn, nt),
                 [pl.BlockSpec((tt, tk), lambda i, j, t: (t, i)), pl.BlockSpec((tt, tn), lambda i, j, t: (t, j))],
                 [pl.BlockSpec((tk, tn), lambda i, j, t: (i, j))], [jax.ShapeDtypeStruct((K, N), BF16)], [a, g],
                 scratch=[pltpu.VMEM((tk, tn), F32)], sem=("parallel", "parallel", "arbitrary"), comm=comm)[0]


def _dot(a, b):
    return jnp.dot(a.astype(BF16), b.astype(BF16), preferred_element_type=F32)


def _dot_nt(a, b):
    return lax.dot_general(a.astype(BF16), b.astype(BF16), (((1,), (1,)), ((), ())), preferred_element_type=F32)


def _dot_tn(a, b):
    return lax.dot_general(a.astype(BF16), b.astype(BF16), (((0,), (0,)), ((), ())), preferred_element_type=F32)


def _sigmoid(x):
    return 0.5 * jnp.tanh(0.5 * x) + 0.5


def _ffn_up(h, wg, wu, comm=None):
    M, K = h.shape
    N = wg.shape[1]
    tm, tn = _tile(M, 512, 8), _tile(N, 1408)

    def body(h_ref, wg_ref, wu_ref, ga_ref, gb_ref, act_ref):
        hb = h_ref[...]
        a = jnp.dot(hb, wg_ref[...], preferred_element_type=F32)
        b = jnp.dot(hb, wu_ref[...], preferred_element_type=F32)
        s = _sigmoid(a)
        silu = a * s
        ga_ref[...] = (b * (s + silu * (1.0 - s))).astype(BF16)
        gb_ref[...] = silu.astype(BF16)
        act_ref[...] = (silu * b).astype(BF16)

    wspec = pl.BlockSpec((K, tn), lambda j, i: (0, j))
    ospec = pl.BlockSpec((tm, tn), lambda j, i: (i, j))
    return _call("ffn_up", body, (N // tn, M // tm), [pl.BlockSpec((tm, K), lambda j, i: (i, 0)), wspec, wspec],
                 [ospec] * 3, [jax.ShapeDtypeStruct((M, N), BF16)] * 3, [h, wg, wu], sem=("parallel", "parallel"),
                 comm=comm)


def _ffn_down_dx(df, wdn, ga, gb):
    M, K = df.shape
    N = wdn.shape[0]
    tm, tn = _tile(M, 512, 8), _tile(N, 1408)

    def body(df_ref, w_ref, ga_ref, gb_ref, da_ref, db_ref):
        d = _dot_nt(df_ref[...], w_ref[...])
        da_ref[...] = (d * ga_ref[...].astype(F32)).astype(BF16)
        db_ref[...] = (d * gb_ref[...].astype(F32)).astype(BF16)

    tspec = pl.BlockSpec((tm, tn), lambda j, i: (i, j))
    return _call("ffn_down_dx", body, (N // tn, M // tm),
                 [pl.BlockSpec((tm, K), lambda j, i: (i, 0)), pl.BlockSpec((tn, K), lambda j, i: (j, 0)), tspec, tspec],
                 [tspec] * 2, [jax.ShapeDtypeStruct((M, N), BF16)] * 2, [df, wdn, ga, gb], sem=("parallel", "parallel"))


def _mm_norm(name, a, b, x, g, g_next=None, comm=None):
    M, K = a.shape
    N = b.shape[1]
    tm = _tile(M, 512, 8)
    more = g_next is not None

    def body(*refs):
        a_ref, b_ref, x_ref, g_ref = refs[:4]
        y_ref, xn_ref = refs[4 + more], refs[5 + more]
        y = jnp.dot(a_ref[...].astype(BF16), b_ref[...], preferred_element_type=F32)
        y_ref[...] = y
        xn = x_ref[...] + _rms(y, g_ref[...])
        xn_ref[...] = xn
        if more:
            refs[7][...] = _rms(xn, refs[4][...]).astype(BF16)

    row = lambda n: pl.BlockSpec((tm, n), lambda i: (i, 0))
    whole = lambda z: pl.BlockSpec(z.shape, lambda i: (0, 0))
    arrays = [a, b, x, g] + ([g_next] if more else [])
    in_specs = [row(K), whole(b), row(N), whole(g)] + ([whole(g_next)] if more else [])
    out_shape = [jax.ShapeDtypeStruct((M, N), F32)] * 2 + ([jax.ShapeDtypeStruct((M, N), BF16)] if more else [])
    return _call(name, body, (M // tm,), in_specs, [row(N)] * (2 + more), out_shape, arrays, sem=("parallel",), comm=comm)


def _rms(x, g):
    r = lax.rsqrt(jnp.mean(x * x, axis=-1, keepdims=True) + EPS)
    return x * r * g


def _rms_bwd(x, g, dy):
    r = lax.rsqrt(jnp.mean(x * x, axis=-1, keepdims=True) + EPS)
    xh = x * r
    dxh = dy * g
    dx = r * (dxh - xh * jnp.mean(dxh * xh, axis=-1, keepdims=True))
    return dx, jnp.sum(dy * xh, axis=0, keepdims=True)


def _shift_down(x, k):
    t = lax.broadcasted_iota(jnp.int32, x.shape, 0)
    return jnp.where(t >= k, pltpu.roll(x, k, 0), 0.0)


def _shift_up(x, k):
    n = x.shape[0]
    t = lax.broadcasted_iota(jnp.int32, x.shape, 0)
    return jnp.where(t < n - k, pltpu.roll(x, n - k, 0), 0.0)


def _group_select(vals):
    lane = lax.broadcasted_iota(jnp.int32, vals[0].shape, 1) // GROUP
    out = vals[-1]
    for g in range(len(vals) - 2, -1, -1):
        out = jnp.where(lane == g, vals[g], out)
    return out


def _pool_count(shape):
    t = lax.broadcasted_iota(jnp.int32, shape, 0)
    lane = lax.broadcasted_iota(jnp.int32, shape, 1) // GROUP
    w = jnp.where(lane == 0, POOL_WINDOWS[0], jnp.where(lane == 1, POOL_WINDOWS[1],
                  jnp.where(lane == 2, POOL_WINDOWS[2], POOL_WINDOWS[3])))
    return jnp.minimum(t + 1, w).astype(F32)


def _pooled(z):
    s = [z]
    for k in (1, 2, 4, 8):
        s.append(s[-1] + _shift_down(s[-1], k))
    return _group_select(s[1:]) / _pool_count(z.shape) - z


def _pooled_bwd(dp):
    u = dp / _pool_count(dp.shape)
    s = [u]
    for k in (1, 2, 4, 8):
        s.append(s[-1] + _shift_up(s[-1], k))
    return _group_select(s[1:]) - dp


def _diag_mask():
    r = lax.broadcasted_iota(jnp.int32, (TQ, TQ), 0) // CHUNK
    c = lax.broadcasted_iota(jnp.int32, (TQ, TQ), 1) // CHUNK
    return r >= c


def _attn_fwd(q, k, v, B, S, comm=None):
    nq = S // TQ

    def body(q_ref, k_ref, v_ref, o_ref, lse_ref):
        mask = _diag_mask()
        for i in range(nq):
            r0, r1 = i * TQ, (i + 1) * TQ
            qb = q_ref[r0:r1, :]
            sd = jnp.where(mask, _dot_nt(qb, k_ref[r0:r1, :]), NEG_INF)
            m = jnp.max(sd, axis=-1, keepdims=True)
            if i:
                so = _dot_nt(qb, k_ref[0:r0, :])
                m = jnp.maximum(m, jnp.max(so, axis=-1, keepdims=True))
            pd = jnp.exp(sd - m)
            l = jnp.sum(pd, axis=-1, keepdims=True)
            acc = _dot(pd, v_ref[r0:r1, :])
            if i:
                po = jnp.exp(so - m)
                l = l + jnp.sum(po, axis=-1, keepdims=True)
                acc = acc + _dot(po, v_ref[0:r0, :])
            o_ref[r0:r1, :] = acc / l
            lse_ref[r0:r1, :] = jnp.broadcast_to(m + jnp.log(l), (TQ, HP))

    spec = pl.BlockSpec((S, HP), lambda b, h: (b, h))
    return _call("attn_fwd", body, (B, HEADS), [spec] * 3, [spec] * 2, [jax.ShapeDtypeStruct((B * S, QW), F32)] * 2,
                 [q, k, v], sem=("parallel", "parallel"), comm=comm)


def _attn_bwd(q, k, v, o, do, lse, B, S, comm=None):
    nq = S // TQ

    def body(q_ref, k_ref, v_ref, o_ref, do_ref, lse_ref, dq_ref, dk_ref, dv_ref, dk_acc, dv_acc):
        mask = _diag_mask()
        dk_acc[...] = jnp.zeros_like(dk_acc)
        dv_acc[...] = jnp.zeros_like(dv_acc)
        for i in range(nq):
            r0, r1 = i * TQ, (i + 1) * TQ
            qb, dob = q_ref[r0:r1, :], do_ref[r0:r1, :]
            delta = jnp.sum(o_ref[r0:r1, :] * dob, axis=-1, keepdims=True)
            lse_b = lse_ref[r0:r1, :][:, :1]
            dq = None
            for c0, c1, diag in ([(0, r0, False)] if i else []) + [(r0, r1, True)]:
                kb, vb = k_ref[c0:c1, :], v_ref[c0:c1, :]
                p = jnp.exp(_dot_nt(qb, kb) - lse_b)
                if diag:
                    p = jnp.where(mask, p, 0.0)
                ds = p * (_dot_nt(dob, vb) - delta)
                part = _dot(ds, kb)
                dq = part if dq is None else dq + part
                dk_acc[c0:c1, :] += _dot_tn(ds, qb)
                dv_acc[c0:c1, :] += _dot_tn(p, dob)
            dq_ref[r0:r1, :] = dq
        dk_ref[...] = dk_acc[...].astype(BF16)
        dv_ref[...] = dv_acc[...].astype(BF16)

    spec = pl.BlockSpec((S, HP), lambda b, h: (b, h))
    return _call("attn_bwd", body, (B, HEADS), [spec] * 6, [spec] * 3,
                 [jax.ShapeDtypeStruct((B * S, QW), F32)] + [jax.ShapeDtypeStruct((B * S, QW), BF16)] * 2,
                 [q, k, v, o, do, lse], scratch=[pltpu.VMEM((S, HP), F32)] * 2, sem=("parallel", "parallel"), comm=comm)


def _all_gather(name, xs):
    n = len(xs)

    def body(*refs):
        x_refs, out_refs = refs[:n], refs[n:2 * n]
        send_sems, recv_sems, local_sems = refs[2 * n:]
        x_, y_, c_ = lax.axis_index("x"), lax.axis_index("y"), lax.axis_index("c")
        me, sibling = (x_, y_, c_), (x_, y_, 1 - c_)
        chips = [(1 - x_, y_), (x_, 1 - y_), (1 - x_, 1 - y_)]

        def copy(a, k, block, to, own=False):
            slot = out_refs[a].at[4 * block[0] + 2 * block[1] + block[2]]
            return pltpu.make_async_remote_copy(
                src_ref=x_refs[a] if own else slot, dst_ref=slot,
                send_sem=send_sems.at[7 * a + k], recv_sem=recv_sems.at[7 * a + k], device_id=to, device_id_type=MESH)

        mine = [pltpu.make_async_copy(x_refs[a], out_refs[a].at[4 * x_ + 2 * y_ + c_], local_sems.at[a]) for a in range(n)]
        for cp in mine:
            cp.start()
        first = [copy(a, 1 + j, me, (*chip, c_), own=True) for j, chip in enumerate(chips) for a in range(n)]
        first += [copy(a, 0, me, sibling, own=True) for a in range(n)]
        for cp in first:
            cp.start()
        passed = []
        for j, chip in enumerate(chips):
            for a in range(n):
                copy(a, 1 + j, (*chip, c_), me).wait_recv()
                passed.append(copy(a, 4 + j, (*chip, c_), sibling))
                passed[-1].start()
        for a in range(n):
            copy(a, 0, sibling, me).wait_recv()
            for j, chip in enumerate(chips):
                copy(a, 4 + j, (*chip, 1 - c_), me).wait_recv()
        for cp in first + passed:
            cp.wait_send()
        for cp in mine:
            cp.wait()

    return pl.pallas_call(
        body, name=name, out_shape=[jax.ShapeDtypeStruct((N_DEV,) + x.shape, x.dtype) for x in xs],
        in_specs=[pl.BlockSpec(memory_space=pl.ANY)] * n, out_specs=[pl.BlockSpec(memory_space=pl.ANY)] * n,
        scratch_shapes=[pltpu.SemaphoreType.DMA((7 * n,)), pltpu.SemaphoreType.DMA((7 * n,)), pltpu.SemaphoreType.DMA((n,))],
    )(*xs)


def _adamw(name, parts, w, m, v, row_off=0, layers=1, stride=0):
    R, C = w.shape
    r = R // layers
    assert parts.shape[2] == C and parts.shape[1] >= (layers - 1) * stride + row_off + r
    if r % 8 == 0:
        tr = max(t for t in range(8, 513, 8) if r % t == 0 and row_off % t == 0 and stride % t == 0)
    else:
        assert layers == 1
        tr = r
    nb = r // tr

    def body(p_ref, w_ref, m_ref, v_ref, g_ref, d_ref, nm_ref, nv_ref):
        g = p_ref[0].astype(F32)
        for j in range(1, N_DEV):
            g = g + p_ref[j].astype(F32)
        m_new = ADAM_B1 * m_ref[...] + (1.0 - ADAM_B1) * g
        v_new = ADAM_B2 * v_ref[...] + (1.0 - ADAM_B2) * (g * g)
        m_hat = m_new / (1.0 - ADAM_B1 ** ADAM_STEP)
        v_hat = v_new / (1.0 - ADAM_B2 ** ADAM_STEP)
        g_ref[...] = g
        d_ref[...] = -ADAM_LR * (m_hat / (jnp.sqrt(v_hat) + ADAM_EPS) + ADAM_WD * w_ref[...])
        nm_ref[...] = m_new
        nv_ref[...] = v_new

    spec = pl.BlockSpec((tr, C), lambda l, i: (l * nb + i, 0))
    pspec = pl.BlockSpec((N_DEV, tr, C), lambda l, i: (0, (l * stride + row_off) // tr + i, 0))
    return _call(name, body, (layers, nb), [pspec, spec, spec, spec], [spec] * 4,
                 [jax.ShapeDtypeStruct((R, C), F32)] * 4, [parts, w, m, v], sem=("parallel", "parallel"))


def _rot_cols(w):
    h = w.shape[-1] // 2
    return jnp.concatenate([-w[..., h:], w[..., :h]], axis=-1)


def _unrot_cols(d):
    h = d.shape[-1] // 2
    return jnp.concatenate([d[..., h:], -d[..., :h]], axis=-1)


IN_WIDTH = sum(IN_SIZES)
IN_SHARD = IN_WIDTH // N_DEV
IN_GATE = IN_WIDTH - IN_SIZES[-1]


def _w_in_cols(buf, a, b):
    out = []
    for j in range(a // IN_SHARD, (b - 1) // IN_SHARD + 1):
        lo, hi = max(a, IN_SHARD * j) - IN_SHARD * j, min(b, IN_SHARD * (j + 1)) - IN_SHARD * j
        out.append(buf[j, :, lo:hi])
    return out


def _w1(buf):
    kr0 = IN_SIZES[0] + IN_SIZES[1]
    kr = jnp.concatenate(_w_in_cols(buf, kr0, kr0 + QK_ROPE), axis=-1)
    w1s = jnp.concatenate(_w_in_cols(buf, 0, kr0) + [kr, _rot_cols(kr), jnp.zeros((D, 64), BF16)]
                          + _w_in_cols(buf, kr0 + QK_ROPE, IN_GATE), axis=-1)
    return w1s, jnp.concatenate(_w_in_cols(buf, IN_GATE, IN_WIDTH), axis=-1)


def _w_in_grad_shards(d1s, d1g):
    kr0 = IN_SIZES[0] + IN_SIZES[1]
    seg = d1s[:, C_KR:C_KR + 128]
    dkr = seg[:, :QK_ROPE] + _unrot_cols(seg[:, QK_ROPE:2 * QK_ROPE])
    runs = [(0, d1s, 0), (kr0, dkr, 0), (kr0 + QK_ROPE, d1s, C_PIN), (IN_GATE, d1g, 0), (IN_WIDTH, None, 0)]
    blocks = []
    for j in range(N_DEV):
        a, b = IN_SHARD * j, IN_SHARD * (j + 1)
        pieces = []
        for (c0, src, s0), (c1, _, _) in zip(runs[:-1], runs[1:]):
            lo, hi = max(a, c0), min(b, c1)
            if lo < hi:
                pieces.append(src[:, s0 + lo - c0:s0 + hi - c0])
        blocks.append(jnp.concatenate(pieces, axis=-1))
    return jnp.stack(blocks)


def _layer_weights(G):
    L = G['w_uq'].shape[0]
    wq = G['w_uq'].reshape(L, Q_LORA, HEADS, QK_NOPE + QK_ROPE)
    nope, rope = wq[..., :QK_NOPE], wq[..., QK_NOPE:]
    z32 = jnp.zeros((L, Q_LORA, HEADS, HP - QK_NOPE - QK_ROPE), BF16)
    wq_a = jnp.concatenate([nope, rope, z32], axis=-1).reshape(L, Q_LORA, QW)
    wq_b = jnp.concatenate([jnp.zeros_like(nope), _rot_cols(rope), z32], axis=-1).reshape(L, Q_LORA, QW)
    wkv = G['w_ukv'].reshape(L, KV_LORA, HEADS, QK_NOPE + V_HEAD)
    z64 = jnp.zeros((L, KV_LORA, HEADS, HP - QK_NOPE), BF16)
    wkn = jnp.concatenate([wkv[..., :QK_NOPE], z64], axis=-1).reshape(L, KV_LORA, QW)
    wv = jnp.concatenate([wkv[..., QK_NOPE:], z64], axis=-1).reshape(L, KV_LORA, QW)
    wa = G['w_br_a'].reshape(L, HEADS, V_HEAD, D)
    wa = jnp.concatenate([wa, jnp.zeros((L, HEADS, HP - V_HEAD, D), BF16)], axis=2).reshape(L, QW, D)
    return dict(
        Wq=jnp.concatenate([wq_a, wq_b], axis=-1),
        Wkv=jnp.concatenate([wkn, wv], axis=-1), Wa=wa, Wb=G['w_br_b'], Wc=G['w_br_c'], Wd=G['w_br_d'],
        conv=G['conv_w'].reshape(L, 3, BR).astype(F32))


def _tables(S):
    inv = ROPE_THETA ** (-jnp.arange(0, QK_ROPE, 2, dtype=F32) / QK_ROPE)
    ang = jnp.arange(S, dtype=F32)[:, None] * inv[None, :]
    cos, sin = jnp.cos(ang), jnp.sin(ang)
    scale = (QK_NOPE + QK_ROPE) ** -0.5
    one, zero = jnp.ones((S, QK_NOPE), F32), jnp.zeros((S, QK_NOPE), F32)
    z32 = jnp.zeros((S, HP - QK_NOPE - QK_ROPE), F32)
    cq = jnp.concatenate([one, cos, cos, z32], axis=1) * scale
    sq = jnp.concatenate([zero, sin, sin, z32], axis=1) * scale
    ck = jnp.concatenate([cos, cos, sin, sin, jnp.zeros((S, HP - 2 * QK_ROPE), F32)], axis=1)
    return cq, sq, ck


def _place_k():
    p = np.zeros((HP, QW), np.float32)
    for r in range(2 * QK_ROPE):
        for h in range(HEADS):
            p[r, h * HP + QK_NOPE + r % QK_ROPE] = 1.0
    return jnp.asarray(p, BF16)


def _layer_fwd(x, h, W, late, P, tabs, B, S, ride, g_next):
    cq_t, sq_t, ck_t, place = tabs
    g_cq, g_ckv = P['g_cq'], P['g_ckv']
    P1 = _mm("in_proj_s", h, W['W1s'], out_dtype=BF16)
    PG = _mm("in_proj_g", h, W['W1g'], out_dtype=BF16, comm=ride.get('in_proj'))

    def q_prep(c, cq, sq, g, w):
        qq = _dot(_rms(c, g), w)
        return qq[:, :QW] * jnp.tile(cq, (1, HEADS)) + qq[:, QW:] * jnp.tile(sq, (1, HEADS))
    q = _rw("q_prep", q_prep, [win(P1, 256, 0), tab(cq_t), tab(sq_t)], [g_cq, W['Wq']], [(QW, BF16)])[0]

    def kv_prep(c, seg, ck, g, w, place):
        kv = _dot(_rms(c, g), w)
        return kv[:, :QW] + _dot(seg * ck, place), kv[:, QW:]
    k, v = _rw("kv_prep", kv_prep, [win(P1, 128, 2), win(P1, 128, 3), tab(ck_t)], [g_ckv, W['Wkv'], place],
               [(QW, BF16), (QW, BF16)])
    o, lse = _attn_fwd(q, k, v, B, S, comm=ride.get('attn_fwd'))

    def pool_f(z, wbd, scale):
        return _dot(_pooled(z), wbd) * scale
    bo = _rw("pool_fwd", pool_f, [win(P1, BR, 2)], [P['pool_bd'], P['pool_scale']], [(BR, BF16)], tile=S)[0]

    def sgu_f(u, sv, g, wm, bias):
        vn = _rms(sv, g)
        blocks = [vn[r:r + SGU_BLOCK] for r in range(0, vn.shape[0], SGU_BLOCK)]
        return u * jnp.concatenate([_sgu_mix(vb, wm, bias) for vb in blocks], axis=0)
    co = _rw("sgu_fwd", sgu_f, [win(P1, BR, 3), win(P1, BR, 4)], [P['g_sgu_v'], P['sgu_wm'], P['sgu_bias']],
             [(BR, BF16)], tile=SGU_TILE)[0]

    def conv_f(b, c, xi, w):
        z = c * xi
        return b * (w[0:1] * _shift_down(z, 2) + w[1:2] * _shift_down(z, 1) + w[2:3] * z)
    do_ = _rw("conv_fwd", conv_f, [win(P1, BR, 5), win(P1, BR, 6), win(P1, BR, 7)], [W['conv']], [(BR, BF16)], tile=S)[0]

    def merge_f(a, b, c, d, l0, l1, l2, l3, wa, wb, wc, wd):
        return (_sigmoid(l0) * _dot(a, wa) + _sigmoid(l1) * _dot(b, wb) + _sigmoid(l2) * _dot(c, wc)
                + _sigmoid(l3) * _dot(d, wd))
    merged = _rw("merge_fwd", merge_f, [o, bo, co, do_] + [win(PG, D, i) for i in range(4)],
                 [W['Wa'], W['Wb'], W['Wc'], W['Wd']], [(D, BF16)])[0]
    W = dict(W, **late())
    om, x1, h2 = _mm_norm("out_proj", merged, W['Wout'], x, P['g_post_mix'], P['g_pre_ffn'])
    ga, gb, act = _ffn_up(h2, W['Wg'], W['Wu'], comm=ride.get('ffn_up'))
    f, x2, *h_next = _mm_norm("ffn_down", act, W['Wdn'], x1, P['g_post_ffn'], g_next, comm=ride.get('ffn_down'))
    saved = dict(x=x, h=h, P1=P1, PG=PG, q=q, k=k, v=v, o=o, lse=lse, bo=bo, co=co, do=do_, merged=merged, om=om, x1=x1,
                 h2=h2, ga=ga, gb=gb, act=act, f=f)
    return x2, (h_next[0] if h_next else None), saved, W


def _sgu_mix(vn, wm, bias):
    outs = [_dot(wm[g], vn) for g in range(4)]
    return _group_select(outs) + bias


def _layer_bwd(dx2, df, sv, W, P, tabs, B, S, own_ride, prev):
    cq_t, sq_t, ck_t, place = tabs
    P1 = sv['P1']
    grads = {}
    da, db = _ffn_down_dx(df, W['Wdn'], sv['ga'], sv['gb'])
    grads['Wdn'] = _mm_t("ffn_down_dw", sv['act'], df)
    dh2 = _mm("ffn_up_dx", da, W['Wg'], tb=True, a2=db, b2=W['Wu'], out_dtype=BF16)
    grads['Wg'] = _mm_t("ffn_gate_dw", sv['h2'], da)
    grads['Wu'] = _mm_t("ffn_up_dw", sv['h2'], db)

    def mid_norms_b(x1, dh, dy, om, g_ffn, g_mix):
        dx, dg_ffn = _rms_bwd(x1, g_ffn, dh)
        dx1 = dy + dx
        dom, dg_mix = _rms_bwd(om, g_mix, dx1)
        return dx1, dom, dg_ffn, dg_mix
    dx1, dom, grads['g_pre_ffn'], grads['g_post_mix'] = _rw(
        "mid_norms_bwd", mid_norms_b, [sv['x1'], dh2, dx2, sv['om']], [P['g_pre_ffn'], P['g_post_mix']],
        [(D, F32), (D, BF16)], [(1, D), (1, D)], tile=512)
    dmerged = _mm("out_proj_dx", dom, W['Wout'], tb=True, out_dtype=BF16)
    grads['Wout'] = _mm_t("out_proj_dw", sv['merged'], dom)
    ride = own_ride('ffn', grads)

    def merge_b(a, b, c, d, l0, l1, l2, l3, dm, wa, wb, wc, wd):
        outs_dl, outs_dy, outs_dbr = [], [], []
        for br, l, w in ((a, l0, wa), (b, l1, wb), (c, l2, wc), (d, l3, wd)):
            s = _sigmoid(l)
            y = _dot(br, w)
            dy = (dm * s).astype(BF16)
            outs_dl.append((dm * y * s * (1.0 - s)).astype(BF16))
            outs_dy.append(dy)
            outs_dbr.append(_dot_nt(dy, w))
        return (jnp.concatenate(outs_dl, axis=1), *outs_dy, *outs_dbr)
    mb = _rw("merge_bwd", merge_b,
             [sv['o'], sv['bo'], sv['co'], sv['do']] + [win(sv['PG'], D, i) for i in range(4)] + [dmerged],
             [W['Wa'], W['Wb'], W['Wc'], W['Wd']],
             [(4 * D, BF16)] + [(D, BF16)] * 4 + [(QW, F32), (BR, F32), (BR, F32), (BR, F32)], comm=ride.get('merge_bwd'))
    dl, dy, (d_o, d_bo, d_co, d_do) = mb[0], mb[1:5], mb[5:9]
    grads['Wa'] = _mm_t("br_a_dw", sv['o'], dy[0])
    grads['Wb'] = _mm_t("br_b_dw", sv['bo'], dy[1])
    grads['Wc'] = _mm_t("br_c_dw", sv['co'], dy[2])
    grads['Wd'] = _mm_t("br_d_dw", sv['do'], dy[3])

    def conv_b(b, c, xi, dout, w):
        z = c * xi
        z1, z2 = _shift_down(z, 1), _shift_down(z, 2)
        y = w[0:1] * z2 + w[1:2] * z1 + w[2:3] * z
        dyv = dout * b
        dz = w[2:3] * dyv + w[1:2] * _shift_up(dyv, 1) + w[0:1] * _shift_up(dyv, 2)
        return (dout * y, dz * xi, dz * c, jnp.sum(dyv * z2, axis=0, keepdims=True),
                jnp.sum(dyv * z1, axis=0, keepdims=True), jnp.sum(dyv * z, axis=0, keepdims=True))
    dcvb, dcvc, dcvx, dw0, dw1, dw2 = _rw("conv_bwd", conv_b, [win(P1, BR, 5), win(P1, BR, 6), win(P1, BR, 7), d_do],
                                          [W['conv']], [(BR, BF16)] * 3, [(1, BR)] * 3, tile=S)
    grads['conv'] = jnp.concatenate([dw0, dw1, dw2], axis=0)

    def sgu_b(u, s_v, dout, g, wm, bias):
        vn = _rms(s_v, g)
        dmix = dout * u
        lane = lax.broadcasted_iota(jnp.int32, (SGU_BLOCK, BR), 1) // GROUP
        mixed, dvn, dwm, dbias = [], [], [0.0] * 4, 0.0
        for r in range(0, vn.shape[0], SGU_BLOCK):
            vb, db = vn[r:r + SGU_BLOCK], dmix[r:r + SGU_BLOCK]
            mixed.append(_sgu_mix(vb, wm, bias))
            dvn.append(_group_select([_dot_tn(wm[gi], db) for gi in range(4)]))
            dwm = [dwm[gi] + _dot_nt(jnp.where(lane == gi, db, 0.0), vb) for gi in range(4)]
            dbias = dbias + db
        dsv, dg = _rms_bwd(s_v, g, jnp.concatenate(dvn, axis=0))
        return (dout * jnp.concatenate(mixed, axis=0), dsv, dg, *dwm, dbias)
    dsu, dsv_, grads['g_sgu_v'], wm0, wm1, wm2, wm3, grads['sgu_bias'] = _rw(
        "sgu_bwd", sgu_b, [win(P1, BR, 3), win(P1, BR, 4), d_co], [P['g_sgu_v'], P['sgu_wm'], P['sgu_bias']],
        [(BR, BF16)] * 2, [(1, BR)] + [(SGU_BLOCK, SGU_BLOCK)] * 4 + [(SGU_BLOCK, BR)], tile=SGU_TILE)
    grads['sgu_wm'] = jnp.stack([wm0, wm1, wm2, wm3])

    def pool_b(z, dout, wbd, scale):
        pooled = _pooled(z)
        mixed = _dot(pooled, wbd)
        dmix = dout * scale
        dz = _pooled_bwd(_dot_nt(dmix, wbd))
        return dz, _dot_tn(pooled, dmix), jnp.sum(dout * mixed, axis=0, keepdims=True)
    dpin, grads['pool_bd'], grads['pool_scale'] = _rw(
        "pool_bwd", pool_b, [win(P1, BR, 2), d_bo], [P['pool_bd'], P['pool_scale']], [(BR, BF16)], [(BR, BR), (1, BR)], tile=S)

    dq, dk, dv = _attn_bwd(sv['q'], sv['k'], sv['v'], sv['o'], d_o, sv['lse'], B, S, comm=ride.get('attn_bwd'))

    def q_prep_b(c, dq, cq, sq, g, w):
        dqq = jnp.concatenate([dq * jnp.tile(cq, (1, HEADS)), dq * jnp.tile(sq, (1, HEADS))], axis=1).astype(BF16)
        dc, dg = _rms_bwd(c, g, _dot_nt(dqq, w))
        return dc, _rms(c, g), dqq, dg
    dcq, cqn, dqq, grads['g_cq'] = _rw("q_prep_bwd", q_prep_b, [win(P1, 256, 0), dq, tab(cq_t), tab(sq_t)],
                                       [P['g_cq'], W['Wq']], [(Q_LORA, BF16), (Q_LORA, BF16), (2 * QW, BF16)], [(1, Q_LORA)])
    grads['Wq'] = _mm_t("uq_dw", cqn, dqq)

    def kv_prep_b(c, dk, dv, ck, g, w, place):
        dc, dg = _rms_bwd(c, g, _dot_nt(dk, w[:, :QW]) + _dot_nt(dv, w[:, QW:]))
        return dc, _dot_nt(dk, place) * ck, _rms(c, g), dg
    dckv, dseg, kvn, grads['g_ckv'] = _rw(
        "kv_prep_bwd", kv_prep_b, [win(P1, 128, 2), dk, dv, tab(ck_t)], [P['g_ckv'], W['Wkv'], place],
        [(KV_LORA, BF16), (HP, BF16), (KV_LORA, BF16)], [(1, KV_LORA)])
    grads['Wkn'] = _mm_t("uk_dw", kvn, dk)
    grads['Wv'] = _mm_t("uv_dw", kvn, dv)
    ride = dict(ride, **own_ride('mixers', grads))

    dps = jnp.concatenate([dcq, dckv, dseg, dpin, dsu, dsv_, dcvb, dcvc, dcvx], axis=1)
    grads['W1s'] = _mm_t("in_proj_s_dw", sv['h'], dps)
    grads['W1g'] = _mm_t("in_proj_g_dw", sv['h'], dl, comm=ride.get('in_proj_g_dw'))
    ride = dict(ride, **own_ride('w_in', grads))
    dh = _mm("in_proj_dx", dps, W['W1s'], tb=True, a2=dl, b2=W['W1g'], out_dtype=BF16, tm=256, comm=ride.get('in_proj_dx'))

    if prev is None:
        def pre_mix_b(x, dh, dy, g):
            dx, dg = _rms_bwd(x, g, dh)
            return dy + dx, dg
        dx, grads['g_pre_mix'] = _rw("pre_mix_bwd", pre_mix_b, [sv['x'], dh, dx1], [P['g_pre_mix']], [(D, F32)], [(1, D)], tile=512)
        return dx, None, None, grads, ride

    def edge_norms_b(x, dh, dy, f, g_mix, g_ffn):
        dx, dg_mix = _rms_bwd(x, g_mix, dh)
        dx = dy + dx
        df, dg_ffn = _rms_bwd(f, g_ffn, dx)
        return dx, df, dg_mix, dg_ffn
    dx, df_prev, grads['g_pre_mix'], dg_prev = _rw(
        "edge_norms_bwd", edge_norms_b, [sv['x'], dh, dx1, prev[0]], [P['g_pre_mix'], prev[1]],
        [(D, F32), (D, BF16)], [(1, D), (1, D)], tile=512)
    return dx, df_prev, dg_prev, grads, ride


def _mixer_grads(g):
    out = {}
    qa = g['Wq'][:, :QW].reshape(1, Q_LORA, HEADS, HP)
    qb = g['Wq'][:, QW:].reshape(1, Q_LORA, HEADS, HP)
    drope = qa[..., QK_NOPE:QK_NOPE + QK_ROPE] + _unrot_cols(qb[..., QK_NOPE:QK_NOPE + QK_ROPE])
    out['w_uq'] = jnp.concatenate([qa[..., :QK_NOPE], drope], axis=-1).reshape(1, Q_LORA, HEADS * (QK_NOPE + QK_ROPE))
    kn = g['Wkn'].reshape(1, KV_LORA, HEADS, HP)[..., :QK_NOPE]
    vv = g['Wv'].reshape(1, KV_LORA, HEADS, HP)[..., :V_HEAD]
    out['w_ukv'] = jnp.concatenate([kn, vv], axis=-1).reshape(1, KV_LORA, HEADS * (QK_NOPE + V_HEAD))
    out['w_br_a'] = g['Wa'].reshape(1, HEADS, HP, D)[:, :, :V_HEAD].reshape(1, HEADS * V_HEAD, D)
    out['w_br_b'], out['w_br_c'], out['w_br_d'] = g['Wb'][None], g['Wc'][None], g['Wd'][None]
    out['conv_w'] = g['conv'].reshape(1, 3, 1, BR)
    return out


def _small_grads(g):
    out = {}
    for n in ('g_pre_mix', 'g_cq', 'g_ckv', 'g_sgu_v', 'g_post_mix', 'g_pre_ffn', 'g_post_ffn', 'pool_scale'):
        out[n] = g[n].reshape(1, -1)
    bd = g['pool_bd'].reshape(4, GROUP, 4, GROUP)
    out['pool_w'] = jnp.stack([bd[i, :, i, :] for i in range(4)])[None]
    out['sgu_w'] = (g['sgu_wm'] * _sgu_mask()[None])[None]
    out['sgu_b'] = g['sgu_bias'].reshape(SGU_BLOCK, 4, GROUP).sum(-1).T[None]
    return out


def _sgu_mask():
    pc = np.arange(SGU_BLOCK) // CHUNK
    return jnp.asarray(pc[:, None] >= pc[None, :], F32)


def _rows(a, lead):
    return a.reshape(a.shape[:lead] + (-1, a.shape[-1]))


def _pad_to(a, rows, cols):
    pad = [(0, 0)] * (a.ndim - 2) + [(0, rows - a.shape[-2]), (0, cols - a.shape[-1])]
    return jnp.pad(a, pad)


GROUPS = (('w_in',), ('w_ffn_gate',), ('w_ffn_up',), ('w_out', 'w_ffn_down'),
          ('w_uq', 'w_ukv', 'w_br_a', 'w_br_b', 'w_br_c', 'w_br_d', 'conv_w'))
PADDED = {'w_uq': (Q_LORA, 128), 'conv_w': (128, 128), 'w_ffn_down': (384, D)}
NARROW = ('w_uq', 'conv_w')


def _group(t, lead, groups):
    bufs = {}
    for b in groups:
        ps = [_rows(t[n], lead) for n in GROUPS[b]]
        ps = [_pad_to(p, *PADDED[n]) if n in PADDED else p for n, p in zip(GROUPS[b], ps)]
        bufs[b] = ps[0] if len(ps) == 1 else jnp.concatenate(ps, axis=-2)
    return bufs


def _group_rows(shapes):
    where, rows = {}, []
    for b, names in enumerate(GROUPS):
        off = 0
        for n in names:
            where[n] = (b, off)
            off += PADDED.get(n, shapes[n])[0]
        rows.append(off)
    return where, rows


def _shard_split(a, axis):
    n = a.shape[axis]
    a = a.reshape(a.shape[:axis] + (N_DEV, n // N_DEV) + a.shape[axis + 1:])
    return jnp.moveaxis(a, axis, 0)


def _shard_join(a, axis):
    a = jnp.moveaxis(a, 0, axis)
    return a.reshape(a.shape[:axis] + (a.shape[axis] * a.shape[axis + 1],) + a.shape[axis + 2:])


def _as2d(a):
    return a.reshape(-1, a.shape[-1])


def kernel(x, w_in, g_pre_mix, g_cq, g_ckv, w_uq, w_ukv, pool_w, pool_scale, g_sgu_v, sgu_w, sgu_b, conv_w, w_br_a, w_br_b, w_br_c, w_br_d, w_out, g_post_mix, g_pre_ffn, w_ffn_gate, w_ffn_up, w_ffn_down, g_post_ffn, loss_target, m_w_in, m_g_pre_mix, m_g_cq, m_g_ckv, m_w_uq, m_w_ukv, m_pool_w, m_pool_scale, m_g_sgu_v, m_sgu_w, m_sgu_b, m_conv_w, m_w_br_a, m_w_br_b, m_w_br_c, m_w_br_d, m_w_out, m_g_post_mix, m_g_pre_ffn, m_w_ffn_gate, m_w_ffn_up, m_w_ffn_down, m_g_post_ffn, v_w_in, v_g_pre_mix, v_g_cq, v_g_ckv, v_w_uq, v_w_ukv, v_pool_w, v_pool_scale, v_g_sgu_v, v_sgu_w, v_sgu_b, v_conv_w, v_w_br_a, v_w_br_b, v_w_br_c, v_w_br_d, v_w_out, v_g_post_mix, v_g_pre_ffn, v_w_ffn_gate, v_w_ffn_up, v_w_ffn_down, v_g_post_ffn):
    args = dict(locals())
    w = {n: args[n] for n in WEIGHTS}
    m = {n: args['m_' + n] for n in WEIGHTS}
    v = {n: args['v_' + n] for n in WEIGHTS}
    B, S, _ = x.shape
    T = B * S
    L = DEPTH
    names = list(SHARDED)
    shapes1 = {n: _as2d(w[n][0]).shape for n in names}
    where, group_rows = _group_rows(shapes1)

    def shard_bufs(l, groups):
        return _group({n: w[n][l:l + 1].astype(BF16) for b in groups for n in GROUPS[b]}, 0, groups)

    def joined(gathered):
        G = {}
        for b, buf in gathered.items():
            for n in GROUPS[b]:
                if n == 'w_in':
                    continue
                off = where[n][1]
                r, c = shapes1[n]
                G[n] = _shard_join(buf[:, off:off + r, :c].reshape((N_DEV, 1) + w[n].shape[1:]), SHARDED[n])
        return G

    eye = jnp.eye(4, dtype=F32)
    pool_bd = (pool_w[:, :, :, None, :] * eye[None, :, None, :, None]).reshape(L, BR, BR).astype(BF16)
    sgu_wm = (sgu_w * _sgu_mask()[None, None]).astype(BF16)
    sgu_bias = jnp.repeat(sgu_b.transpose(0, 2, 1), GROUP, axis=2)
    tabs = _tables(S) + (_place_k(),)

    def layer_params(l):
        P = {n: w[n][l][None, :] for n in ('g_pre_mix', 'g_cq', 'g_ckv', 'g_sgu_v', 'g_post_mix', 'g_pre_ffn',
                                            'g_post_ffn', 'pool_scale')}
        P.update(pool_bd=pool_bd[l], sgu_wm=sgu_wm[l], sgu_bias=sgu_bias[l])
        return P

    xt = x.reshape(T, D)
    h = _rw("pre_mix", lambda x, g: _rms(x, g), [xt], [w['g_pre_mix'][0][None, :]], [(D, BF16)], tile=512)[0]
    saved, Ws = [], []
    first = shard_bufs(0, (0, 4))
    early = dict(zip((0, 4), _all_gather("gather_weights", [first[0], first[4]])))
    for l in range(L):
        own = shard_bufs(l, (1, 2, 3))
        nxt = shard_bufs(l + 1, (0, 4)) if l + 1 < L else None
        ride = {'in_proj': _Gather([own[1]]), 'attn_fwd': _Gather([own[3], own[2]])}
        if nxt:
            ride.update(ffn_up=_Gather([nxt[0]]), ffn_down=_Gather([nxt[4]]))

        def late(ride=ride):
            G = joined({1: ride['in_proj'].results[0], 3: ride['attn_fwd'].results[0], 2: ride['attn_fwd'].results[1]})
            return dict(Wout=G['w_out'][0], Wg=G['w_ffn_gate'][0], Wu=G['w_ffn_up'][0], Wdn=G['w_ffn_down'][0])

        We = {k: a[0] for k, a in _layer_weights(joined(early)).items()}
        We['W1s'], We['W1g'] = _w1(early[0])
        g_next = w['g_pre_mix'][l + 1][None, :] if nxt else None
        xt, h, sv, W = _layer_fwd(xt, h, We, late, layer_params(l), tabs, B, S, ride, g_next)
        saved.append(sv)
        Ws.append(W)
        if nxt:
            early = {0: ride['ffn_up'].results[0], 4: ride['ffn_down'].results[0]}

    def loss_f(y, t, f, g):
        e = y - t
        dy = e * (1.0 / D)
        df, dg = _rms_bwd(f, g, dy)
        return dy, df, jnp.sum(e * e, axis=0, keepdims=True), dg
    dy, df, sq, dg_ffn = _rw("loss", loss_f, [xt, loss_target.reshape(T, D), saved[-1]['f']], [w['g_post_ffn'][L - 1][None, :]],
                             [(D, F32), (D, BF16)], [(1, D), (1, D)], tile=512)
    loss = lax.psum(0.5 * jnp.sum(sq) / D, ("x", "y", "c"))

    recv = [lax.empty((N_DEV, L * r, PADDED.get(ns[0], shapes1[ns[0]])[1]), BF16) for ns, r in zip(GROUPS, group_rows)]

    def send_bufs(pg, groups):
        return _group({n: _shard_split(pg[n], SHARDED[n]).astype(BF16) for b in groups for n in GROUPS[b]}, 1, groups)

    small = [None] * L
    for l in reversed(range(L)):
        def own_ride(stage, grads, l=l):
            if stage == 'ffn':
                bufs = send_bufs({'w_ffn_gate': grads['Wg'][None], 'w_ffn_up': grads['Wu'][None],
                                  'w_ffn_down': grads['Wdn'][None], 'w_out': grads['Wout'][None]}, (1, 2, 3))
                return {'merge_bwd': _Scatter([bufs[1]], [recv[1]], [l * group_rows[1]]),
                        'attn_bwd': _Scatter([bufs[3], bufs[2]], [recv[3], recv[2]], [l * group_rows[3], l * group_rows[2]])}
            if stage == 'mixers':
                bufs = send_bufs(_mixer_grads(grads), (4,))
                return {'in_proj_g_dw': _Scatter([bufs[4]], [recv[4]], [l * group_rows[4]])}
            return {'in_proj_dx': _Scatter([_w_in_grad_shards(grads['W1s'], grads['W1g'])], [recv[0]], [l * group_rows[0]])}

        prev = (saved[l - 1]['f'], w['g_post_ffn'][l - 1][None, :]) if l else None
        dy, df_prev, dg_prev, gl, ride = _layer_bwd(dy, df, saved[l], Ws[l], layer_params(l), tabs, B, S, own_ride, prev)
        gl['g_post_ffn'] = dg_ffn
        df, dg_ffn = df_prev, dg_prev
        for name, bs in (('in_proj_g_dw', (4,)), ('in_proj_dx', (0,)), ('merge_bwd', (1,)), ('attn_bwd', (3, 2))):
            for b, res_b in zip(bs, ride[name].results):
                recv[b] = res_b
        pg = _small_grads(gl)
        rep = jnp.concatenate([pg[n].reshape(-1, 128) for n in REPLICATED], axis=0)
        small[l] = _pad_to(rep, -(-rep.shape[0] // 8) * 8, 128)
    grad_x = dy.reshape(B, S, D)

    small_rows = small[0].shape[0]
    got_small = _all_gather("gather_small_grads", [jnp.concatenate(small, axis=0)])[0].reshape(N_DEV, L, small_rows, 128)

    res = {}
    for n in names:
        b, off = where[n]
        r, c = shapes1[n]
        if n in NARROW:
            parts = recv[b].reshape(N_DEV, L, group_rows[b], -1)[:, :, off:off + r, :c].reshape(N_DEV, L * r, c)
            res[n] = _adamw("adamw_" + n, parts, _as2d(w[n]), _as2d(m[n]), _as2d(v[n]))
        else:
            res[n] = _adamw("adamw_" + n, recv[b], _as2d(w[n]), _as2d(m[n]), _as2d(v[n]), row_off=off, layers=L,
                            stride=group_rows[b])
    off = 0
    for n in REPLICATED:
        r = int(np.prod(w[n].shape[1:])) // 128
        shp2 = _as2d(w[n]).shape
        res[n] = _adamw("adamw_" + n, got_small[:, :, off:off + r].reshape((N_DEV,) + shp2), _as2d(w[n]), _as2d(m[n]), _as2d(v[n]))
        off += r
    outs = [loss, grad_x]
    for k in range(4):
        outs += [res[n][k].reshape(w[n].shape) for n in WEIGHTS]
    return tuple(outs)
```

```python
import functools

import numpy as np
import jax
import jax.numpy as jnp
from jax import lax
from jax.experimental import pallas as pl
from jax.experimental.pallas import tpu as pltpu

F32, BF16 = jnp.float32, jnp.bfloat16
MESH = pl.DeviceIdType.MESH
N_DEV = 8

D = 1024
DEPTH = 4
CHUNK = 64
EPS = 1e-6
NEG_INF = -1e30
HEADS, QK_NOPE, QK_ROPE, V_HEAD = 8, 64, 32, 64
Q_LORA, KV_LORA = 256, 128
ROPE_THETA = 10000.0
POOL_WINDOWS = (2, 4, 8, 16)
GROUP = 64
BR = 256
SGU_BLOCK = 128
SGU_TILE = 4 * SGU_BLOCK
D_FF = 2816
IN_SIZES = (256, 128, 32, 256, 256, 256, 256, 256, 256, 4096)
HP = 128
QW = HEADS * HP
C_CQ, C_CKV, C_KR, C_PIN, C_SU, C_SV, C_CVB, C_CVC, C_CVX, C_GATE = 0, 256, 384, 512, 768, 1024, 1280, 1536, 1792, 2048

ADAM_LR, ADAM_B1, ADAM_B2, ADAM_EPS, ADAM_WD, ADAM_STEP = 0.001, 0.9, 0.999, 1e-08, 0.01, 10

WEIGHTS = ['w_in', 'g_pre_mix', 'g_cq', 'g_ckv', 'w_uq', 'w_ukv', 'pool_w', 'pool_scale', 'g_sgu_v', 'sgu_w', 'sgu_b',
           'conv_w', 'w_br_a', 'w_br_b', 'w_br_c', 'w_br_d', 'w_out', 'g_post_mix', 'g_pre_ffn', 'w_ffn_gate',
           'w_ffn_up', 'w_ffn_down', 'g_post_ffn']
SHARDED = {'w_in': 2, 'w_uq': 2, 'w_ukv': 2, 'conv_w': 3, 'w_br_a': 2, 'w_br_b': 2, 'w_br_c': 2, 'w_br_d': 2,
           'w_out': 1, 'w_ffn_gate': 2, 'w_ffn_up': 2, 'w_ffn_down': 1}
REPLICATED = [n for n in WEIGHTS if n not in SHARDED]

VMEM_LIMIT = 56 * 1024 * 1024
TQ = 256


def _cparams(sem=None):
    return pltpu.CompilerParams(vmem_limit_bytes=VMEM_LIMIT, dimension_semantics=sem)


def _tile(n, cap, align=128):
    if n <= cap:
        return n
    for t in range(cap - cap % align, 0, -align):
        if n % t == 0:
            return t
    return n


def _peers():
    x_, y_, c_ = lax.axis_index("x"), lax.axis_index("y"), lax.axis_index("c")
    out = []
    for k in range(1, N_DEV):
        px, py, pc = (x_ + (k >> 2)) % 2, (y_ + ((k >> 1) & 1)) % 2, (c_ + (k & 1)) % 2
        out.append((k, (px, py, pc), 4 * px + 2 * py + pc))
    return 4 * x_ + 2 * y_ + c_, out


class _Exchange:
    def __init__(self, n):
        self.n = n
        self.scratch = [pltpu.SemaphoreType.DMA((7 * n,)), pltpu.SemaphoreType.DMA((7 * n,)),
                        pltpu.SemaphoreType.DMA((n,))]
        self.results = None

    def start(self, cin, cout, sems):
        local, sends, _ = self.copies(cin, cout, sems)
        for cp in local + sends:
            cp.start()

    def wait(self, cin, cout, sems):
        local, sends, recvs = self.copies(cin, cout, sems)
        for cp in recvs:
            cp.wait_recv()
        for cp in sends:
            cp.wait_send()
        for cp in local:
            cp.wait()


class _Gather(_Exchange):
    def __init__(self, xs):
        super().__init__(len(xs))
        self.inputs = list(xs)
        self.out_shape = [jax.ShapeDtypeStruct((N_DEV,) + x.shape, x.dtype) for x in xs]
        self.aliases = {}

    def copies(self, cin, cout, sems):
        send_sems, recv_sems, local_sems = sems
        me, peers = _peers()
        local, sends, recvs = [], [], []
        for a in range(self.n):
            local.append(pltpu.make_async_copy(cin[a], cout[a].at[me], local_sems.at[a]))
            for k, dev, flat in peers:
                sem = dict(send_sem=send_sems.at[7 * a + k - 1], recv_sem=recv_sems.at[7 * a + k - 1],
                           device_id=dev, device_id_type=MESH)
                sends.append(pltpu.make_async_remote_copy(src_ref=cin[a], dst_ref=cout[a].at[me], **sem))
                recvs.append(pltpu.make_async_remote_copy(src_ref=cin[a], dst_ref=cout[a].at[flat], **sem))
        return local, sends, recvs


class _Scatter(_Exchange):
    def __init__(self, xs, recv, row0, ks=range(N_DEV)):
        super().__init__(len(xs))
        self.inputs = list(xs) + list(recv)
        self.out_shape = [jax.ShapeDtypeStruct(r.shape, r.dtype) for r in recv]
        self.aliases = {self.n + a: a for a in range(self.n)}
        self.row0 = list(row0)
        self.ks = tuple(ks)

    def copies(self, cin, cout, sems):
        send_sems, recv_sems, local_sems = sems
        me, peers = _peers()
        local, sends, recvs = [], [], []
        for a in range(self.n):
            rows = pl.ds(self.row0[a], self.inputs[a].shape[1])
            if 0 in self.ks:
                local.append(pltpu.make_async_copy(cin[a].at[me], cout[a].at[me, rows], local_sems.at[a]))
            for k, dev, flat in peers:
                if k not in self.ks:
                    continue
                sem = dict(send_sem=send_sems.at[7 * a + k - 1], recv_sem=recv_sems.at[7 * a + k - 1],
                           device_id=dev, device_id_type=MESH)
                sends.append(pltpu.make_async_remote_copy(src_ref=cin[a].at[flat], dst_ref=cout[a].at[me, rows], **sem))
                recvs.append(pltpu.make_async_remote_copy(src_ref=cin[a].at[me], dst_ref=cout[a].at[flat, rows], **sem))
        return local, sends, recvs


def _call(name, body, grid, in_specs, out_specs, out_shape, arrays, scratch=(), sem=None, comm=None):
    if comm is None:
        res = pl.pallas_call(body, name=name, grid=grid, in_specs=list(in_specs), out_specs=list(out_specs),
                             out_shape=list(out_shape), scratch_shapes=list(scratch), compiler_params=_cparams(sem))(*arrays)
        return list(res)
    ni, no, ns = len(in_specs), len(out_specs), len(scratch)
    nci, nco = len(comm.inputs), len(comm.out_shape)
    hbm = pl.BlockSpec(memory_space=pl.ANY)

    def wrapped(*refs):
        ins, cin = refs[:ni], refs[ni:ni + nci]
        o0 = ni + nci
        outs, cout = refs[o0:o0 + no], refs[o0 + no:o0 + no + nco]
        s0 = o0 + no + nco
        scr, sems = refs[s0:s0 + ns], refs[s0 + ns:]
        ids = [pl.program_id(d) for d in range(len(grid))]
        first = functools.reduce(jnp.logical_and, [i == 0 for i in ids])
        last = functools.reduce(jnp.logical_and, [i == g - 1 for i, g in zip(ids, grid)])

        @pl.when(first)
        def _():
            comm.start(cin, cout, sems)

        body(*ins, *outs, *scr)

        @pl.when(last)
        def _():
            comm.wait(cin, cout, sems)

    res = pl.pallas_call(
        wrapped, name=name, grid=grid, in_specs=list(in_specs) + [hbm] * nci, out_specs=list(out_specs) + [hbm] * nco,
        out_shape=list(out_shape) + list(comm.out_shape), scratch_shapes=list(scratch) + comm.scratch,
        input_output_aliases={ni + i: no + j for i, j in comm.aliases.items()},
        compiler_params=_cparams(("arbitrary",) * len(grid)))(*arrays, *comm.inputs)
    comm.results = list(res[no:])
    return list(res[:no])


def win(arr, width, idx):
    return ('win', arr, width, idx)


def tab(arr):
    return ('tab', arr)


def _rw(name, fn, ins, consts, outs, accs=(), tile=256, comm=None):
    first = ins[0][1] if isinstance(ins[0], tuple) else ins[0]
    T = first.shape[0]
    tile = min(tile, T)
    grid = (T // tile,)
    arrays, in_specs = [], []
    for x in ins:
        if isinstance(x, tuple) and x[0] == 'win':
            _, a, w, k = x
            arrays.append(a)
            in_specs.append(pl.BlockSpec((tile, w), lambda i, k=k: (i, k)))
        elif isinstance(x, tuple) and x[0] == 'tab':
            a = x[1]
            nb = a.shape[0] // tile
            arrays.append(a)
            in_specs.append(pl.BlockSpec((tile, a.shape[1]), lambda i, nb=nb: (i % nb, 0)))
        else:
            arrays.append(x)
            in_specs.append(pl.BlockSpec((tile, x.shape[1]), lambda i: (i, 0)))
    for c in consts:
        arrays.append(c)
        in_specs.append(pl.BlockSpec(c.shape, lambda i, n=c.ndim: (0,) * n))
    out_shape = [jax.ShapeDtypeStruct((T, f), dt) for f, dt in outs]
    out_specs = [pl.BlockSpec((tile, f), lambda i: (i, 0)) for f, _ in outs]
    for shp in accs:
        out_shape.append(jax.ShapeDtypeStruct(shp, F32))
        out_specs.append(pl.BlockSpec(shp, lambda i, n=len(shp): (0,) * n))
    ni, nc, no = len(ins), len(consts), len(outs)

    def body(*refs):
        vals_in = [r[...] for r in refs[:ni]]
        vals_in = [v.astype(F32) if v.dtype == BF16 else v for v in vals_in]
        vals = fn(*vals_in, *[r[...] for r in refs[ni:ni + nc]])
        if not isinstance(vals, (tuple, list)):
            vals = (vals,)
        o_refs = refs[ni + nc:]
        for r, v in zip(o_refs[:no], vals[:no]):
            r[...] = v.astype(r.dtype)
        i = pl.program_id(0)
        for r, v in zip(o_refs[no:], vals[no:]):
            @pl.when(i == 0)
            def _(r=r, v=v):
                r[...] = v

            @pl.when(i > 0)
            def _(r=r, v=v):
                r[...] += v

    return _call(name, body, grid, in_specs, out_specs, out_shape, arrays, sem=("arbitrary",), comm=comm)


def _mm(name, a, b, tb=False, out_dtype=F32, tm=512, tn=3072, a2=None, b2=None, comm=None):
    M, K = a.shape
    N = b.shape[0] if tb else b.shape[1]
    tm, tn = _tile(M, tm, 8), _tile(N, tn)
    dims = (((1,), (1,)), ((), ())) if tb else (((1,), (0,)), ((), ()))
    pairs = [(a, b)] + ([(a2, b2)] if a2 is not None else [])

    def body(*refs):
        o_ref = refs[-1]
        acc = None
        for i in range(len(pairs)):
            p = lax.dot_general(refs[2 * i][...].astype(BF16), refs[2 * i + 1][...].astype(BF16), dims,
                                preferred_element_type=F32)
            acc = p if acc is None else acc + p
        o_ref[...] = acc.astype(o_ref.dtype)

    in_specs, arrays = [], []
    for x, y in pairs:
        k = x.shape[1]
        in_specs.append(pl.BlockSpec((tm, k), lambda j, i: (i, 0)))
        in_specs.append(pl.BlockSpec((tn, k), lambda j, i: (j, 0)) if tb else pl.BlockSpec((k, tn), lambda j, i: (0, j)))
        arrays += [x, y]
    return _call(name, body, (N // tn, M // tm), in_specs, [pl.BlockSpec((tm, tn), lambda j, i: (i, j))],
                 [jax.ShapeDtypeStruct((M, N), out_dtype)], arrays, sem=("parallel", "parallel"), comm=comm)[0]


def _mm_t(name, a, g, tk=1408, tn=1408, tt=2048, comm=None):
    T, K = a.shape
    N = g.shape[1]
    tk, tn, tt = _tile(K, tk), _tile(N, tn), _tile(T, tt, 8)
    nt = T // tt

    def body(a_ref, g_ref, o_ref, acc_ref):
        p = lax.dot_general(a_ref[...].astype(BF16), g_ref[...].astype(BF16), (((0,), (0,)), ((), ())),
                            preferred_element_type=F32)
        t = pl.program_id(2)

        @pl.when(t == 0)
        def _():
            acc_ref[...] = p

        @pl.when(t > 0)
        def _():
            acc_ref[...] += p

        @pl.when(t == nt - 1)
        def _():
            o_ref[...] = acc_ref[...].astype(BF16)

    return _call(name, body, (K // tk, N // tn, nt),
                 [pl.BlockSpec((tt, tk), lambda i, j, t: (t, i)), pl.BlockSpec((tt, tn), lambda i, j, t: (t, j))],
                 [pl.BlockSpec((tk, tn), lambda i, j, t: (i, j))], [jax.ShapeDtypeStruct((K, N), BF16)], [a, g],
                 scratch=[pltpu.VMEM((tk, tn), F32)], sem=("parallel", "parallel", "arbitrary"), comm=comm)[0]


def _dot(a, b):
    return jnp.dot(a.astype(BF16), b.astype(BF16), preferred_element_type=F32)


def _dot_nt(a, b):
    return lax.dot_general(a.astype(BF16), b.astype(BF16), (((1,), (1,)), ((), ())), preferred_element_type=F32)


def _dot_tn(a, b):
    return lax.dot_general(a.astype(BF16), b.astype(BF16), (((0,), (0,)), ((), ())), preferred_element_type=F32)


def _sigmoid(x):
    return 0.5 * jnp.tanh(0.5 * x) + 0.5


def _ffn_up(h, wg, wu, comm=None):
    M, K = h.shape
    N = wg.shape[1]
    tm, tn = _tile(M, 512, 8), _tile(N, 1408)

    def body(h_ref, wg_ref, wu_ref, ga_ref, gb_ref, act_ref):
        hb = h_ref[...]
        a = jnp.dot(hb, wg_ref[...], preferred_element_type=F32)
        b = jnp.dot(hb, wu_ref[...], preferred_element_type=F32)
        s = _sigmoid(a)
        silu = a * s
        ga_ref[...] = (b * (s + silu * (1.0 - s))).astype(BF16)
        gb_ref[...] = silu.astype(BF16)
        act_ref[...] = (silu * b).astype(BF16)

    wspec = pl.BlockSpec((K, tn), lambda j, i: (0, j))
    ospec = pl.BlockSpec((tm, tn), lambda j, i: (i, j))
    return _call("ffn_up", body, (N // tn, M // tm), [pl.BlockSpec((tm, K), lambda j, i: (i, 0)), wspec, wspec],
                 [ospec] * 3, [jax.ShapeDtypeStruct((M, N), BF16)] * 3, [h, wg, wu], sem=("parallel", "parallel"),
                 comm=comm)


def _ffn_down_dx(df, wdn, ga, gb):
    M, K = df.shape
    N = wdn.shape[0]
    tm, tn = _tile(M, 512, 8), _tile(N, 1408)

    def body(df_ref, w_ref, ga_ref, gb_ref, da_ref, db_ref):
        d = _dot_nt(df_ref[...], w_ref[...])
        da_ref[...] = (d * ga_ref[...].astype(F32)).astype(BF16)
        db_ref[...] = (d * gb_ref[...].astype(F32)).astype(BF16)

    tspec = pl.BlockSpec((tm, tn), lambda j, i: (i, j))
    return _call("ffn_down_dx", body, (N // tn, M // tm),
                 [pl.BlockSpec((tm, K), lambda j, i: (i, 0)), pl.BlockSpec((tn, K), lambda j, i: (j, 0)), tspec, tspec],
                 [tspec] * 2, [jax.ShapeDtypeStruct((M, N), BF16)] * 2, [df, wdn, ga, gb], sem=("parallel", "parallel"))


def _mm_norm(name, a, b, x, g, g_next=None, comm=None):
    M, K = a.shape
    N = b.shape[1]
    tm = _tile(M, 512, 8)
    more = g_next is not None

    def body(*refs):
        a_ref, b_ref, x_ref, g_ref = refs[:4]
        y_ref, xn_ref = refs[4 + more], refs[5 + more]
        y = jnp.dot(a_ref[...].astype(BF16), b_ref[...], preferred_element_type=F32)
        y_ref[...] = y.astype(BF16)
        xn = x_ref[...] + _rms(y, g_ref[...])
        xn_ref[...] = xn
        if more:
            refs[7][...] = _rms(xn, refs[4][...]).astype(BF16)

    row = lambda n: pl.BlockSpec((tm, n), lambda i: (i, 0))
    whole = lambda z: pl.BlockSpec(z.shape, lambda i: (0, 0))
    arrays = [a, b, x, g] + ([g_next] if more else [])
    in_specs = [row(K), whole(b), row(N), whole(g)] + ([whole(g_next)] if more else [])
    out_shape = [jax.ShapeDtypeStruct((M, N), dt) for dt in (BF16, F32) + ((BF16,) if more else ())]
    return _call(name, body, (M // tm,), in_specs, [row(N)] * (2 + more), out_shape, arrays, sem=("parallel",), comm=comm)


def _rms(x, g):
    r = lax.rsqrt(jnp.mean(x * x, axis=-1, keepdims=True) + EPS)
    return x * r * g


def _rms_bwd(x, g, dy):
    r = lax.rsqrt(jnp.mean(x * x, axis=-1, keepdims=True) + EPS)
    xh = x * r
    dxh = dy * g
    dx = r * (dxh - xh * jnp.mean(dxh * xh, axis=-1, keepdims=True))
    return dx, jnp.sum(dy * xh, axis=0, keepdims=True)


def _shift_down(x, k):
    t = lax.broadcasted_iota(jnp.int32, x.shape, 0)
    return jnp.where(t >= k, pltpu.roll(x, k, 0), 0.0)


def _shift_up(x, k):
    n = x.shape[0]
    t = lax.broadcasted_iota(jnp.int32, x.shape, 0)
    return jnp.where(t < n - k, pltpu.roll(x, n - k, 0), 0.0)


def _group_select(vals):
    lane = lax.broadcasted_iota(jnp.int32, vals[0].shape, 1) // GROUP
    out = vals[-1]
    for g in range(len(vals) - 2, -1, -1):
        out = jnp.where(lane == g, vals[g], out)
    return out


def _pool_count(shape):
    t = lax.broadcasted_iota(jnp.int32, shape, 0)
    lane = lax.broadcasted_iota(jnp.int32, shape, 1) // GROUP
    w = jnp.where(lane == 0, POOL_WINDOWS[0], jnp.where(lane == 1, POOL_WINDOWS[1],
                  jnp.where(lane == 2, POOL_WINDOWS[2], POOL_WINDOWS[3])))
    return jnp.minimum(t + 1, w).astype(F32)


def _pooled(z):
    s = [z]
    for k in (1, 2, 4, 8):
        s.append(s[-1] + _shift_down(s[-1], k))
    return _group_select(s[1:]) / _pool_count(z.shape) - z


def _pooled_bwd(dp):
    u = dp / _pool_count(dp.shape)
    s = [u]
    for k in (1, 2, 4, 8):
        s.append(s[-1] + _shift_up(s[-1], k))
    return _group_select(s[1:]) - dp


def _diag_mask():
    r = lax.broadcasted_iota(jnp.int32, (TQ, TQ), 0) // CHUNK
    c = lax.broadcasted_iota(jnp.int32, (TQ, TQ), 1) // CHUNK
    return r >= c


def _attn_fwd(q, k, v, B, S, comm=None):
    nq = S // TQ

    def body(q_ref, k_ref, v_ref, o_ref, lse_ref):
        mask = _diag_mask()
        for i in range(nq):
            r0, r1 = i * TQ, (i + 1) * TQ
            qb = q_ref[r0:r1, :]
            sd = jnp.where(mask, _dot_nt(qb, k_ref[r0:r1, :]), NEG_INF)
            m = jnp.max(sd, axis=-1, keepdims=True)
            if i:
                so = _dot_nt(qb, k_ref[0:r0, :])
                m = jnp.maximum(m, jnp.max(so, axis=-1, keepdims=True))
            pd = jnp.exp(sd - m)
            l = jnp.sum(pd, axis=-1, keepdims=True)
            acc = _dot(pd, v_ref[r0:r1, :])
            if i:
                po = jnp.exp(so - m)
                l = l + jnp.sum(po, axis=-1, keepdims=True)
                acc = acc + _dot(po, v_ref[0:r0, :])
            o_ref[r0:r1, :] = (acc / l).astype(BF16)
            lse_ref[r0:r1, :] = jnp.broadcast_to(m + jnp.log(l), (TQ, HP))

    spec = pl.BlockSpec((S, HP), lambda b, h: (b, h))
    return _call("attn_fwd", body, (B, HEADS), [spec] * 3, [spec] * 2,
                 [jax.ShapeDtypeStruct((B * S, QW), BF16), jax.ShapeDtypeStruct((B * S, QW), F32)],
                 [q, k, v], sem=("parallel", "parallel"), comm=comm)


def _attn_bwd(q, k, v, o, do, lse, B, S, comm=None):
    nq = S // TQ

    def body(q_ref, k_ref, v_ref, o_ref, do_ref, lse_ref, dq_ref, dk_ref, dv_ref, dk_acc, dv_acc):
        mask = _diag_mask()
        dk_acc[...] = jnp.zeros_like(dk_acc)
        dv_acc[...] = jnp.zeros_like(dv_acc)
        for i in range(nq):
            r0, r1 = i * TQ, (i + 1) * TQ
            qb, dob = q_ref[r0:r1, :], do_ref[r0:r1, :]
            delta = jnp.sum(o_ref[r0:r1, :].astype(F32) * dob.astype(F32), axis=-1, keepdims=True)
            lse_b = lse_ref[r0:r1, :][:, :1]
            dq = None
            for c0, c1, diag in ([(0, r0, False)] if i else []) + [(r0, r1, True)]:
                kb, vb = k_ref[c0:c1, :], v_ref[c0:c1, :]
                p = jnp.exp(_dot_nt(qb, kb) - lse_b)
                if diag:
                    p = jnp.where(mask, p, 0.0)
                ds = p * (_dot_nt(dob, vb) - delta)
                part = _dot(ds, kb)
                dq = part if dq is None else dq + part
                dk_acc[c0:c1, :] += _dot_tn(ds, qb)
                dv_acc[c0:c1, :] += _dot_tn(p, dob)
            dq_ref[r0:r1, :] = dq
        dk_ref[...] = dk_acc[...].astype(BF16)
        dv_ref[...] = dv_acc[...].astype(BF16)

    spec = pl.BlockSpec((S, HP), lambda b, h: (b, h))
    return _call("attn_bwd", body, (B, HEADS), [spec] * 6, [spec] * 3,
                 [jax.ShapeDtypeStruct((B * S, QW), F32)] + [jax.ShapeDtypeStruct((B * S, QW), BF16)] * 2,
                 [q, k, v, o, do, lse], scratch=[pltpu.VMEM((S, HP), F32)] * 2, sem=("parallel", "parallel"), comm=comm)


def _all_gather(name, xs):
    n = len(xs)

    def body(*refs):
        x_refs, out_refs = refs[:n], refs[n:2 * n]
        send_sems, recv_sems, local_sems = refs[2 * n:]
        x_, y_, c_ = lax.axis_index("x"), lax.axis_index("y"), lax.axis_index("c")
        me, sibling = (x_, y_, c_), (x_, y_, 1 - c_)
        chips = [(1 - x_, y_), (x_, 1 - y_), (1 - x_, 1 - y_)]

        def copy(a, k, block, to, own=False):
            slot = out_refs[a].at[4 * block[0] + 2 * block[1] + block[2]]
            return pltpu.make_async_remote_copy(
                src_ref=x_refs[a] if own else slot, dst_ref=slot,
                send_sem=send_sems.at[7 * a + k], recv_sem=recv_sems.at[7 * a + k], device_id=to, device_id_type=MESH)

        mine = [pltpu.make_async_copy(x_refs[a], out_refs[a].at[4 * x_ + 2 * y_ + c_], local_sems.at[a]) for a in range(n)]
        for cp in mine:
            cp.start()
        first = [copy(a, 1 + j, me, (*chip, c_), own=True) for j, chip in enumerate(chips) for a in range(n)]
        first += [copy(a, 0, me, sibling, own=True) for a in range(n)]
        for cp in first:
            cp.start()
        passed = []
        for j, chip in enumerate(chips):
            for a in range(n):
                copy(a, 1 + j, (*chip, c_), me).wait_recv()
                passed.append(copy(a, 4 + j, (*chip, c_), sibling))
                passed[-1].start()
        for a in range(n):
            copy(a, 0, sibling, me).wait_recv()
            for j, chip in enumerate(chips):
                copy(a, 4 + j, (*chip, 1 - c_), me).wait_recv()
        for cp in first + passed:
            cp.wait_send()
        for cp in mine:
            cp.wait()

    return pl.pallas_call(
        body, name=name, out_shape=[jax.ShapeDtypeStruct((N_DEV,) + x.shape, x.dtype) for x in xs],
        in_specs=[pl.BlockSpec(memory_space=pl.ANY)] * n, out_specs=[pl.BlockSpec(memory_space=pl.ANY)] * n,
        scratch_shapes=[pltpu.SemaphoreType.DMA((7 * n,)), pltpu.SemaphoreType.DMA((7 * n,)), pltpu.SemaphoreType.DMA((n,))],
    )(*xs)


def _adamw(name, parts, w, m, v, row_off=0, layers=1, stride=0):
    R, C = w.shape
    r = R // layers
    assert parts.shape[2] == C and parts.shape[1] >= (layers - 1) * stride + row_off + r
    if r % 8 == 0:
        tr = max(t for t in range(8, 513, 8) if r % t == 0 and row_off % t == 0 and stride % t == 0)
    else:
        assert layers == 1
        tr = r
    nb = r // tr

    def body(p_ref, w_ref, m_ref, v_ref, g_ref, d_ref, nm_ref, nv_ref):
        g = p_ref[0].astype(F32)
        for j in range(1, N_DEV):
            g = g + p_ref[j].astype(F32)
        m_new = ADAM_B1 * m_ref[...] + (1.0 - ADAM_B1) * g
        v_new = ADAM_B2 * v_ref[...] + (1.0 - ADAM_B2) * (g * g)
        m_hat = m_new / (1.0 - ADAM_B1 ** ADAM_STEP)
        v_hat = v_new / (1.0 - ADAM_B2 ** ADAM_STEP)
        g_ref[...] = g
        d_ref[...] = -ADAM_LR * (m_hat / (jnp.sqrt(v_hat) + ADAM_EPS) + ADAM_WD * w_ref[...])
        nm_ref[...] = m_new
        nv_ref[...] = v_new

    spec = pl.BlockSpec((tr, C), lambda l, i: (l * nb + i, 0))
    pspec = pl.BlockSpec((N_DEV, tr, C), lambda l, i: (0, (l * stride + row_off) // tr + i, 0))
    return _call(name, body, (layers, nb), [pspec, spec, spec, spec], [spec] * 4,
                 [jax.ShapeDtypeStruct((R, C), F32)] * 4, [parts, w, m, v], sem=("parallel", "parallel"))


def _rot_cols(w):
    h = w.shape[-1] // 2
    return jnp.concatenate([-w[..., h:], w[..., :h]], axis=-1)


def _unrot_cols(d):
    h = d.shape[-1] // 2
    return jnp.concatenate([d[..., h:], -d[..., :h]], axis=-1)


IN_WIDTH = sum(IN_SIZES)
IN_SHARD = IN_WIDTH // N_DEV
IN_GATE = IN_WIDTH - IN_SIZES[-1]


def _w_in_cols(buf, a, b):
    out = []
    for j in range(a // IN_SHARD, (b - 1) // IN_SHARD + 1):
        lo, hi = max(a, IN_SHARD * j) - IN_SHARD * j, min(b, IN_SHARD * (j + 1)) - IN_SHARD * j
        out.append(buf[j, :, lo:hi])
    return out


def _w1(buf):
    kr0 = IN_SIZES[0] + IN_SIZES[1]
    kr = jnp.concatenate(_w_in_cols(buf, kr0, kr0 + QK_ROPE), axis=-1)
    w1s = jnp.concatenate(_w_in_cols(buf, 0, kr0) + [kr, _rot_cols(kr), jnp.zeros((D, 64), BF16)]
                          + _w_in_cols(buf, kr0 + QK_ROPE, IN_GATE), axis=-1)
    return w1s, jnp.concatenate(_w_in_cols(buf, IN_GATE, IN_WIDTH), axis=-1)


def _w_in_grad_shards(d1s, d1g):
    kr0 = IN_SIZES[0] + IN_SIZES[1]
    seg = d1s[:, C_KR:C_KR + 128]
    dkr = seg[:, :QK_ROPE] + _unrot_cols(seg[:, QK_ROPE:2 * QK_ROPE])
    runs = [(0, d1s, 0), (kr0, dkr, 0), (kr0 + QK_ROPE, d1s, C_PIN), (IN_GATE, d1g, 0), (IN_WIDTH, None, 0)]
    blocks = []
    for j in range(N_DEV):
        a, b = IN_SHARD * j, IN_SHARD * (j + 1)
        pieces = []
        for (c0, src, s0), (c1, _, _) in zip(runs[:-1], runs[1:]):
            lo, hi = max(a, c0), min(b, c1)
            if lo < hi:
                pieces.append(src[:, s0 + lo - c0:s0 + hi - c0])
        blocks.append(jnp.concatenate(pieces, axis=-1))
    return jnp.stack(blocks)


def _layer_weights(G):
    L = G['w_uq'].shape[0]
    wq = G['w_uq'].reshape(L, Q_LORA, HEADS, QK_NOPE + QK_ROPE)
    nope, rope = wq[..., :QK_NOPE], wq[..., QK_NOPE:]
    z32 = jnp.zeros((L, Q_LORA, HEADS, HP - QK_NOPE - QK_ROPE), BF16)
    wq_a = jnp.concatenate([nope, rope, z32], axis=-1).reshape(L, Q_LORA, QW)
    wq_b = jnp.concatenate([jnp.zeros_like(nope), _rot_cols(rope), z32], axis=-1).reshape(L, Q_LORA, QW)
    wkv = G['w_ukv'].reshape(L, KV_LORA, HEADS, QK_NOPE + V_HEAD)
    z64 = jnp.zeros((L, KV_LORA, HEADS, HP - QK_NOPE), BF16)
    wkn = jnp.concatenate([wkv[..., :QK_NOPE], z64], axis=-1).reshape(L, KV_LORA, QW)
    wv = jnp.concatenate([wkv[..., QK_NOPE:], z64], axis=-1).reshape(L, KV_LORA, QW)
    wa = G['w_br_a'].reshape(L, HEADS, V_HEAD, D)
    wa = jnp.concatenate([wa, jnp.zeros((L, HEADS, HP - V_HEAD, D), BF16)], axis=2).reshape(L, QW, D)
    return dict(
        Wq=jnp.concatenate([wq_a, wq_b], axis=-1),
        Wkv=jnp.concatenate([wkn, wv], axis=-1), Wa=wa, Wb=G['w_br_b'], Wc=G['w_br_c'], Wd=G['w_br_d'],
        conv=G['conv_w'].reshape(L, 3, BR).astype(F32))


def _tables(S):
    inv = ROPE_THETA ** (-jnp.arange(0, QK_ROPE, 2, dtype=F32) / QK_ROPE)
    ang = jnp.arange(S, dtype=F32)[:, None] * inv[None, :]
    cos, sin = jnp.cos(ang), jnp.sin(ang)
    scale = (QK_NOPE + QK_ROPE) ** -0.5
    one, zero = jnp.ones((S, QK_NOPE), F32), jnp.zeros((S, QK_NOPE), F32)
    z32 = jnp.zeros((S, HP - QK_NOPE - QK_ROPE), F32)
    cq = jnp.concatenate([one, cos, cos, z32], axis=1) * scale
    sq = jnp.concatenate([zero, sin, sin, z32], axis=1) * scale
    ck = jnp.concatenate([cos, cos, sin, sin, jnp.zeros((S, HP - 2 * QK_ROPE), F32)], axis=1)
    return cq, sq, ck


def _place_k():
    p = np.zeros((HP, QW), np.float32)
    for r in range(2 * QK_ROPE):
        for h in range(HEADS):
            p[r, h * HP + QK_NOPE + r % QK_ROPE] = 1.0
    return jnp.asarray(p, BF16)


def _layer_fwd(x, h, W, late, P, tabs, B, S, ride, g_next):
    cq_t, sq_t, ck_t, place = tabs
    g_cq, g_ckv = P['g_cq'], P['g_ckv']
    P1 = _mm("in_proj_s", h, W['W1s'], out_dtype=BF16)
    PG = _mm("in_proj_g", h, W['W1g'], out_dtype=BF16, comm=ride.get('in_proj'))

    def q_prep(c, cq, sq, g, w):
        qq = _dot(_rms(c, g), w)
        return qq[:, :QW] * jnp.tile(cq, (1, HEADS)) + qq[:, QW:] * jnp.tile(sq, (1, HEADS))
    q = _rw("q_prep", q_prep, [win(P1, 256, 0), tab(cq_t), tab(sq_t)], [g_cq, W['Wq']], [(QW, BF16)])[0]

    def kv_prep(c, seg, ck, g, w, place):
        kv = _dot(_rms(c, g), w)
        return kv[:, :QW] + _dot(seg * ck, place), kv[:, QW:]
    k, v = _rw("kv_prep", kv_prep, [win(P1, 128, 2), win(P1, 128, 3), tab(ck_t)], [g_ckv, W['Wkv'], place],
               [(QW, BF16), (QW, BF16)])
    o, lse = _attn_fwd(q, k, v, B, S, comm=ride.get('attn_fwd'))

    def pool_f(z, wbd, scale):
        return _dot(_pooled(z), wbd) * scale
    bo = _rw("pool_fwd", pool_f, [win(P1, BR, 2)], [P['pool_bd'], P['pool_scale']], [(BR, BF16)], tile=S)[0]

    def sgu_f(u, sv, g, wm, bias):
        vn = _rms(sv, g)
        blocks = [vn[r:r + SGU_BLOCK] for r in range(0, vn.shape[0], SGU_BLOCK)]
        return u * jnp.concatenate([_sgu_mix(vb, wm, bias) for vb in blocks], axis=0)
    co = _rw("sgu_fwd", sgu_f, [win(P1, BR, 3), win(P1, BR, 4)], [P['g_sgu_v'], P['sgu_wm'], P['sgu_bias']],
             [(BR, BF16)], tile=SGU_TILE)[0]

    def conv_f(b, c, xi, w):
        z = c * xi
        return b * (w[0:1] * _shift_down(z, 2) + w[1:2] * _shift_down(z, 1) + w[2:3] * z)
    do_ = _rw("conv_fwd", conv_f, [win(P1, BR, 5), win(P1, BR, 6), win(P1, BR, 7)], [W['conv']], [(BR, BF16)], tile=S)[0]

    def merge_f(a, b, c, d, l0, l1, l2, l3, wa, wb, wc, wd):
        return (_sigmoid(l0) * _dot(a, wa) + _sigmoid(l1) * _dot(b, wb) + _sigmoid(l2) * _dot(c, wc)
                + _sigmoid(l3) * _dot(d, wd))
    merged = _rw("merge_fwd", merge_f, [o, bo, co, do_] + [win(PG, D, i) for i in range(4)],
                 [W['Wa'], W['Wb'], W['Wc'], W['Wd']], [(D, BF16)])[0]
    W = dict(W, **late())
    om, x1, h2 = _mm_norm("out_proj", merged, W['Wout'], x, P['g_post_mix'], P['g_pre_ffn'])
    ga, gb, act = _ffn_up(h2, W['Wg'], W['Wu'], comm=ride.get('ffn_up'))
    f, x2, *h_next = _mm_norm("ffn_down", act, W['Wdn'], x1, P['g_post_ffn'], g_next, comm=ride.get('ffn_down'))
    saved = dict(x=x, h=h, P1=P1, PG=PG, q=q, k=k, v=v, o=o, lse=lse, bo=bo, co=co, do=do_, merged=merged, om=om, x1=x1,
                 h2=h2, ga=ga, gb=gb, act=act, f=f)
    return x2, (h_next[0] if h_next else None), saved, W


def _sgu_mix(vn, wm, bias):
    outs = [_dot(wm[g], vn) for g in range(4)]
    return _group_select(outs) + bias


def _layer_bwd(dx2, df, sv, W, P, tabs, B, S, own_ride, prev):
    cq_t, sq_t, ck_t, place = tabs
    P1 = sv['P1']
    grads = {}
    da, db = _ffn_down_dx(df, W['Wdn'], sv['ga'], sv['gb'])
    grads['Wdn'] = _mm_t("ffn_down_dw", sv['act'], df)
    dh2 = _mm("ffn_up_dx", da, W['Wg'], tb=True, a2=db, b2=W['Wu'], out_dtype=BF16)
    grads['Wg'] = _mm_t("ffn_gate_dw", sv['h2'], da)
    grads['Wu'] = _mm_t("ffn_up_dw", sv['h2'], db)

    def mid_norms_b(x1, dh, dy, om, g_ffn, g_mix):
        dx, dg_ffn = _rms_bwd(x1, g_ffn, dh)
        dx1 = dy + dx
        dom, dg_mix = _rms_bwd(om, g_mix, dx1)
        return dx1, dom, dg_ffn, dg_mix
    dx1, dom, grads['g_pre_ffn'], grads['g_post_mix'] = _rw(
        "mid_norms_bwd", mid_norms_b, [sv['x1'], dh2, dx2, sv['om']], [P['g_pre_ffn'], P['g_post_mix']],
        [(D, F32), (D, BF16)], [(1, D), (1, D)], tile=512)
    dmerged = _mm("out_proj_dx", dom, W['Wout'], tb=True, out_dtype=BF16)
    grads['Wout'] = _mm_t("out_proj_dw", sv['merged'], dom)
    ride = own_ride('ffn', grads)

    def merge_b(a, b, c, d, l0, l1, l2, l3, dm, wa, wb, wc, wd):
        outs_dl, outs_dy, outs_dbr = [], [], []
        for br, l, w in ((a, l0, wa), (b, l1, wb), (c, l2, wc), (d, l3, wd)):
            s = _sigmoid(l)
            y = _dot(br, w)
            dy = (dm * s).astype(BF16)
            outs_dl.append((dm * y * s * (1.0 - s)).astype(BF16))
            outs_dy.append(dy)
            outs_dbr.append(_dot_nt(dy, w))
        return (jnp.concatenate(outs_dl, axis=1), *outs_dy, *outs_dbr)
    mb = _rw("merge_bwd", merge_b,
             [sv['o'], sv['bo'], sv['co'], sv['do']] + [win(sv['PG'], D, i) for i in range(4)] + [dmerged],
             [W['Wa'], W['Wb'], W['Wc'], W['Wd']],
             [(4 * D, BF16)] + [(D, BF16)] * 4 + [(QW, BF16), (BR, F32), (BR, F32), (BR, F32)], comm=ride.get('merge_bwd'))
    dl, dy, (d_o, d_bo, d_co, d_do) = mb[0], mb[1:5], mb[5:9]
    grads['Wa'] = _mm_t("br_a_dw", sv['o'], dy[0])
    grads['Wb'] = _mm_t("br_b_dw", sv['bo'], dy[1])
    grads['Wc'] = _mm_t("br_c_dw", sv['co'], dy[2])
    grads['Wd'] = _mm_t("br_d_dw", sv['do'], dy[3])

    def conv_b(b, c, xi, dout, w):
        z = c * xi
        z1, z2 = _shift_down(z, 1), _shift_down(z, 2)
        y = w[0:1] * z2 + w[1:2] * z1 + w[2:3] * z
        dyv = dout * b
        dz = w[2:3] * dyv + w[1:2] * _shift_up(dyv, 1) + w[0:1] * _shift_up(dyv, 2)
        return (dout * y, dz * xi, dz * c, jnp.sum(dyv * z2, axis=0, keepdims=True),
                jnp.sum(dyv * z1, axis=0, keepdims=True), jnp.sum(dyv * z, axis=0, keepdims=True))
    dcvb, dcvc, dcvx, dw0, dw1, dw2 = _rw("conv_bwd", conv_b, [win(P1, BR, 5), win(P1, BR, 6), win(P1, BR, 7), d_do],
                                          [W['conv']], [(BR, BF16)] * 3, [(1, BR)] * 3, tile=S)
    grads['conv'] = jnp.concatenate([dw0, dw1, dw2], axis=0)

    def sgu_b(u, s_v, dout, g, wm, bias):
        vn = _rms(s_v, g)
        dmix = dout * u
        lane = lax.broadcasted_iota(jnp.int32, (SGU_BLOCK, BR), 1) // GROUP
        mixed, dvn, dwm, dbias = [], [], [0.0] * 4, 0.0
        for r in range(0, vn.shape[0], SGU_BLOCK):
            vb, db = vn[r:r + SGU_BLOCK], dmix[r:r + SGU_BLOCK]
            mixed.append(_sgu_mix(vb, wm, bias))
            dvn.append(_group_select([_dot_tn(wm[gi], db) for gi in range(4)]))
            dwm = [dwm[gi] + _dot_nt(jnp.where(lane == gi, db, 0.0), vb) for gi in range(4)]
            dbias = dbias + db
        dsv, dg = _rms_bwd(s_v, g, jnp.concatenate(dvn, axis=0))
        return (dout * jnp.concatenate(mixed, axis=0), dsv, dg, *dwm, dbias)
    dsu, dsv_, grads['g_sgu_v'], wm0, wm1, wm2, wm3, grads['sgu_bias'] = _rw(
        "sgu_bwd", sgu_b, [win(P1, BR, 3), win(P1, BR, 4), d_co], [P['g_sgu_v'], P['sgu_wm'], P['sgu_bias']],
        [(BR, BF16)] * 2, [(1, BR)] + [(SGU_BLOCK, SGU_BLOCK)] * 4 + [(SGU_BLOCK, BR)], tile=SGU_TILE)
    grads['sgu_wm'] = jnp.stack([wm0, wm1, wm2, wm3])

    def pool_b(z, dout, wbd, scale):
        pooled = _pooled(z)
        mixed = _dot(pooled, wbd)
        dmix = dout * scale
        dz = _pooled_bwd(_dot_nt(dmix, wbd))
        return dz, _dot_tn(pooled, dmix), jnp.sum(dout * mixed, axis=0, keepdims=True)
    dpin, grads['pool_bd'], grads['pool_scale'] = _rw(
        "pool_bwd", pool_b, [win(P1, BR, 2), d_bo], [P['pool_bd'], P['pool_scale']], [(BR, BF16)], [(BR, BR), (1, BR)], tile=S)

    dq, dk, dv = _attn_bwd(sv['q'], sv['k'], sv['v'], sv['o'], d_o, sv['lse'], B, S, comm=ride.get('attn_bwd'))

    def q_prep_b(c, dq, cq, sq, g, w):
        dqq = jnp.concatenate([dq * jnp.tile(cq, (1, HEADS)), dq * jnp.tile(sq, (1, HEADS))], axis=1).astype(BF16)
        dc, dg = _rms_bwd(c, g, _dot_nt(dqq, w))
        return dc, _rms(c, g), dqq, dg
    dcq, cqn, dqq, grads['g_cq'] = _rw("q_prep_bwd", q_prep_b, [win(P1, 256, 0), dq, tab(cq_t), tab(sq_t)],
                                       [P['g_cq'], W['Wq']], [(Q_LORA, BF16), (Q_LORA, BF16), (2 * QW, BF16)], [(1, Q_LORA)])
    grads['Wq'] = _mm_t("uq_dw", cqn, dqq)

    def kv_prep_b(c, dk, dv, ck, g, w, place):
        dc, dg = _rms_bwd(c, g, _dot_nt(dk, w[:, :QW]) + _dot_nt(dv, w[:, QW:]))
        return dc, _dot_nt(dk, place) * ck, _rms(c, g), dg
    dckv, dseg, kvn, grads['g_ckv'] = _rw(
        "kv_prep_bwd", kv_prep_b, [win(P1, 128, 2), dk, dv, tab(ck_t)], [P['g_ckv'], W['Wkv'], place],
        [(KV_LORA, BF16), (HP, BF16), (KV_LORA, BF16)], [(1, KV_LORA)])
    grads['Wkn'] = _mm_t("uk_dw", kvn, dk)
    grads['Wv'] = _mm_t("uv_dw", kvn, dv)
    ride = dict(ride, **own_ride('mixers', grads))

    dps = jnp.concatenate([dcq, dckv, dseg, dpin, dsu, dsv_, dcvb, dcvc, dcvx], axis=1)
    grads['W1s'] = _mm_t("in_proj_s_dw", sv['h'], dps)
    grads['W1g'] = _mm_t("in_proj_g_dw", sv['h'], dl, comm=ride.get('in_proj_g_dw'))
    ride = dict(ride, **own_ride('w_in', grads))
    dh = _mm("in_proj_dx", dps, W['W1s'], tb=True, a2=dl, b2=W['W1g'], out_dtype=BF16, tm=256, comm=ride.get('in_proj_dx'))
    ride = dict(ride, **own_ride('w_in_rest', grads))

    if prev is None:
        def pre_mix_b(x, dh, dy, g):
            dx, dg = _rms_bwd(x, g, dh)
            return dy + dx, dg
        dx, grads['g_pre_mix'] = _rw("pre_mix_bwd", pre_mix_b, [sv['x'], dh, dx1], [P['g_pre_mix']], [(D, F32)], [(1, D)],
                                     tile=512, comm=ride.get('edge'))
        return dx, None, None, grads, ride

    def edge_norms_b(x, dh, dy, f, g_mix, g_ffn):
        dx, dg_mix = _rms_bwd(x, g_mix, dh)
        dx = dy + dx
        df, dg_ffn = _rms_bwd(f, g_ffn, dx)
        return dx, df, dg_mix, dg_ffn
    dx, df_prev, grads['g_pre_mix'], dg_prev = _rw(
        "edge_norms_bwd", edge_norms_b, [sv['x'], dh, dx1, prev[0]], [P['g_pre_mix'], prev[1]],
        [(D, F32), (D, BF16)], [(1, D), (1, D)], tile=512, comm=ride.get('edge'))
    return dx, df_prev, dg_prev, grads, ride


def _mixer_grads(g):
    out = {}
    qa = g['Wq'][:, :QW].reshape(1, Q_LORA, HEADS, HP)
    qb = g['Wq'][:, QW:].reshape(1, Q_LORA, HEADS, HP)
    drope = qa[..., QK_NOPE:QK_NOPE + QK_ROPE] + _unrot_cols(qb[..., QK_NOPE:QK_NOPE + QK_ROPE])
    out['w_uq'] = jnp.concatenate([qa[..., :QK_NOPE], drope], axis=-1).reshape(1, Q_LORA, HEADS * (QK_NOPE + QK_ROPE))
    kn = g['Wkn'].reshape(1, KV_LORA, HEADS, HP)[..., :QK_NOPE]
    vv = g['Wv'].reshape(1, KV_LORA, HEADS, HP)[..., :V_HEAD]
    out['w_ukv'] = jnp.concatenate([kn, vv], axis=-1).reshape(1, KV_LORA, HEADS * (QK_NOPE + V_HEAD))
    out['w_br_a'] = g['Wa'].reshape(1, HEADS, HP, D)[:, :, :V_HEAD].reshape(1, HEADS * V_HEAD, D)
    out['w_br_b'], out['w_br_c'], out['w_br_d'] = g['Wb'][None], g['Wc'][None], g['Wd'][None]
    out['conv_w'] = g['conv'].reshape(1, 3, 1, BR)
    return out


def _small_grads(g):
    out = {}
    for n in ('g_pre_mix', 'g_cq', 'g_ckv', 'g_sgu_v', 'g_post_mix', 'g_pre_ffn', 'g_post_ffn', 'pool_scale'):
        out[n] = g[n].reshape(1, -1)
    bd = g['pool_bd'].reshape(4, GROUP, 4, GROUP)
    out['pool_w'] = jnp.stack([bd[i, :, i, :] for i in range(4)])[None]
    out['sgu_w'] = (g['sgu_wm'] * _sgu_mask()[None])[None]
    out['sgu_b'] = g['sgu_bias'].reshape(SGU_BLOCK, 4, GROUP).sum(-1).T[None]
    return out


def _sgu_mask():
    pc = np.arange(SGU_BLOCK) // CHUNK
    return jnp.asarray(pc[:, None] >= pc[None, :], F32)


def _rows(a, lead):
    return a.reshape(a.shape[:lead] + (-1, a.shape[-1]))


def _pad_to(a, rows, cols):
    pad = [(0, 0)] * (a.ndim - 2) + [(0, rows - a.shape[-2]), (0, cols - a.shape[-1])]
    return jnp.pad(a, pad)


GROUPS = (('w_in',), ('w_ffn_gate',), ('w_ffn_up',), ('w_out', 'w_ffn_down'),
          ('w_uq', 'w_ukv', 'w_br_a', 'w_br_b', 'w_br_c', 'w_br_d', 'conv_w'))
PADDED = {'w_uq': (Q_LORA, 128), 'conv_w': (128, 128), 'w_ffn_down': (384, D)}
NARROW = ('w_uq', 'conv_w')


def _group(t, lead, groups):
    bufs = {}
    for b in groups:
        ps = [_rows(t[n], lead) for n in GROUPS[b]]
        ps = [_pad_to(p, *PADDED[n]) if n in PADDED else p for n, p in zip(GROUPS[b], ps)]
        bufs[b] = ps[0] if len(ps) == 1 else jnp.concatenate(ps, axis=-2)
    return bufs


def _group_rows(shapes):
    where, rows = {}, []
    for b, names in enumerate(GROUPS):
        off = 0
        for n in names:
            where[n] = (b, off)
            off += PADDED.get(n, shapes[n])[0]
        rows.append(off)
    return where, rows


def _shard_split(a, axis):
    n = a.shape[axis]
    a = a.reshape(a.shape[:axis] + (N_DEV, n // N_DEV) + a.shape[axis + 1:])
    return jnp.moveaxis(a, axis, 0)


def _shard_join(a, axis):
    a = jnp.moveaxis(a, 0, axis)
    return a.reshape(a.shape[:axis] + (a.shape[axis] * a.shape[axis + 1],) + a.shape[axis + 2:])


def _as2d(a):
    return a.reshape(-1, a.shape[-1])


def kernel(x, w_in, g_pre_mix, g_cq, g_ckv, w_uq, w_ukv, pool_w, pool_scale, g_sgu_v, sgu_w, sgu_b, conv_w, w_br_a, w_br_b, w_br_c, w_br_d, w_out, g_post_mix, g_pre_ffn, w_ffn_gate, w_ffn_up, w_ffn_down, g_post_ffn, loss_target, m_w_in, m_g_pre_mix, m_g_cq, m_g_ckv, m_w_uq, m_w_ukv, m_pool_w, m_pool_scale, m_g_sgu_v, m_sgu_w, m_sgu_b, m_conv_w, m_w_br_a, m_w_br_b, m_w_br_c, m_w_br_d, m_w_out, m_g_post_mix, m_g_pre_ffn, m_w_ffn_gate, m_w_ffn_up, m_w_ffn_down, m_g_post_ffn, v_w_in, v_g_pre_mix, v_g_cq, v_g_ckv, v_w_uq, v_w_ukv, v_pool_w, v_pool_scale, v_g_sgu_v, v_sgu_w, v_sgu_b, v_conv_w, v_w_br_a, v_w_br_b, v_w_br_c, v_w_br_d, v_w_out, v_g_post_mix, v_g_pre_ffn, v_w_ffn_gate, v_w_ffn_up, v_w_ffn_down, v_g_post_ffn):
    args = dict(locals())
    w = {n: args[n] for n in WEIGHTS}
    m = {n: args['m_' + n] for n in WEIGHTS}
    v = {n: args['v_' + n] for n in WEIGHTS}
    B, S, _ = x.shape
    T = B * S
    L = DEPTH
    names = list(SHARDED)
    shapes1 = {n: _as2d(w[n][0]).shape for n in names}
    where, group_rows = _group_rows(shapes1)

    def shard_bufs(l, groups):
        return _group({n: w[n][l:l + 1].astype(BF16) for b in groups for n in GROUPS[b]}, 0, groups)

    def joined(gathered):
        G = {}
        for b, buf in gathered.items():
            for n in GROUPS[b]:
                if n == 'w_in':
                    continue
                off = where[n][1]
                r, c = shapes1[n]
                G[n] = _shard_join(buf[:, off:off + r, :c].reshape((N_DEV, 1) + w[n].shape[1:]), SHARDED[n])
        return G

    eye = jnp.eye(4, dtype=F32)
    pool_bd = (pool_w[:, :, :, None, :] * eye[None, :, None, :, None]).reshape(L, BR, BR).astype(BF16)
    sgu_wm = (sgu_w * _sgu_mask()[None, None]).astype(BF16)
    sgu_bias = jnp.repeat(sgu_b.transpose(0, 2, 1), GROUP, axis=2)
    tabs = _tables(S) + (_place_k(),)

    def layer_params(l):
        P = {n: w[n][l][None, :] for n in ('g_pre_mix', 'g_cq', 'g_ckv', 'g_sgu_v', 'g_post_mix', 'g_pre_ffn',
                                            'g_post_ffn', 'pool_scale')}
        P.update(pool_bd=pool_bd[l], sgu_wm=sgu_wm[l], sgu_bias=sgu_bias[l])
        return P

    xt = x.reshape(T, D)
    h = _rw("pre_mix", lambda x, g: _rms(x, g), [xt], [w['g_pre_mix'][0][None, :]], [(D, BF16)], tile=512)[0]
    saved, Ws = [], []
    first = shard_bufs(0, (0, 4))
    early = dict(zip((0, 4), _all_gather("gather_weights", [first[0], first[4]])))
    for l in range(L):
        own = shard_bufs(l, (1, 2, 3))
        nxt = shard_bufs(l + 1, (0, 4)) if l + 1 < L else None
        ride = {'in_proj': _Gather([own[1]]), 'attn_fwd': _Gather([own[3], own[2]])}
        if nxt:
            ride.update(ffn_up=_Gather([nxt[0]]), ffn_down=_Gather([nxt[4]]))

        def late(ride=ride):
            G = joined({1: ride['in_proj'].results[0], 3: ride['attn_fwd'].results[0], 2: ride['attn_fwd'].results[1]})
            return dict(Wout=G['w_out'][0], Wg=G['w_ffn_gate'][0], Wu=G['w_ffn_up'][0], Wdn=G['w_ffn_down'][0])

        We = {k: a[0] for k, a in _layer_weights(joined(early)).items()}
        We['W1s'], We['W1g'] = _w1(early[0])
        g_next = w['g_pre_mix'][l + 1][None, :] if nxt else None
        xt, h, sv, W = _layer_fwd(xt, h, We, late, layer_params(l), tabs, B, S, ride, g_next)
        saved.append(sv)
        Ws.append(W)
        if nxt:
            early = {0: ride['ffn_up'].results[0], 4: ride['ffn_down'].results[0]}

    def loss_f(y, t, f, g):
        e = y - t
        dy = e * (1.0 / D)
        df, dg = _rms_bwd(f, g, dy)
        return dy, df, jnp.sum(e * e, axis=0, keepdims=True), dg
    dy, df, sq, dg_ffn = _rw("loss", loss_f, [xt, loss_target.reshape(T, D), saved[-1]['f']], [w['g_post_ffn'][L - 1][None, :]],
                             [(D, F32), (D, BF16)], [(1, D), (1, D)], tile=512)
    loss = lax.psum(0.5 * jnp.sum(sq) / D, ("x", "y", "c"))

    recv = [lax.empty((N_DEV, L * r, PADDED.get(ns[0], shapes1[ns[0]])[1]), BF16) for ns, r in zip(GROUPS, group_rows)]

    def send_bufs(pg, groups):
        return _group({n: _shard_split(pg[n], SHARDED[n]).astype(BF16) for b in groups for n in GROUPS[b]}, 1, groups)

    small, w_in_first = [None] * L, []
    for l in reversed(range(L)):
        def own_ride(stage, grads, l=l):
            if stage == 'ffn':
                bufs = send_bufs({'w_ffn_gate': grads['Wg'][None], 'w_ffn_up': grads['Wu'][None],
                                  'w_ffn_down': grads['Wdn'][None], 'w_out': grads['Wout'][None]}, (1, 2, 3))
                return {'merge_bwd': _Scatter([bufs[1]], [recv[1]], [l * group_rows[1]]),
                        'attn_bwd': _Scatter([bufs[3], bufs[2]], [recv[3], recv[2]], [l * group_rows[3], l * group_rows[2]])}
            if stage == 'mixers':
                bufs = send_bufs(_mixer_grads(grads), (4,))
                return {'in_proj_g_dw': _Scatter([bufs[4]], [recv[4]], [l * group_rows[4]])}
            if stage == 'w_in':
                shards = _w_in_grad_shards(grads['W1s'], grads['W1g'])
                w_in_first.append(_Scatter([shards], [recv[0]], [l * group_rows[0]], ks=(0, 1, 2, 4, 6)))
                return {'in_proj_dx': w_in_first[-1]}
            first = w_in_first[-1]
            return {'edge': _Scatter(first.inputs[:1], first.results, [l * group_rows[0]], ks=(3, 5, 7))}

        prev = (saved[l - 1]['f'], w['g_post_ffn'][l - 1][None, :]) if l else None
        dy, df_prev, dg_prev, gl, ride = _layer_bwd(dy, df, saved[l], Ws[l], layer_params(l), tabs, B, S, own_ride, prev)
        gl['g_post_ffn'] = dg_ffn
        df, dg_ffn = df_prev, dg_prev
        for name, bs in (('in_proj_g_dw', (4,)), ('edge', (0,)), ('merge_bwd', (1,)), ('attn_bwd', (3, 2))):
            for b, res_b in zip(bs, ride[name].results):
                recv[b] = res_b
        pg = _small_grads(gl)
        rep = jnp.concatenate([pg[n].reshape(-1, 128) for n in REPLICATED], axis=0)
        small[l] = _pad_to(rep, -(-rep.shape[0] // 8) * 8, 128)
    grad_x = dy.reshape(B, S, D)

    small_rows = small[0].shape[0]
    got_small = _all_gather("gather_small_grads", [jnp.concatenate(small, axis=0)])[0].reshape(N_DEV, L, small_rows, 128)

    res = {}
    for n in names:
        b, off = where[n]
        r, c = shapes1[n]
        if n in NARROW:
            parts = recv[b].reshape(N_DEV, L, group_rows[b], -1)[:, :, off:off + r, :c].reshape(N_DEV, L * r, c)
            res[n] = _adamw("adamw_" + n, parts, _as2d(w[n]), _as2d(m[n]), _as2d(v[n]))
        else:
            res[n] = _adamw("adamw_" + n, recv[b], _as2d(w[n]), _as2d(m[n]), _as2d(v[n]), row_off=off, layers=L,
                            stride=group_rows[b])
    off = 0
    for n in REPLICATED:
        r = int(np.prod(w[n].shape[1:])) // 128
        shp2 = _as2d(w[n]).shape
        res[n] = _adamw("adamw_" + n, got_small[:, :, off:off + r].reshape((N_DEV,) + shp2), _as2d(w[n]), _as2d(m[n]), _as2d(v[n]))
        off += r
    outs = [loss, grad_x]
    for k in range(4):
        outs += [res[n][k].reshape(w[n].shape) for n in WEIGHTS]
    return tuple(outs)
```

```python
import functools

import numpy as np
import jax
import jax.numpy as jnp
from jax import lax
from jax.experimental import pallas as pl
from jax.experimental.pallas import tpu as pltpu

F32, BF16 = jnp.float32, jnp.bfloat16
MESH = pl.DeviceIdType.MESH
N_DEV = 8

D = 1024
DEPTH = 4
CHUNK = 64
EPS = 1e-6
NEG_INF = -1e30
HEADS, QK_NOPE, QK_ROPE, V_HEAD = 8, 64, 32, 64
Q_LORA, KV_LORA = 256, 128
ROPE_THETA = 10000.0
POOL_WINDOWS = (2, 4, 8, 16)
GROUP = 64
BR = 256
SGU_BLOCK = 128
SGU_TILE = 4 * SGU_BLOCK
D_FF = 2816
IN_SIZES = (256, 128, 32, 256, 256, 256, 256, 256, 256, 4096)
HP = 128
QW = HEADS * HP
C_CQ, C_CKV, C_KR, C_PIN, C_SU, C_SV, C_CVB, C_CVC, C_CVX, C_GATE = 0, 256, 384, 512, 768, 1024, 1280, 1536, 1792, 2048

ADAM_LR, ADAM_B1, ADAM_B2, ADAM_EPS, ADAM_WD, ADAM_STEP = 0.001, 0.9, 0.999, 1e-08, 0.01, 10

WEIGHTS = ['w_in', 'g_pre_mix', 'g_cq', 'g_ckv', 'w_uq', 'w_ukv', 'pool_w', 'pool_scale', 'g_sgu_v', 'sgu_w', 'sgu_b',
           'conv_w', 'w_br_a', 'w_br_b', 'w_br_c', 'w_br_d', 'w_out', 'g_post_mix', 'g_pre_ffn', 'w_ffn_gate',
           'w_ffn_up', 'w_ffn_down', 'g_post_ffn']
SHARDED = {'w_in': 2, 'w_uq': 2, 'w_ukv': 2, 'conv_w': 3, 'w_br_a': 2, 'w_br_b': 2, 'w_br_c': 2, 'w_br_d': 2,
           'w_out': 1, 'w_ffn_gate': 2, 'w_ffn_up': 2, 'w_ffn_down': 1}
REPLICATED = [n for n in WEIGHTS if n not in SHARDED]

VMEM_LIMIT = 56 * 1024 * 1024
TQ = 256


def _cparams(sem=None):
    return pltpu.CompilerParams(vmem_limit_bytes=VMEM_LIMIT, dimension_semantics=sem)


def _tile(n, cap, align=128):
    if n <= cap:
        return n
    for t in range(cap - cap % align, 0, -align):
        if n % t == 0:
            return t
    return n


def _peers():
    x_, y_, c_ = lax.axis_index("x"), lax.axis_index("y"), lax.axis_index("c")
    out = []
    for k in range(1, N_DEV):
        px, py, pc = (x_ + (k >> 2)) % 2, (y_ + ((k >> 1) & 1)) % 2, (c_ + (k & 1)) % 2
        out.append((k, (px, py, pc), 4 * px + 2 * py + pc))
    return 4 * x_ + 2 * y_ + c_, out


class _Exchange:
    def __init__(self, n):
        self.n = n
        self.scratch = [pltpu.SemaphoreType.DMA((7 * n,)), pltpu.SemaphoreType.DMA((7 * n,)),
                        pltpu.SemaphoreType.DMA((n,))]
        self.results = None

    def start(self, cin, cout, sems):
        local, sends, _ = self.copies(cin, cout, sems)
        for cp in local + sends:
            cp.start()

    def wait(self, cin, cout, sems):
        local, sends, recvs = self.copies(cin, cout, sems)
        for cp in recvs:
            cp.wait_recv()
        for cp in sends:
            cp.wait_send()
        for cp in local:
            cp.wait()


class _Gather(_Exchange):
    def __init__(self, xs, ks=range(N_DEV), out=()):
        super().__init__(len(xs))
        self.inputs = list(xs) + list(out)
        self.out_shape = [jax.ShapeDtypeStruct((N_DEV,) + x.shape, x.dtype) for x in xs]
        self.aliases = {self.n + a: a for a in range(len(out))}
        self.ks = tuple(ks)

    def copies(self, cin, cout, sems):
        send_sems, recv_sems, local_sems = sems
        me, peers = _peers()
        local, sends, recvs = [], [], []
        for a in range(self.n):
            if 0 in self.ks:
                local.append(pltpu.make_async_copy(cin[a], cout[a].at[me], local_sems.at[a]))
            for k, dev, flat in peers:
                if k not in self.ks:
                    continue
                sem = dict(send_sem=send_sems.at[7 * a + k - 1], recv_sem=recv_sems.at[7 * a + k - 1],
                           device_id=dev, device_id_type=MESH)
                sends.append(pltpu.make_async_remote_copy(src_ref=cin[a], dst_ref=cout[a].at[me], **sem))
                recvs.append(pltpu.make_async_remote_copy(src_ref=cin[a], dst_ref=cout[a].at[flat], **sem))
        return local, sends, recvs


class _Scatter(_Exchange):
    def __init__(self, xs, recv, row0, ks=range(N_DEV)):
        super().__init__(len(xs))
        self.inputs = list(xs) + list(recv)
        self.out_shape = [jax.ShapeDtypeStruct(r.shape, r.dtype) for r in recv]
        self.aliases = {self.n + a: a for a in range(self.n)}
        self.row0 = list(row0)
        self.ks = tuple(ks)

    def copies(self, cin, cout, sems):
        send_sems, recv_sems, local_sems = sems
        me, peers = _peers()
        local, sends, recvs = [], [], []
        for a in range(self.n):
            rows = pl.ds(self.row0[a], self.inputs[a].shape[1])
            if 0 in self.ks:
                local.append(pltpu.make_async_copy(cin[a].at[me], cout[a].at[me, rows], local_sems.at[a]))
            for k, dev, flat in peers:
                if k not in self.ks:
                    continue
                sem = dict(send_sem=send_sems.at[7 * a + k - 1], recv_sem=recv_sems.at[7 * a + k - 1],
                           device_id=dev, device_id_type=MESH)
                sends.append(pltpu.make_async_remote_copy(src_ref=cin[a].at[flat], dst_ref=cout[a].at[me, rows], **sem))
                recvs.append(pltpu.make_async_remote_copy(src_ref=cin[a].at[me], dst_ref=cout[a].at[flat, rows], **sem))
        return local, sends, recvs


def _call(name, body, grid, in_specs, out_specs, out_shape, arrays, scratch=(), sem=None, comm=None):
    if comm is None:
        res = pl.pallas_call(body, name=name, grid=grid, in_specs=list(in_specs), out_specs=list(out_specs),
                             out_shape=list(out_shape), scratch_shapes=list(scratch), compiler_params=_cparams(sem))(*arrays)
        return list(res)
    ni, no, ns = len(in_specs), len(out_specs), len(scratch)
    nci, nco = len(comm.inputs), len(comm.out_shape)
    hbm = pl.BlockSpec(memory_space=pl.ANY)

    def wrapped(*refs):
        ins, cin = refs[:ni], refs[ni:ni + nci]
        o0 = ni + nci
        outs, cout = refs[o0:o0 + no], refs[o0 + no:o0 + no + nco]
        s0 = o0 + no + nco
        scr, sems = refs[s0:s0 + ns], refs[s0 + ns:]
        ids = [pl.program_id(d) for d in range(len(grid))]
        first = functools.reduce(jnp.logical_and, [i == 0 for i in ids])
        last = functools.reduce(jnp.logical_and, [i == g - 1 for i, g in zip(ids, grid)])

        @pl.when(first)
        def _():
            comm.start(cin, cout, sems)

        body(*ins, *outs, *scr)

        @pl.when(last)
        def _():
            comm.wait(cin, cout, sems)

    res = pl.pallas_call(
        wrapped, name=name, grid=grid, in_specs=list(in_specs) + [hbm] * nci, out_specs=list(out_specs) + [hbm] * nco,
        out_shape=list(out_shape) + list(comm.out_shape), scratch_shapes=list(scratch) + comm.scratch,
        input_output_aliases={ni + i: no + j for i, j in comm.aliases.items()},
        compiler_params=_cparams(("arbitrary",) * len(grid)))(*arrays, *comm.inputs)
    comm.results = list(res[no:])
    return list(res[:no])


def win(arr, width, idx):
    return ('win', arr, width, idx)


def tab(arr):
    return ('tab', arr)


def _rw(name, fn, ins, consts, outs, accs=(), tile=256, comm=None):
    first = ins[0][1] if isinstance(ins[0], tuple) else ins[0]
    T = first.shape[0]
    tile = min(tile, T)
    grid = (T // tile,)
    arrays, in_specs = [], []
    for x in ins:
        if isinstance(x, tuple) and x[0] == 'win':
            _, a, w, k = x
            arrays.append(a)
            in_specs.append(pl.BlockSpec((tile, w), lambda i, k=k: (i, k)))
        elif isinstance(x, tuple) and x[0] == 'tab':
            a = x[1]
            nb = a.shape[0] // tile
            arrays.append(a)
            in_specs.append(pl.BlockSpec((tile, a.shape[1]), lambda i, nb=nb: (i % nb, 0)))
        else:
            arrays.append(x)
            in_specs.append(pl.BlockSpec((tile, x.shape[1]), lambda i: (i, 0)))
    for c in consts:
        arrays.append(c)
        in_specs.append(pl.BlockSpec(c.shape, lambda i, n=c.ndim: (0,) * n))
    out_shape = [jax.ShapeDtypeStruct((T, f), dt) for f, dt in outs]
    out_specs = [pl.BlockSpec((tile, f), lambda i: (i, 0)) for f, _ in outs]
    for shp in accs:
        out_shape.append(jax.ShapeDtypeStruct(shp, F32))
        out_specs.append(pl.BlockSpec(shp, lambda i, n=len(shp): (0,) * n))
    ni, nc, no = len(ins), len(consts), len(outs)

    def body(*refs):
        vals_in = [r[...] for r in refs[:ni]]
        vals_in = [v.astype(F32) if v.dtype == BF16 else v for v in vals_in]
        vals = fn(*vals_in, *[r[...] for r in refs[ni:ni + nc]])
        if not isinstance(vals, (tuple, list)):
            vals = (vals,)
        o_refs = refs[ni + nc:]
        for r, v in zip(o_refs[:no], vals[:no]):
            r[...] = v.astype(r.dtype)
        i = pl.program_id(0)
        for r, v in zip(o_refs[no:], vals[no:]):
            @pl.when(i == 0)
            def _(r=r, v=v):
                r[...] = v

            @pl.when(i > 0)
            def _(r=r, v=v):
                r[...] += v

    return _call(name, body, grid, in_specs, out_specs, out_shape, arrays, sem=("arbitrary",), comm=comm)


def _mm(name, a, b, tb=False, out_dtype=F32, tm=512, tn=3072, a2=None, b2=None, comm=None):
    M, K = a.shape
    N = b.shape[0] if tb else b.shape[1]
    tm, tn = _tile(M, tm, 8), _tile(N, tn)
    dims = (((1,), (1,)), ((), ())) if tb else (((1,), (0,)), ((), ()))
    pairs = [(a, b)] + ([(a2, b2)] if a2 is not None else [])

    def body(*refs):
        o_ref = refs[-1]
        acc = None
        for i in range(len(pairs)):
            p = lax.dot_general(refs[2 * i][...].astype(BF16), refs[2 * i + 1][...].astype(BF16), dims,
                                preferred_element_type=F32)
            acc = p if acc is None else acc + p
        o_ref[...] = acc.astype(o_ref.dtype)

    in_specs, arrays = [], []
    for x, y in pairs:
        k = x.shape[1]
        in_specs.append(pl.BlockSpec((tm, k), lambda j, i: (i, 0)))
        in_specs.append(pl.BlockSpec((tn, k), lambda j, i: (j, 0)) if tb else pl.BlockSpec((k, tn), lambda j, i: (0, j)))
        arrays += [x, y]
    return _call(name, body, (N // tn, M // tm), in_specs, [pl.BlockSpec((tm, tn), lambda j, i: (i, j))],
                 [jax.ShapeDtypeStruct((M, N), out_dtype)], arrays, sem=("parallel", "parallel"), comm=comm)[0]


def _mm_t(name, a, g, tk=1408, tn=1408, tt=2048, comm=None):
    T, K = a.shape
    N = g.shape[1]
    tk, tn, tt = _tile(K, tk), _tile(N, tn), _tile(T, tt, 8)
    nt = T // tt

    def body(a_ref, g_ref, o_ref, acc_ref):
        p = lax.dot_general(a_ref[...].astype(BF16), g_ref[...].astype(BF16), (((0,), (0,)), ((), ())),
                            preferred_element_type=F32)
        t = pl.program_id(2)

        @pl.when(t == 0)
        def _():
            acc_ref[...] = p

        @pl.when(t > 0)
        def _():
            acc_ref[...] += p

        @pl.when(t == nt - 1)
        def _():
            o_ref[...] = acc_ref[...].astype(BF16)

    return _call(name, body, (K // tk, N // tn, nt),
                 [pl.BlockSpec((tt, tk), lambda i, j, t: (t, i)), pl.BlockSpec((tt, tn), lambda i, j, t: (t, j))],
                 [pl.BlockSpec((tk, tn), lambda i, j, t: (i, j))], [jax.ShapeDtypeStruct((K, N), BF16)], [a, g],
                 scratch=[pltpu.VMEM((tk, tn), F32)], sem=("parallel", "parallel", "arbitrary"), comm=comm)[0]


def _dot(a, b):
    return jnp.dot(a.astype(BF16), b.astype(BF16), preferred_element_type=F32)


def _dot_nt(a, b):
    return lax.dot_general(a.astype(BF16), b.astype(BF16), (((1,), (1,)), ((), ())), preferred_element_type=F32)


def _dot_tn(a, b):
    return lax.dot_general(a.astype(BF16), b.astype(BF16), (((0,), (0,)), ((), ())), preferred_element_type=F32)


def _sigmoid(x):
    return 0.5 * jnp.tanh(0.5 * x) + 0.5


def _ffn_up(h, wg, wu, comm=None):
    M, K = h.shape
    N = wg.shape[1]
    tm, tn = _tile(M, 512, 8), _tile(N, 1408)

    def body(h_ref, wg_ref, wu_ref, ga_ref, gb_ref, act_ref):
        hb = h_ref[...]
        a = jnp.dot(hb, wg_ref[...], preferred_element_type=F32)
        b = jnp.dot(hb, wu_ref[...], preferred_element_type=F32)
        s = _sigmoid(a)
        silu = a * s
        ga_ref[...] = (b * (s + silu * (1.0 - s))).astype(BF16)
        gb_ref[...] = silu.astype(BF16)
        act_ref[...] = (silu * b).astype(BF16)

    wspec = pl.BlockSpec((K, tn), lambda j, i: (0, j))
    ospec = pl.BlockSpec((tm, tn), lambda j, i: (i, j))
    return _call("ffn_up", body, (N // tn, M // tm), [pl.BlockSpec((tm, K), lambda j, i: (i, 0)), wspec, wspec],
                 [ospec] * 3, [jax.ShapeDtypeStruct((M, N), BF16)] * 3, [h, wg, wu], sem=("parallel", "parallel"),
                 comm=comm)


def _ffn_down_dx(df, wdn, ga, gb):
    M, K = df.shape
    N = wdn.shape[0]
    tm, tn = _tile(M, 512, 8), _tile(N, 1408)

    def body(df_ref, w_ref, ga_ref, gb_ref, da_ref, db_ref):
        d = _dot_nt(df_ref[...], w_ref[...])
        da_ref[...] = (d * ga_ref[...].astype(F32)).astype(BF16)
        db_ref[...] = (d * gb_ref[...].astype(F32)).astype(BF16)

    tspec = pl.BlockSpec((tm, tn), lambda j, i: (i, j))
    return _call("ffn_down_dx", body, (N // tn, M // tm),
                 [pl.BlockSpec((tm, K), lambda j, i: (i, 0)), pl.BlockSpec((tn, K), lambda j, i: (j, 0)), tspec, tspec],
                 [tspec] * 2, [jax.ShapeDtypeStruct((M, N), BF16)] * 2, [df, wdn, ga, gb], sem=("parallel", "parallel"))


def _mm_norm(name, a, b, x, g, g_next=None, comm=None):
    M, K = a.shape
    N = b.shape[1]
    tm = _tile(M, 512, 8)
    more = g_next is not None

    def body(*refs):
        a_ref, b_ref, x_ref, g_ref = refs[:4]
        y_ref, xn_ref = refs[4 + more], refs[5 + more]
        y = jnp.dot(a_ref[...].astype(BF16), b_ref[...], preferred_element_type=F32)
        y_ref[...] = y.astype(BF16)
        xn = x_ref[...] + _rms(y, g_ref[...])
        xn_ref[...] = xn
        if more:
            refs[7][...] = _rms(xn, refs[4][...]).astype(BF16)

    row = lambda n: pl.BlockSpec((tm, n), lambda i: (i, 0))
    whole = lambda z: pl.BlockSpec(z.shape, lambda i: (0, 0))
    arrays = [a, b, x, g] + ([g_next] if more else [])
    in_specs = [row(K), whole(b), row(N), whole(g)] + ([whole(g_next)] if more else [])
    out_shape = [jax.ShapeDtypeStruct((M, N), dt) for dt in (BF16, F32) + ((BF16,) if more else ())]
    return _call(name, body, (M // tm,), in_specs, [row(N)] * (2 + more), out_shape, arrays, sem=("parallel",), comm=comm)


def _rms(x, g):
    r = lax.rsqrt(jnp.mean(x * x, axis=-1, keepdims=True) + EPS)
    return x * r * g


def _rms_bwd(x, g, dy):
    r = lax.rsqrt(jnp.mean(x * x, axis=-1, keepdims=True) + EPS)
    xh = x * r
    dxh = dy * g
    dx = r * (dxh - xh * jnp.mean(dxh * xh, axis=-1, keepdims=True))
    return dx, jnp.sum(dy * xh, axis=0, keepdims=True)


def _shift_down(x, k):
    t = lax.broadcasted_iota(jnp.int32, x.shape, 0)
    return jnp.where(t >= k, pltpu.roll(x, k, 0), 0.0)


def _shift_up(x, k):
    n = x.shape[0]
    t = lax.broadcasted_iota(jnp.int32, x.shape, 0)
    return jnp.where(t < n - k, pltpu.roll(x, n - k, 0), 0.0)


def _group_select(vals):
    lane = lax.broadcasted_iota(jnp.int32, vals[0].shape, 1) // GROUP
    out = vals[-1]
    for g in range(len(vals) - 2, -1, -1):
        out = jnp.where(lane == g, vals[g], out)
    return out


def _pool_count(shape):
    t = lax.broadcasted_iota(jnp.int32, shape, 0)
    lane = lax.broadcasted_iota(jnp.int32, shape, 1) // GROUP
    w = jnp.where(lane == 0, POOL_WINDOWS[0], jnp.where(lane == 1, POOL_WINDOWS[1],
                  jnp.where(lane == 2, POOL_WINDOWS[2], POOL_WINDOWS[3])))
    return jnp.minimum(t + 1, w).astype(F32)


def _pooled(z):
    s = [z]
    for k in (1, 2, 4, 8):
        s.append(s[-1] + _shift_down(s[-1], k))
    return _group_select(s[1:]) / _pool_count(z.shape) - z


def _pooled_bwd(dp):
    u = dp / _pool_count(dp.shape)
    s = [u]
    for k in (1, 2, 4, 8):
        s.append(s[-1] + _shift_up(s[-1], k))
    return _group_select(s[1:]) - dp


def _diag_mask():
    r = lax.broadcasted_iota(jnp.int32, (TQ, TQ), 0) // CHUNK
    c = lax.broadcasted_iota(jnp.int32, (TQ, TQ), 1) // CHUNK
    return r >= c


def _attn_fwd(q, k, v, B, S, comm=None):
    nq = S // TQ

    def body(q_ref, k_ref, v_ref, o_ref, lse_ref):
        mask = _diag_mask()
        for i in range(nq):
            r0, r1 = i * TQ, (i + 1) * TQ
            qb = q_ref[r0:r1, :]
            sd = jnp.where(mask, _dot_nt(qb, k_ref[r0:r1, :]), NEG_INF)
            m = jnp.max(sd, axis=-1, keepdims=True)
            if i:
                so = _dot_nt(qb, k_ref[0:r0, :])
                m = jnp.maximum(m, jnp.max(so, axis=-1, keepdims=True))
            pd = jnp.exp(sd - m)
            l = jnp.sum(pd, axis=-1, keepdims=True)
            acc = _dot(pd, v_ref[r0:r1, :])
            if i:
                po = jnp.exp(so - m)
                l = l + jnp.sum(po, axis=-1, keepdims=True)
                acc = acc + _dot(po, v_ref[0:r0, :])
            o_ref[r0:r1, :] = (acc / l).astype(BF16)
            lse_ref[r0:r1, :] = jnp.broadcast_to(m + jnp.log(l), (TQ, HP))

    spec = pl.BlockSpec((S, HP), lambda b, h: (b, h))
    return _call("attn_fwd", body, (B, HEADS), [spec] * 3, [spec] * 2,
                 [jax.ShapeDtypeStruct((B * S, QW), BF16), jax.ShapeDtypeStruct((B * S, QW), F32)],
                 [q, k, v], sem=("parallel", "parallel"), comm=comm)


def _attn_bwd(q, k, v, o, do, lse, B, S, comm=None):
    nq = S // TQ

    def body(q_ref, k_ref, v_ref, o_ref, do_ref, lse_ref, dq_ref, dk_ref, dv_ref, dk_acc, dv_acc):
        mask = _diag_mask()
        dk_acc[...] = jnp.zeros_like(dk_acc)
        dv_acc[...] = jnp.zeros_like(dv_acc)
        for i in range(nq):
            r0, r1 = i * TQ, (i + 1) * TQ
            qb, dob = q_ref[r0:r1, :], do_ref[r0:r1, :]
            delta = jnp.sum(o_ref[r0:r1, :].astype(F32) * dob.astype(F32), axis=-1, keepdims=True)
            lse_b = lse_ref[r0:r1, :][:, :1]
            dq = None
            for c0, c1, diag in ([(0, r0, False)] if i else []) + [(r0, r1, True)]:
                kb, vb = k_ref[c0:c1, :], v_ref[c0:c1, :]
                p = jnp.exp(_dot_nt(qb, kb) - lse_b)
                if diag:
                    p = jnp.where(mask, p, 0.0)
                ds = p * (_dot_nt(dob, vb) - delta)
                part = _dot(ds, kb)
                dq = part if dq is None else dq + part
                dk_acc[c0:c1, :] += _dot_tn(ds, qb)
                dv_acc[c0:c1, :] += _dot_tn(p, dob)
            dq_ref[r0:r1, :] = dq
        dk_ref[...] = dk_acc[...].astype(BF16)
        dv_ref[...] = dv_acc[...].astype(BF16)

    spec = pl.BlockSpec((S, HP), lambda b, h: (b, h))
    return _call("attn_bwd", body, (B, HEADS), [spec] * 6, [spec] * 3,
                 [jax.ShapeDtypeStruct((B * S, QW), F32)] + [jax.ShapeDtypeStruct((B * S, QW), BF16)] * 2,
                 [q, k, v, o, do, lse], scratch=[pltpu.VMEM((S, HP), F32)] * 2, sem=("parallel", "parallel"), comm=comm)


def _all_gather(name, xs):
    n = len(xs)

    def body(*refs):
        x_refs, out_refs = refs[:n], refs[n:2 * n]
        send_sems, recv_sems, local_sems = refs[2 * n:]
        x_, y_, c_ = lax.axis_index("x"), lax.axis_index("y"), lax.axis_index("c")
        me, sibling = (x_, y_, c_), (x_, y_, 1 - c_)
        chips = [(1 - x_, y_), (x_, 1 - y_), (1 - x_, 1 - y_)]

        def copy(a, k, block, to, own=False):
            slot = out_refs[a].at[4 * block[0] + 2 * block[1] + block[2]]
            return pltpu.make_async_remote_copy(
                src_ref=x_refs[a] if own else slot, dst_ref=slot,
                send_sem=send_sems.at[7 * a + k], recv_sem=recv_sems.at[7 * a + k], device_id=to, device_id_type=MESH)

        mine = [pltpu.make_async_copy(x_refs[a], out_refs[a].at[4 * x_ + 2 * y_ + c_], local_sems.at[a]) for a in range(n)]
        for cp in mine:
            cp.start()
        first = [copy(a, 1 + j, me, (*chip, c_), own=True) for j, chip in enumerate(chips) for a in range(n)]
        first += [copy(a, 0, me, sibling, own=True) for a in range(n)]
        for cp in first:
            cp.start()
        passed = []
        for j, chip in enumerate(chips):
            for a in range(n):
                copy(a, 1 + j, (*chip, c_), me).wait_recv()
                passed.append(copy(a, 4 + j, (*chip, c_), sibling))
                passed[-1].start()
        for a in range(n):
            copy(a, 0, sibling, me).wait_recv()
            for j, chip in enumerate(chips):
                copy(a, 4 + j, (*chip, 1 - c_), me).wait_recv()
        for cp in first + passed:
            cp.wait_send()
        for cp in mine:
            cp.wait()

    return pl.pallas_call(
        body, name=name, out_shape=[jax.ShapeDtypeStruct((N_DEV,) + x.shape, x.dtype) for x in xs],
        in_specs=[pl.BlockSpec(memory_space=pl.ANY)] * n, out_specs=[pl.BlockSpec(memory_space=pl.ANY)] * n,
        scratch_shapes=[pltpu.SemaphoreType.DMA((7 * n,)), pltpu.SemaphoreType.DMA((7 * n,)), pltpu.SemaphoreType.DMA((n,))],
    )(*xs)


def _adamw(name, parts, w, m, v, row_off=0, layers=1, stride=0):
    R, C = w.shape
    r = R // layers
    assert parts.shape[2] == C and parts.shape[1] >= (layers - 1) * stride + row_off + r
    if r % 8 == 0:
        tr = max(t for t in range(8, 513, 8) if r % t == 0 and row_off % t == 0 and stride % t == 0)
    else:
        assert layers == 1
        tr = r
    nb = r // tr

    def body(p_ref, w_ref, m_ref, v_ref, g_ref, d_ref, nm_ref, nv_ref):
        g = p_ref[0].astype(F32)
        for j in range(1, N_DEV):
            g = g + p_ref[j].astype(F32)
        m_new = ADAM_B1 * m_ref[...] + (1.0 - ADAM_B1) * g
        v_new = ADAM_B2 * v_ref[...] + (1.0 - ADAM_B2) * (g * g)
        m_hat = m_new / (1.0 - ADAM_B1 ** ADAM_STEP)
        v_hat = v_new / (1.0 - ADAM_B2 ** ADAM_STEP)
        g_ref[...] = g
        d_ref[...] = -ADAM_LR * (m_hat / (jnp.sqrt(v_hat) + ADAM_EPS) + ADAM_WD * w_ref[...])
        nm_ref[...] = m_new
        nv_ref[...] = v_new

    spec = pl.BlockSpec((tr, C), lambda l, i: (l * nb + i, 0))
    pspec = pl.BlockSpec((N_DEV, tr, C), lambda l, i: (0, (l * stride + row_off) // tr + i, 0))
    return _call(name, body, (layers, nb), [pspec, spec, spec, spec], [spec] * 4,
                 [jax.ShapeDtypeStruct((R, C), F32)] * 4, [parts, w, m, v], sem=("parallel", "parallel"))


def _rot_cols(w):
    h = w.shape[-1] // 2
    return jnp.concatenate([-w[..., h:], w[..., :h]], axis=-1)


def _unrot_cols(d):
    h = d.shape[-1] // 2
    return jnp.concatenate([d[..., h:], -d[..., :h]], axis=-1)


IN_WIDTH = sum(IN_SIZES)
IN_SHARD = IN_WIDTH // N_DEV
IN_GATE = IN_WIDTH - IN_SIZES[-1]


def _w_in_cols(buf, a, b):
    out = []
    for j in range(a // IN_SHARD, (b - 1) // IN_SHARD + 1):
        lo, hi = max(a, IN_SHARD * j) - IN_SHARD * j, min(b, IN_SHARD * (j + 1)) - IN_SHARD * j
        out.append(buf[j, :, lo:hi])
    return out


def _w1(buf):
    kr0 = IN_SIZES[0] + IN_SIZES[1]
    kr = jnp.concatenate(_w_in_cols(buf, kr0, kr0 + QK_ROPE), axis=-1)
    w1s = jnp.concatenate(_w_in_cols(buf, 0, kr0) + [kr, _rot_cols(kr), jnp.zeros((D, 64), BF16)]
                          + _w_in_cols(buf, kr0 + QK_ROPE, IN_GATE), axis=-1)
    return w1s, jnp.concatenate(_w_in_cols(buf, IN_GATE, IN_WIDTH), axis=-1)


def _w_in_grad_shards(d1s, d1g):
    kr0 = IN_SIZES[0] + IN_SIZES[1]
    seg = d1s[:, C_KR:C_KR + 128]
    dkr = seg[:, :QK_ROPE] + _unrot_cols(seg[:, QK_ROPE:2 * QK_ROPE])
    runs = [(0, d1s, 0), (kr0, dkr, 0), (kr0 + QK_ROPE, d1s, C_PIN), (IN_GATE, d1g, 0), (IN_WIDTH, None, 0)]
    blocks = []
    for j in range(N_DEV):
        a, b = IN_SHARD * j, IN_SHARD * (j + 1)
        pieces = []
        for (c0, src, s0), (c1, _, _) in zip(runs[:-1], runs[1:]):
            lo, hi = max(a, c0), min(b, c1)
            if lo < hi:
                pieces.append(src[:, s0 + lo - c0:s0 + hi - c0])
        blocks.append(jnp.concatenate(pieces, axis=-1))
    return jnp.stack(blocks)


def _layer_weights(G):
    L = G['w_uq'].shape[0]
    wq = G['w_uq'].reshape(L, Q_LORA, HEADS, QK_NOPE + QK_ROPE)
    nope, rope = wq[..., :QK_NOPE], wq[..., QK_NOPE:]
    z32 = jnp.zeros((L, Q_LORA, HEADS, HP - QK_NOPE - QK_ROPE), BF16)
    wq_a = jnp.concatenate([nope, rope, z32], axis=-1).reshape(L, Q_LORA, QW)
    wq_b = jnp.concatenate([jnp.zeros_like(nope), _rot_cols(rope), z32], axis=-1).reshape(L, Q_LORA, QW)
    wkv = G['w_ukv'].reshape(L, KV_LORA, HEADS, QK_NOPE + V_HEAD)
    z64 = jnp.zeros((L, KV_LORA, HEADS, HP - QK_NOPE), BF16)
    wkn = jnp.concatenate([wkv[..., :QK_NOPE], z64], axis=-1).reshape(L, KV_LORA, QW)
    wv = jnp.concatenate([wkv[..., QK_NOPE:], z64], axis=-1).reshape(L, KV_LORA, QW)
    wa = G['w_br_a'].reshape(L, HEADS, V_HEAD, D)
    wa = jnp.concatenate([wa, jnp.zeros((L, HEADS, HP - V_HEAD, D), BF16)], axis=2).reshape(L, QW, D)
    return dict(
        Wq=jnp.concatenate([wq_a, wq_b], axis=-1),
        Wkv=jnp.concatenate([wkn, wv], axis=-1), Wa=wa, Wb=G['w_br_b'], Wc=G['w_br_c'], Wd=G['w_br_d'],
        conv=G['conv_w'].reshape(L, 3, BR).astype(F32))


def _tables(S):
    inv = ROPE_THETA ** (-jnp.arange(0, QK_ROPE, 2, dtype=F32) / QK_ROPE)
    ang = jnp.arange(S, dtype=F32)[:, None] * inv[None, :]
    cos, sin = jnp.cos(ang), jnp.sin(ang)
    scale = (QK_NOPE + QK_ROPE) ** -0.5
    one, zero = jnp.ones((S, QK_NOPE), F32), jnp.zeros((S, QK_NOPE), F32)
    z32 = jnp.zeros((S, HP - QK_NOPE - QK_ROPE), F32)
    cq = jnp.concatenate([one, cos, cos, z32], axis=1) * scale
    sq = jnp.concatenate([zero, sin, sin, z32], axis=1) * scale
    ck = jnp.concatenate([cos, cos, sin, sin, jnp.zeros((S, HP - 2 * QK_ROPE), F32)], axis=1)
    return cq, sq, ck


def _place_k():
    p = np.zeros((HP, QW), np.float32)
    for r in range(2 * QK_ROPE):
        for h in range(HEADS):
            p[r, h * HP + QK_NOPE + r % QK_ROPE] = 1.0
    return jnp.asarray(p, BF16)


def _layer_fwd(x, h, W, late, P, tabs, B, S, ride, g_next):
    cq_t, sq_t, ck_t, place = tabs
    g_cq, g_ckv = P['g_cq'], P['g_ckv']
    P1 = _mm("in_proj_s", h, W['W1s'], out_dtype=BF16)
    PG = _mm("in_proj_g", h, W['W1g'], out_dtype=BF16, comm=ride.get('in_proj'))

    def q_prep(c, cq, sq, g, w):
        qq = _dot(_rms(c, g), w)
        return qq[:, :QW] * jnp.tile(cq, (1, HEADS)) + qq[:, QW:] * jnp.tile(sq, (1, HEADS))
    q = _rw("q_prep", q_prep, [win(P1, Q_LORA, C_CQ // Q_LORA), tab(cq_t), tab(sq_t)], [g_cq, W['Wq']], [(QW, BF16)])[0]

    def kv_prep(c, seg, ck, g, w, place):
        kv = _dot(_rms(c, g), w)
        return kv[:, :QW] + _dot(seg * ck, place), kv[:, QW:]
    k, v = _rw("kv_prep", kv_prep, [win(P1, KV_LORA, C_CKV // KV_LORA), win(P1, HP, C_KR // HP), tab(ck_t)], [g_ckv, W['Wkv'], place],
               [(QW, BF16), (QW, BF16)])
    o, lse = _attn_fwd(q, k, v, B, S, comm=ride.get('attn_fwd'))

    def pool_f(z, wbd, scale):
        return _dot(_pooled(z), wbd) * scale
    bo = _rw("pool_fwd", pool_f, [win(P1, BR, C_PIN // BR)], [P['pool_bd'], P['pool_scale']], [(BR, BF16)], tile=S)[0]

    def sgu_f(u, sv, g, wm, bias):
        vn = _rms(sv, g)
        blocks = [vn[r:r + SGU_BLOCK] for r in range(0, vn.shape[0], SGU_BLOCK)]
        return u * jnp.concatenate([_sgu_mix(vb, wm, bias) for vb in blocks], axis=0)
    co = _rw("sgu_fwd", sgu_f, [win(P1, BR, C_SU // BR), win(P1, BR, C_SV // BR)], [P['g_sgu_v'], P['sgu_wm'], P['sgu_bias']],
             [(BR, BF16)], tile=SGU_TILE)[0]

    def conv_f(b, c, xi, w):
        z = c * xi
        return b * (w[0:1] * _shift_down(z, 2) + w[1:2] * _shift_down(z, 1) + w[2:3] * z)
    do_ = _rw("conv_fwd", conv_f, [win(P1, BR, C_CVB // BR), win(P1, BR, C_CVC // BR), win(P1, BR, C_CVX // BR)], [W['conv']], [(BR, BF16)], tile=S)[0]

    def merge_f(a, b, c, d, l0, l1, l2, l3, wa, wb, wc, wd):
        return (_sigmoid(l0) * _dot(a, wa) + _sigmoid(l1) * _dot(b, wb) + _sigmoid(l2) * _dot(c, wc)
                + _sigmoid(l3) * _dot(d, wd))
    merged = _rw("merge_fwd", merge_f, [o, bo, co, do_] + [win(PG, D, i) for i in range(4)],
                 [W['Wa'], W['Wb'], W['Wc'], W['Wd']], [(D, BF16)], comm=ride.get('merge_fwd'))[0]
    W = dict(W, **late())
    om, x1, h2 = _mm_norm("out_proj", merged, W['Wout'], x, P['g_post_mix'], P['g_pre_ffn'])
    ga, gb, act = _ffn_up(h2, W['Wg'], W['Wu'], comm=ride.get('ffn_up'))
    rest = ride['ffn_down']() if 'ffn_down' in ride else None
    f, x2, *h_next = _mm_norm("ffn_down", act, W['Wdn'], x1, P['g_post_ffn'], g_next, comm=rest)
    saved = dict(x=x, h=h, P1=P1, PG=PG, q=q, k=k, v=v, o=o, lse=lse, bo=bo, co=co, do=do_, merged=merged, om=om, x1=x1,
                 h2=h2, ga=ga, gb=gb, act=act, f=f)
    return x2, (h_next[0] if h_next else None), saved, W


def _sgu_mix(vn, wm, bias):
    outs = [_dot(wm[g], vn) for g in range(4)]
    return _group_select(outs) + bias


def _layer_bwd(dx2, df, sv, W, P, tabs, B, S, own_ride, prev):
    cq_t, sq_t, ck_t, place = tabs
    P1 = sv['P1']
    grads = {}
    da, db = _ffn_down_dx(df, W['Wdn'], sv['ga'], sv['gb'])
    grads['Wdn'] = _mm_t("ffn_down_dw", sv['act'], df)
    dh2 = _mm("ffn_up_dx", da, W['Wg'], tb=True, a2=db, b2=W['Wu'], out_dtype=BF16)
    grads['Wg'] = _mm_t("ffn_gate_dw", sv['h2'], da)
    grads['Wu'] = _mm_t("ffn_up_dw", sv['h2'], db)

    def mid_norms_b(x1, dh, dy, om, g_ffn, g_mix):
        dx, dg_ffn = _rms_bwd(x1, g_ffn, dh)
        dx1 = dy + dx
        dom, dg_mix = _rms_bwd(om, g_mix, dx1)
        return dx1, dom, dg_ffn, dg_mix
    dx1, dom, grads['g_pre_ffn'], grads['g_post_mix'] = _rw(
        "mid_norms_bwd", mid_norms_b, [sv['x1'], dh2, dx2, sv['om']], [P['g_pre_ffn'], P['g_post_mix']],
        [(D, F32), (D, BF16)], [(1, D), (1, D)], tile=512)
    dmerged = _mm("out_proj_dx", dom, W['Wout'], tb=True, out_dtype=BF16)
    grads['Wout'] = _mm_t("out_proj_dw", sv['merged'], dom)
    ride = own_ride('ffn', grads)

    def merge_b(a, b, c, d, l0, l1, l2, l3, dm, wa, wb, wc, wd):
        outs_dl, outs_dy, outs_dbr = [], [], []
        for br, l, w in ((a, l0, wa), (b, l1, wb), (c, l2, wc), (d, l3, wd)):
            s = _sigmoid(l)
            y = _dot(br, w)
            dy = (dm * s).astype(BF16)
            outs_dl.append((dm * y * s * (1.0 - s)).astype(BF16))
            outs_dy.append(dy)
            outs_dbr.append(_dot_nt(dy, w))
        return (jnp.concatenate(outs_dl, axis=1), *outs_dy, *outs_dbr)
    mb = _rw("merge_bwd", merge_b,
             [sv['o'], sv['bo'], sv['co'], sv['do']] + [win(sv['PG'], D, i) for i in range(4)] + [dmerged],
             [W['Wa'], W['Wb'], W['Wc'], W['Wd']],
             [(4 * D, BF16)] + [(D, BF16)] * 4 + [(QW, BF16), (BR, BF16), (BR, BF16), (BR, BF16)], comm=ride.get('merge_bwd'))
    dl, dy, (d_o, d_bo, d_co, d_do) = mb[0], mb[1:5], mb[5:9]
    grads['Wa'] = _mm_t("br_a_dw", sv['o'], dy[0])
    grads['Wb'] = _mm_t("br_b_dw", sv['bo'], dy[1])
    grads['Wc'] = _mm_t("br_c_dw", sv['co'], dy[2])
    grads['Wd'] = _mm_t("br_d_dw", sv['do'], dy[3])

    def conv_b(b, c, xi, dout, w):
        z = c * xi
        z1, z2 = _shift_down(z, 1), _shift_down(z, 2)
        y = w[0:1] * z2 + w[1:2] * z1 + w[2:3] * z
        dyv = dout * b
        dz = w[2:3] * dyv + w[1:2] * _shift_up(dyv, 1) + w[0:1] * _shift_up(dyv, 2)
        return (dout * y, dz * xi, dz * c, jnp.sum(dyv * z2, axis=0, keepdims=True),
                jnp.sum(dyv * z1, axis=0, keepdims=True), jnp.sum(dyv * z, axis=0, keepdims=True))
    dcvb, dcvc, dcvx, dw0, dw1, dw2 = _rw("conv_bwd", conv_b, [win(P1, BR, C_CVB // BR), win(P1, BR, C_CVC // BR), win(P1, BR, C_CVX // BR), d_do],
                                          [W['conv']], [(BR, BF16)] * 3, [(1, BR)] * 3, tile=S)
    grads['conv'] = jnp.concatenate([dw0, dw1, dw2], axis=0)

    def sgu_b(u, s_v, dout, g, wm, bias):
        vn = _rms(s_v, g)
        dmix = dout * u
        lane = lax.broadcasted_iota(jnp.int32, (SGU_BLOCK, BR), 1) // GROUP
        mixed, dvn, dwm, dbias = [], [], [0.0] * 4, 0.0
        for r in range(0, vn.shape[0], SGU_BLOCK):
            vb, db = vn[r:r + SGU_BLOCK], dmix[r:r + SGU_BLOCK]
            mixed.append(_sgu_mix(vb, wm, bias))
            dvn.append(_group_select([_dot_tn(wm[gi], db) for gi in range(4)]))
            dwm = [dwm[gi] + _dot_nt(jnp.where(lane == gi, db, 0.0), vb) for gi in range(4)]
            dbias = dbias + db
        dsv, dg = _rms_bwd(s_v, g, jnp.concatenate(dvn, axis=0))
        return (dout * jnp.concatenate(mixed, axis=0), dsv, dg, *dwm, dbias)
    dsu, dsv_, grads['g_sgu_v'], wm0, wm1, wm2, wm3, grads['sgu_bias'] = _rw(
        "sgu_bwd", sgu_b, [win(P1, BR, C_SU // BR), win(P1, BR, C_SV // BR), d_co], [P['g_sgu_v'], P['sgu_wm'], P['sgu_bias']],
        [(BR, BF16)] * 2, [(1, BR)] + [(SGU_BLOCK, SGU_BLOCK)] * 4 + [(SGU_BLOCK, BR)], tile=SGU_TILE)
    grads['sgu_wm'] = jnp.stack([wm0, wm1, wm2, wm3])

    def pool_b(z, dout, wbd, scale):
        pooled = _pooled(z)
        mixed = _dot(pooled, wbd)
        dmix = dout * scale
        dz = _pooled_bwd(_dot_nt(dmix, wbd))
        return dz, _dot_tn(pooled, dmix), jnp.sum(dout * mixed, axis=0, keepdims=True)
    dpin, grads['pool_bd'], grads['pool_scale'] = _rw(
        "pool_bwd", pool_b, [win(P1, BR, C_PIN // BR), d_bo], [P['pool_bd'], P['pool_scale']], [(BR, BF16)], [(BR, BR), (1, BR)], tile=S)

    dq, dk, dv = _attn_bwd(sv['q'], sv['k'], sv['v'], sv['o'], d_o, sv['lse'], B, S, comm=ride.get('attn_bwd'))

    def q_prep_b(c, dq, cq, sq, g, w):
        dqq = jnp.concatenate([dq * jnp.tile(cq, (1, HEADS)), dq * jnp.tile(sq, (1, HEADS))], axis=1).astype(BF16)
        dc, dg = _rms_bwd(c, g, _dot_nt(dqq, w))
        return dc, _rms(c, g), dqq, dg
    dcq, cqn, dqq, grads['g_cq'] = _rw("q_prep_bwd", q_prep_b, [win(P1, Q_LORA, C_CQ // Q_LORA), dq, tab(cq_t), tab(sq_t)],
                                       [P['g_cq'], W['Wq']], [(Q_LORA, BF16), (Q_LORA, BF16), (2 * QW, BF16)], [(1, Q_LORA)])
    grads['Wq'] = _mm_t("uq_dw", cqn, dqq)

    def kv_prep_b(c, dk, dv, ck, g, w, place):
        dc, dg = _rms_bwd(c, g, _dot_nt(dk, w[:, :QW]) + _dot_nt(dv, w[:, QW:]))
        return dc, _dot_nt(dk, place) * ck, _rms(c, g), dg
    dckv, dseg, kvn, grads['g_ckv'] = _rw(
        "kv_prep_bwd", kv_prep_b, [win(P1, KV_LORA, C_CKV // KV_LORA), dk, dv, tab(ck_t)], [P['g_ckv'], W['Wkv'], place],
        [(KV_LORA, BF16), (HP, BF16), (KV_LORA, BF16)], [(1, KV_LORA)])
    grads['Wkn'] = _mm_t("uk_dw", kvn, dk)
    grads['Wv'] = _mm_t("uv_dw", kvn, dv)
    ride = dict(ride, **own_ride('mixers', grads))

    dps = jnp.concatenate([dcq, dckv, dseg, dpin, dsu, dsv_, dcvb, dcvc, dcvx], axis=1)
    grads['W1s'] = _mm_t("in_proj_s_dw", sv['h'], dps)
    grads['W1g'] = _mm_t("in_proj_g_dw", sv['h'], dl, comm=ride.get('in_proj_g_dw'))
    ride = dict(ride, **own_ride('w_in', grads))
    dh = _mm("in_proj_dx", dps, W['W1s'], tb=True, a2=dl, b2=W['W1g'], out_dtype=BF16, tm=256, comm=ride.get('in_proj_dx'))
    ride = dict(ride, **own_ride('w_in_rest', grads))

    if prev is None:
        def pre_mix_b(x, dh, dy, g):
            dx, dg = _rms_bwd(x, g, dh)
            return dy + dx, dg
        dx, grads['g_pre_mix'] = _rw("pre_mix_bwd", pre_mix_b, [sv['x'], dh, dx1], [P['g_pre_mix']], [(D, F32)], [(1, D)],
                                     tile=512, comm=ride.get('edge'))
        return dx, None, None, grads, ride

    def edge_norms_b(x, dh, dy, f, g_mix, g_ffn):
        dx, dg_mix = _rms_bwd(x, g_mix, dh)
        dx = dy + dx
        df, dg_ffn = _rms_bwd(f, g_ffn, dx)
        return dx, df, dg_mix, dg_ffn
    dx, df_prev, grads['g_pre_mix'], dg_prev = _rw(
        "edge_norms_bwd", edge_norms_b, [sv['x'], dh, dx1, prev[0]], [P['g_pre_mix'], prev[1]],
        [(D, F32), (D, BF16)], [(1, D), (1, D)], tile=512, comm=ride.get('edge'))
    return dx, df_prev, dg_prev, grads, ride


def _mixer_grads(g):
    out = {}
    qa = g['Wq'][:, :QW].reshape(1, Q_LORA, HEADS, HP)
    qb = g['Wq'][:, QW:].reshape(1, Q_LORA, HEADS, HP)
    drope = qa[..., QK_NOPE:QK_NOPE + QK_ROPE] + _unrot_cols(qb[..., QK_NOPE:QK_NOPE + QK_ROPE])
    out['w_uq'] = jnp.concatenate([qa[..., :QK_NOPE], drope], axis=-1).reshape(1, Q_LORA, HEADS * (QK_NOPE + QK_ROPE))
    kn = g['Wkn'].reshape(1, KV_LORA, HEADS, HP)[..., :QK_NOPE]
    vv = g['Wv'].reshape(1, KV_LORA, HEADS, HP)[..., :V_HEAD]
    out['w_ukv'] = jnp.concatenate([kn, vv], axis=-1).reshape(1, KV_LORA, HEADS * (QK_NOPE + V_HEAD))
    out['w_br_a'] = g['Wa'].reshape(1, HEADS, HP, D)[:, :, :V_HEAD].reshape(1, HEADS * V_HEAD, D)
    out['w_br_b'], out['w_br_c'], out['w_br_d'] = g['Wb'][None], g['Wc'][None], g['Wd'][None]
    out['conv_w'] = g['conv'].reshape(1, 3, 1, BR)
    return out


def _small_grads(g):
    out = {}
    for n in ('g_pre_mix', 'g_cq', 'g_ckv', 'g_sgu_v', 'g_post_mix', 'g_pre_ffn', 'g_post_ffn', 'pool_scale'):
        out[n] = g[n].reshape(1, -1)
    bd = g['pool_bd'].reshape(4, GROUP, 4, GROUP)
    out['pool_w'] = jnp.stack([bd[i, :, i, :] for i in range(4)])[None]
    out['sgu_w'] = (g['sgu_wm'] * _sgu_mask()[None])[None]
    out['sgu_b'] = g['sgu_bias'].reshape(SGU_BLOCK, 4, GROUP).sum(-1).T[None]
    return out


def _sgu_mask():
    pc = np.arange(SGU_BLOCK) // CHUNK
    return jnp.asarray(pc[:, None] >= pc[None, :], F32)


def _rows(a, lead):
    return a.reshape(a.shape[:lead] + (-1, a.shape[-1]))


def _pad_to(a, rows, cols):
    pad = [(0, 0)] * (a.ndim - 2) + [(0, rows - a.shape[-2]), (0, cols - a.shape[-1])]
    return jnp.pad(a, pad)


GROUPS = (('w_in',), ('w_ffn_gate',), ('w_ffn_up',), ('w_out', 'w_ffn_down'),
          ('w_uq', 'w_ukv', 'w_br_a', 'w_br_b', 'w_br_c', 'w_br_d', 'conv_w'))
PADDED = {'w_uq': (Q_LORA, 128), 'conv_w': (128, 128), 'w_ffn_down': (384, D)}
NARROW = ('w_uq', 'conv_w')


def _group(t, lead, groups):
    bufs = {}
    for b in groups:
        ps = [_rows(t[n], lead) for n in GROUPS[b]]
        ps = [_pad_to(p, *PADDED[n]) if n in PADDED else p for n, p in zip(GROUPS[b], ps)]
        bufs[b] = ps[0] if len(ps) == 1 else jnp.concatenate(ps, axis=-2)
    return bufs


def _group_rows(shapes):
    where, rows = {}, []
    for b, names in enumerate(GROUPS):
        off = 0
        for n in names:
            where[n] = (b, off)
            off += PADDED.get(n, shapes[n])[0]
        rows.append(off)
    return where, rows


def _shard_split(a, axis):
    n = a.shape[axis]
    a = a.reshape(a.shape[:axis] + (N_DEV, n // N_DEV) + a.shape[axis + 1:])
    return jnp.moveaxis(a, axis, 0)


def _shard_join(a, axis):
    a = jnp.moveaxis(a, 0, axis)
    return a.reshape(a.shape[:axis] + (a.shape[axis] * a.shape[axis + 1],) + a.shape[axis + 2:])


def _as2d(a):
    return a.reshape(-1, a.shape[-1])


def kernel(x, w_in, g_pre_mix, g_cq, g_ckv, w_uq, w_ukv, pool_w, pool_scale, g_sgu_v, sgu_w, sgu_b, conv_w, w_br_a, w_br_b, w_br_c, w_br_d, w_out, g_post_mix, g_pre_ffn, w_ffn_gate, w_ffn_up, w_ffn_down, g_post_ffn, loss_target, m_w_in, m_g_pre_mix, m_g_cq, m_g_ckv, m_w_uq, m_w_ukv, m_pool_w, m_pool_scale, m_g_sgu_v, m_sgu_w, m_sgu_b, m_conv_w, m_w_br_a, m_w_br_b, m_w_br_c, m_w_br_d, m_w_out, m_g_post_mix, m_g_pre_ffn, m_w_ffn_gate, m_w_ffn_up, m_w_ffn_down, m_g_post_ffn, v_w_in, v_g_pre_mix, v_g_cq, v_g_ckv, v_w_uq, v_w_ukv, v_pool_w, v_pool_scale, v_g_sgu_v, v_sgu_w, v_sgu_b, v_conv_w, v_w_br_a, v_w_br_b, v_w_br_c, v_w_br_d, v_w_out, v_g_post_mix, v_g_pre_ffn, v_w_ffn_gate, v_w_ffn_up, v_w_ffn_down, v_g_post_ffn):
    args = dict(locals())
    w = {n: args[n] for n in WEIGHTS}
    m = {n: args['m_' + n] for n in WEIGHTS}
    v = {n: args['v_' + n] for n in WEIGHTS}
    B, S, _ = x.shape
    T = B * S
    L = DEPTH
    names = list(SHARDED)
    shapes1 = {n: _as2d(w[n][0]).shape for n in names}
    where, group_rows = _group_rows(shapes1)

    def shard_bufs(l, groups):
        return _group({n: w[n][l:l + 1].astype(BF16) for b in groups for n in GROUPS[b]}, 0, groups)

    def joined(gathered):
        G = {}
        for b, buf in gathered.items():
            for n in GROUPS[b]:
                if n == 'w_in':
                    continue
                off = where[n][1]
                r, c = shapes1[n]
                G[n] = _shard_join(buf[:, off:off + r, :c].reshape((N_DEV, 1) + w[n].shape[1:]), SHARDED[n])
        return G

    eye = jnp.eye(4, dtype=F32)
    pool_bd = (pool_w[:, :, :, None, :] * eye[None, :, None, :, None]).reshape(L, BR, BR).astype(BF16)
    sgu_wm = (sgu_w * _sgu_mask()[None, None]).astype(BF16)
    sgu_bias = jnp.repeat(sgu_b.transpose(0, 2, 1), GROUP, axis=2)
    tabs = _tables(S) + (_place_k(),)

    def layer_params(l):
        P = {n: w[n][l][None, :] for n in ('g_pre_mix', 'g_cq', 'g_ckv', 'g_sgu_v', 'g_post_mix', 'g_pre_ffn',
                                            'g_post_ffn', 'pool_scale')}
        P.update(pool_bd=pool_bd[l], sgu_wm=sgu_wm[l], sgu_bias=sgu_bias[l])
        return P

    xt = x.reshape(T, D)
    h = _rw("pre_mix", lambda x, g: _rms(x, g), [xt], [w['g_pre_mix'][0][None, :]], [(D, BF16)], tile=512)[0]
    saved, Ws = [], []
    first = shard_bufs(0, (0, 4))
    early = dict(zip((0, 4), _all_gather("gather_weights", [first[0], first[4]])))
    for l in range(L):
        own = shard_bufs(l, (1, 2, 3))
        nxt = shard_bufs(l + 1, (0, 4)) if l + 1 < L else None
        ride = {'in_proj': _Gather([own[1]]), 'attn_fwd': _Gather([own[3], own[2]])}
        w_in_rest = []
        if nxt:
            ride.update(ffn_up=_Gather([nxt[0]], ks=(0, 1, 2, 4, 6)), merge_fwd=_Gather([nxt[4]]))

            def rest(ride=ride, nxt=nxt, made=w_in_rest):
                made.append(_Gather([nxt[0]], ks=(3, 5, 7), out=ride['ffn_up'].results))
                return made[0]
            ride['ffn_down'] = rest

        def late(ride=ride):
            G = joined({1: ride['in_proj'].results[0], 3: ride['attn_fwd'].results[0], 2: ride['attn_fwd'].results[1]})
            return dict(Wout=G['w_out'][0], Wg=G['w_ffn_gate'][0], Wu=G['w_ffn_up'][0], Wdn=G['w_ffn_down'][0])

        We = {k: a[0] for k, a in _layer_weights(joined(early)).items()}
        We['W1s'], We['W1g'] = _w1(early[0])
        g_next = w['g_pre_mix'][l + 1][None, :] if nxt else None
        xt, h, sv, W = _layer_fwd(xt, h, We, late, layer_params(l), tabs, B, S, ride, g_next)
        saved.append(sv)
        Ws.append(W)
        if nxt:
            early = {0: w_in_rest[0].results[0], 4: ride['merge_fwd'].results[0]}

    def loss_f(y, t, f, g):
        e = y - t
        dy = e * (1.0 / D)
        df, dg = _rms_bwd(f, g, dy)
        return dy, df, jnp.sum(e * e, axis=0, keepdims=True), dg
    dy, df, sq, dg_ffn = _rw("loss", loss_f, [xt, loss_target.reshape(T, D), saved[-1]['f']], [w['g_post_ffn'][L - 1][None, :]],
                             [(D, F32), (D, BF16)], [(1, D), (1, D)], tile=512)
    loss = lax.psum(0.5 * jnp.sum(sq) / D, ("x", "y", "c"))

    recv = [lax.empty((N_DEV, L * r, PADDED.get(ns[0], shapes1[ns[0]])[1]), BF16) for ns, r in zip(GROUPS, group_rows)]

    def send_bufs(pg, groups):
        return _group({n: _shard_split(pg[n], SHARDED[n]).astype(BF16) for b in groups for n in GROUPS[b]}, 1, groups)

    small, w_in_first = [None] * L, []
    for l in reversed(range(L)):
        def own_ride(stage, grads, l=l):
            if stage == 'ffn':
                bufs = send_bufs({'w_ffn_gate': grads['Wg'][None], 'w_ffn_up': grads['Wu'][None],
                                  'w_ffn_down': grads['Wdn'][None], 'w_out': grads['Wout'][None]}, (1, 2, 3))
                return {'merge_bwd': _Scatter([bufs[1]], [recv[1]], [l * group_rows[1]]),
                        'attn_bwd': _Scatter([bufs[3], bufs[2]], [recv[3], recv[2]], [l * group_rows[3], l * group_rows[2]])}
            if stage == 'mixers':
                bufs = send_bufs(_mixer_grads(grads), (4,))
                return {'in_proj_g_dw': _Scatter([bufs[4]], [recv[4]], [l * group_rows[4]])}
            if stage == 'w_in':
                shards = _w_in_grad_shards(grads['W1s'], grads['W1g'])
                w_in_first.append(_Scatter([shards], [recv[0]], [l * group_rows[0]], ks=(0, 1, 2, 4, 6)))
                return {'in_proj_dx': w_in_first[-1]}
            first = w_in_first[-1]
            return {'edge': _Scatter(first.inputs[:1], first.results, [l * group_rows[0]], ks=(3, 5, 7))}

        prev = (saved[l - 1]['f'], w['g_post_ffn'][l - 1][None, :]) if l else None
        dy, df_prev, dg_prev, gl, ride = _layer_bwd(dy, df, saved[l], Ws[l], layer_params(l), tabs, B, S, own_ride, prev)
        gl['g_post_ffn'] = dg_ffn
        df, dg_ffn = df_prev, dg_prev
        for name, bs in (('in_proj_g_dw', (4,)), ('edge', (0,)), ('merge_bwd', (1,)), ('attn_bwd', (3, 2))):
            for b, res_b in zip(bs, ride[name].results):
                recv[b] = res_b
        pg = _small_grads(gl)
        rep = jnp.concatenate([pg[n].reshape(-1, 128) for n in REPLICATED], axis=0)
        small[l] = _pad_to(rep, -(-rep.shape[0] // 8) * 8, 128)
    grad_x = dy.reshape(B, S, D)

    small_rows = small[0].shape[0]
    got_small = _all_gather("gather_small_grads", [jnp.concatenate(small, axis=0)])[0].reshape(N_DEV, L, small_rows, 128)

    res = {}
    for n in names:
        b, off = where[n]
        r, c = shapes1[n]
        if n in NARROW:
            parts = recv[b].reshape(N_DEV, L, group_rows[b], -1)[:, :, off:off + r, :c].reshape(N_DEV, L * r, c)
            res[n] = _adamw("adamw_" + n, parts, _as2d(w[n]), _as2d(m[n]), _as2d(v[n]))
        else:
            res[n] = _adamw("adamw_" + n, recv[b], _as2d(w[n]), _as2d(m[n]), _as2d(v[n]), row_off=off, layers=L,
                            stride=group_rows[b])
    off = 0
    for n in REPLICATED:
        r = int(np.prod(w[n].shape[1:])) // 128
        shp2 = _as2d(w[n]).shape
        res[n] = _adamw("adamw_" + n, got_small[:, :, off:off + r].reshape((N_DEV,) + shp2), _as2d(w[n]), _as2d(m[n]), _as2d(v[n]))
        off += r
    outs = [loss, grad_x]
    for k in range(4):
        outs += [res[n][k].reshape(w[n].shape) for n in WEIGHTS]
    return tuple(outs)
```

```python
import functools

import numpy as np
import jax
import jax.numpy as jnp
from jax import lax
from jax.experimental import pallas as pl
from jax.experimental.pallas import tpu as pltpu

F32, BF16 = jnp.float32, jnp.bfloat16
MESH = pl.DeviceIdType.MESH
N_DEV = 8

D = 1024
DEPTH = 4
CHUNK = 64
EPS = 1e-6
NEG_INF = -1e30
HEADS, QK_NOPE, QK_ROPE, V_HEAD = 8, 64, 32, 64
Q_LORA, KV_LORA = 256, 128
ROPE_THETA = 10000.0
POOL_WINDOWS = (2, 4, 8, 16)
GROUP = 64
BR = 256
SGU_BLOCK = 128
SGU_TILE = 4 * SGU_BLOCK
D_FF = 2816
IN_SIZES = (256, 128, 32, 256, 256, 256, 256, 256, 256, 4096)
HP = 128
QW = HEADS * HP
C_CQ, C_CKV, C_KR, C_PIN, C_SU, C_SV, C_CVB, C_CVC, C_CVX, C_GATE = 0, 256, 384, 512, 768, 1024, 1280, 1536, 1792, 2048

ADAM_LR, ADAM_B1, ADAM_B2, ADAM_EPS, ADAM_WD, ADAM_STEP = 0.001, 0.9, 0.999, 1e-08, 0.01, 10

WEIGHTS = ['w_in', 'g_pre_mix', 'g_cq', 'g_ckv', 'w_uq', 'w_ukv', 'pool_w', 'pool_scale', 'g_sgu_v', 'sgu_w', 'sgu_b',
           'conv_w', 'w_br_a', 'w_br_b', 'w_br_c', 'w_br_d', 'w_out', 'g_post_mix', 'g_pre_ffn', 'w_ffn_gate',
           'w_ffn_up', 'w_ffn_down', 'g_post_ffn']
SHARDED = {'w_in': 2, 'w_uq': 2, 'w_ukv': 2, 'conv_w': 3, 'w_br_a': 2, 'w_br_b': 2, 'w_br_c': 2, 'w_br_d': 2,
           'w_out': 1, 'w_ffn_gate': 2, 'w_ffn_up': 2, 'w_ffn_down': 1}
REPLICATED = [n for n in WEIGHTS if n not in SHARDED]

W_IN_FIRST, W_IN_REST = (0, 1, 2, 3, 4, 6), (5, 7)

VMEM_LIMIT = 56 * 1024 * 1024
TQ = 256


def _cparams(sem=None):
    return pltpu.CompilerParams(vmem_limit_bytes=VMEM_LIMIT, dimension_semantics=sem)


def _tile(n, cap, align=128):
    if n <= cap:
        return n
    for t in range(cap - cap % align, 0, -align):
        if n % t == 0:
            return t
    return n


def _peers():
    x_, y_, c_ = lax.axis_index("x"), lax.axis_index("y"), lax.axis_index("c")
    out = []
    for k in range(1, N_DEV):
        px, py, pc = (x_ + (k >> 2)) % 2, (y_ + ((k >> 1) & 1)) % 2, (c_ + (k & 1)) % 2
        out.append((k, (px, py, pc), 4 * px + 2 * py + pc))
    return 4 * x_ + 2 * y_ + c_, out


class _Exchange:
    def __init__(self, n):
        self.n = n
        self.scratch = [pltpu.SemaphoreType.DMA((7 * n,)), pltpu.SemaphoreType.DMA((7 * n,)),
                        pltpu.SemaphoreType.DMA((n,))]
        self.results = None

    def start(self, cin, cout, sems):
        local, sends, _ = self.copies(cin, cout, sems)
        for cp in local + sends:
            cp.start()

    def wait(self, cin, cout, sems):
        local, sends, recvs = self.copies(cin, cout, sems)
        for cp in recvs:
            cp.wait_recv()
        for cp in sends:
            cp.wait_send()
        for cp in local:
            cp.wait()


class _Gather(_Exchange):
    def __init__(self, xs, ks=range(N_DEV), out=()):
        super().__init__(len(xs))
        self.inputs = list(xs) + list(out)
        self.out_shape = [jax.ShapeDtypeStruct((N_DEV,) + x.shape, x.dtype) for x in xs]
        self.aliases = {self.n + a: a for a in range(len(out))}
        self.ks = tuple(ks)

    def copies(self, cin, cout, sems):
        send_sems, recv_sems, local_sems = sems
        me, peers = _peers()
        local, sends, recvs = [], [], []
        for a in range(self.n):
            if 0 in self.ks:
                local.append(pltpu.make_async_copy(cin[a], cout[a].at[me], local_sems.at[a]))
            for k, dev, flat in peers:
                if k not in self.ks:
                    continue
                sem = dict(send_sem=send_sems.at[7 * a + k - 1], recv_sem=recv_sems.at[7 * a + k - 1],
                           device_id=dev, device_id_type=MESH)
                sends.append(pltpu.make_async_remote_copy(src_ref=cin[a], dst_ref=cout[a].at[me], **sem))
                recvs.append(pltpu.make_async_remote_copy(src_ref=cin[a], dst_ref=cout[a].at[flat], **sem))
        return local, sends, recvs


class _Scatter(_Exchange):
    def __init__(self, xs, recv, row0, ks=range(N_DEV)):
        super().__init__(len(xs))
        self.inputs = list(xs) + list(recv)
        self.out_shape = [jax.ShapeDtypeStruct(r.shape, r.dtype) for r in recv]
        self.aliases = {self.n + a: a for a in range(self.n)}
        self.row0 = list(row0)
        self.ks = tuple(ks)

    def copies(self, cin, cout, sems):
        send_sems, recv_sems, local_sems = sems
        me, peers = _peers()
        local, sends, recvs = [], [], []
        for a in range(self.n):
            rows = pl.ds(self.row0[a], self.inputs[a].shape[1])
            if 0 in self.ks:
                local.append(pltpu.make_async_copy(cin[a].at[me], cout[a].at[me, rows], local_sems.at[a]))
            for k, dev, flat in peers:
                if k not in self.ks:
                    continue
                sem = dict(send_sem=send_sems.at[7 * a + k - 1], recv_sem=recv_sems.at[7 * a + k - 1],
                           device_id=dev, device_id_type=MESH)
                sends.append(pltpu.make_async_remote_copy(src_ref=cin[a].at[flat], dst_ref=cout[a].at[me, rows], **sem))
                recvs.append(pltpu.make_async_remote_copy(src_ref=cin[a].at[me], dst_ref=cout[a].at[flat, rows], **sem))
        return local, sends, recvs


def _call(name, body, grid, in_specs, out_specs, out_shape, arrays, scratch=(), sem=None, comm=None):
    if comm is None:
        res = pl.pallas_call(body, name=name, grid=grid, in_specs=list(in_specs), out_specs=list(out_specs),
                             out_shape=list(out_shape), scratch_shapes=list(scratch), compiler_params=_cparams(sem))(*arrays)
        return list(res)
    ni, no, ns = len(in_specs), len(out_specs), len(scratch)
    nci, nco = len(comm.inputs), len(comm.out_shape)
    hbm = pl.BlockSpec(memory_space=pl.ANY)

    def wrapped(*refs):
        ins, cin = refs[:ni], refs[ni:ni + nci]
        o0 = ni + nci
        outs, cout = refs[o0:o0 + no], refs[o0 + no:o0 + no + nco]
        s0 = o0 + no + nco
        scr, sems = refs[s0:s0 + ns], refs[s0 + ns:]
        ids = [pl.program_id(d) for d in range(len(grid))]
        first = functools.reduce(jnp.logical_and, [i == 0 for i in ids])
        last = functools.reduce(jnp.logical_and, [i == g - 1 for i, g in zip(ids, grid)])

        @pl.when(first)
        def _():
            comm.start(cin, cout, sems)

        body(*ins, *outs, *scr)

        @pl.when(last)
        def _():
            comm.wait(cin, cout, sems)

    res = pl.pallas_call(
        wrapped, name=name, grid=grid, in_specs=list(in_specs) + [hbm] * nci, out_specs=list(out_specs) + [hbm] * nco,
        out_shape=list(out_shape) + list(comm.out_shape), scratch_shapes=list(scratch) + comm.scratch,
        input_output_aliases={ni + i: no + j for i, j in comm.aliases.items()},
        compiler_params=_cparams(("arbitrary",) * len(grid)))(*arrays, *comm.inputs)
    comm.results = list(res[no:])
    return list(res[:no])


def win(arr, width, idx):
    return ('win', arr, width, idx)


def tab(arr):
    return ('tab', arr)


def _rw(name, fn, ins, consts, outs, accs=(), tile=256, comm=None):
    first = ins[0][1] if isinstance(ins[0], tuple) else ins[0]
    T = first.shape[0]
    tile = min(tile, T)
    grid = (T // tile,)
    arrays, in_specs = [], []
    for x in ins:
        if isinstance(x, tuple) and x[0] == 'win':
            _, a, w, k = x
            arrays.append(a)
            in_specs.append(pl.BlockSpec((tile, w), lambda i, k=k: (i, k)))
        elif isinstance(x, tuple) and x[0] == 'tab':
            a = x[1]
            nb = a.shape[0] // tile
            arrays.append(a)
            in_specs.append(pl.BlockSpec((tile, a.shape[1]), lambda i, nb=nb: (i % nb, 0)))
        else:
            arrays.append(x)
            in_specs.append(pl.BlockSpec((tile, x.shape[1]), lambda i: (i, 0)))
    for c in consts:
        arrays.append(c)
        in_specs.append(pl.BlockSpec(c.shape, lambda i, n=c.ndim: (0,) * n))
    out_shape = [jax.ShapeDtypeStruct((T, f), dt) for f, dt in outs]
    out_specs = [pl.BlockSpec((tile, f), lambda i: (i, 0)) for f, _ in outs]
    for shp in accs:
        out_shape.append(jax.ShapeDtypeStruct(shp, F32))
        out_specs.append(pl.BlockSpec(shp, lambda i, n=len(shp): (0,) * n))
    ni, nc, no = len(ins), len(consts), len(outs)

    def body(*refs):
        vals_in = [r[...] for r in refs[:ni]]
        vals_in = [v.astype(F32) if v.dtype == BF16 else v for v in vals_in]
        vals = fn(*vals_in, *[r[...] for r in refs[ni:ni + nc]])
        if not isinstance(vals, (tuple, list)):
            vals = (vals,)
        o_refs = refs[ni + nc:]
        for r, v in zip(o_refs[:no], vals[:no]):
            r[...] = v.astype(r.dtype)
        i = pl.program_id(0)
        for r, v in zip(o_refs[no:], vals[no:]):
            @pl.when(i == 0)
            def _(r=r, v=v):
                r[...] = v

            @pl.when(i > 0)
            def _(r=r, v=v):
                r[...] += v

    return _call(name, body, grid, in_specs, out_specs, out_shape, arrays, sem=("arbitrary",), comm=comm)


def _mm(name, a, b, tb=False, out_dtype=F32, tm=512, tn=3072, a2=None, b2=None, comm=None):
    M, K = a.shape
    N = b.shape[0] if tb else b.shape[1]
    tm, tn = _tile(M, tm, 8), _tile(N, tn)
    dims = (((1,), (1,)), ((), ())) if tb else (((1,), (0,)), ((), ()))
    pairs = [(a, b)] + ([(a2, b2)] if a2 is not None else [])

    def body(*refs):
        o_ref = refs[-1]
        acc = None
        for i in range(len(pairs)):
            p = lax.dot_general(refs[2 * i][...].astype(BF16), refs[2 * i + 1][...].astype(BF16), dims,
                                preferred_element_type=F32)
            acc = p if acc is None else acc + p
        o_ref[...] = acc.astype(o_ref.dtype)

    in_specs, arrays = [], []
    for x, y in pairs:
        k = x.shape[1]
        in_specs.append(pl.BlockSpec((tm, k), lambda j, i: (i, 0)))
        in_specs.append(pl.BlockSpec((tn, k), lambda j, i: (j, 0)) if tb else pl.BlockSpec((k, tn), lambda j, i: (0, j)))
        arrays += [x, y]
    return _call(name, body, (N // tn, M // tm), in_specs, [pl.BlockSpec((tm, tn), lambda j, i: (i, j))],
                 [jax.ShapeDtypeStruct((M, N), out_dtype)], arrays, sem=("parallel", "parallel"), comm=comm)[0]


def _mm_t(name, a, g, tk=1408, tn=1408, tt=2048, comm=None):
    T, K = a.shape
    N = g.shape[1]
    tk, tn, tt = _tile(K, tk), _tile(N, tn), _tile(T, tt, 8)
    nt = T // tt

    def body(a_ref, g_ref, o_ref, acc_ref):
        p = lax.dot_general(a_ref[...].astype(BF16), g_ref[...].astype(BF16), (((0,), (0,)), ((), ())),
                            preferred_element_type=F32)
        t = pl.program_id(2)

        @pl.when(t == 0)
        def _():
            acc_ref[...] = p

        @pl.when(t > 0)
        def _():
            acc_ref[...] += p

        @pl.when(t == nt - 1)
        def _():
            o_ref[...] = acc_ref[...].astype(BF16)

    return _call(name, body, (K // tk, N // tn, nt),
                 [pl.BlockSpec((tt, tk), lambda i, j, t: (t, i)), pl.BlockSpec((tt, tn), lambda i, j, t: (t, j))],
                 [pl.BlockSpec((tk, tn), lambda i, j, t: (i, j))], [jax.ShapeDtypeStruct((K, N), BF16)], [a, g],
                 scratch=[pltpu.VMEM((tk, tn), F32)], sem=("parallel", "parallel", "arbitrary"), comm=comm)[0]


def _dot(a, b):
    return jnp.dot(a.astype(BF16), b.astype(BF16), preferred_element_type=F32)


def _dot_nt(a, b):
    return lax.dot_general(a.astype(BF16), b.astype(BF16), (((1,), (1,)), ((), ())), preferred_element_type=F32)


def _dot_tn(a, b):
    return lax.dot_general(a.astype(BF16), b.astype(BF16), (((0,), (0,)), ((), ())), preferred_element_type=F32)


def _sigmoid(x):
    return 0.5 * jnp.tanh(0.5 * x) + 0.5


def _ffn_up(h, wg, wu, comm=None):
    M, K = h.shape
    N = wg.shape[1]
    tm, tn = _tile(M, 512, 8), _tile(N, 1408)

    def body(h_ref, wg_ref, wu_ref, ga_ref, gb_ref, act_ref):
        hb = h_ref[...]
        a = jnp.dot(hb, wg_ref[...], preferred_element_type=F32)
        b = jnp.dot(hb, wu_ref[...], preferred_element_type=F32)
        s = _sigmoid(a)
        silu = a * s
        ga_ref[...] = (b * (s + silu * (1.0 - s))).astype(BF16)
        gb_ref[...] = silu.astype(BF16)
        act_ref[...] = (silu * b).astype(BF16)

    wspec = pl.BlockSpec((K, tn), lambda j, i: (0, j))
    ospec = pl.BlockSpec((tm, tn), lambda j, i: (i, j))
    return _call("ffn_up", body, (N // tn, M // tm), [pl.BlockSpec((tm, K), lambda j, i: (i, 0)), wspec, wspec],
                 [ospec] * 3, [jax.ShapeDtypeStruct((M, N), BF16)] * 3, [h, wg, wu], sem=("parallel", "parallel"),
                 comm=comm)


def _ffn_down_dx(df, wdn, ga, gb):
    M, K = df.shape
    N = wdn.shape[0]
    tm, tn = _tile(M, 512, 8), _tile(N, 1408)

    def body(df_ref, w_ref, ga_ref, gb_ref, da_ref, db_ref):
        d = _dot_nt(df_ref[...], w_ref[...])
        da_ref[...] = (d * ga_ref[...].astype(F32)).astype(BF16)
        db_ref[...] = (d * gb_ref[...].astype(F32)).astype(BF16)

    tspec = pl.BlockSpec((tm, tn), lambda j, i: (i, j))
    return _call("ffn_down_dx", body, (N // tn, M // tm),
                 [pl.BlockSpec((tm, K), lambda j, i: (i, 0)), pl.BlockSpec((tn, K), lambda j, i: (j, 0)), tspec, tspec],
                 [tspec] * 2, [jax.ShapeDtypeStruct((M, N), BF16)] * 2, [df, wdn, ga, gb], sem=("parallel", "parallel"))


def _mm_norm(name, a, b, x, g, g_next=None, comm=None):
    M, K = a.shape
    N = b.shape[1]
    tm = _tile(M, 512, 8)
    more = g_next is not None

    def body(*refs):
        a_ref, b_ref, x_ref, g_ref = refs[:4]
        y_ref, xn_ref = refs[4 + more], refs[5 + more]
        y = jnp.dot(a_ref[...].astype(BF16), b_ref[...], preferred_element_type=F32)
        y_ref[...] = y.astype(BF16)
        xn = x_ref[...] + _rms(y, g_ref[...])
        xn_ref[...] = xn
        if more:
            refs[7][...] = _rms(xn, refs[4][...]).astype(BF16)

    row = lambda n: pl.BlockSpec((tm, n), lambda i: (i, 0))
    whole = lambda z: pl.BlockSpec(z.shape, lambda i: (0, 0))
    arrays = [a, b, x, g] + ([g_next] if more else [])
    in_specs = [row(K), whole(b), row(N), whole(g)] + ([whole(g_next)] if more else [])
    out_shape = [jax.ShapeDtypeStruct((M, N), dt) for dt in (BF16, F32) + ((BF16,) if more else ())]
    return _call(name, body, (M // tm,), in_specs, [row(N)] * (2 + more), out_shape, arrays, sem=("parallel",), comm=comm)


def _rms(x, g):
    r = lax.rsqrt(jnp.mean(x * x, axis=-1, keepdims=True) + EPS)
    return x * r * g


def _rms_bwd(x, g, dy):
    r = lax.rsqrt(jnp.mean(x * x, axis=-1, keepdims=True) + EPS)
    xh = x * r
    dxh = dy * g
    dx = r * (dxh - xh * jnp.mean(dxh * xh, axis=-1, keepdims=True))
    return dx, jnp.sum(dy * xh, axis=0, keepdims=True)


def _shift_down(x, k):
    t = lax.broadcasted_iota(jnp.int32, x.shape, 0)
    return jnp.where(t >= k, pltpu.roll(x, k, 0), 0.0)


def _shift_up(x, k):
    n = x.shape[0]
    t = lax.broadcasted_iota(jnp.int32, x.shape, 0)
    return jnp.where(t < n - k, pltpu.roll(x, n - k, 0), 0.0)


def _group_select(vals):
    lane = lax.broadcasted_iota(jnp.int32, vals[0].shape, 1) // GROUP
    out = vals[-1]
    for g in range(len(vals) - 2, -1, -1):
        out = jnp.where(lane == g, vals[g], out)
    return out


def _pool_count(shape):
    t = lax.broadcasted_iota(jnp.int32, shape, 0)
    lane = lax.broadcasted_iota(jnp.int32, shape, 1) // GROUP
    w = jnp.where(lane == 0, POOL_WINDOWS[0], jnp.where(lane == 1, POOL_WINDOWS[1],
                  jnp.where(lane == 2, POOL_WINDOWS[2], POOL_WINDOWS[3])))
    return jnp.minimum(t + 1, w).astype(F32)


def _pooled(z):
    s = [z]
    for k in (1, 2, 4, 8):
        s.append(s[-1] + _shift_down(s[-1], k))
    return _group_select(s[1:]) / _pool_count(z.shape) - z


def _pooled_bwd(dp):
    u = dp / _pool_count(dp.shape)
    s = [u]
    for k in (1, 2, 4, 8):
        s.append(s[-1] + _shift_up(s[-1], k))
    return _group_select(s[1:]) - dp


def _diag_mask():
    r = lax.broadcasted_iota(jnp.int32, (TQ, TQ), 0) // CHUNK
    c = lax.broadcasted_iota(jnp.int32, (TQ, TQ), 1) // CHUNK
    return r >= c


def _attn_fwd(q, k, v, B, S, comm=None):
    nq = S // TQ

    def body(q_ref, k_ref, v_ref, o_ref, lse_ref):
        mask = _diag_mask()
        for i in range(nq):
            r0, r1 = i * TQ, (i + 1) * TQ
            qb = q_ref[r0:r1, :]
            sd = jnp.where(mask, _dot_nt(qb, k_ref[r0:r1, :]), NEG_INF)
            m = jnp.max(sd, axis=-1, keepdims=True)
            if i:
                so = _dot_nt(qb, k_ref[0:r0, :])
                m = jnp.maximum(m, jnp.max(so, axis=-1, keepdims=True))
            pd = jnp.exp(sd - m)
            l = jnp.sum(pd, axis=-1, keepdims=True)
            acc = _dot(pd, v_ref[r0:r1, :])
            if i:
                po = jnp.exp(so - m)
                l = l + jnp.sum(po, axis=-1, keepdims=True)
                acc = acc + _dot(po, v_ref[0:r0, :])
            o_ref[r0:r1, :] = (acc / l).astype(BF16)
            lse_ref[r0:r1, :] = jnp.broadcast_to(m + jnp.log(l), (TQ, HP))

    spec = pl.BlockSpec((S, HP), lambda b, h: (b, h))
    return _call("attn_fwd", body, (B, HEADS), [spec] * 3, [spec] * 2,
                 [jax.ShapeDtypeStruct((B * S, QW), BF16), jax.ShapeDtypeStruct((B * S, QW), F32)],
                 [q, k, v], sem=("parallel", "parallel"), comm=comm)


def _attn_bwd(q, k, v, o, do, lse, B, S, comm=None):
    nq = S // TQ

    def body(q_ref, k_ref, v_ref, o_ref, do_ref, lse_ref, dq_ref, dk_ref, dv_ref, dk_acc, dv_acc):
        mask = _diag_mask()
        dk_acc[...] = jnp.zeros_like(dk_acc)
        dv_acc[...] = jnp.zeros_like(dv_acc)
        for i in range(nq):
            r0, r1 = i * TQ, (i + 1) * TQ
            qb, dob = q_ref[r0:r1, :], do_ref[r0:r1, :]
            delta = jnp.sum(o_ref[r0:r1, :].astype(F32) * dob.astype(F32), axis=-1, keepdims=True)
            lse_b = lse_ref[r0:r1, :][:, :1]
            dq = None
            for c0, c1, diag in ([(0, r0, False)] if i else []) + [(r0, r1, True)]:
                kb, vb = k_ref[c0:c1, :], v_ref[c0:c1, :]
                p = jnp.exp(_dot_nt(qb, kb) - lse_b)
                if diag:
                    p = jnp.where(mask, p, 0.0)
                ds = p * (_dot_nt(dob, vb) - delta)
                part = _dot(ds, kb)
                dq = part if dq is None else dq + part
                dk_acc[c0:c1, :] += _dot_tn(ds, qb)
                dv_acc[c0:c1, :] += _dot_tn(p, dob)
            dq_ref[r0:r1, :] = dq
        dk_ref[...] = dk_acc[...].astype(BF16)
        dv_ref[...] = dv_acc[...].astype(BF16)

    spec = pl.BlockSpec((S, HP), lambda b, h: (b, h))
    return _call("attn_bwd", body, (B, HEADS), [spec] * 6, [spec] * 3,
                 [jax.ShapeDtypeStruct((B * S, QW), F32)] + [jax.ShapeDtypeStruct((B * S, QW), BF16)] * 2,
                 [q, k, v, o, do, lse], scratch=[pltpu.VMEM((S, HP), F32)] * 2, sem=("parallel", "parallel"), comm=comm)


def _all_gather(name, xs):
    n = len(xs)

    def body(*refs):
        x_refs, out_refs = refs[:n], refs[n:2 * n]
        send_sems, recv_sems, local_sems = refs[2 * n:]
        x_, y_, c_ = lax.axis_index("x"), lax.axis_index("y"), lax.axis_index("c")
        me, sibling = (x_, y_, c_), (x_, y_, 1 - c_)
        chips = [(1 - x_, y_), (x_, 1 - y_), (1 - x_, 1 - y_)]

        def copy(a, k, block, to, own=False):
            slot = out_refs[a].at[4 * block[0] + 2 * block[1] + block[2]]
            return pltpu.make_async_remote_copy(
                src_ref=x_refs[a] if own else slot, dst_ref=slot,
                send_sem=send_sems.at[7 * a + k], recv_sem=recv_sems.at[7 * a + k], device_id=to, device_id_type=MESH)

        mine = [pltpu.make_async_copy(x_refs[a], out_refs[a].at[4 * x_ + 2 * y_ + c_], local_sems.at[a]) for a in range(n)]
        for cp in mine:
            cp.start()
        first = [copy(a, 1 + j, me, (*chip, c_), own=True) for j, chip in enumerate(chips) for a in range(n)]
        first += [copy(a, 0, me, sibling, own=True) for a in range(n)]
        for cp in first:
            cp.start()
        passed = []
        for j, chip in enumerate(chips):
            for a in range(n):
                copy(a, 1 + j, (*chip, c_), me).wait_recv()
                passed.append(copy(a, 4 + j, (*chip, c_), sibling))
                passed[-1].start()
        for a in range(n):
            copy(a, 0, sibling, me).wait_recv()
            for j, chip in enumerate(chips):
                copy(a, 4 + j, (*chip, 1 - c_), me).wait_recv()
        for cp in first + passed:
            cp.wait_send()
        for cp in mine:
            cp.wait()

    return pl.pallas_call(
        body, name=name, out_shape=[jax.ShapeDtypeStruct((N_DEV,) + x.shape, x.dtype) for x in xs],
        in_specs=[pl.BlockSpec(memory_space=pl.ANY)] * n, out_specs=[pl.BlockSpec(memory_space=pl.ANY)] * n,
        scratch_shapes=[pltpu.SemaphoreType.DMA((7 * n,)), pltpu.SemaphoreType.DMA((7 * n,)), pltpu.SemaphoreType.DMA((n,))],
    )(*xs)


def _adamw(name, parts, w, m, v, row_off=0, layers=1, stride=0):
    R, C = w.shape
    r = R // layers
    assert parts.shape[2] == C and parts.shape[1] >= (layers - 1) * stride + row_off + r
    if r % 8 == 0:
        tr = max(t for t in range(8, 513, 8) if r % t == 0 and row_off % t == 0 and stride % t == 0)
    else:
        assert layers == 1
        tr = r
    nb = r // tr

    def body(p_ref, w_ref, m_ref, v_ref, g_ref, d_ref, nm_ref, nv_ref):
        g = p_ref[0].astype(F32)
        for j in range(1, N_DEV):
            g = g + p_ref[j].astype(F32)
        m_new = ADAM_B1 * m_ref[...] + (1.0 - ADAM_B1) * g
        v_new = ADAM_B2 * v_ref[...] + (1.0 - ADAM_B2) * (g * g)
        m_hat = m_new / (1.0 - ADAM_B1 ** ADAM_STEP)
        v_hat = v_new / (1.0 - ADAM_B2 ** ADAM_STEP)
        g_ref[...] = g
        d_ref[...] = -ADAM_LR * (m_hat / (jnp.sqrt(v_hat) + ADAM_EPS) + ADAM_WD * w_ref[...])
        nm_ref[...] = m_new
        nv_ref[...] = v_new

    spec = pl.BlockSpec((tr, C), lambda l, i: (l * nb + i, 0))
    pspec = pl.BlockSpec((N_DEV, tr, C), lambda l, i: (0, (l * stride + row_off) // tr + i, 0))
    return _call(name, body, (layers, nb), [pspec, spec, spec, spec], [spec] * 4,
                 [jax.ShapeDtypeStruct((R, C), F32)] * 4, [parts, w, m, v], sem=("parallel", "parallel"))


def _rot_cols(w):
    h = w.shape[-1] // 2
    return jnp.concatenate([-w[..., h:], w[..., :h]], axis=-1)


def _unrot_cols(d):
    h = d.shape[-1] // 2
    return jnp.concatenate([d[..., h:], -d[..., :h]], axis=-1)


IN_WIDTH = sum(IN_SIZES)
IN_SHARD = IN_WIDTH // N_DEV
IN_GATE = IN_WIDTH - IN_SIZES[-1]


def _w_in_cols(buf, a, b):
    out = []
    for j in range(a // IN_SHARD, (b - 1) // IN_SHARD + 1):
        lo, hi = max(a, IN_SHARD * j) - IN_SHARD * j, min(b, IN_SHARD * (j + 1)) - IN_SHARD * j
        out.append(buf[j, :, lo:hi])
    return out


def _w1(buf):
    kr0 = IN_SIZES[0] + IN_SIZES[1]
    kr = jnp.concatenate(_w_in_cols(buf, kr0, kr0 + QK_ROPE), axis=-1)
    w1s = jnp.concatenate(_w_in_cols(buf, 0, kr0) + [kr, _rot_cols(kr), jnp.zeros((D, 64), BF16)]
                          + _w_in_cols(buf, kr0 + QK_ROPE, IN_GATE), axis=-1)
    return w1s, jnp.concatenate(_w_in_cols(buf, IN_GATE, IN_WIDTH), axis=-1)


def _w_in_grad_shards(d1s, d1g):
    kr0 = IN_SIZES[0] + IN_SIZES[1]
    seg = d1s[:, C_KR:C_KR + 128]
    dkr = seg[:, :QK_ROPE] + _unrot_cols(seg[:, QK_ROPE:2 * QK_ROPE])
    runs = [(0, d1s, 0), (kr0, dkr, 0), (kr0 + QK_ROPE, d1s, C_PIN), (IN_GATE, d1g, 0), (IN_WIDTH, None, 0)]
    blocks = []
    for j in range(N_DEV):
        a, b = IN_SHARD * j, IN_SHARD * (j + 1)
        pieces = []
        for (c0, src, s0), (c1, _, _) in zip(runs[:-1], runs[1:]):
            lo, hi = max(a, c0), min(b, c1)
            if lo < hi:
                pieces.append(src[:, s0 + lo - c0:s0 + hi - c0])
        blocks.append(jnp.concatenate(pieces, axis=-1))
    return jnp.stack(blocks)


def _layer_weights(G):
    L = G['w_uq'].shape[0]
    wq = G['w_uq'].reshape(L, Q_LORA, HEADS, QK_NOPE + QK_ROPE)
    nope, rope = wq[..., :QK_NOPE], wq[..., QK_NOPE:]
    z32 = jnp.zeros((L, Q_LORA, HEADS, HP - QK_NOPE - QK_ROPE), BF16)
    wq_a = jnp.concatenate([nope, rope, z32], axis=-1).reshape(L, Q_LORA, QW)
    wq_b = jnp.concatenate([jnp.zeros_like(nope), _rot_cols(rope), z32], axis=-1).reshape(L, Q_LORA, QW)
    wkv = G['w_ukv'].reshape(L, KV_LORA, HEADS, QK_NOPE + V_HEAD)
    z64 = jnp.zeros((L, KV_LORA, HEADS, HP - QK_NOPE), BF16)
    wkn = jnp.concatenate([wkv[..., :QK_NOPE], z64], axis=-1).reshape(L, KV_LORA, QW)
    wv = jnp.concatenate([wkv[..., QK_NOPE:], z64], axis=-1).reshape(L, KV_LORA, QW)
    wa = G['w_br_a'].reshape(L, HEADS, V_HEAD, D)
    wa = jnp.concatenate([wa, jnp.zeros((L, HEADS, HP - V_HEAD, D), BF16)], axis=2).reshape(L, QW, D)
    return dict(
        Wq=jnp.concatenate([wq_a, wq_b], axis=-1),
        Wkv=jnp.concatenate([wkn, wv], axis=-1), Wa=wa, Wb=G['w_br_b'], Wc=G['w_br_c'], Wd=G['w_br_d'],
        conv=G['conv_w'].reshape(L, 3, BR).astype(F32))


def _tables(S):
    inv = ROPE_THETA ** (-jnp.arange(0, QK_ROPE, 2, dtype=F32) / QK_ROPE)
    ang = jnp.arange(S, dtype=F32)[:, None] * inv[None, :]
    cos, sin = jnp.cos(ang), jnp.sin(ang)
    scale = (QK_NOPE + QK_ROPE) ** -0.5
    one, zero = jnp.ones((S, QK_NOPE), F32), jnp.zeros((S, QK_NOPE), F32)
    z32 = jnp.zeros((S, HP - QK_NOPE - QK_ROPE), F32)
    cq = jnp.concatenate([one, cos, cos, z32], axis=1) * scale
    sq = jnp.concatenate([zero, sin, sin, z32], axis=1) * scale
    ck = jnp.concatenate([cos, cos, sin, sin, jnp.zeros((S, HP - 2 * QK_ROPE), F32)], axis=1)
    return cq, sq, ck


def _place_k():
    p = np.zeros((HP, QW), np.float32)
    for r in range(2 * QK_ROPE):
        for h in range(HEADS):
            p[r, h * HP + QK_NOPE + r % QK_ROPE] = 1.0
    return jnp.asarray(p, BF16)


def _layer_fwd(x, h, W, late, P, tabs, B, S, ride, g_next):
    cq_t, sq_t, ck_t, place = tabs
    g_cq, g_ckv = P['g_cq'], P['g_ckv']
    P1 = _mm("in_proj_s", h, W['W1s'], out_dtype=BF16)
    PG = _mm("in_proj_g", h, W['W1g'], out_dtype=BF16, comm=ride.get('in_proj'))

    def q_prep(c, cq, sq, g, w):
        qq = _dot(_rms(c, g), w)
        return qq[:, :QW] * jnp.tile(cq, (1, HEADS)) + qq[:, QW:] * jnp.tile(sq, (1, HEADS))
    q = _rw("q_prep", q_prep, [win(P1, Q_LORA, C_CQ // Q_LORA), tab(cq_t), tab(sq_t)], [g_cq, W['Wq']], [(QW, BF16)])[0]

    def kv_prep(c, seg, ck, g, w, place):
        kv = _dot(_rms(c, g), w)
        return kv[:, :QW] + _dot(seg * ck, place), kv[:, QW:]
    k, v = _rw("kv_prep", kv_prep, [win(P1, KV_LORA, C_CKV // KV_LORA), win(P1, HP, C_KR // HP), tab(ck_t)], [g_ckv, W['Wkv'], place],
               [(QW, BF16), (QW, BF16)])
    o, lse = _attn_fwd(q, k, v, B, S, comm=ride.get('attn_fwd'))

    def pool_f(z, wbd, scale):
        return _dot(_pooled(z), wbd) * scale
    bo = _rw("pool_fwd", pool_f, [win(P1, BR, C_PIN // BR)], [P['pool_bd'], P['pool_scale']], [(BR, BF16)], tile=S)[0]

    def sgu_f(u, sv, g, wm, bias):
        vn = _rms(sv, g)
        blocks = [vn[r:r + SGU_BLOCK] for r in range(0, vn.shape[0], SGU_BLOCK)]
        return u * jnp.concatenate([_sgu_mix(vb, wm, bias) for vb in blocks], axis=0)
    co = _rw("sgu_fwd", sgu_f, [win(P1, BR, C_SU // BR), win(P1, BR, C_SV // BR)], [P['g_sgu_v'], P['sgu_wm'], P['sgu_bias']],
             [(BR, BF16)], tile=SGU_TILE)[0]

    def conv_f(b, c, xi, w):
        z = c * xi
        return b * (w[0:1] * _shift_down(z, 2) + w[1:2] * _shift_down(z, 1) + w[2:3] * z)
    do_ = _rw("conv_fwd", conv_f, [win(P1, BR, C_CVB // BR), win(P1, BR, C_CVC // BR), win(P1, BR, C_CVX // BR)], [W['conv']], [(BR, BF16)], tile=S)[0]

    def merge_f(a, b, c, d, l0, l1, l2, l3, wa, wb, wc, wd):
        return (_sigmoid(l0) * _dot(a, wa) + _sigmoid(l1) * _dot(b, wb) + _sigmoid(l2) * _dot(c, wc)
                + _sigmoid(l3) * _dot(d, wd))
    merged = _rw("merge_fwd", merge_f, [o, bo, co, do_] + [win(PG, D, i) for i in range(4)],
                 [W['Wa'], W['Wb'], W['Wc'], W['Wd']], [(D, BF16)], comm=ride.get('merge_fwd'))[0]
    W = dict(W, **late())
    om, x1, h2 = _mm_norm("out_proj", merged, W['Wout'], x, P['g_post_mix'], P['g_pre_ffn'])
    ga, gb, act = _ffn_up(h2, W['Wg'], W['Wu'], comm=ride.get('ffn_up'))
    rest = ride['ffn_down']() if 'ffn_down' in ride else None
    f, x2, *h_next = _mm_norm("ffn_down", act, W['Wdn'], x1, P['g_post_ffn'], g_next, comm=rest)
    saved = dict(x=x, h=h, P1=P1, PG=PG, q=q, k=k, v=v, o=o, lse=lse, bo=bo, co=co, do=do_, merged=merged, om=om, x1=x1,
                 h2=h2, ga=ga, gb=gb, act=act, f=f)
    return x2, (h_next[0] if h_next else None), saved, W


def _sgu_mix(vn, wm, bias):
    outs = [_dot(wm[g], vn) for g in range(4)]
    return _group_select(outs) + bias


def _layer_bwd(dx2, df, sv, W, P, tabs, B, S, own_ride, prev):
    cq_t, sq_t, ck_t, place = tabs
    P1 = sv['P1']
    grads = {}
    da, db = _ffn_down_dx(df, W['Wdn'], sv['ga'], sv['gb'])
    grads['Wdn'] = _mm_t("ffn_down_dw", sv['act'], df)
    dh2 = _mm("ffn_up_dx", da, W['Wg'], tb=True, a2=db, b2=W['Wu'], out_dtype=BF16)
    grads['Wg'] = _mm_t("ffn_gate_dw", sv['h2'], da)
    grads['Wu'] = _mm_t("ffn_up_dw", sv['h2'], db)

    def mid_norms_b(x1, dh, dy, om, g_ffn, g_mix):
        dx, dg_ffn = _rms_bwd(x1, g_ffn, dh)
        dx1 = dy + dx
        dom, dg_mix = _rms_bwd(om, g_mix, dx1)
        return dx1, dom, dg_ffn, dg_mix
    dx1, dom, grads['g_pre_ffn'], grads['g_post_mix'] = _rw(
        "mid_norms_bwd", mid_norms_b, [sv['x1'], dh2, dx2, sv['om']], [P['g_pre_ffn'], P['g_post_mix']],
        [(D, F32), (D, BF16)], [(1, D), (1, D)], tile=512)
    dmerged = _mm("out_proj_dx", dom, W['Wout'], tb=True, out_dtype=BF16)
    grads['Wout'] = _mm_t("out_proj_dw", sv['merged'], dom)
    ride = own_ride('ffn', grads)

    def merge_b(a, b, c, d, l0, l1, l2, l3, dm, wa, wb, wc, wd):
        outs_dl, outs_dy, outs_dbr = [], [], []
        for br, l, w in ((a, l0, wa), (b, l1, wb), (c, l2, wc), (d, l3, wd)):
            s = _sigmoid(l)
            y = _dot(br, w)
            dy = (dm * s).astype(BF16)
            outs_dl.append((dm * y * s * (1.0 - s)).astype(BF16))
            outs_dy.append(dy)
            outs_dbr.append(_dot_nt(dy, w))
        return (jnp.concatenate(outs_dl, axis=1), *outs_dy, *outs_dbr)
    mb = _rw("merge_bwd", merge_b,
             [sv['o'], sv['bo'], sv['co'], sv['do']] + [win(sv['PG'], D, i) for i in range(4)] + [dmerged],
             [W['Wa'], W['Wb'], W['Wc'], W['Wd']],
             [(4 * D, BF16)] + [(D, BF16)] * 4 + [(QW, BF16), (BR, BF16), (BR, BF16), (BR, BF16)], comm=ride.get('merge_bwd'))
    dl, dy, (d_o, d_bo, d_co, d_do) = mb[0], mb[1:5], mb[5:9]
    grads['Wa'] = _mm_t("br_a_dw", sv['o'], dy[0])
    grads['Wb'] = _mm_t("br_b_dw", sv['bo'], dy[1])
    grads['Wc'] = _mm_t("br_c_dw", sv['co'], dy[2])
    grads['Wd'] = _mm_t("br_d_dw", sv['do'], dy[3])

    def conv_b(b, c, xi, dout, w):
        z = c * xi
        z1, z2 = _shift_down(z, 1), _shift_down(z, 2)
        y = w[0:1] * z2 + w[1:2] * z1 + w[2:3] * z
        dyv = dout * b
        dz = w[2:3] * dyv + w[1:2] * _shift_up(dyv, 1) + w[0:1] * _shift_up(dyv, 2)
        return (dout * y, dz * xi, dz * c, jnp.sum(dyv * z2, axis=0, keepdims=True),
                jnp.sum(dyv * z1, axis=0, keepdims=True), jnp.sum(dyv * z, axis=0, keepdims=True))
    dcvb, dcvc, dcvx, dw0, dw1, dw2 = _rw("conv_bwd", conv_b, [win(P1, BR, C_CVB // BR), win(P1, BR, C_CVC // BR), win(P1, BR, C_CVX // BR), d_do],
                                          [W['conv']], [(BR, BF16)] * 3, [(1, BR)] * 3, tile=S)
    grads['conv'] = jnp.concatenate([dw0, dw1, dw2], axis=0)

    def sgu_b(u, s_v, dout, g, wm, bias):
        vn = _rms(s_v, g)
        dmix = dout * u
        lane = lax.broadcasted_iota(jnp.int32, (SGU_BLOCK, BR), 1) // GROUP
        mixed, dvn, dwm, dbias = [], [], [0.0] * 4, 0.0
        for r in range(0, vn.shape[0], SGU_BLOCK):
            vb, db = vn[r:r + SGU_BLOCK], dmix[r:r + SGU_BLOCK]
            mixed.append(_sgu_mix(vb, wm, bias))
            dvn.append(_group_select([_dot_tn(wm[gi], db) for gi in range(4)]))
            dwm = [dwm[gi] + _dot_nt(jnp.where(lane == gi, db, 0.0), vb) for gi in range(4)]
            dbias = dbias + db
        dsv, dg = _rms_bwd(s_v, g, jnp.concatenate(dvn, axis=0))
        return (dout * jnp.concatenate(mixed, axis=0), dsv, dg, *dwm, dbias)
    dsu, dsv_, grads['g_sgu_v'], wm0, wm1, wm2, wm3, grads['sgu_bias'] = _rw(
        "sgu_bwd", sgu_b, [win(P1, BR, C_SU // BR), win(P1, BR, C_SV // BR), d_co], [P['g_sgu_v'], P['sgu_wm'], P['sgu_bias']],
        [(BR, BF16)] * 2, [(1, BR)] + [(SGU_BLOCK, SGU_BLOCK)] * 4 + [(SGU_BLOCK, BR)], tile=SGU_TILE)
    grads['sgu_wm'] = jnp.stack([wm0, wm1, wm2, wm3])

    def pool_b(z, dout, wbd, scale):
        pooled = _pooled(z)
        mixed = _dot(pooled, wbd)
        dmix = dout * scale
        dz = _pooled_bwd(_dot_nt(dmix, wbd))
        return dz, _dot_tn(pooled, dmix), jnp.sum(dout * mixed, axis=0, keepdims=True)
    dpin, grads['pool_bd'], grads['pool_scale'] = _rw(
        "pool_bwd", pool_b, [win(P1, BR, C_PIN // BR), d_bo], [P['pool_bd'], P['pool_scale']], [(BR, BF16)], [(BR, BR), (1, BR)], tile=S)

    dq, dk, dv = _attn_bwd(sv['q'], sv['k'], sv['v'], sv['o'], d_o, sv['lse'], B, S, comm=ride.get('attn_bwd'))

    def q_prep_b(c, dq, cq, sq, g, w):
        dqq = jnp.concatenate([dq * jnp.tile(cq, (1, HEADS)), dq * jnp.tile(sq, (1, HEADS))], axis=1).astype(BF16)
        dc, dg = _rms_bwd(c, g, _dot_nt(dqq, w))
        return dc, _rms(c, g), dqq, dg
    dcq, cqn, dqq, grads['g_cq'] = _rw("q_prep_bwd", q_prep_b, [win(P1, Q_LORA, C_CQ // Q_LORA), dq, tab(cq_t), tab(sq_t)],
                                       [P['g_cq'], W['Wq']], [(Q_LORA, BF16), (Q_LORA, BF16), (2 * QW, BF16)], [(1, Q_LORA)])
    grads['Wq'] = _mm_t("uq_dw", cqn, dqq)

    def kv_prep_b(c, dk, dv, ck, g, w, place):
        dc, dg = _rms_bwd(c, g, _dot_nt(dk, w[:, :QW]) + _dot_nt(dv, w[:, QW:]))
        return dc, _dot_nt(dk, place) * ck, _rms(c, g), dg
    dckv, dseg, kvn, grads['g_ckv'] = _rw(
        "kv_prep_bwd", kv_prep_b, [win(P1, KV_LORA, C_CKV // KV_LORA), dk, dv, tab(ck_t)], [P['g_ckv'], W['Wkv'], place],
        [(KV_LORA, BF16), (HP, BF16), (KV_LORA, BF16)], [(1, KV_LORA)])
    grads['Wkn'] = _mm_t("uk_dw", kvn, dk)
    grads['Wv'] = _mm_t("uv_dw", kvn, dv)
    ride = dict(ride, **own_ride('mixers', grads))

    dps = jnp.concatenate([dcq, dckv, dseg, dpin, dsu, dsv_, dcvb, dcvc, dcvx], axis=1)
    grads['W1s'] = _mm_t("in_proj_s_dw", sv['h'], dps)
    grads['W1g'] = _mm_t("in_proj_g_dw", sv['h'], dl, comm=ride.get('in_proj_g_dw'))
    ride = dict(ride, **own_ride('w_in', grads))
    dh = _mm("in_proj_dx", dps, W['W1s'], tb=True, a2=dl, b2=W['W1g'], out_dtype=BF16, tm=256, comm=ride.get('in_proj_dx'))
    ride = dict(ride, **own_ride('w_in_rest', grads))

    if prev is None:
        def pre_mix_b(x, dh, dy, g):
            dx, dg = _rms_bwd(x, g, dh)
            return dy + dx, dg
        dx, grads['g_pre_mix'] = _rw("pre_mix_bwd", pre_mix_b, [sv['x'], dh, dx1], [P['g_pre_mix']], [(D, F32)], [(1, D)],
                                     tile=512, comm=ride.get('edge'))
        return dx, None, None, grads, ride

    def edge_norms_b(x, dh, dy, f, g_mix, g_ffn):
        dx, dg_mix = _rms_bwd(x, g_mix, dh)
        dx = dy + dx
        df, dg_ffn = _rms_bwd(f, g_ffn, dx)
        return dx, df, dg_mix, dg_ffn
    dx, df_prev, grads['g_pre_mix'], dg_prev = _rw(
        "edge_norms_bwd", edge_norms_b, [sv['x'], dh, dx1, prev[0]], [P['g_pre_mix'], prev[1]],
        [(D, F32), (D, BF16)], [(1, D), (1, D)], tile=512, comm=ride.get('edge'))
    return dx, df_prev, dg_prev, grads, ride


def _mixer_grads(g):
    out = {}
    qa = g['Wq'][:, :QW].reshape(1, Q_LORA, HEADS, HP)
    qb = g['Wq'][:, QW:].reshape(1, Q_LORA, HEADS, HP)
    drope = qa[..., QK_NOPE:QK_NOPE + QK_ROPE] + _unrot_cols(qb[..., QK_NOPE:QK_NOPE + QK_ROPE])
    out['w_uq'] = jnp.concatenate([qa[..., :QK_NOPE], drope], axis=-1).reshape(1, Q_LORA, HEADS * (QK_NOPE + QK_ROPE))
    kn = g['Wkn'].reshape(1, KV_LORA, HEADS, HP)[..., :QK_NOPE]
    vv = g['Wv'].reshape(1, KV_LORA, HEADS, HP)[..., :V_HEAD]
    out['w_ukv'] = jnp.concatenate([kn, vv], axis=-1).reshape(1, KV_LORA, HEADS * (QK_NOPE + V_HEAD))
    out['w_br_a'] = g['Wa'].reshape(1, HEADS, HP, D)[:, :, :V_HEAD].reshape(1, HEADS * V_HEAD, D)
    out['w_br_b'], out['w_br_c'], out['w_br_d'] = g['Wb'][None], g['Wc'][None], g['Wd'][None]
    out['conv_w'] = g['conv'].reshape(1, 3, 1, BR)
    return out


def _small_grads(g):
    out = {}
    for n in ('g_pre_mix', 'g_cq', 'g_ckv', 'g_sgu_v', 'g_post_mix', 'g_pre_ffn', 'g_post_ffn', 'pool_scale'):
        out[n] = g[n].reshape(1, -1)
    bd = g['pool_bd'].reshape(4, GROUP, 4, GROUP)
    out['pool_w'] = jnp.stack([bd[i, :, i, :] for i in range(4)])[None]
    out['sgu_w'] = (g['sgu_wm'] * _sgu_mask()[None])[None]
    out['sgu_b'] = g['sgu_bias'].reshape(SGU_BLOCK, 4, GROUP).sum(-1).T[None]
    return out


def _sgu_mask():
    pc = np.arange(SGU_BLOCK) // CHUNK
    return jnp.asarray(pc[:, None] >= pc[None, :], F32)


def _rows(a, lead):
    return a.reshape(a.shape[:lead] + (-1, a.shape[-1]))


def _pad_to(a, rows, cols):
    pad = [(0, 0)] * (a.ndim - 2) + [(0, rows - a.shape[-2]), (0, cols - a.shape[-1])]
    return jnp.pad(a, pad)


GROUPS = (('w_in',), ('w_ffn_gate',), ('w_ffn_up',), ('w_out', 'w_ffn_down'),
          ('w_uq', 'w_ukv', 'w_br_a', 'w_br_b', 'w_br_c', 'w_br_d', 'conv_w'))
PADDED = {'w_uq': (Q_LORA, 128), 'conv_w': (128, 128), 'w_ffn_down': (384, D)}
NARROW = ('w_uq', 'conv_w')


def _group(t, lead, groups):
    bufs = {}
    for b in groups:
        ps = [_rows(t[n], lead) for n in GROUPS[b]]
        ps = [_pad_to(p, *PADDED[n]) if n in PADDED else p for n, p in zip(GROUPS[b], ps)]
        bufs[b] = ps[0] if len(ps) == 1 else jnp.concatenate(ps, axis=-2)
    return bufs


def _group_rows(shapes):
    where, rows = {}, []
    for b, names in enumerate(GROUPS):
        off = 0
        for n in names:
            where[n] = (b, off)
            off += PADDED.get(n, shapes[n])[0]
        rows.append(off)
    return where, rows


def _shard_split(a, axis):
    n = a.shape[axis]
    a = a.reshape(a.shape[:axis] + (N_DEV, n // N_DEV) + a.shape[axis + 1:])
    return jnp.moveaxis(a, axis, 0)


def _shard_join(a, axis):
    a = jnp.moveaxis(a, 0, axis)
    return a.reshape(a.shape[:axis] + (a.shape[axis] * a.shape[axis + 1],) + a.shape[axis + 2:])


def _as2d(a):
    return a.reshape(-1, a.shape[-1])


def kernel(x, w_in, g_pre_mix, g_cq, g_ckv, w_uq, w_ukv, pool_w, pool_scale, g_sgu_v, sgu_w, sgu_b, conv_w, w_br_a, w_br_b, w_br_c, w_br_d, w_out, g_post_mix, g_pre_ffn, w_ffn_gate, w_ffn_up, w_ffn_down, g_post_ffn, loss_target, m_w_in, m_g_pre_mix, m_g_cq, m_g_ckv, m_w_uq, m_w_ukv, m_pool_w, m_pool_scale, m_g_sgu_v, m_sgu_w, m_sgu_b, m_conv_w, m_w_br_a, m_w_br_b, m_w_br_c, m_w_br_d, m_w_out, m_g_post_mix, m_g_pre_ffn, m_w_ffn_gate, m_w_ffn_up, m_w_ffn_down, m_g_post_ffn, v_w_in, v_g_pre_mix, v_g_cq, v_g_ckv, v_w_uq, v_w_ukv, v_pool_w, v_pool_scale, v_g_sgu_v, v_sgu_w, v_sgu_b, v_conv_w, v_w_br_a, v_w_br_b, v_w_br_c, v_w_br_d, v_w_out, v_g_post_mix, v_g_pre_ffn, v_w_ffn_gate, v_w_ffn_up, v_w_ffn_down, v_g_post_ffn):
    args = dict(locals())
    w = {n: args[n] for n in WEIGHTS}
    m = {n: args['m_' + n] for n in WEIGHTS}
    v = {n: args['v_' + n] for n in WEIGHTS}
    B, S, _ = x.shape
    T = B * S
    L = DEPTH
    names = list(SHARDED)
    shapes1 = {n: _as2d(w[n][0]).shape for n in names}
    where, group_rows = _group_rows(shapes1)

    def shard_bufs(l, groups):
        return _group({n: w[n][l:l + 1].astype(BF16) for b in groups for n in GROUPS[b]}, 0, groups)

    def joined(gathered):
        G = {}
        for b, buf in gathered.items():
            for n in GROUPS[b]:
                if n == 'w_in':
                    continue
                off = where[n][1]
                r, c = shapes1[n]
                G[n] = _shard_join(buf[:, off:off + r, :c].reshape((N_DEV, 1) + w[n].shape[1:]), SHARDED[n])
        return G

    eye = jnp.eye(4, dtype=F32)
    pool_bd = (pool_w[:, :, :, None, :] * eye[None, :, None, :, None]).reshape(L, BR, BR).astype(BF16)
    sgu_wm = (sgu_w * _sgu_mask()[None, None]).astype(BF16)
    sgu_bias = jnp.repeat(sgu_b.transpose(0, 2, 1), GROUP, axis=2)
    tabs = _tables(S) + (_place_k(),)

    def layer_params(l):
        P = {n: w[n][l][None, :] for n in ('g_pre_mix', 'g_cq', 'g_ckv', 'g_sgu_v', 'g_post_mix', 'g_pre_ffn',
                                            'g_post_ffn', 'pool_scale')}
        P.update(pool_bd=pool_bd[l], sgu_wm=sgu_wm[l], sgu_bias=sgu_bias[l])
        return P

    xt = x.reshape(T, D)
    h = _rw("pre_mix", lambda x, g: _rms(x, g), [xt], [w['g_pre_mix'][0][None, :]], [(D, BF16)], tile=512)[0]
    saved, Ws = [], []
    first = shard_bufs(0, (0, 4))
    early = dict(zip((0, 4), _all_gather("gather_weights", [first[0], first[4]])))
    for l in range(L):
        own = shard_bufs(l, (1, 2, 3))
        nxt = shard_bufs(l + 1, (0, 4)) if l + 1 < L else None
        ride = {'in_proj': _Gather([own[1]]), 'attn_fwd': _Gather([own[3], own[2]])}
        w_in_rest = []
        if nxt:
            ride.update(ffn_up=_Gather([nxt[0]], ks=W_IN_FIRST), merge_fwd=_Gather([nxt[4]]))

            def rest(ride=ride, nxt=nxt, made=w_in_rest):
                made.append(_Gather([nxt[0]], ks=W_IN_REST, out=ride['ffn_up'].results))
                return made[0]
            ride['ffn_down'] = rest

        def late(ride=ride):
            G = joined({1: ride['in_proj'].results[0], 3: ride['attn_fwd'].results[0], 2: ride['attn_fwd'].results[1]})
            return dict(Wout=G['w_out'][0], Wg=G['w_ffn_gate'][0], Wu=G['w_ffn_up'][0], Wdn=G['w_ffn_down'][0])

        We = {k: a[0] for k, a in _layer_weights(joined(early)).items()}
        We['W1s'], We['W1g'] = _w1(early[0])
        g_next = w['g_pre_mix'][l + 1][None, :] if nxt else None
        xt, h, sv, W = _layer_fwd(xt, h, We, late, layer_params(l), tabs, B, S, ride, g_next)
        saved.append(sv)
        Ws.append(W)
        if nxt:
            early = {0: w_in_rest[0].results[0], 4: ride['merge_fwd'].results[0]}

    def loss_f(y, t, f, g):
        e = y - t
        dy = e * (1.0 / D)
        df, dg = _rms_bwd(f, g, dy)
        return dy, df, jnp.sum(e * e, axis=0, keepdims=True), dg
    dy, df, sq, dg_ffn = _rw("loss", loss_f, [xt, loss_target.reshape(T, D), saved[-1]['f']], [w['g_post_ffn'][L - 1][None, :]],
                             [(D, F32), (D, BF16)], [(1, D), (1, D)], tile=512)
    loss = lax.psum(0.5 * jnp.sum(sq) / D, ("x", "y", "c"))

    recv = [lax.empty((N_DEV, L * r, PADDED.get(ns[0], shapes1[ns[0]])[1]), BF16) for ns, r in zip(GROUPS, group_rows)]

    def send_bufs(pg, groups):
        return _group({n: _shard_split(pg[n], SHARDED[n]).astype(BF16) for b in groups for n in GROUPS[b]}, 1, groups)

    small, w_in_first = [None] * L, []
    for l in reversed(range(L)):
        def own_ride(stage, grads, l=l):
            if stage == 'ffn':
                bufs = send_bufs({'w_ffn_gate': grads['Wg'][None], 'w_ffn_up': grads['Wu'][None],
                                  'w_ffn_down': grads['Wdn'][None], 'w_out': grads['Wout'][None]}, (1, 2, 3))
                return {'merge_bwd': _Scatter([bufs[1]], [recv[1]], [l * group_rows[1]]),
                        'attn_bwd': _Scatter([bufs[3], bufs[2]], [recv[3], recv[2]], [l * group_rows[3], l * group_rows[2]])}
            if stage == 'mixers':
                bufs = send_bufs(_mixer_grads(grads), (4,))
                return {'in_proj_g_dw': _Scatter([bufs[4]], [recv[4]], [l * group_rows[4]])}
            if stage == 'w_in':
                shards = _w_in_grad_shards(grads['W1s'], grads['W1g'])
                w_in_first.append(_Scatter([shards], [recv[0]], [l * group_rows[0]], ks=W_IN_FIRST))
                return {'in_proj_dx': w_in_first[-1]}
            first = w_in_first[-1]
            return {'edge': _Scatter(first.inputs[:1], first.results, [l * group_rows[0]], ks=W_IN_REST)}

        prev = (saved[l - 1]['f'], w['g_post_ffn'][l - 1][None, :]) if l else None
        dy, df_prev, dg_prev, gl, ride = _layer_bwd(dy, df, saved[l], Ws[l], layer_params(l), tabs, B, S, own_ride, prev)
        gl['g_post_ffn'] = dg_ffn
        df, dg_ffn = df_prev, dg_prev
        for name, bs in (('in_proj_g_dw', (4,)), ('edge', (0,)), ('merge_bwd', (1,)), ('attn_bwd', (3, 2))):
            for b, res_b in zip(bs, ride[name].results):
                recv[b] = res_b
        pg = _small_grads(gl)
        rep = jnp.concatenate([pg[n].reshape(-1, 128) for n in REPLICATED], axis=0)
        small[l] = _pad_to(rep, -(-rep.shape[0] // 8) * 8, 128)
    grad_x = dy.reshape(B, S, D)

    small_rows = small[0].shape[0]
    got_small = _all_gather("gather_small_grads", [jnp.concatenate(small, axis=0)])[0].reshape(N_DEV, L, small_rows, 128)

    res = {}
    for n in names:
        b, off = where[n]
        r, c = shapes1[n]
        if n in NARROW:
            parts = recv[b].reshape(N_DEV, L, group_rows[b], -1)[:, :, off:off + r, :c].reshape(N_DEV, L * r, c)
            res[n] = _adamw("adamw_" + n, parts, _as2d(w[n]), _as2d(m[n]), _as2d(v[n]))
        else:
            res[n] = _adamw("adamw_" + n, recv[b], _as2d(w[n]), _as2d(m[n]), _as2d(v[n]), row_off=off, layers=L,
                            stride=group_rows[b])
    off = 0
    for n in REPLICATED:
        r = int(np.prod(w[n].shape[1:])) // 128
        shp2 = _as2d(w[n]).shape
        res[n] = _adamw("adamw_" + n, got_small[:, :, off:off + r].reshape((N_DEV,) + shp2), _as2d(w[n]), _as2d(m[n]), _as2d(v[n]))
        off += r
    outs = [loss, grad_x]
    for k in range(4):
        outs += [res[n][k].reshape(w[n].shape) for n in WEIGHTS]
    return tuple(outs)
```

```python
import functools

import numpy as np
import jax
import jax.numpy as jnp
from jax import lax
from jax.experimental import pallas as pl
from jax.experimental.pallas import tpu as pltpu

F32, BF16 = jnp.float32, jnp.bfloat16
MESH = pl.DeviceIdType.MESH
N_DEV = 8

D = 1024
DEPTH = 4
CHUNK = 64
EPS = 1e-6
NEG_INF = -1e30
HEADS, QK_NOPE, QK_ROPE, V_HEAD = 8, 64, 32, 64
Q_LORA, KV_LORA = 256, 128
ROPE_THETA = 10000.0
POOL_WINDOWS = (2, 4, 8, 16)
GROUP = 64
BR = 256
SGU_BLOCK = 128
SGU_TILE = 4 * SGU_BLOCK
D_FF = 2816
IN_SIZES = (256, 128, 32, 256, 256, 256, 256, 256, 256, 4096)
HP = 128
QW = HEADS * HP
C_CQ, C_CKV, C_KR, C_PIN, C_SU, C_SV, C_CVB, C_CVC, C_CVX, C_GATE = 0, 256, 384, 512, 768, 1024, 1280, 1536, 1792, 2048

ADAM_LR, ADAM_B1, ADAM_B2, ADAM_EPS, ADAM_WD, ADAM_STEP = 0.001, 0.9, 0.999, 1e-08, 0.01, 10

WEIGHTS = ['w_in', 'g_pre_mix', 'g_cq', 'g_ckv', 'w_uq', 'w_ukv', 'pool_w', 'pool_scale', 'g_sgu_v', 'sgu_w', 'sgu_b',
           'conv_w', 'w_br_a', 'w_br_b', 'w_br_c', 'w_br_d', 'w_out', 'g_post_mix', 'g_pre_ffn', 'w_ffn_gate',
           'w_ffn_up', 'w_ffn_down', 'g_post_ffn']
SHARDED = {'w_in': 2, 'w_uq': 2, 'w_ukv': 2, 'conv_w': 3, 'w_br_a': 2, 'w_br_b': 2, 'w_br_c': 2, 'w_br_d': 2,
           'w_out': 1, 'w_ffn_gate': 2, 'w_ffn_up': 2, 'w_ffn_down': 1}
REPLICATED = [n for n in WEIGHTS if n not in SHARDED]

VMEM_LIMIT = 56 * 1024 * 1024
TQ = 256


def _cparams(sem=None):
    return pltpu.CompilerParams(vmem_limit_bytes=VMEM_LIMIT, dimension_semantics=sem)


def _tile(n, cap, align=128):
    if n <= cap:
        return n
    for t in range(cap - cap % align, 0, -align):
        if n % t == 0:
            return t
    return n


def _peers():
    x_, y_, c_ = lax.axis_index("x"), lax.axis_index("y"), lax.axis_index("c")
    out = []
    for k in range(1, N_DEV):
        px, py, pc = (x_ + (k >> 2)) % 2, (y_ + ((k >> 1) & 1)) % 2, (c_ + (k & 1)) % 2
        out.append((k, (px, py, pc), 4 * px + 2 * py + pc))
    return 4 * x_ + 2 * y_ + c_, out


class _Exchange:
    def __init__(self, n):
        self.n = n
        self.scratch = [pltpu.SemaphoreType.DMA((7 * n,)), pltpu.SemaphoreType.DMA((7 * n,)),
                        pltpu.SemaphoreType.DMA((n,))]
        self.results = None

    def start(self, cin, cout, sems):
        local, sends, _ = self.copies(cin, cout, sems)
        for cp in local + sends:
            cp.start()

    def wait(self, cin, cout, sems):
        local, sends, recvs = self.copies(cin, cout, sems)
        for cp in recvs:
            cp.wait_recv()
        for cp in sends:
            cp.wait_send()
        for cp in local:
            cp.wait()


class _Gather(_Exchange):
    def __init__(self, xs, ks=range(N_DEV), out=()):
        super().__init__(len(xs))
        self.inputs = list(xs) + list(out)
        self.out_shape = [jax.ShapeDtypeStruct((N_DEV,) + x.shape, x.dtype) for x in xs]
        self.aliases = {self.n + a: a for a in range(len(out))}
        self.ks = tuple(ks)

    def copies(self, cin, cout, sems):
        send_sems, recv_sems, local_sems = sems
        me, peers = _peers()
        local, sends, recvs = [], [], []
        for a in range(self.n):
            if 0 in self.ks:
                local.append(pltpu.make_async_copy(cin[a], cout[a].at[me], local_sems.at[a]))
            for k, dev, flat in peers:
                if k not in self.ks:
                    continue
                sem = dict(send_sem=send_sems.at[7 * a + k - 1], recv_sem=recv_sems.at[7 * a + k - 1],
                           device_id=dev, device_id_type=MESH)
                sends.append(pltpu.make_async_remote_copy(src_ref=cin[a], dst_ref=cout[a].at[me], **sem))
                recvs.append(pltpu.make_async_remote_copy(src_ref=cin[a], dst_ref=cout[a].at[flat], **sem))
        return local, sends, recvs


class _Scatter(_Exchange):
    def __init__(self, xs, recv, row0, ks=range(N_DEV)):
        super().__init__(len(xs))
        self.inputs = list(xs) + list(recv)
        self.out_shape = [jax.ShapeDtypeStruct(r.shape, r.dtype) for r in recv]
        self.aliases = {self.n + a: a for a in range(self.n)}
        self.row0 = list(row0)
        self.ks = tuple(ks)

    def copies(self, cin, cout, sems):
        send_sems, recv_sems, local_sems = sems
        me, peers = _peers()
        local, sends, recvs = [], [], []
        for a in range(self.n):
            rows = pl.ds(self.row0[a], self.inputs[a].shape[1])
            if 0 in self.ks:
                local.append(pltpu.make_async_copy(cin[a].at[me], cout[a].at[me, rows], local_sems.at[a]))
            for k, dev, flat in peers:
                if k not in self.ks:
                    continue
                sem = dict(send_sem=send_sems.at[7 * a + k - 1], recv_sem=recv_sems.at[7 * a + k - 1],
                           device_id=dev, device_id_type=MESH)
                sends.append(pltpu.make_async_remote_copy(src_ref=cin[a].at[flat], dst_ref=cout[a].at[me, rows], **sem))
                recvs.append(pltpu.make_async_remote_copy(src_ref=cin[a].at[me], dst_ref=cout[a].at[flat, rows], **sem))
        return local, sends, recvs


def _call(name, body, grid, in_specs, out_specs, out_shape, arrays, scratch=(), sem=None, comm=None):
    if comm is None:
        res = pl.pallas_call(body, name=name, grid=grid, in_specs=list(in_specs), out_specs=list(out_specs),
                             out_shape=list(out_shape), scratch_shapes=list(scratch), compiler_params=_cparams(sem))(*arrays)
        return list(res)
    ni, no, ns = len(in_specs), len(out_specs), len(scratch)
    nci, nco = len(comm.inputs), len(comm.out_shape)
    hbm = pl.BlockSpec(memory_space=pl.ANY)

    def wrapped(*refs):
        ins, cin = refs[:ni], refs[ni:ni + nci]
        o0 = ni + nci
        outs, cout = refs[o0:o0 + no], refs[o0 + no:o0 + no + nco]
        s0 = o0 + no + nco
        scr, sems = refs[s0:s0 + ns], refs[s0 + ns:]
        ids = [pl.program_id(d) for d in range(len(grid))]
        first = functools.reduce(jnp.logical_and, [i == 0 for i in ids])
        last = functools.reduce(jnp.logical_and, [i == g - 1 for i, g in zip(ids, grid)])

        @pl.when(first)
        def _():
            comm.start(cin, cout, sems)

        body(*ins, *outs, *scr)

        @pl.when(last)
        def _():
            comm.wait(cin, cout, sems)

    res = pl.pallas_call(
        wrapped, name=name, grid=grid, in_specs=list(in_specs) + [hbm] * nci, out_specs=list(out_specs) + [hbm] * nco,
        out_shape=list(out_shape) + list(comm.out_shape), scratch_shapes=list(scratch) + comm.scratch,
        input_output_aliases={ni + i: no + j for i, j in comm.aliases.items()},
        compiler_params=_cparams(("arbitrary",) * len(grid)))(*arrays, *comm.inputs)
    comm.results = list(res[no:])
    return list(res[:no])


def win(arr, width, idx):
    return ('win', arr, width, idx)


def tab(arr):
    return ('tab', arr)


def _rw(name, fn, ins, consts, outs, accs=(), tile=256, comm=None):
    first = ins[0][1] if isinstance(ins[0], tuple) else ins[0]
    T = first.shape[0]
    tile = min(tile, T)
    grid = (T // tile,)
    arrays, in_specs = [], []
    for x in ins:
        if isinstance(x, tuple) and x[0] == 'win':
            _, a, w, k = x
            arrays.append(a)
            in_specs.append(pl.BlockSpec((tile, w), lambda i, k=k: (i, k)))
        elif isinstance(x, tuple) and x[0] == 'tab':
            a = x[1]
            nb = a.shape[0] // tile
            arrays.append(a)
            in_specs.append(pl.BlockSpec((tile, a.shape[1]), lambda i, nb=nb: (i % nb, 0)))
        else:
            arrays.append(x)
            in_specs.append(pl.BlockSpec((tile, x.shape[1]), lambda i: (i, 0)))
    for c in consts:
        arrays.append(c)
        in_specs.append(pl.BlockSpec(c.shape, lambda i, n=c.ndim: (0,) * n))
    out_shape = [jax.ShapeDtypeStruct((T, f), dt) for f, dt in outs]
    out_specs = [pl.BlockSpec((tile, f), lambda i: (i, 0)) for f, _ in outs]
    for shp in accs:
        out_shape.append(jax.ShapeDtypeStruct(shp, F32))
        out_specs.append(pl.BlockSpec(shp, lambda i, n=len(shp): (0,) * n))
    ni, nc, no = len(ins), len(consts), len(outs)

    def body(*refs):
        vals_in = [r[...] for r in refs[:ni]]
        vals_in = [v.astype(F32) if v.dtype == BF16 else v for v in vals_in]
        vals = fn(*vals_in, *[r[...] for r in refs[ni:ni + nc]])
        if not isinstance(vals, (tuple, list)):
            vals = (vals,)
        o_refs = refs[ni + nc:]
        for r, v in zip(o_refs[:no], vals[:no]):
            r[...] = v.astype(r.dtype)
        i = pl.program_id(0)
        for r, v in zip(o_refs[no:], vals[no:]):
            @pl.when(i == 0)
            def _(r=r, v=v):
                r[...] = v

            @pl.when(i > 0)
            def _(r=r, v=v):
                r[...] += v

    return _call(name, body, grid, in_specs, out_specs, out_shape, arrays, sem=("arbitrary",), comm=comm)


def _mm(name, a, b, tb=False, out_dtype=F32, tm=512, tn=3072, a2=None, b2=None, comm=None):
    M, K = a.shape
    N = b.shape[0] if tb else b.shape[1]
    tm, tn = _tile(M, tm, 8), _tile(N, tn)
    dims = (((1,), (1,)), ((), ())) if tb else (((1,), (0,)), ((), ()))
    pairs = [(a, b)] + ([(a2, b2)] if a2 is not None else [])

    def body(*refs):
        o_ref = refs[-1]
        acc = None
        for i in range(len(pairs)):
            p = lax.dot_general(refs[2 * i][...].astype(BF16), refs[2 * i + 1][...].astype(BF16), dims,
                                preferred_element_type=F32)
            acc = p if acc is None else acc + p
        o_ref[...] = acc.astype(o_ref.dtype)

    in_specs, arrays = [], []
    for x, y in pairs:
        k = x.shape[1]
        in_specs.append(pl.BlockSpec((tm, k), lambda j, i: (i, 0)))
        in_specs.append(pl.BlockSpec((tn, k), lambda j, i: (j, 0)) if tb else pl.BlockSpec((k, tn), lambda j, i: (0, j)))
        arrays += [x, y]
    return _call(name, body, (N // tn, M // tm), in_specs, [pl.BlockSpec((tm, tn), lambda j, i: (i, j))],
                 [jax.ShapeDtypeStruct((M, N), out_dtype)], arrays, sem=("parallel", "parallel"), comm=comm)[0]


def _mm_t(name, a, g, tk=1408, tn=1408, tt=2048, comm=None):
    T, K = a.shape
    N = g.shape[1]
    tk, tn, tt = _tile(K, tk), _tile(N, tn), _tile(T, tt, 8)
    nt = T // tt

    def body(a_ref, g_ref, o_ref, acc_ref):
        p = lax.dot_general(a_ref[...].astype(BF16), g_ref[...].astype(BF16), (((0,), (0,)), ((), ())),
                            preferred_element_type=F32)
        t = pl.program_id(2)

        @pl.when(t == 0)
        def _():
            acc_ref[...] = p

        @pl.when(t > 0)
        def _():
            acc_ref[...] += p

        @pl.when(t == nt - 1)
        def _():
            o_ref[...] = acc_ref[...].astype(BF16)

    return _call(name, body, (K // tk, N // tn, nt),
                 [pl.BlockSpec((tt, tk), lambda i, j, t: (t, i)), pl.BlockSpec((tt, tn), lambda i, j, t: (t, j))],
                 [pl.BlockSpec((tk, tn), lambda i, j, t: (i, j))], [jax.ShapeDtypeStruct((K, N), BF16)], [a, g],
                 scratch=[pltpu.VMEM((tk, tn), F32)], sem=("parallel", "parallel", "arbitrary"), comm=comm)[0]


def _dot(a, b):
    return jnp.dot(a.astype(BF16), b.astype(BF16), preferred_element_type=F32)


def _dot_nt(a, b):
    return lax.dot_general(a.astype(BF16), b.astype(BF16), (((1,), (1,)), ((), ())), preferred_element_type=F32)


def _dot_tn(a, b):
    return lax.dot_general(a.astype(BF16), b.astype(BF16), (((0,), (0,)), ((), ())), preferred_element_type=F32)


def _sigmoid(x):
    return 0.5 * jnp.tanh(0.5 * x) + 0.5


def _ffn_up(h, wg, wu, comm=None):
    M, K = h.shape
    N = wg.shape[1]
    tm, tn = _tile(M, 512, 8), _tile(N, 1408)

    def body(h_ref, wg_ref, wu_ref, ga_ref, gb_ref, act_ref):
        hb = h_ref[...]
        a = jnp.dot(hb, wg_ref[...], preferred_element_type=F32)
        b = jnp.dot(hb, wu_ref[...], preferred_element_type=F32)
        s = _sigmoid(a)
        silu = a * s
        ga_ref[...] = (b * (s + silu * (1.0 - s))).astype(BF16)
        gb_ref[...] = silu.astype(BF16)
        act_ref[...] = (silu * b).astype(BF16)

    wspec = pl.BlockSpec((K, tn), lambda j, i: (0, j))
    ospec = pl.BlockSpec((tm, tn), lambda j, i: (i, j))
    return _call("ffn_up", body, (N // tn, M // tm), [pl.BlockSpec((tm, K), lambda j, i: (i, 0)), wspec, wspec],
                 [ospec] * 3, [jax.ShapeDtypeStruct((M, N), BF16)] * 3, [h, wg, wu], sem=("parallel", "parallel"),
                 comm=comm)


def _ffn_down_dx(df, wdn, ga, gb):
    M, K = df.shape
    N = wdn.shape[0]
    tm, tn = _tile(M, 512, 8), _tile(N, 1408)

    def body(df_ref, w_ref, ga_ref, gb_ref, da_ref, db_ref):
        d = _dot_nt(df_ref[...], w_ref[...])
        da_ref[...] = (d * ga_ref[...].astype(F32)).astype(BF16)
        db_ref[...] = (d * gb_ref[...].astype(F32)).astype(BF16)

    tspec = pl.BlockSpec((tm, tn), lambda j, i: (i, j))
    return _call("ffn_down_dx", body, (N // tn, M // tm),
                 [pl.BlockSpec((tm, K), lambda j, i: (i, 0)), pl.BlockSpec((tn, K), lambda j, i: (j, 0)), tspec, tspec],
                 [tspec] * 2, [jax.ShapeDtypeStruct((M, N), BF16)] * 2, [df, wdn, ga, gb], sem=("parallel", "parallel"))


def _mm_norm(name, a, b, x, g, g_next=None, comm=None):
    M, K = a.shape
    N = b.shape[1]
    tm = _tile(M, 512, 8)
    more = g_next is not None

    def body(*refs):
        a_ref, b_ref, x_ref, g_ref = refs[:4]
        y_ref, xn_ref = refs[4 + more], refs[5 + more]
        y = jnp.dot(a_ref[...].astype(BF16), b_ref[...], preferred_element_type=F32)
        y_ref[...] = y.astype(BF16)
        xn = x_ref[...] + _rms(y, g_ref[...])
        xn_ref[...] = xn
        if more:
            refs[7][...] = _rms(xn, refs[4][...]).astype(BF16)

    row = lambda n: pl.BlockSpec((tm, n), lambda i: (i, 0))
    whole = lambda z: pl.BlockSpec(z.shape, lambda i: (0, 0))
    arrays = [a, b, x, g] + ([g_next] if more else [])
    in_specs = [row(K), whole(b), row(N), whole(g)] + ([whole(g_next)] if more else [])
    out_shape = [jax.ShapeDtypeStruct((M, N), dt) for dt in (BF16, F32) + ((BF16,) if more else ())]
    return _call(name, body, (M // tm,), in_specs, [row(N)] * (2 + more), out_shape, arrays, sem=("parallel",), comm=comm)


def _rms(x, g):
    r = lax.rsqrt(jnp.mean(x * x, axis=-1, keepdims=True) + EPS)
    return x * r * g


def _rms_bwd(x, g, dy):
    r = lax.rsqrt(jnp.mean(x * x, axis=-1, keepdims=True) + EPS)
    xh = x * r
    dxh = dy * g
    dx = r * (dxh - xh * jnp.mean(dxh * xh, axis=-1, keepdims=True))
    return dx, jnp.sum(dy * xh, axis=0, keepdims=True)


def _shift_down(x, k):
    t = lax.broadcasted_iota(jnp.int32, x.shape, 0)
    return jnp.where(t >= k, pltpu.roll(x, k, 0), 0.0)


def _shift_up(x, k):
    n = x.shape[0]
    t = lax.broadcasted_iota(jnp.int32, x.shape, 0)
    return jnp.where(t < n - k, pltpu.roll(x, n - k, 0), 0.0)


def _group_select(vals):
    lane = lax.broadcasted_iota(jnp.int32, vals[0].shape, 1) // GROUP
    out = vals[-1]
    for g in range(len(vals) - 2, -1, -1):
        out = jnp.where(lane == g, vals[g], out)
    return out


def _pool_count(shape):
    t = lax.broadcasted_iota(jnp.int32, shape, 0)
    lane = lax.broadcasted_iota(jnp.int32, shape, 1) // GROUP
    w = jnp.where(lane == 0, POOL_WINDOWS[0], jnp.where(lane == 1, POOL_WINDOWS[1],
                  jnp.where(lane == 2, POOL_WINDOWS[2], POOL_WINDOWS[3])))
    return jnp.minimum(t + 1, w).astype(F32)


def _pooled(z):
    s = [z]
    for k in (1, 2, 4, 8):
        s.append(s[-1] + _shift_down(s[-1], k))
    return _group_select(s[1:]) / _pool_count(z.shape) - z


def _pooled_bwd(dp):
    u = dp / _pool_count(dp.shape)
    s = [u]
    for k in (1, 2, 4, 8):
        s.append(s[-1] + _shift_up(s[-1], k))
    return _group_select(s[1:]) - dp


def _diag_mask():
    r = lax.broadcasted_iota(jnp.int32, (TQ, TQ), 0) // CHUNK
    c = lax.broadcasted_iota(jnp.int32, (TQ, TQ), 1) // CHUNK
    return r >= c


def _attn_fwd(q, k, v, B, S, comm=None):
    nq = S // TQ

    def body(q_ref, k_ref, v_ref, o_ref, lse_ref):
        mask = _diag_mask()
        for i in range(nq):
            r0, r1 = i * TQ, (i + 1) * TQ
            qb = q_ref[r0:r1, :]
            sd = jnp.where(mask, _dot_nt(qb, k_ref[r0:r1, :]), NEG_INF)
            m = jnp.max(sd, axis=-1, keepdims=True)
            if i:
                so = _dot_nt(qb, k_ref[0:r0, :])
                m = jnp.maximum(m, jnp.max(so, axis=-1, keepdims=True))
            pd = jnp.exp(sd - m)
            l = jnp.sum(pd, axis=-1, keepdims=True)
            acc = _dot(pd, v_ref[r0:r1, :])
            if i:
                po = jnp.exp(so - m)
                l = l + jnp.sum(po, axis=-1, keepdims=True)
                acc = acc + _dot(po, v_ref[0:r0, :])
            o_ref[r0:r1, :] = (acc / l).astype(BF16)
            lse_ref[r0:r1, :] = jnp.broadcast_to(m + jnp.log(l), (TQ, HP))

    spec = pl.BlockSpec((S, HP), lambda b, h: (b, h))
    return _call("attn_fwd", body, (B, HEADS), [spec] * 3, [spec] * 2,
                 [jax.ShapeDtypeStruct((B * S, QW), BF16), jax.ShapeDtypeStruct((B * S, QW), F32)],
                 [q, k, v], sem=("parallel", "parallel"), comm=comm)


def _attn_bwd(q, k, v, o, do, lse, B, S, comm=None):
    nq = S // TQ

    def body(q_ref, k_ref, v_ref, o_ref, do_ref, lse_ref, dq_ref, dk_ref, dv_ref, dk_acc, dv_acc):
        mask = _diag_mask()
        dk_acc[...] = jnp.zeros_like(dk_acc)
        dv_acc[...] = jnp.zeros_like(dv_acc)
        for i in range(nq):
            r0, r1 = i * TQ, (i + 1) * TQ
            qb, dob = q_ref[r0:r1, :], do_ref[r0:r1, :]
            delta = jnp.sum(o_ref[r0:r1, :].astype(F32) * dob.astype(F32), axis=-1, keepdims=True)
            lse_b = lse_ref[r0:r1, :][:, :1]
            dq = None
            for c0, c1, diag in ([(0, r0, False)] if i else []) + [(r0, r1, True)]:
                kb, vb = k_ref[c0:c1, :], v_ref[c0:c1, :]
                p = jnp.exp(_dot_nt(qb, kb) - lse_b)
                if diag:
                    p = jnp.where(mask, p, 0.0)
                ds = p * (_dot_nt(dob, vb) - delta)
                part = _dot(ds, kb)
                dq = part if dq is None else dq + part
                dk_acc[c0:c1, :] += _dot_tn(ds, qb)
                dv_acc[c0:c1, :] += _dot_tn(p, dob)
            dq_ref[r0:r1, :] = dq
        dk_ref[...] = dk_acc[...].astype(BF16)
        dv_ref[...] = dv_acc[...].astype(BF16)

    spec = pl.BlockSpec((S, HP), lambda b, h: (b, h))
    return _call("attn_bwd", body, (B, HEADS), [spec] * 6, [spec] * 3,
                 [jax.ShapeDtypeStruct((B * S, QW), F32)] + [jax.ShapeDtypeStruct((B * S, QW), BF16)] * 2,
                 [q, k, v, o, do, lse], scratch=[pltpu.VMEM((S, HP), F32)] * 2, sem=("parallel", "parallel"), comm=comm)


def _all_gather(name, xs):
    n = len(xs)

    def body(*refs):
        x_refs, out_refs = refs[:n], refs[n:2 * n]
        send_sems, recv_sems, local_sems = refs[2 * n:]
        x_, y_, c_ = lax.axis_index("x"), lax.axis_index("y"), lax.axis_index("c")
        me, sibling = (x_, y_, c_), (x_, y_, 1 - c_)
        chips = [(1 - x_, y_), (x_, 1 - y_), (1 - x_, 1 - y_)]

        def copy(a, k, block, to, own=False):
            slot = out_refs[a].at[4 * block[0] + 2 * block[1] + block[2]]
            return pltpu.make_async_remote_copy(
                src_ref=x_refs[a] if own else slot, dst_ref=slot,
                send_sem=send_sems.at[7 * a + k], recv_sem=recv_sems.at[7 * a + k], device_id=to, device_id_type=MESH)

        mine = [pltpu.make_async_copy(x_refs[a], out_refs[a].at[4 * x_ + 2 * y_ + c_], local_sems.at[a]) for a in range(n)]
        for cp in mine:
            cp.start()
        first = [copy(a, 1 + j, me, (*chip, c_), own=True) for j, chip in enumerate(chips) for a in range(n)]
        first += [copy(a, 0, me, sibling, own=True) for a in range(n)]
        for cp in first:
            cp.start()
        passed = []
        for j, chip in enumerate(chips):
            for a in range(n):
                copy(a, 1 + j, (*chip, c_), me).wait_recv()
                passed.append(copy(a, 4 + j, (*chip, c_), sibling))
                passed[-1].start()
        for a in range(n):
            copy(a, 0, sibling, me).wait_recv()
            for j, chip in enumerate(chips):
                copy(a, 4 + j, (*chip, 1 - c_), me).wait_recv()
        for cp in first + passed:
            cp.wait_send()
        for cp in mine:
            cp.wait()

    return pl.pallas_call(
        body, name=name, out_shape=[jax.ShapeDtypeStruct((N_DEV,) + x.shape, x.dtype) for x in xs],
        in_specs=[pl.BlockSpec(memory_space=pl.ANY)] * n, out_specs=[pl.BlockSpec(memory_space=pl.ANY)] * n,
        scratch_shapes=[pltpu.SemaphoreType.DMA((7 * n,)), pltpu.SemaphoreType.DMA((7 * n,)), pltpu.SemaphoreType.DMA((n,))],
    )(*xs)


def _adamw(name, parts, w, m, v, row_off=0, layers=1, stride=0):
    R, C = w.shape
    r = R // layers
    assert parts.shape[2] == C and parts.shape[1] >= (layers - 1) * stride + row_off + r
    if r % 8 == 0:
        tr = max(t for t in range(8, 513, 8) if r % t == 0 and row_off % t == 0 and stride % t == 0)
    else:
        assert layers == 1
        tr = r
    nb = r // tr

    def body(p_ref, w_ref, m_ref, v_ref, g_ref, d_ref, nm_ref, nv_ref):
        g = p_ref[0].astype(F32)
        for j in range(1, N_DEV):
            g = g + p_ref[j].astype(F32)
        m_new = ADAM_B1 * m_ref[...] + (1.0 - ADAM_B1) * g
        v_new = ADAM_B2 * v_ref[...] + (1.0 - ADAM_B2) * (g * g)
        m_hat = m_new / (1.0 - ADAM_B1 ** ADAM_STEP)
        v_hat = v_new / (1.0 - ADAM_B2 ** ADAM_STEP)
        g_ref[...] = g
        d_ref[...] = -ADAM_LR * (m_hat / (jnp.sqrt(v_hat) + ADAM_EPS) + ADAM_WD * w_ref[...])
        nm_ref[...] = m_new
        nv_ref[...] = v_new

    spec = pl.BlockSpec((tr, C), lambda l, i: (l * nb + i, 0))
    pspec = pl.BlockSpec((N_DEV, tr, C), lambda l, i: (0, (l * stride + row_off) // tr + i, 0))
    return _call(name, body, (layers, nb), [pspec, spec, spec, spec], [spec] * 4,
                 [jax.ShapeDtypeStruct((R, C), F32)] * 4, [parts, w, m, v], sem=("parallel", "parallel"))


def _rot_cols(w):
    h = w.shape[-1] // 2
    return jnp.concatenate([-w[..., h:], w[..., :h]], axis=-1)


def _unrot_cols(d):
    h = d.shape[-1] // 2
    return jnp.concatenate([d[..., h:], -d[..., :h]], axis=-1)


IN_WIDTH = sum(IN_SIZES)
IN_SHARD = IN_WIDTH // N_DEV
IN_GATE = IN_WIDTH - IN_SIZES[-1]


def _w_in_cols(buf, a, b):
    out = []
    for j in range(a // IN_SHARD, (b - 1) // IN_SHARD + 1):
        lo, hi = max(a, IN_SHARD * j) - IN_SHARD * j, min(b, IN_SHARD * (j + 1)) - IN_SHARD * j
        out.append(buf[j, :, lo:hi])
    return out


def _w1(buf):
    kr0 = IN_SIZES[0] + IN_SIZES[1]
    kr = jnp.concatenate(_w_in_cols(buf, kr0, kr0 + QK_ROPE), axis=-1)
    w1s = jnp.concatenate(_w_in_cols(buf, 0, kr0) + [kr, _rot_cols(kr), jnp.zeros((D, 64), BF16)]
                          + _w_in_cols(buf, kr0 + QK_ROPE, IN_GATE), axis=-1)
    return w1s, jnp.concatenate(_w_in_cols(buf, IN_GATE, IN_WIDTH), axis=-1)


def _w_in_grad_shards(d1s, d1g):
    kr0 = IN_SIZES[0] + IN_SIZES[1]
    seg = d1s[:, C_KR:C_KR + 128]
    dkr = seg[:, :QK_ROPE] + _unrot_cols(seg[:, QK_ROPE:2 * QK_ROPE])
    runs = [(0, d1s, 0), (kr0, dkr, 0), (kr0 + QK_ROPE, d1s, C_PIN), (IN_GATE, d1g, 0), (IN_WIDTH, None, 0)]
    blocks = []
    for j in range(N_DEV):
        a, b = IN_SHARD * j, IN_SHARD * (j + 1)
        pieces = []
        for (c0, src, s0), (c1, _, _) in zip(runs[:-1], runs[1:]):
            lo, hi = max(a, c0), min(b, c1)
            if lo < hi:
                pieces.append(src[:, s0 + lo - c0:s0 + hi - c0])
        blocks.append(jnp.concatenate(pieces, axis=-1))
    return jnp.stack(blocks)


def _layer_weights(G):
    L = G['w_uq'].shape[0]
    wq = G['w_uq'].reshape(L, Q_LORA, HEADS, QK_NOPE + QK_ROPE)
    nope, rope = wq[..., :QK_NOPE], wq[..., QK_NOPE:]
    z32 = jnp.zeros((L, Q_LORA, HEADS, HP - QK_NOPE - QK_ROPE), BF16)
    wq_a = jnp.concatenate([nope, rope, z32], axis=-1).reshape(L, Q_LORA, QW)
    wq_b = jnp.concatenate([jnp.zeros_like(nope), _rot_cols(rope), z32], axis=-1).reshape(L, Q_LORA, QW)
    wkv = G['w_ukv'].reshape(L, KV_LORA, HEADS, QK_NOPE + V_HEAD)
    z64 = jnp.zeros((L, KV_LORA, HEADS, HP - QK_NOPE), BF16)
    wkn = jnp.concatenate([wkv[..., :QK_NOPE], z64], axis=-1).reshape(L, KV_LORA, QW)
    wv = jnp.concatenate([wkv[..., QK_NOPE:], z64], axis=-1).reshape(L, KV_LORA, QW)
    wa = G['w_br_a'].reshape(L, HEADS, V_HEAD, D)
    wa = jnp.concatenate([wa, jnp.zeros((L, HEADS, HP - V_HEAD, D), BF16)], axis=2).reshape(L, QW, D)
    return dict(
        Wq=jnp.concatenate([wq_a, wq_b], axis=-1),
        Wkv=jnp.concatenate([wkn, wv], axis=-1), Wa=wa, Wb=G['w_br_b'], Wc=G['w_br_c'], Wd=G['w_br_d'],
        conv=G['conv_w'].reshape(L, 3, BR).astype(F32))


def _tables(S):
    inv = ROPE_THETA ** (-jnp.arange(0, QK_ROPE, 2, dtype=F32) / QK_ROPE)
    ang = jnp.arange(S, dtype=F32)[:, None] * inv[None, :]
    cos, sin = jnp.cos(ang), jnp.sin(ang)
    scale = (QK_NOPE + QK_ROPE) ** -0.5
    one, zero = jnp.ones((S, QK_NOPE), F32), jnp.zeros((S, QK_NOPE), F32)
    z32 = jnp.zeros((S, HP - QK_NOPE - QK_ROPE), F32)
    cq = jnp.concatenate([one, cos, cos, z32], axis=1) * scale
    sq = jnp.concatenate([zero, sin, sin, z32], axis=1) * scale
    ck = jnp.concatenate([cos, cos, sin, sin, jnp.zeros((S, HP - 2 * QK_ROPE), F32)], axis=1)
    return cq, sq, ck


def _place_k():
    p = np.zeros((HP, QW), np.float32)
    for r in range(2 * QK_ROPE):
        for h in range(HEADS):
            p[r, h * HP + QK_NOPE + r % QK_ROPE] = 1.0
    return jnp.asarray(p, BF16)


def _layer_fwd(x, h, W, late, P, tabs, B, S, ride, g_next):
    cq_t, sq_t, ck_t, place = tabs
    g_cq, g_ckv = P['g_cq'], P['g_ckv']
    P1 = _mm("in_proj_s", h, W['W1s'], out_dtype=BF16)
    PG = _mm("in_proj_g", h, W['W1g'], out_dtype=BF16, comm=ride.get('in_proj'))

    def q_prep(c, cq, sq, g, w):
        qq = _dot(_rms(c, g), w)
        return qq[:, :QW] * jnp.tile(cq, (1, HEADS)) + qq[:, QW:] * jnp.tile(sq, (1, HEADS))
    q = _rw("q_prep", q_prep, [win(P1, Q_LORA, C_CQ // Q_LORA), tab(cq_t), tab(sq_t)], [g_cq, W['Wq']], [(QW, BF16)], tile=512)[0]

    def kv_prep(c, seg, ck, g, w, place):
        kv = _dot(_rms(c, g), w)
        return kv[:, :QW] + _dot(seg * ck, place), kv[:, QW:]
    k, v = _rw("kv_prep", kv_prep, [win(P1, KV_LORA, C_CKV // KV_LORA), win(P1, HP, C_KR // HP), tab(ck_t)], [g_ckv, W['Wkv'], place],
               [(QW, BF16), (QW, BF16)], tile=512)
    o, lse = _attn_fwd(q, k, v, B, S, comm=ride.get('attn_fwd'))

    def pool_f(z, wbd, scale):
        return _dot(_pooled(z), wbd) * scale
    bo = _rw("pool_fwd", pool_f, [win(P1, BR, C_PIN // BR)], [P['pool_bd'], P['pool_scale']], [(BR, BF16)], tile=S)[0]

    def sgu_f(u, sv, g, wm, bias):
        vn = _rms(sv, g)
        blocks = [vn[r:r + SGU_BLOCK] for r in range(0, vn.shape[0], SGU_BLOCK)]
        return u * jnp.concatenate([_sgu_mix(vb, wm, bias) for vb in blocks], axis=0)
    co = _rw("sgu_fwd", sgu_f, [win(P1, BR, C_SU // BR), win(P1, BR, C_SV // BR)], [P['g_sgu_v'], P['sgu_wm'], P['sgu_bias']],
             [(BR, BF16)], tile=SGU_TILE)[0]

    def conv_f(b, c, xi, w):
        z = c * xi
        return b * (w[0:1] * _shift_down(z, 2) + w[1:2] * _shift_down(z, 1) + w[2:3] * z)
    do_ = _rw("conv_fwd", conv_f, [win(P1, BR, C_CVB // BR), win(P1, BR, C_CVC // BR), win(P1, BR, C_CVX // BR)], [W['conv']], [(BR, BF16)], tile=S)[0]

    def merge_f(a, b, c, d, l0, l1, l2, l3, wa, wb, wc, wd):
        return (_sigmoid(l0) * _dot(a, wa) + _sigmoid(l1) * _dot(b, wb) + _sigmoid(l2) * _dot(c, wc)
                + _sigmoid(l3) * _dot(d, wd))
    merged = _rw("merge_fwd", merge_f, [o, bo, co, do_] + [win(PG, D, i) for i in range(4)],
                 [W['Wa'], W['Wb'], W['Wc'], W['Wd']], [(D, BF16)], tile=512, comm=ride.get('merge_fwd'))[0]
    W = dict(W, **late())
    om, x1, h2 = _mm_norm("out_proj", merged, W['Wout'], x, P['g_post_mix'], P['g_pre_ffn'])
    ga, gb, act = _ffn_up(h2, W['Wg'], W['Wu'], comm=ride.get('ffn_up'))
    rest = ride['ffn_down']() if 'ffn_down' in ride else None
    f, x2, *h_next = _mm_norm("ffn_down", act, W['Wdn'], x1, P['g_post_ffn'], g_next, comm=rest)
    saved = dict(x=x, h=h, P1=P1, PG=PG, q=q, k=k, v=v, o=o, lse=lse, bo=bo, co=co, do=do_, merged=merged, om=om, x1=x1,
                 h2=h2, ga=ga, gb=gb, act=act, f=f)
    return x2, (h_next[0] if h_next else None), saved, W


def _sgu_mix(vn, wm, bias):
    outs = [_dot(wm[g], vn) for g in range(4)]
    return _group_select(outs) + bias


def _layer_bwd(dx2, df, sv, W, P, tabs, B, S, own_ride, prev):
    cq_t, sq_t, ck_t, place = tabs
    P1 = sv['P1']
    grads = {}
    da, db = _ffn_down_dx(df, W['Wdn'], sv['ga'], sv['gb'])
    grads['Wdn'] = _mm_t("ffn_down_dw", sv['act'], df)
    dh2 = _mm("ffn_up_dx", da, W['Wg'], tb=True, a2=db, b2=W['Wu'], out_dtype=BF16)
    grads['Wg'] = _mm_t("ffn_gate_dw", sv['h2'], da)
    grads['Wu'] = _mm_t("ffn_up_dw", sv['h2'], db)

    def mid_norms_b(x1, dh, dy, om, g_ffn, g_mix):
        dx, dg_ffn = _rms_bwd(x1, g_ffn, dh)
        dx1 = dy + dx
        dom, dg_mix = _rms_bwd(om, g_mix, dx1)
        return dx1, dom, dg_ffn, dg_mix
    dx1, dom, grads['g_pre_ffn'], grads['g_post_mix'] = _rw(
        "mid_norms_bwd", mid_norms_b, [sv['x1'], dh2, dx2, sv['om']], [P['g_pre_ffn'], P['g_post_mix']],
        [(D, F32), (D, BF16)], [(1, D), (1, D)], tile=512)
    dmerged = _mm("out_proj_dx", dom, W['Wout'], tb=True, out_dtype=BF16)
    grads['Wout'] = _mm_t("out_proj_dw", sv['merged'], dom)
    ride = own_ride('ffn', grads)

    def merge_b(a, b, c, d, l0, l1, l2, l3, dm, wa, wb, wc, wd):
        outs_dl, outs_dy, outs_dbr = [], [], []
        for br, l, w in ((a, l0, wa), (b, l1, wb), (c, l2, wc), (d, l3, wd)):
            s = _sigmoid(l)
            y = _dot(br, w)
            dy = (dm * s).astype(BF16)
            outs_dl.append((dm * y * s * (1.0 - s)).astype(BF16))
            outs_dy.append(dy)
            outs_dbr.append(_dot_nt(dy, w))
        return (jnp.concatenate(outs_dl, axis=1), *outs_dy, *outs_dbr)
    mb = _rw("merge_bwd", merge_b,
             [sv['o'], sv['bo'], sv['co'], sv['do']] + [win(sv['PG'], D, i) for i in range(4)] + [dmerged],
             [W['Wa'], W['Wb'], W['Wc'], W['Wd']],
             [(4 * D, BF16)] + [(D, BF16)] * 4 + [(QW, BF16), (BR, BF16), (BR, BF16), (BR, BF16)], comm=ride.get('merge_bwd'))
    dl, dy, (d_o, d_bo, d_co, d_do) = mb[0], mb[1:5], mb[5:9]
    grads['Wa'] = _mm_t("br_a_dw", sv['o'], dy[0])
    grads['Wb'] = _mm_t("br_b_dw", sv['bo'], dy[1])
    grads['Wc'] = _mm_t("br_c_dw", sv['co'], dy[2])
    grads['Wd'] = _mm_t("br_d_dw", sv['do'], dy[3])

    def conv_b(b, c, xi, dout, w):
        z = c * xi
        z1, z2 = _shift_down(z, 1), _shift_down(z, 2)
        y = w[0:1] * z2 + w[1:2] * z1 + w[2:3] * z
        dyv = dout * b
        dz = w[2:3] * dyv + w[1:2] * _shift_up(dyv, 1) + w[0:1] * _shift_up(dyv, 2)
        return (dout * y, dz * xi, dz * c, jnp.sum(dyv * z2, axis=0, keepdims=True),
                jnp.sum(dyv * z1, axis=0, keepdims=True), jnp.sum(dyv * z, axis=0, keepdims=True))
    dcvb, dcvc, dcvx, dw0, dw1, dw2 = _rw("conv_bwd", conv_b, [win(P1, BR, C_CVB // BR), win(P1, BR, C_CVC // BR), win(P1, BR, C_CVX // BR), d_do],
                                          [W['conv']], [(BR, BF16)] * 3, [(1, BR)] * 3, tile=S)
    grads['conv'] = jnp.concatenate([dw0, dw1, dw2], axis=0)

    def sgu_b(u, s_v, dout, g, wm, bias):
        vn = _rms(s_v, g)
        dmix = dout * u
        lane = lax.broadcasted_iota(jnp.int32, (SGU_BLOCK, BR), 1) // GROUP
        mixed, dvn, dwm, dbias = [], [], [0.0] * 4, 0.0
        for r in range(0, vn.shape[0], SGU_BLOCK):
            vb, db = vn[r:r + SGU_BLOCK], dmix[r:r + SGU_BLOCK]
            mixed.append(_sgu_mix(vb, wm, bias))
            dvn.append(_group_select([_dot_tn(wm[gi], db) for gi in range(4)]))
            dwm = [dwm[gi] + _dot_nt(jnp.where(lane == gi, db, 0.0), vb) for gi in range(4)]
            dbias = dbias + db
        dsv, dg = _rms_bwd(s_v, g, jnp.concatenate(dvn, axis=0))
        return (dout * jnp.concatenate(mixed, axis=0), dsv, dg, *dwm, dbias)
    dsu, dsv_, grads['g_sgu_v'], wm0, wm1, wm2, wm3, grads['sgu_bias'] = _rw(
        "sgu_bwd", sgu_b, [win(P1, BR, C_SU // BR), win(P1, BR, C_SV // BR), d_co], [P['g_sgu_v'], P['sgu_wm'], P['sgu_bias']],
        [(BR, BF16)] * 2, [(1, BR)] + [(SGU_BLOCK, SGU_BLOCK)] * 4 + [(SGU_BLOCK, BR)], tile=SGU_TILE)
    grads['sgu_wm'] = jnp.stack([wm0, wm1, wm2, wm3])

    def pool_b(z, dout, wbd, scale):
        pooled = _pooled(z)
        mixed = _dot(pooled, wbd)
        dmix = dout * scale
        dz = _pooled_bwd(_dot_nt(dmix, wbd))
        return dz, _dot_tn(pooled, dmix), jnp.sum(dout * mixed, axis=0, keepdims=True)
    dpin, grads['pool_bd'], grads['pool_scale'] = _rw(
        "pool_bwd", pool_b, [win(P1, BR, C_PIN // BR), d_bo], [P['pool_bd'], P['pool_scale']], [(BR, BF16)], [(BR, BR), (1, BR)], tile=S)

    dq, dk, dv = _attn_bwd(sv['q'], sv['k'], sv['v'], sv['o'], d_o, sv['lse'], B, S, comm=ride.get('attn_bwd'))

    def q_prep_b(c, dq, cq, sq, g, w):
        dqq = jnp.concatenate([dq * jnp.tile(cq, (1, HEADS)), dq * jnp.tile(sq, (1, HEADS))], axis=1).astype(BF16)
        dc, dg = _rms_bwd(c, g, _dot_nt(dqq, w))
        return dc, _rms(c, g), dqq, dg
    dcq, cqn, dqq, grads['g_cq'] = _rw("q_prep_bwd", q_prep_b, [win(P1, Q_LORA, C_CQ // Q_LORA), dq, tab(cq_t), tab(sq_t)],
                                       [P['g_cq'], W['Wq']], [(Q_LORA, BF16), (Q_LORA, BF16), (2 * QW, BF16)], [(1, Q_LORA)], tile=512)
    grads['Wq'] = _mm_t("uq_dw", cqn, dqq)

    def kv_prep_b(c, dk, dv, ck, g, w, place):
        dc, dg = _rms_bwd(c, g, _dot_nt(dk, w[:, :QW]) + _dot_nt(dv, w[:, QW:]))
        return dc, _dot_nt(dk, place) * ck, _rms(c, g), dg
    dckv, dseg, kvn, grads['g_ckv'] = _rw(
        "kv_prep_bwd", kv_prep_b, [win(P1, KV_LORA, C_CKV // KV_LORA), dk, dv, tab(ck_t)], [P['g_ckv'], W['Wkv'], place],
        [(KV_LORA, BF16), (HP, BF16), (KV_LORA, BF16)], [(1, KV_LORA)], tile=512)
    grads['Wkn'] = _mm_t("uk_dw", kvn, dk)
    grads['Wv'] = _mm_t("uv_dw", kvn, dv)
    ride = dict(ride, **own_ride('mixers', grads))

    dps = jnp.concatenate([dcq, dckv, dseg, dpin, dsu, dsv_, dcvb, dcvc, dcvx], axis=1)
    grads['W1s'] = _mm_t("in_proj_s_dw", sv['h'], dps)
    grads['W1g'] = _mm_t("in_proj_g_dw", sv['h'], dl, comm=ride.get('in_proj_g_dw'))
    ride = dict(ride, **own_ride('w_in', grads))
    dh = _mm("in_proj_dx", dps, W['W1s'], tb=True, a2=dl, b2=W['W1g'], out_dtype=BF16, tm=256, comm=ride.get('in_proj_dx'))
    ride = dict(ride, **own_ride('w_in_rest', grads))

    if prev is None:
        def pre_mix_b(x, dh, dy, g):
            dx, dg = _rms_bwd(x, g, dh)
            return dy + dx, dg
        dx, grads['g_pre_mix'] = _rw("pre_mix_bwd", pre_mix_b, [sv['x'], dh, dx1], [P['g_pre_mix']], [(D, F32)], [(1, D)],
                                     tile=512, comm=ride.get('edge'))
        return dx, None, None, grads, ride

    def edge_norms_b(x, dh, dy, f, g_mix, g_ffn):
        dx, dg_mix = _rms_bwd(x, g_mix, dh)
        dx = dy + dx
        df, dg_ffn = _rms_bwd(f, g_ffn, dx)
        return dx, df, dg_mix, dg_ffn
    dx, df_prev, grads['g_pre_mix'], dg_prev = _rw(
        "edge_norms_bwd", edge_norms_b, [sv['x'], dh, dx1, prev[0]], [P['g_pre_mix'], prev[1]],
        [(D, F32), (D, BF16)], [(1, D), (1, D)], tile=512, comm=ride.get('edge'))
    return dx, df_prev, dg_prev, grads, ride


def _mixer_grads(g):
    out = {}
    qa = g['Wq'][:, :QW].reshape(1, Q_LORA, HEADS, HP)
    qb = g['Wq'][:, QW:].reshape(1, Q_LORA, HEADS, HP)
    drope = qa[..., QK_NOPE:QK_NOPE + QK_ROPE] + _unrot_cols(qb[..., QK_NOPE:QK_NOPE + QK_ROPE])
    out['w_uq'] = jnp.concatenate([qa[..., :QK_NOPE], drope], axis=-1).reshape(1, Q_LORA, HEADS * (QK_NOPE + QK_ROPE))
    kn = g['Wkn'].reshape(1, KV_LORA, HEADS, HP)[..., :QK_NOPE]
    vv = g['Wv'].reshape(1, KV_LORA, HEADS, HP)[..., :V_HEAD]
    out['w_ukv'] = jnp.concatenate([kn, vv], axis=-1).reshape(1, KV_LORA, HEADS * (QK_NOPE + V_HEAD))
    out['w_br_a'] = g['Wa'].reshape(1, HEADS, HP, D)[:, :, :V_HEAD].reshape(1, HEADS * V_HEAD, D)
    out['w_br_b'], out['w_br_c'], out['w_br_d'] = g['Wb'][None], g['Wc'][None], g['Wd'][None]
    out['conv_w'] = g['conv'].reshape(1, 3, 1, BR)
    return out


def _small_grads(g):
    out = {}
    for n in ('g_pre_mix', 'g_cq', 'g_ckv', 'g_sgu_v', 'g_post_mix', 'g_pre_ffn', 'g_post_ffn', 'pool_scale'):
        out[n] = g[n].reshape(1, -1)
    bd = g['pool_bd'].reshape(4, GROUP, 4, GROUP)
    out['pool_w'] = jnp.stack([bd[i, :, i, :] for i in range(4)])[None]
    out['sgu_w'] = (g['sgu_wm'] * _sgu_mask()[None])[None]
    out['sgu_b'] = g['sgu_bias'].reshape(SGU_BLOCK, 4, GROUP).sum(-1).T[None]
    return out


def _sgu_mask():
    pc = np.arange(SGU_BLOCK) // CHUNK
    return jnp.asarray(pc[:, None] >= pc[None, :], F32)


def _rows(a, lead):
    return a.reshape(a.shape[:lead] + (-1, a.shape[-1]))


def _pad_to(a, rows, cols):
    pad = [(0, 0)] * (a.ndim - 2) + [(0, rows - a.shape[-2]), (0, cols - a.shape[-1])]
    return jnp.pad(a, pad)


GROUPS = (('w_in',), ('w_ffn_gate',), ('w_ffn_up',), ('w_out', 'w_ffn_down'),
          ('w_uq', 'w_ukv', 'w_br_a', 'w_br_b', 'w_br_c', 'w_br_d', 'conv_w'))
PADDED = {'w_uq': (Q_LORA, 128), 'conv_w': (128, 128), 'w_ffn_down': (384, D)}
NARROW = ('w_uq', 'conv_w')


def _group(t, lead, groups):
    bufs = {}
    for b in groups:
        ps = [_rows(t[n], lead) for n in GROUPS[b]]
        ps = [_pad_to(p, *PADDED[n]) if n in PADDED else p for n, p in zip(GROUPS[b], ps)]
        bufs[b] = ps[0] if len(ps) == 1 else jnp.concatenate(ps, axis=-2)
    return bufs


def _group_rows(shapes):
    where, rows = {}, []
    for b, names in enumerate(GROUPS):
        off = 0
        for n in names:
            where[n] = (b, off)
            off += PADDED.get(n, shapes[n])[0]
        rows.append(off)
    return where, rows


def _shard_split(a, axis):
    n = a.shape[axis]
    a = a.reshape(a.shape[:axis] + (N_DEV, n // N_DEV) + a.shape[axis + 1:])
    return jnp.moveaxis(a, axis, 0)


def _shard_join(a, axis):
    a = jnp.moveaxis(a, 0, axis)
    return a.reshape(a.shape[:axis] + (a.shape[axis] * a.shape[axis + 1],) + a.shape[axis + 2:])


def _as2d(a):
    return a.reshape(-1, a.shape[-1])


def kernel(x, w_in, g_pre_mix, g_cq, g_ckv, w_uq, w_ukv, pool_w, pool_scale, g_sgu_v, sgu_w, sgu_b, conv_w, w_br_a, w_br_b, w_br_c, w_br_d, w_out, g_post_mix, g_pre_ffn, w_ffn_gate, w_ffn_up, w_ffn_down, g_post_ffn, loss_target, m_w_in, m_g_pre_mix, m_g_cq, m_g_ckv, m_w_uq, m_w_ukv, m_pool_w, m_pool_scale, m_g_sgu_v, m_sgu_w, m_sgu_b, m_conv_w, m_w_br_a, m_w_br_b, m_w_br_c, m_w_br_d, m_w_out, m_g_post_mix, m_g_pre_ffn, m_w_ffn_gate, m_w_ffn_up, m_w_ffn_down, m_g_post_ffn, v_w_in, v_g_pre_mix, v_g_cq, v_g_ckv, v_w_uq, v_w_ukv, v_pool_w, v_pool_scale, v_g_sgu_v, v_sgu_w, v_sgu_b, v_conv_w, v_w_br_a, v_w_br_b, v_w_br_c, v_w_br_d, v_w_out, v_g_post_mix, v_g_pre_ffn, v_w_ffn_gate, v_w_ffn_up, v_w_ffn_down, v_g_post_ffn):
    args = dict(locals())
    w = {n: args[n] for n in WEIGHTS}
    m = {n: args['m_' + n] for n in WEIGHTS}
    v = {n: args['v_' + n] for n in WEIGHTS}
    B, S, _ = x.shape
    T = B * S
    L = DEPTH
    names = list(SHARDED)
    shapes1 = {n: _as2d(w[n][0]).shape for n in names}
    where, group_rows = _group_rows(shapes1)

    def shard_bufs(l, groups):
        return _group({n: w[n][l:l + 1].astype(BF16) for b in groups for n in GROUPS[b]}, 0, groups)

    def joined(gathered):
        G = {}
        for b, buf in gathered.items():
            for n in GROUPS[b]:
                if n == 'w_in':
                    continue
                off = where[n][1]
                r, c = shapes1[n]
                G[n] = _shard_join(buf[:, off:off + r, :c].reshape((N_DEV, 1) + w[n].shape[1:]), SHARDED[n])
        return G

    eye = jnp.eye(4, dtype=F32)
    pool_bd = (pool_w[:, :, :, None, :] * eye[None, :, None, :, None]).reshape(L, BR, BR).astype(BF16)
    sgu_wm = (sgu_w * _sgu_mask()[None, None]).astype(BF16)
    sgu_bias = jnp.repeat(sgu_b.transpose(0, 2, 1), GROUP, axis=2)
    tabs = _tables(S) + (_place_k(),)

    def layer_params(l):
        P = {n: w[n][l][None, :] for n in ('g_pre_mix', 'g_cq', 'g_ckv', 'g_sgu_v', 'g_post_mix', 'g_pre_ffn',
                                            'g_post_ffn', 'pool_scale')}
        P.update(pool_bd=pool_bd[l], sgu_wm=sgu_wm[l], sgu_bias=sgu_bias[l])
        return P

    xt = x.reshape(T, D)
    h = _rw("pre_mix", lambda x, g: _rms(x, g), [xt], [w['g_pre_mix'][0][None, :]], [(D, BF16)], tile=512)[0]
    saved, Ws = [], []
    first = shard_bufs(0, (0, 4))
    early = dict(zip((0, 4), _all_gather("gather_weights", [first[0], first[4]])))
    for l in range(L):
        own = shard_bufs(l, (1, 2, 3))
        nxt = shard_bufs(l + 1, (0, 4)) if l + 1 < L else None
        ride = {'in_proj': _Gather([own[1]]), 'attn_fwd': _Gather([own[3], own[2]])}
        w_in_rest = []
        if nxt:
            ride.update(ffn_up=_Gather([nxt[0]], ks=(0, 1, 2, 4, 6)), merge_fwd=_Gather([nxt[4]]))

            def rest(ride=ride, nxt=nxt, made=w_in_rest):
                made.append(_Gather([nxt[0]], ks=(3, 5, 7), out=ride['ffn_up'].results))
                return made[0]
            ride['ffn_down'] = rest

        def late(ride=ride):
            G = joined({1: ride['in_proj'].results[0], 3: ride['attn_fwd'].results[0], 2: ride['attn_fwd'].results[1]})
            return dict(Wout=G['w_out'][0], Wg=G['w_ffn_gate'][0], Wu=G['w_ffn_up'][0], Wdn=G['w_ffn_down'][0])

        We = {k: a[0] for k, a in _layer_weights(joined(early)).items()}
        We['W1s'], We['W1g'] = _w1(early[0])
        g_next = w['g_pre_mix'][l + 1][None, :] if nxt else None
        xt, h, sv, W = _layer_fwd(xt, h, We, late, layer_params(l), tabs, B, S, ride, g_next)
        saved.append(sv)
        Ws.append(W)
        if nxt:
            early = {0: w_in_rest[0].results[0], 4: ride['merge_fwd'].results[0]}

    def loss_f(y, t, f, g):
        e = y - t
        dy = e * (1.0 / D)
        df, dg = _rms_bwd(f, g, dy)
        return dy, df, jnp.sum(e * e, axis=0, keepdims=True), dg
    dy, df, sq, dg_ffn = _rw("loss", loss_f, [xt, loss_target.reshape(T, D), saved[-1]['f']], [w['g_post_ffn'][L - 1][None, :]],
                             [(D, F32), (D, BF16)], [(1, D), (1, D)], tile=512)
    loss = lax.psum(0.5 * jnp.sum(sq) / D, ("x", "y", "c"))

    recv = [lax.empty((N_DEV, L * r, PADDED.get(ns[0], shapes1[ns[0]])[1]), BF16) for ns, r in zip(GROUPS, group_rows)]

    def send_bufs(pg, groups):
        return _group({n: _shard_split(pg[n], SHARDED[n]).astype(BF16) for b in groups for n in GROUPS[b]}, 1, groups)

    small, w_in_first = [None] * L, []
    for l in reversed(range(L)):
        def own_ride(stage, grads, l=l):
            if stage == 'ffn':
                bufs = send_bufs({'w_ffn_gate': grads['Wg'][None], 'w_ffn_up': grads['Wu'][None],
                                  'w_ffn_down': grads['Wdn'][None], 'w_out': grads['Wout'][None]}, (1, 2, 3))
                return {'merge_bwd': _Scatter([bufs[1]], [recv[1]], [l * group_rows[1]]),
                        'attn_bwd': _Scatter([bufs[3], bufs[2]], [recv[3], recv[2]], [l * group_rows[3], l * group_rows[2]])}
            if stage == 'mixers':
                bufs = send_bufs(_mixer_grads(grads), (4,))
                return {'in_proj_g_dw': _Scatter([bufs[4]], [recv[4]], [l * group_rows[4]])}
            if stage == 'w_in':
                shards = _w_in_grad_shards(grads['W1s'], grads['W1g'])
                w_in_first.append(_Scatter([shards], [recv[0]], [l * group_rows[0]], ks=(0, 1, 2, 4, 6)))
                return {'in_proj_dx': w_in_first[-1]}
            first = w_in_first[-1]
            return {'edge': _Scatter(first.inputs[:1], first.results, [l * group_rows[0]], ks=(3, 5, 7))}

        prev = (saved[l - 1]['f'], w['g_post_ffn'][l - 1][None, :]) if l else None
        dy, df_prev, dg_prev, gl, ride = _layer_bwd(dy, df, saved[l], Ws[l], layer_params(l), tabs, B, S, own_ride, prev)
        gl['g_post_ffn'] = dg_ffn
        df, dg_ffn = df_prev, dg_prev
        for name, bs in (('in_proj_g_dw', (4,)), ('edge', (0,)), ('merge_bwd', (1,)), ('attn_bwd', (3, 2))):
            for b, res_b in zip(bs, ride[name].results):
                recv[b] = res_b
        pg = _small_grads(gl)
        rep = jnp.concatenate([pg[n].reshape(-1, 128) for n in REPLICATED], axis=0)
        small[l] = _pad_to(rep, -(-rep.shape[0] // 8) * 8, 128)
    grad_x = dy.reshape(B, S, D)

    small_rows = small[0].shape[0]
    got_small = _all_gather("gather_small_grads", [jnp.concatenate(small, axis=0)])[0].reshape(N_DEV, L, small_rows, 128)

    res = {}
    for n in names:
        b, off = where[n]
        r, c = shapes1[n]
        if n in NARROW:
            parts = recv[b].reshape(N_DEV, L, group_rows[b], -1)[:, :, off:off + r, :c].reshape(N_DEV, L * r, c)
            res[n] = _adamw("adamw_" + n, parts, _as2d(w[n]), _as2d(m[n]), _as2d(v[n]))
        else:
            res[n] = _adamw("adamw_" + n, recv[b], _as2d(w[n]), _as2d(m[n]), _as2d(v[n]), row_off=off, layers=L,
                            stride=group_rows[b])
    off = 0
    for n in REPLICATED:
        r = int(np.prod(w[n].shape[1:])) // 128
        shp2 = _as2d(w[n]).shape
        res[n] = _adamw("adamw_" + n, got_small[:, :, off:off + r].reshape((N_DEV,) + shp2), _as2d(w[n]), _as2d(m[n]), _as2d(v[n]))
        off += r
    outs = [loss, grad_x]
    for k in range(4):
        outs += [res[n][k].reshape(w[n].shape) for n in WEIGHTS]
    return tuple(outs)
```

```python
import functools

import numpy as np
import jax
import jax.numpy as jnp
from jax import lax
from jax.experimental import pallas as pl
from jax.experimental.pallas import tpu as pltpu

F32, BF16 = jnp.float32, jnp.bfloat16
MESH = pl.DeviceIdType.MESH
N_DEV = 8

D = 1024
DEPTH = 4
CHUNK = 64
EPS = 1e-6
NEG_INF = -1e30
HEADS, QK_NOPE, QK_ROPE, V_HEAD = 8, 64, 32, 64
Q_LORA, KV_LORA = 256, 128
ROPE_THETA = 10000.0
POOL_WINDOWS = (2, 4, 8, 16)
GROUP = 64
BR = 256
SGU_BLOCK = 128
SGU_TILE = 4 * SGU_BLOCK
D_FF = 2816
IN_SIZES = (256, 128, 32, 256, 256, 256, 256, 256, 256, 4096)
HP = 128
QW = HEADS * HP
C_CQ, C_CKV, C_KR, C_PIN, C_SU, C_SV, C_CVB, C_CVC, C_CVX, C_GATE = 0, 256, 384, 512, 768, 1024, 1280, 1536, 1792, 2048

ADAM_LR, ADAM_B1, ADAM_B2, ADAM_EPS, ADAM_WD, ADAM_STEP = 0.001, 0.9, 0.999, 1e-08, 0.01, 10

WEIGHTS = ['w_in', 'g_pre_mix', 'g_cq', 'g_ckv', 'w_uq', 'w_ukv', 'pool_w', 'pool_scale', 'g_sgu_v', 'sgu_w', 'sgu_b',
           'conv_w', 'w_br_a', 'w_br_b', 'w_br_c', 'w_br_d', 'w_out', 'g_post_mix', 'g_pre_ffn', 'w_ffn_gate',
           'w_ffn_up', 'w_ffn_down', 'g_post_ffn']
SHARDED = {'w_in': 2, 'w_uq': 2, 'w_ukv': 2, 'conv_w': 3, 'w_br_a': 2, 'w_br_b': 2, 'w_br_c': 2, 'w_br_d': 2,
           'w_out': 1, 'w_ffn_gate': 2, 'w_ffn_up': 2, 'w_ffn_down': 1}
REPLICATED = [n for n in WEIGHTS if n not in SHARDED]

VMEM_LIMIT = 56 * 1024 * 1024
TQ = 256


def _cparams(sem=None):
    return pltpu.CompilerParams(vmem_limit_bytes=VMEM_LIMIT, dimension_semantics=sem)


def _tile(n, cap, align=128):
    if n <= cap:
        return n
    for t in range(cap - cap % align, 0, -align):
        if n % t == 0:
            return t
    return n


def _peers():
    x_, y_, c_ = lax.axis_index("x"), lax.axis_index("y"), lax.axis_index("c")
    out = []
    for k in range(1, N_DEV):
        px, py, pc = (x_ + (k >> 2)) % 2, (y_ + ((k >> 1) & 1)) % 2, (c_ + (k & 1)) % 2
        out.append((k, (px, py, pc), 4 * px + 2 * py + pc))
    return 4 * x_ + 2 * y_ + c_, out


class _Exchange:
    def __init__(self, n):
        self.n = n
        self.scratch = [pltpu.SemaphoreType.DMA((7 * n,)), pltpu.SemaphoreType.DMA((7 * n,)),
                        pltpu.SemaphoreType.DMA((n,))]
        self.results = None

    def start(self, cin, cout, sems):
        local, sends, _ = self.copies(cin, cout, sems)
        for cp in local + sends:
            cp.start()

    def wait(self, cin, cout, sems):
        local, sends, recvs = self.copies(cin, cout, sems)
        for cp in recvs:
            cp.wait_recv()
        for cp in sends:
            cp.wait_send()
        for cp in local:
            cp.wait()


class _Gather(_Exchange):
    def __init__(self, xs, ks=range(N_DEV), out=()):
        super().__init__(len(xs))
        self.inputs = list(xs) + list(out)
        self.out_shape = [jax.ShapeDtypeStruct((N_DEV,) + x.shape, x.dtype) for x in xs]
        self.aliases = {self.n + a: a for a in range(len(out))}
        self.ks = tuple(ks)

    def copies(self, cin, cout, sems):
        send_sems, recv_sems, local_sems = sems
        me, peers = _peers()
        local, sends, recvs = [], [], []
        for a in range(self.n):
            if 0 in self.ks:
                local.append(pltpu.make_async_copy(cin[a], cout[a].at[me], local_sems.at[a]))
            for k, dev, flat in peers:
                if k not in self.ks:
                    continue
                sem = dict(send_sem=send_sems.at[7 * a + k - 1], recv_sem=recv_sems.at[7 * a + k - 1],
                           device_id=dev, device_id_type=MESH)
                sends.append(pltpu.make_async_remote_copy(src_ref=cin[a], dst_ref=cout[a].at[me], **sem))
                recvs.append(pltpu.make_async_remote_copy(src_ref=cin[a], dst_ref=cout[a].at[flat], **sem))
        return local, sends, recvs


class _Scatter(_Exchange):
    def __init__(self, xs, recv, row0, ks=range(N_DEV)):
        super().__init__(len(xs))
        self.inputs = list(xs) + list(recv)
        self.out_shape = [jax.ShapeDtypeStruct(r.shape, r.dtype) for r in recv]
        self.aliases = {self.n + a: a for a in range(self.n)}
        self.row0 = list(row0)
        self.ks = tuple(ks)

    def copies(self, cin, cout, sems):
        send_sems, recv_sems, local_sems = sems
        me, peers = _peers()
        local, sends, recvs = [], [], []
        for a in range(self.n):
            rows = pl.ds(self.row0[a], self.inputs[a].shape[1])
            if 0 in self.ks:
                local.append(pltpu.make_async_copy(cin[a].at[me], cout[a].at[me, rows], local_sems.at[a]))
            for k, dev, flat in peers:
                if k not in self.ks:
                    continue
                sem = dict(send_sem=send_sems.at[7 * a + k - 1], recv_sem=recv_sems.at[7 * a + k - 1],
                           device_id=dev, device_id_type=MESH)
                sends.append(pltpu.make_async_remote_copy(src_ref=cin[a].at[flat], dst_ref=cout[a].at[me, rows], **sem))
                recvs.append(pltpu.make_async_remote_copy(src_ref=cin[a].at[me], dst_ref=cout[a].at[flat, rows], **sem))
        return local, sends, recvs


def _call(name, body, grid, in_specs, out_specs, out_shape, arrays, scratch=(), sem=None, comm=None):
    if comm is None:
        res = pl.pallas_call(body, name=name, grid=grid, in_specs=list(in_specs), out_specs=list(out_specs),
                             out_shape=list(out_shape), scratch_shapes=list(scratch), compiler_params=_cparams(sem))(*arrays)
        return list(res)
    ni, no, ns = len(in_specs), len(out_specs), len(scratch)
    nci, nco = len(comm.inputs), len(comm.out_shape)
    hbm = pl.BlockSpec(memory_space=pl.ANY)

    def wrapped(*refs):
        ins, cin = refs[:ni], refs[ni:ni + nci]
        o0 = ni + nci
        outs, cout = refs[o0:o0 + no], refs[o0 + no:o0 + no + nco]
        s0 = o0 + no + nco
        scr, sems = refs[s0:s0 + ns], refs[s0 + ns:]
        ids = [pl.program_id(d) for d in range(len(grid))]
        first = functools.reduce(jnp.logical_and, [i == 0 for i in ids])
        last = functools.reduce(jnp.logical_and, [i == g - 1 for i, g in zip(ids, grid)])

        @pl.when(first)
        def _():
            comm.start(cin, cout, sems)

        body(*ins, *outs, *scr)

        @pl.when(last)
        def _():
            comm.wait(cin, cout, sems)

    res = pl.pallas_call(
        wrapped, name=name, grid=grid, in_specs=list(in_specs) + [hbm] * nci, out_specs=list(out_specs) + [hbm] * nco,
        out_shape=list(out_shape) + list(comm.out_shape), scratch_shapes=list(scratch) + comm.scratch,
        input_output_aliases={ni + i: no + j for i, j in comm.aliases.items()},
        compiler_params=_cparams(("arbitrary",) * len(grid)))(*arrays, *comm.inputs)
    comm.results = list(res[no:])
    return list(res[:no])


def win(arr, width, idx):
    return ('win', arr, width, idx)


def tab(arr):
    return ('tab', arr)


def _rw(name, fn, ins, consts, outs, accs=(), tile=256, comm=None):
    first = ins[0][1] if isinstance(ins[0], tuple) else ins[0]
    T = first.shape[0]
    tile = min(tile, T)
    grid = (T // tile,)
    arrays, in_specs = [], []
    for x in ins:
        if isinstance(x, tuple) and x[0] == 'win':
            _, a, w, k = x
            arrays.append(a)
            in_specs.append(pl.BlockSpec((tile, w), lambda i, k=k: (i, k)))
        elif isinstance(x, tuple) and x[0] == 'tab':
            a = x[1]
            nb = a.shape[0] // tile
            arrays.append(a)
            in_specs.append(pl.BlockSpec((tile, a.shape[1]), lambda i, nb=nb: (i % nb, 0)))
        else:
            arrays.append(x)
            in_specs.append(pl.BlockSpec((tile, x.shape[1]), lambda i: (i, 0)))
    for c in consts:
        arrays.append(c)
        in_specs.append(pl.BlockSpec(c.shape, lambda i, n=c.ndim: (0,) * n))
    out_shape = [jax.ShapeDtypeStruct((T, f), dt) for f, dt in outs]
    out_specs = [pl.BlockSpec((tile, f), lambda i: (i, 0)) for f, _ in outs]
    for shp in accs:
        out_shape.append(jax.ShapeDtypeStruct(shp, F32))
        out_specs.append(pl.BlockSpec(shp, lambda i, n=len(shp): (0,) * n))
    ni, nc, no = len(ins), len(consts), len(outs)

    def body(*refs):
        vals_in = [r[...] for r in refs[:ni]]
        vals_in = [v.astype(F32) if v.dtype == BF16 else v for v in vals_in]
        vals = fn(*vals_in, *[r[...] for r in refs[ni:ni + nc]])
        if not isinstance(vals, (tuple, list)):
            vals = (vals,)
        o_refs = refs[ni + nc:]
        for r, v in zip(o_refs[:no], vals[:no]):
            r[...] = v.astype(r.dtype)
        i = pl.program_id(0)
        for r, v in zip(o_refs[no:], vals[no:]):
            @pl.when(i == 0)
            def _(r=r, v=v):
                r[...] = v

            @pl.when(i > 0)
            def _(r=r, v=v):
                r[...] += v

    return _call(name, body, grid, in_specs, out_specs, out_shape, arrays, sem=("arbitrary",), comm=comm)


def _mm(name, a, b, tb=False, out_dtype=F32, tm=512, tn=3072, a2=None, b2=None, comm=None):
    M, K = a.shape
    N = b.shape[0] if tb else b.shape[1]
    tm, tn = _tile(M, tm, 8), _tile(N, tn)
    dims = (((1,), (1,)), ((), ())) if tb else (((1,), (0,)), ((), ()))
    pairs = [(a, b)] + ([(a2, b2)] if a2 is not None else [])

    def body(*refs):
        o_ref = refs[-1]
        acc = None
        for i in range(len(pairs)):
            p = lax.dot_general(refs[2 * i][...].astype(BF16), refs[2 * i + 1][...].astype(BF16), dims,
                                preferred_element_type=F32)
            acc = p if acc is None else acc + p
        o_ref[...] = acc.astype(o_ref.dtype)

    in_specs, arrays = [], []
    for x, y in pairs:
        k = x.shape[1]
        in_specs.append(pl.BlockSpec((tm, k), lambda j, i: (i, 0)))
        in_specs.append(pl.BlockSpec((tn, k), lambda j, i: (j, 0)) if tb else pl.BlockSpec((k, tn), lambda j, i: (0, j)))
        arrays += [x, y]
    return _call(name, body, (N // tn, M // tm), in_specs, [pl.BlockSpec((tm, tn), lambda j, i: (i, j))],
                 [jax.ShapeDtypeStruct((M, N), out_dtype)], arrays, sem=("parallel", "parallel"), comm=comm)[0]


def _mm_t(name, a, g, tk=1408, tn=1408, tt=2048, comm=None):
    T, K = a.shape
    N = g.shape[1]
    tk, tn, tt = _tile(K, tk), _tile(N, tn), _tile(T, tt, 8)
    nt = T // tt

    def body(a_ref, g_ref, o_ref, acc_ref):
        p = lax.dot_general(a_ref[...].astype(BF16), g_ref[...].astype(BF16), (((0,), (0,)), ((), ())),
                            preferred_element_type=F32)
        t = pl.program_id(2)

        @pl.when(t == 0)
        def _():
            acc_ref[...] = p

        @pl.when(t > 0)
        def _():
            acc_ref[...] += p

        @pl.when(t == nt - 1)
        def _():
            o_ref[...] = acc_ref[...].astype(BF16)

    return _call(name, body, (K // tk, N // tn, nt),
                 [pl.BlockSpec((tt, tk), lambda i, j, t: (t, i)), pl.BlockSpec((tt, tn), lambda i, j, t: (t, j))],
                 [pl.BlockSpec((tk, tn), lambda i, j, t: (i, j))], [jax.ShapeDtypeStruct((K, N), BF16)], [a, g],
                 scratch=[pltpu.VMEM((tk, tn), F32)], sem=("parallel", "parallel", "arbitrary"), comm=comm)[0]


def _dot(a, b):
    return jnp.dot(a.astype(BF16), b.astype(BF16), preferred_element_type=F32)


def _dot_nt(a, b):
    return lax.dot_general(a.astype(BF16), b.astype(BF16), (((1,), (1,)), ((), ())), preferred_element_type=F32)


def _dot_tn(a, b):
    return lax.dot_general(a.astype(BF16), b.astype(BF16), (((0,), (0,)), ((), ())), preferred_element_type=F32)


def _sigmoid(x):
    return 0.5 * jnp.tanh(0.5 * x) + 0.5


def _ffn_up(h, wg, wu, comm=None):
    M, K = h.shape
    N = wg.shape[1]
    tm, tn = _tile(M, 512, 8), _tile(N, 1408)

    def body(h_ref, wg_ref, wu_ref, ga_ref, gb_ref, act_ref):
        hb = h_ref[...]
        a = jnp.dot(hb, wg_ref[...], preferred_element_type=F32)
        b = jnp.dot(hb, wu_ref[...], preferred_element_type=F32)
        s = _sigmoid(a)
        silu = a * s
        ga_ref[...] = (b * (s + silu * (1.0 - s))).astype(BF16)
        gb_ref[...] = silu.astype(BF16)
        act_ref[...] = (silu * b).astype(BF16)

    wspec = pl.BlockSpec((K, tn), lambda j, i: (0, j))
    ospec = pl.BlockSpec((tm, tn), lambda j, i: (i, j))
    return _call("ffn_up", body, (N // tn, M // tm), [pl.BlockSpec((tm, K), lambda j, i: (i, 0)), wspec, wspec],
                 [ospec] * 3, [jax.ShapeDtypeStruct((M, N), BF16)] * 3, [h, wg, wu], sem=("parallel", "parallel"),
                 comm=comm)


def _ffn_down_dx(df, wdn, ga, gb):
    M, K = df.shape
    N = wdn.shape[0]
    tm, tn = _tile(M, 512, 8), _tile(N, 1408)

    def body(df_ref, w_ref, ga_ref, gb_ref, da_ref, db_ref):
        d = _dot_nt(df_ref[...], w_ref[...])
        da_ref[...] = (d * ga_ref[...].astype(F32)).astype(BF16)
        db_ref[...] = (d * gb_ref[...].astype(F32)).astype(BF16)

    tspec = pl.BlockSpec((tm, tn), lambda j, i: (i, j))
    return _call("ffn_down_dx", body, (N // tn, M // tm),
                 [pl.BlockSpec((tm, K), lambda j, i: (i, 0)), pl.BlockSpec((tn, K), lambda j, i: (j, 0)), tspec, tspec],
                 [tspec] * 2, [jax.ShapeDtypeStruct((M, N), BF16)] * 2, [df, wdn, ga, gb], sem=("parallel", "parallel"))


def _mm_norm(name, a, b, x, g, g_next=None, comm=None):
    M, K = a.shape
    N = b.shape[1]
    tm = _tile(M, 512, 8)
    more = g_next is not None

    def body(*refs):
        a_ref, b_ref, x_ref, g_ref = refs[:4]
        y_ref, xn_ref = refs[4 + more], refs[5 + more]
        y = jnp.dot(a_ref[...].astype(BF16), b_ref[...], preferred_element_type=F32)
        y_ref[...] = y.astype(BF16)
        xn = x_ref[...] + _rms(y, g_ref[...])
        xn_ref[...] = xn
        if more:
            refs[7][...] = _rms(xn, refs[4][...]).astype(BF16)

    row = lambda n: pl.BlockSpec((tm, n), lambda i: (i, 0))
    whole = lambda z: pl.BlockSpec(z.shape, lambda i: (0, 0))
    arrays = [a, b, x, g] + ([g_next] if more else [])
    in_specs = [row(K), whole(b), row(N), whole(g)] + ([whole(g_next)] if more else [])
    out_shape = [jax.ShapeDtypeStruct((M, N), dt) for dt in (BF16, F32) + ((BF16,) if more else ())]
    return _call(name, body, (M // tm,), in_specs, [row(N)] * (2 + more), out_shape, arrays, sem=("parallel",), comm=comm)


def _rms(x, g):
    r = lax.rsqrt(jnp.mean(x * x, axis=-1, keepdims=True) + EPS)
    return x * r * g


def _rms_bwd(x, g, dy):
    r = lax.rsqrt(jnp.mean(x * x, axis=-1, keepdims=True) + EPS)
    xh = x * r
    dxh = dy * g
    dx = r * (dxh - xh * jnp.mean(dxh * xh, axis=-1, keepdims=True))
    return dx, jnp.sum(dy * xh, axis=0, keepdims=True)


def _shift_down(x, k):
    t = lax.broadcasted_iota(jnp.int32, x.shape, 0)
    return jnp.where(t >= k, pltpu.roll(x, k, 0), 0.0)


def _shift_up(x, k):
    n = x.shape[0]
    t = lax.broadcasted_iota(jnp.int32, x.shape, 0)
    return jnp.where(t < n - k, pltpu.roll(x, n - k, 0), 0.0)


def _group_select(vals):
    lane = lax.broadcasted_iota(jnp.int32, vals[0].shape, 1) // GROUP
    out = vals[-1]
    for g in range(len(vals) - 2, -1, -1):
        out = jnp.where(lane == g, vals[g], out)
    return out


def _pool_count(shape):
    t = lax.broadcasted_iota(jnp.int32, shape, 0)
    lane = lax.broadcasted_iota(jnp.int32, shape, 1) // GROUP
    w = jnp.where(lane == 0, POOL_WINDOWS[0], jnp.where(lane == 1, POOL_WINDOWS[1],
                  jnp.where(lane == 2, POOL_WINDOWS[2], POOL_WINDOWS[3])))
    return jnp.minimum(t + 1, w).astype(F32)


def _pooled(z):
    s = [z]
    for k in (1, 2, 4, 8):
        s.append(s[-1] + _shift_down(s[-1], k))
    return _group_select(s[1:]) / _pool_count(z.shape) - z


def _pooled_bwd(dp):
    u = dp / _pool_count(dp.shape)
    s = [u]
    for k in (1, 2, 4, 8):
        s.append(s[-1] + _shift_up(s[-1], k))
    return _group_select(s[1:]) - dp


def _diag_mask():
    r = lax.broadcasted_iota(jnp.int32, (TQ, TQ), 0) // CHUNK
    c = lax.broadcasted_iota(jnp.int32, (TQ, TQ), 1) // CHUNK
    return r >= c


def _attn_fwd(q, k, v, B, S, comm=None):
    nq = S // TQ

    def body(q_ref, k_ref, v_ref, o_ref, lse_ref):
        mask = _diag_mask()
        for i in range(nq):
            r0, r1 = i * TQ, (i + 1) * TQ
            qb = q_ref[r0:r1, :]
            sd = jnp.where(mask, _dot_nt(qb, k_ref[r0:r1, :]), NEG_INF)
            m = jnp.max(sd, axis=-1, keepdims=True)
            if i:
                so = _dot_nt(qb, k_ref[0:r0, :])
                m = jnp.maximum(m, jnp.max(so, axis=-1, keepdims=True))
            pd = jnp.exp(sd - m)
            l = jnp.sum(pd, axis=-1, keepdims=True)
            acc = _dot(pd, v_ref[r0:r1, :])
            if i:
                po = jnp.exp(so - m)
                l = l + jnp.sum(po, axis=-1, keepdims=True)
                acc = acc + _dot(po, v_ref[0:r0, :])
            o_ref[r0:r1, :] = (acc / l).astype(BF16)
            lse_ref[r0:r1, :] = jnp.broadcast_to(m + jnp.log(l), (TQ, HP))

    spec = pl.BlockSpec((S, HP), lambda b, h: (b, h))
    return _call("attn_fwd", body, (B, HEADS), [spec] * 3, [spec] * 2,
                 [jax.ShapeDtypeStruct((B * S, QW), BF16), jax.ShapeDtypeStruct((B * S, QW), F32)],
                 [q, k, v], sem=("parallel", "parallel"), comm=comm)


def _attn_bwd(q, k, v, o, do, lse, B, S, comm=None):
    nq = S // TQ

    def body(q_ref, k_ref, v_ref, o_ref, do_ref, lse_ref, dq_ref, dk_ref, dv_ref, dk_acc, dv_acc):
        mask = _diag_mask()
        dk_acc[...] = jnp.zeros_like(dk_acc)
        dv_acc[...] = jnp.zeros_like(dv_acc)
        for i in range(nq):
            r0, r1 = i * TQ, (i + 1) * TQ
            qb, dob = q_ref[r0:r1, :], do_ref[r0:r1, :]
            delta = jnp.sum(o_ref[r0:r1, :].astype(F32) * dob.astype(F32), axis=-1, keepdims=True)
            lse_b = lse_ref[r0:r1, :][:, :1]
            dq = None
            for c0, c1, diag in ([(0, r0, False)] if i else []) + [(r0, r1, True)]:
                kb, vb = k_ref[c0:c1, :], v_ref[c0:c1, :]
                p = jnp.exp(_dot_nt(qb, kb) - lse_b)
                if diag:
                    p = jnp.where(mask, p, 0.0)
                ds = p * (_dot_nt(dob, vb) - delta)
                part = _dot(ds, kb)
                dq = part if dq is None else dq + part
                dk_acc[c0:c1, :] += _dot_tn(ds, qb)
                dv_acc[c0:c1, :] += _dot_tn(p, dob)
            dq_ref[r0:r1, :] = dq
        dk_ref[...] = dk_acc[...].astype(BF16)
        dv_ref[...] = dv_acc[...].astype(BF16)

    spec = pl.BlockSpec((S, HP), lambda b, h: (b, h))
    return _call("attn_bwd", body, (B, HEADS), [spec] * 6, [spec] * 3,
                 [jax.ShapeDtypeStruct((B * S, QW), F32)] + [jax.ShapeDtypeStruct((B * S, QW), BF16)] * 2,
                 [q, k, v, o, do, lse], scratch=[pltpu.VMEM((S, HP), F32)] * 2, sem=("parallel", "parallel"), comm=comm)


def _all_gather(name, xs):
    n = len(xs)

    def body(*refs):
        x_refs, out_refs = refs[:n], refs[n:2 * n]
        send_sems, recv_sems, local_sems = refs[2 * n:]
        x_, y_, c_ = lax.axis_index("x"), lax.axis_index("y"), lax.axis_index("c")
        me, sibling = (x_, y_, c_), (x_, y_, 1 - c_)
        chips = [(1 - x_, y_), (x_, 1 - y_), (1 - x_, 1 - y_)]

        def copy(a, k, block, to, own=False):
            slot = out_refs[a].at[4 * block[0] + 2 * block[1] + block[2]]
            return pltpu.make_async_remote_copy(
                src_ref=x_refs[a] if own else slot, dst_ref=slot,
                send_sem=send_sems.at[7 * a + k], recv_sem=recv_sems.at[7 * a + k], device_id=to, device_id_type=MESH)

        mine = [pltpu.make_async_copy(x_refs[a], out_refs[a].at[4 * x_ + 2 * y_ + c_], local_sems.at[a]) for a in range(n)]
        for cp in mine:
            cp.start()
        first = [copy(a, 1 + j, me, (*chip, c_), own=True) for j, chip in enumerate(chips) for a in range(n)]
        first += [copy(a, 0, me, sibling, own=True) for a in range(n)]
        for cp in first:
            cp.start()
        passed = []
        for j, chip in enumerate(chips):
            for a in range(n):
                copy(a, 1 + j, (*chip, c_), me).wait_recv()
                passed.append(copy(a, 4 + j, (*chip, c_), sibling))
                passed[-1].start()
        for a in range(n):
            copy(a, 0, sibling, me).wait_recv()
            for j, chip in enumerate(chips):
                copy(a, 4 + j, (*chip, 1 - c_), me).wait_recv()
        for cp in first + passed:
            cp.wait_send()
        for cp in mine:
            cp.wait()

    return pl.pallas_call(
        body, name=name, out_shape=[jax.ShapeDtypeStruct((N_DEV,) + x.shape, x.dtype) for x in xs],
        in_specs=[pl.BlockSpec(memory_space=pl.ANY)] * n, out_specs=[pl.BlockSpec(memory_space=pl.ANY)] * n,
        scratch_shapes=[pltpu.SemaphoreType.DMA((7 * n,)), pltpu.SemaphoreType.DMA((7 * n,)), pltpu.SemaphoreType.DMA((n,))],
    )(*xs)


def _adamw(name, parts, w, m, v, row_off=0, layers=1, stride=0):
    R, C = w.shape
    r = R // layers
    assert parts.shape[2] == C and parts.shape[1] >= (layers - 1) * stride + row_off + r
    if r % 8 == 0:
        tr = max(t for t in range(8, 513, 8) if r % t == 0 and row_off % t == 0 and stride % t == 0)
    else:
        assert layers == 1
        tr = r
    nb = r // tr

    def body(p_ref, w_ref, m_ref, v_ref, g_ref, d_ref, nm_ref, nv_ref):
        g = p_ref[0].astype(F32)
        for j in range(1, N_DEV):
            g = g + p_ref[j].astype(F32)
        m_new = ADAM_B1 * m_ref[...] + (1.0 - ADAM_B1) * g
        v_new = ADAM_B2 * v_ref[...] + (1.0 - ADAM_B2) * (g * g)
        m_hat = m_new / (1.0 - ADAM_B1 ** ADAM_STEP)
        v_hat = v_new / (1.0 - ADAM_B2 ** ADAM_STEP)
        g_ref[...] = g
        d_ref[...] = -ADAM_LR * (m_hat / (jnp.sqrt(v_hat) + ADAM_EPS) + ADAM_WD * w_ref[...])
        nm_ref[...] = m_new
        nv_ref[...] = v_new

    spec = pl.BlockSpec((tr, C), lambda l, i: (l * nb + i, 0))
    pspec = pl.BlockSpec((N_DEV, tr, C), lambda l, i: (0, (l * stride + row_off) // tr + i, 0))
    return _call(name, body, (layers, nb), [pspec, spec, spec, spec], [spec] * 4,
                 [jax.ShapeDtypeStruct((R, C), F32)] * 4, [parts, w, m, v], sem=("parallel", "parallel"))


def _rot_cols(w):
    h = w.shape[-1] // 2
    return jnp.concatenate([-w[..., h:], w[..., :h]], axis=-1)


def _unrot_cols(d):
    h = d.shape[-1] // 2
    return jnp.concatenate([d[..., h:], -d[..., :h]], axis=-1)


IN_WIDTH = sum(IN_SIZES)
IN_SHARD = IN_WIDTH // N_DEV
IN_GATE = IN_WIDTH - IN_SIZES[-1]


def _w_in_cols(buf, a, b):
    out = []
    for j in range(a // IN_SHARD, (b - 1) // IN_SHARD + 1):
        lo, hi = max(a, IN_SHARD * j) - IN_SHARD * j, min(b, IN_SHARD * (j + 1)) - IN_SHARD * j
        out.append(buf[j, :, lo:hi])
    return out


def _w1(buf):
    kr0 = IN_SIZES[0] + IN_SIZES[1]
    kr = jnp.concatenate(_w_in_cols(buf, kr0, kr0 + QK_ROPE), axis=-1)
    w1s = jnp.concatenate(_w_in_cols(buf, 0, kr0) + [kr, _rot_cols(kr), jnp.zeros((D, 64), BF16)]
                          + _w_in_cols(buf, kr0 + QK_ROPE, IN_GATE), axis=-1)
    return w1s, jnp.concatenate(_w_in_cols(buf, IN_GATE, IN_WIDTH), axis=-1)


def _w_in_grad_shards(d1s, d1g):
    kr0 = IN_SIZES[0] + IN_SIZES[1]
    seg = d1s[:, C_KR:C_KR + 128]
    dkr = seg[:, :QK_ROPE] + _unrot_cols(seg[:, QK_ROPE:2 * QK_ROPE])
    runs = [(0, d1s, 0), (kr0, dkr, 0), (kr0 + QK_ROPE, d1s, C_PIN), (IN_GATE, d1g, 0), (IN_WIDTH, None, 0)]
    blocks = []
    for j in range(N_DEV):
        a, b = IN_SHARD * j, IN_SHARD * (j + 1)
        pieces = []
        for (c0, src, s0), (c1, _, _) in zip(runs[:-1], runs[1:]):
            lo, hi = max(a, c0), min(b, c1)
            if lo < hi:
                pieces.append(src[:, s0 + lo - c0:s0 + hi - c0])
        blocks.append(jnp.concatenate(pieces, axis=-1))
    return jnp.stack(blocks)


def _layer_weights(G):
    L = G['w_uq'].shape[0]
    wq = G['w_uq'].reshape(L, Q_LORA, HEADS, QK_NOPE + QK_ROPE)
    nope, rope = wq[..., :QK_NOPE], wq[..., QK_NOPE:]
    z32 = jnp.zeros((L, Q_LORA, HEADS, HP - QK_NOPE - QK_ROPE), BF16)
    wq_a = jnp.concatenate([nope, rope, z32], axis=-1).reshape(L, Q_LORA, QW)
    wq_b = jnp.concatenate([jnp.zeros_like(nope), _rot_cols(rope), z32], axis=-1).reshape(L, Q_LORA, QW)
    wkv = G['w_ukv'].reshape(L, KV_LORA, HEADS, QK_NOPE + V_HEAD)
    z64 = jnp.zeros((L, KV_LORA, HEADS, HP - QK_NOPE), BF16)
    wkn = jnp.concatenate([wkv[..., :QK_NOPE], z64], axis=-1).reshape(L, KV_LORA, QW)
    wv = jnp.concatenate([wkv[..., QK_NOPE:], z64], axis=-1).reshape(L, KV_LORA, QW)
    wa = G['w_br_a'].reshape(L, HEADS, V_HEAD, D)
    wa = jnp.concatenate([wa, jnp.zeros((L, HEADS, HP - V_HEAD, D), BF16)], axis=2).reshape(L, QW, D)
    return dict(
        Wq=jnp.concatenate([wq_a, wq_b], axis=-1),
        Wkv=jnp.concatenate([wkn, wv], axis=-1), Wa=wa, Wb=G['w_br_b'], Wc=G['w_br_c'], Wd=G['w_br_d'],
        conv=G['conv_w'].reshape(L, 3, BR).astype(F32))


def _tables(S):
    inv = ROPE_THETA ** (-jnp.arange(0, QK_ROPE, 2, dtype=F32) / QK_ROPE)
    ang = jnp.arange(S, dtype=F32)[:, None] * inv[None, :]
    cos, sin = jnp.cos(ang), jnp.sin(ang)
    scale = (QK_NOPE + QK_ROPE) ** -0.5
    one, zero = jnp.ones((S, QK_NOPE), F32), jnp.zeros((S, QK_NOPE), F32)
    z32 = jnp.zeros((S, HP - QK_NOPE - QK_ROPE), F32)
    cq = jnp.concatenate([one, cos, cos, z32], axis=1) * scale
    sq = jnp.concatenate([zero, sin, sin, z32], axis=1) * scale
    ck = jnp.concatenate([cos, cos, sin, sin, jnp.zeros((S, HP - 2 * QK_ROPE), F32)], axis=1)
    return cq, sq, ck


def _place_k():
    p = np.zeros((HP, QW), np.float32)
    for r in range(2 * QK_ROPE):
        for h in range(HEADS):
            p[r, h * HP + QK_NOPE + r % QK_ROPE] = 1.0
    return jnp.asarray(p, BF16)


def _layer_fwd(x, h, W, late, P, tabs, B, S, ride, g_next):
    cq_t, sq_t, ck_t, place = tabs
    g_cq, g_ckv = P['g_cq'], P['g_ckv']
    P1 = _mm("in_proj_s", h, W['W1s'], out_dtype=BF16)
    PG = _mm("in_proj_g", h, W['W1g'], out_dtype=BF16, comm=ride.get('in_proj'))

    def q_prep(c, cq, sq, g, w):
        qq = _dot(_rms(c, g), w)
        return qq[:, :QW] * jnp.tile(cq, (1, HEADS)) + qq[:, QW:] * jnp.tile(sq, (1, HEADS))
    q = _rw("q_prep", q_prep, [win(P1, Q_LORA, C_CQ // Q_LORA), tab(cq_t), tab(sq_t)], [g_cq, W['Wq']], [(QW, BF16)], tile=512)[0]

    def kv_prep(c, seg, ck, g, w, place):
        kv = _dot(_rms(c, g), w)
        return kv[:, :QW] + _dot(seg * ck, place), kv[:, QW:]
    k, v = _rw("kv_prep", kv_prep, [win(P1, KV_LORA, C_CKV // KV_LORA), win(P1, HP, C_KR // HP), tab(ck_t)], [g_ckv, W['Wkv'], place],
               [(QW, BF16), (QW, BF16)], tile=512)
    o, lse = _attn_fwd(q, k, v, B, S, comm=ride.get('attn_fwd'))

    def pool_f(z, wbd, scale):
        return _dot(_pooled(z), wbd) * scale
    bo = _rw("pool_fwd", pool_f, [win(P1, BR, C_PIN // BR)], [P['pool_bd'], P['pool_scale']], [(BR, BF16)], tile=S)[0]

    def sgu_f(u, sv, g, wm, bias):
        vn = _rms(sv, g)
        blocks = [vn[r:r + SGU_BLOCK] for r in range(0, vn.shape[0], SGU_BLOCK)]
        return u * jnp.concatenate([_sgu_mix(vb, wm, bias) for vb in blocks], axis=0)
    co = _rw("sgu_fwd", sgu_f, [win(P1, BR, C_SU // BR), win(P1, BR, C_SV // BR)], [P['g_sgu_v'], P['sgu_wm'], P['sgu_bias']],
             [(BR, BF16)], tile=SGU_TILE)[0]

    def conv_f(b, c, xi, w):
        z = c * xi
        return b * (w[0:1] * _shift_down(z, 2) + w[1:2] * _shift_down(z, 1) + w[2:3] * z)
    do_ = _rw("conv_fwd", conv_f, [win(P1, BR, C_CVB // BR), win(P1, BR, C_CVC // BR), win(P1, BR, C_CVX // BR)], [W['conv']], [(BR, BF16)], tile=S)[0]

    def merge_f(a, b, c, d, l0, l1, l2, l3, wa, wb, wc, wd):
        return (_sigmoid(l0) * _dot(a, wa) + _sigmoid(l1) * _dot(b, wb) + _sigmoid(l2) * _dot(c, wc)
                + _sigmoid(l3) * _dot(d, wd))
    merged = _rw("merge_fwd", merge_f, [o, bo, co, do_] + [win(PG, D, i) for i in range(4)],
                 [W['Wa'], W['Wb'], W['Wc'], W['Wd']], [(D, BF16)], tile=512, comm=ride.get('merge_fwd'))[0]
    W = dict(W, **late())
    om, x1, h2 = _mm_norm("out_proj", merged, W['Wout'], x, P['g_post_mix'], P['g_pre_ffn'])
    ga, gb, act = _ffn_up(h2, W['Wg'], W['Wu'], comm=ride.get('ffn_up'))
    rest = ride['ffn_down']() if 'ffn_down' in ride else None
    f, x2, *h_next = _mm_norm("ffn_down", act, W['Wdn'], x1, P['g_post_ffn'], g_next, comm=rest)
    saved = dict(x=x, h=h, P1=P1, PG=PG, q=q, k=k, v=v, o=o, lse=lse, bo=bo, co=co, do=do_, merged=merged, om=om, x1=x1,
                 h2=h2, ga=ga, gb=gb, act=act, f=f)
    return x2, (h_next[0] if h_next else None), saved, W


def _sgu_mix(vn, wm, bias):
    outs = [_dot(wm[g], vn) for g in range(4)]
    return _group_select(outs) + bias


def _layer_bwd(dx2, df, sv, W, P, tabs, B, S, own_ride, prev):
    cq_t, sq_t, ck_t, place = tabs
    P1 = sv['P1']
    grads = {}
    da, db = _ffn_down_dx(df, W['Wdn'], sv['ga'], sv['gb'])
    grads['Wdn'] = _mm_t("ffn_down_dw", sv['act'], df)
    dh2 = _mm("ffn_up_dx", da, W['Wg'], tb=True, a2=db, b2=W['Wu'], out_dtype=BF16)
    grads['Wg'] = _mm_t("ffn_gate_dw", sv['h2'], da)
    grads['Wu'] = _mm_t("ffn_up_dw", sv['h2'], db)

    def mid_norms_b(x1, dh, dy, om, g_ffn, g_mix):
        dx, dg_ffn = _rms_bwd(x1, g_ffn, dh)
        dx1 = dy + dx
        dom, dg_mix = _rms_bwd(om, g_mix, dx1)
        return dx1, dom, dg_ffn, dg_mix
    dx1, dom, grads['g_pre_ffn'], grads['g_post_mix'] = _rw(
        "mid_norms_bwd", mid_norms_b, [sv['x1'], dh2, dx2, sv['om']], [P['g_pre_ffn'], P['g_post_mix']],
        [(D, F32), (D, BF16)], [(1, D), (1, D)], tile=512)
    dmerged = _mm("out_proj_dx", dom, W['Wout'], tb=True, out_dtype=BF16)
    grads['Wout'] = _mm_t("out_proj_dw", sv['merged'], dom)
    ride = own_ride('ffn', grads)

    def merge_b(a, b, c, d, l0, l1, l2, l3, dm, wa, wb, wc, wd):
        outs_dl, outs_dy, outs_dbr = [], [], []
        for br, l, w in ((a, l0, wa), (b, l1, wb), (c, l2, wc), (d, l3, wd)):
            s = _sigmoid(l)
            y = _dot(br, w)
            dy = (dm * s).astype(BF16)
            outs_dl.append((dm * y * s * (1.0 - s)).astype(BF16))
            outs_dy.append(dy)
            outs_dbr.append(_dot_nt(dy, w))
        return (jnp.concatenate(outs_dl, axis=1), *outs_dy, *outs_dbr)
    mb = _rw("merge_bwd", merge_b,
             [sv['o'], sv['bo'], sv['co'], sv['do']] + [win(sv['PG'], D, i) for i in range(4)] + [dmerged],
             [W['Wa'], W['Wb'], W['Wc'], W['Wd']],
             [(4 * D, BF16)] + [(D, BF16)] * 4 + [(QW, BF16), (BR, BF16), (BR, BF16), (BR, BF16)], comm=ride.get('merge_bwd'))
    dl, dy, (d_o, d_bo, d_co, d_do) = mb[0], mb[1:5], mb[5:9]
    grads['Wa'] = _mm_t("br_a_dw", sv['o'], dy[0])
    grads['Wb'] = _mm_t("br_b_dw", sv['bo'], dy[1])
    grads['Wc'] = _mm_t("br_c_dw", sv['co'], dy[2])
    grads['Wd'] = _mm_t("br_d_dw", sv['do'], dy[3])

    def conv_b(b, c, xi, dout, w):
        z = c * xi
        z1, z2 = _shift_down(z, 1), _shift_down(z, 2)
        y = w[0:1] * z2 + w[1:2] * z1 + w[2:3] * z
        dyv = dout * b
        dz = w[2:3] * dyv + w[1:2] * _shift_up(dyv, 1) + w[0:1] * _shift_up(dyv, 2)
        return (dout * y, dz * xi, dz * c, jnp.sum(dyv * z2, axis=0, keepdims=True),
                jnp.sum(dyv * z1, axis=0, keepdims=True), jnp.sum(dyv * z, axis=0, keepdims=True))
    dcvb, dcvc, dcvx, dw0, dw1, dw2 = _rw("conv_bwd", conv_b, [win(P1, BR, C_CVB // BR), win(P1, BR, C_CVC // BR), win(P1, BR, C_CVX // BR), d_do],
                                          [W['conv']], [(BR, BF16)] * 3, [(1, BR)] * 3, tile=S)
    grads['conv'] = jnp.concatenate([dw0, dw1, dw2], axis=0)

    def sgu_b(u, s_v, dout, g, wm, bias):
        vn = _rms(s_v, g)
        dmix = dout * u
        lane = lax.broadcasted_iota(jnp.int32, (SGU_BLOCK, BR), 1) // GROUP
        mixed, dvn, dwm, dbias = [], [], [0.0] * 4, 0.0
        for r in range(0, vn.shape[0], SGU_BLOCK):
            vb, db = vn[r:r + SGU_BLOCK], dmix[r:r + SGU_BLOCK]
            mixed.append(_sgu_mix(vb, wm, bias))
            dvn.append(_group_select([_dot_tn(wm[gi], db) for gi in range(4)]))
            dwm = [dwm[gi] + _dot_nt(jnp.where(lane == gi, db, 0.0), vb) for gi in range(4)]
            dbias = dbias + db
        dsv, dg = _rms_bwd(s_v, g, jnp.concatenate(dvn, axis=0))
        return (dout * jnp.concatenate(mixed, axis=0), dsv, dg, *dwm, dbias)
    dsu, dsv_, grads['g_sgu_v'], wm0, wm1, wm2, wm3, grads['sgu_bias'] = _rw(
        "sgu_bwd", sgu_b, [win(P1, BR, C_SU // BR), win(P1, BR, C_SV // BR), d_co], [P['g_sgu_v'], P['sgu_wm'], P['sgu_bias']],
        [(BR, BF16)] * 2, [(1, BR)] + [(SGU_BLOCK, SGU_BLOCK)] * 4 + [(SGU_BLOCK, BR)], tile=SGU_TILE)
    grads['sgu_wm'] = jnp.stack([wm0, wm1, wm2, wm3])

    def pool_b(z, dout, wbd, scale):
        pooled = _pooled(z)
        mixed = _dot(pooled, wbd)
        dmix = dout * scale
        dz = _pooled_bwd(_dot_nt(dmix, wbd))
        return dz, _dot_tn(pooled, dmix), jnp.sum(dout * mixed, axis=0, keepdims=True)
    dpin, grads['pool_bd'], grads['pool_scale'] = _rw(
        "pool_bwd", pool_b, [win(P1, BR, C_PIN // BR), d_bo], [P['pool_bd'], P['pool_scale']], [(BR, BF16)], [(BR, BR), (1, BR)], tile=S)

    dq, dk, dv = _attn_bwd(sv['q'], sv['k'], sv['v'], sv['o'], d_o, sv['lse'], B, S, comm=ride.get('attn_bwd'))

    def q_prep_b(c, dq, cq, sq, g, w):
        dqq = jnp.concatenate([dq * jnp.tile(cq, (1, HEADS)), dq * jnp.tile(sq, (1, HEADS))], axis=1).astype(BF16)
        dc, dg = _rms_bwd(c, g, _dot_nt(dqq, w))
        return dc, _rms(c, g), dqq, dg
    dcq, cqn, dqq, grads['g_cq'] = _rw("q_prep_bwd", q_prep_b, [win(P1, Q_LORA, C_CQ // Q_LORA), dq, tab(cq_t), tab(sq_t)],
                                       [P['g_cq'], W['Wq']], [(Q_LORA, BF16), (Q_LORA, BF16), (2 * QW, BF16)], [(1, Q_LORA)], tile=512)
    grads['Wq'] = _mm_t("uq_dw", cqn, dqq)

    def kv_prep_b(c, dk, dv, ck, g, w, place):
        dc, dg = _rms_bwd(c, g, _dot_nt(dk, w[:, :QW]) + _dot_nt(dv, w[:, QW:]))
        return dc, _dot_nt(dk, place) * ck, _rms(c, g), dg
    dckv, dseg, kvn, grads['g_ckv'] = _rw(
        "kv_prep_bwd", kv_prep_b, [win(P1, KV_LORA, C_CKV // KV_LORA), dk, dv, tab(ck_t)], [P['g_ckv'], W['Wkv'], place],
        [(KV_LORA, BF16), (HP, BF16), (KV_LORA, BF16)], [(1, KV_LORA)], tile=512)
    grads['Wkn'] = _mm_t("uk_dw", kvn, dk)
    grads['Wv'] = _mm_t("uv_dw", kvn, dv)
    ride = dict(ride, **own_ride('mixers', grads))

    dps = jnp.concatenate([dcq, dckv, dseg, dpin, dsu, dsv_, dcvb, dcvc, dcvx], axis=1)
    grads['W1s'] = _mm_t("in_proj_s_dw", sv['h'], dps)
    grads['W1g'] = _mm_t("in_proj_g_dw", sv['h'], dl, comm=ride.get('in_proj_g_dw'))
    ride = dict(ride, **own_ride('w_in', grads))
    dh = _mm("in_proj_dx", dps, W['W1s'], tb=True, a2=dl, b2=W['W1g'], out_dtype=BF16, comm=ride.get('in_proj_dx'))
    ride = dict(ride, **own_ride('w_in_rest', grads))

    if prev is None:
        def pre_mix_b(x, dh, dy, g):
            dx, dg = _rms_bwd(x, g, dh)
            return dy + dx, dg
        dx, grads['g_pre_mix'] = _rw("pre_mix_bwd", pre_mix_b, [sv['x'], dh, dx1], [P['g_pre_mix']], [(D, F32)], [(1, D)],
                                     tile=512, comm=ride.get('edge'))
        return dx, None, None, grads, ride

    def edge_norms_b(x, dh, dy, f, g_mix, g_ffn):
        dx, dg_mix = _rms_bwd(x, g_mix, dh)
        dx = dy + dx
        df, dg_ffn = _rms_bwd(f, g_ffn, dx)
        return dx, df, dg_mix, dg_ffn
    dx, df_prev, grads['g_pre_mix'], dg_prev = _rw(
        "edge_norms_bwd", edge_norms_b, [sv['x'], dh, dx1, prev[0]], [P['g_pre_mix'], prev[1]],
        [(D, F32), (D, BF16)], [(1, D), (1, D)], tile=512, comm=ride.get('edge'))
    return dx, df_prev, dg_prev, grads, ride


def _mixer_grads(g):
    out = {}
    qa = g['Wq'][:, :QW].reshape(1, Q_LORA, HEADS, HP)
    qb = g['Wq'][:, QW:].reshape(1, Q_LORA, HEADS, HP)
    drope = qa[..., QK_NOPE:QK_NOPE + QK_ROPE] + _unrot_cols(qb[..., QK_NOPE:QK_NOPE + QK_ROPE])
    out['w_uq'] = jnp.concatenate([qa[..., :QK_NOPE], drope], axis=-1).reshape(1, Q_LORA, HEADS * (QK_NOPE + QK_ROPE))
    kn = g['Wkn'].reshape(1, KV_LORA, HEADS, HP)[..., :QK_NOPE]
    vv = g['Wv'].reshape(1, KV_LORA, HEADS, HP)[..., :V_HEAD]
    out['w_ukv'] = jnp.concatenate([kn, vv], axis=-1).reshape(1, KV_LORA, HEADS * (QK_NOPE + V_HEAD))
    out['w_br_a'] = g['Wa'].reshape(1, HEADS, HP, D)[:, :, :V_HEAD].reshape(1, HEADS * V_HEAD, D)
    out['w_br_b'], out['w_br_c'], out['w_br_d'] = g['Wb'][None], g['Wc'][None], g['Wd'][None]
    out['conv_w'] = g['conv'].reshape(1, 3, 1, BR)
    return out


def _small_grads(g):
    out = {}
    for n in ('g_pre_mix', 'g_cq', 'g_ckv', 'g_sgu_v', 'g_post_mix', 'g_pre_ffn', 'g_post_ffn', 'pool_scale'):
        out[n] = g[n].reshape(1, -1)
    bd = g['pool_bd'].reshape(4, GROUP, 4, GROUP)
    out['pool_w'] = jnp.stack([bd[i, :, i, :] for i in range(4)])[None]
    out['sgu_w'] = (g['sgu_wm'] * _sgu_mask()[None])[None]
    out['sgu_b'] = g['sgu_bias'].reshape(SGU_BLOCK, 4, GROUP).sum(-1).T[None]
    return out


def _sgu_mask():
    pc = np.arange(SGU_BLOCK) // CHUNK
    return jnp.asarray(pc[:, None] >= pc[None, :], F32)


def _rows(a, lead):
    return a.reshape(a.shape[:lead] + (-1, a.shape[-1]))


def _pad_to(a, rows, cols):
    pad = [(0, 0)] * (a.ndim - 2) + [(0, rows - a.shape[-2]), (0, cols - a.shape[-1])]
    return jnp.pad(a, pad)


GROUPS = (('w_in',), ('w_ffn_gate',), ('w_ffn_up',), ('w_out', 'w_ffn_down'),
          ('w_uq', 'w_ukv', 'w_br_a', 'w_br_b', 'w_br_c', 'w_br_d', 'conv_w'))
PADDED = {'w_uq': (Q_LORA, 128), 'conv_w': (128, 128), 'w_ffn_down': (384, D)}
NARROW = ('w_uq', 'conv_w')


def _group(t, lead, groups):
    bufs = {}
    for b in groups:
        ps = [_rows(t[n], lead) for n in GROUPS[b]]
        ps = [_pad_to(p, *PADDED[n]) if n in PADDED else p for n, p in zip(GROUPS[b], ps)]
        bufs[b] = ps[0] if len(ps) == 1 else jnp.concatenate(ps, axis=-2)
    return bufs


def _group_rows(shapes):
    where, rows = {}, []
    for b, names in enumerate(GROUPS):
        off = 0
        for n in names:
            where[n] = (b, off)
            off += PADDED.get(n, shapes[n])[0]
        rows.append(off)
    return where, rows


def _shard_split(a, axis):
    n = a.shape[axis]
    a = a.reshape(a.shape[:axis] + (N_DEV, n // N_DEV) + a.shape[axis + 1:])
    return jnp.moveaxis(a, axis, 0)


def _shard_join(a, axis):
    a = jnp.moveaxis(a, 0, axis)
    return a.reshape(a.shape[:axis] + (a.shape[axis] * a.shape[axis + 1],) + a.shape[axis + 2:])


def _as2d(a):
    return a.reshape(-1, a.shape[-1])


def kernel(x, w_in, g_pre_mix, g_cq, g_ckv, w_uq, w_ukv, pool_w, pool_scale, g_sgu_v, sgu_w, sgu_b, conv_w, w_br_a, w_br_b, w_br_c, w_br_d, w_out, g_post_mix, g_pre_ffn, w_ffn_gate, w_ffn_up, w_ffn_down, g_post_ffn, loss_target, m_w_in, m_g_pre_mix, m_g_cq, m_g_ckv, m_w_uq, m_w_ukv, m_pool_w, m_pool_scale, m_g_sgu_v, m_sgu_w, m_sgu_b, m_conv_w, m_w_br_a, m_w_br_b, m_w_br_c, m_w_br_d, m_w_out, m_g_post_mix, m_g_pre_ffn, m_w_ffn_gate, m_w_ffn_up, m_w_ffn_down, m_g_post_ffn, v_w_in, v_g_pre_mix, v_g_cq, v_g_ckv, v_w_uq, v_w_ukv, v_pool_w, v_pool_scale, v_g_sgu_v, v_sgu_w, v_sgu_b, v_conv_w, v_w_br_a, v_w_br_b, v_w_br_c, v_w_br_d, v_w_out, v_g_post_mix, v_g_pre_ffn, v_w_ffn_gate, v_w_ffn_up, v_w_ffn_down, v_g_post_ffn):
    args = dict(locals())
    w = {n: args[n] for n in WEIGHTS}
    m = {n: args['m_' + n] for n in WEIGHTS}
    v = {n: args['v_' + n] for n in WEIGHTS}
    B, S, _ = x.shape
    T = B * S
    L = DEPTH
    names = list(SHARDED)
    shapes1 = {n: _as2d(w[n][0]).shape for n in names}
    where, group_rows = _group_rows(shapes1)

    def shard_bufs(l, groups):
        return _group({n: w[n][l:l + 1].astype(BF16) for b in groups for n in GROUPS[b]}, 0, groups)

    def joined(gathered):
        G = {}
        for b, buf in gathered.items():
            for n in GROUPS[b]:
                if n == 'w_in':
                    continue
                off = where[n][1]
                r, c = shapes1[n]
                G[n] = _shard_join(buf[:, off:off + r, :c].reshape((N_DEV, 1) + w[n].shape[1:]), SHARDED[n])
        return G

    eye = jnp.eye(4, dtype=F32)
    pool_bd = (pool_w[:, :, :, None, :] * eye[None, :, None, :, None]).reshape(L, BR, BR).astype(BF16)
    sgu_wm = (sgu_w * _sgu_mask()[None, None]).astype(BF16)
    sgu_bias = jnp.repeat(sgu_b.transpose(0, 2, 1), GROUP, axis=2)
    tabs = _tables(S) + (_place_k(),)

    def layer_params(l):
        P = {n: w[n][l][None, :] for n in ('g_pre_mix', 'g_cq', 'g_ckv', 'g_sgu_v', 'g_post_mix', 'g_pre_ffn',
                                            'g_post_ffn', 'pool_scale')}
        P.update(pool_bd=pool_bd[l], sgu_wm=sgu_wm[l], sgu_bias=sgu_bias[l])
        return P

    xt = x.reshape(T, D)
    h = _rw("pre_mix", lambda x, g: _rms(x, g), [xt], [w['g_pre_mix'][0][None, :]], [(D, BF16)], tile=512)[0]
    saved, Ws = [], []
    first = shard_bufs(0, (0, 4))
    early = dict(zip((0, 4), _all_gather("gather_weights", [first[0], first[4]])))
    for l in range(L):
        own = shard_bufs(l, (1, 2, 3))
        nxt = shard_bufs(l + 1, (0, 4)) if l + 1 < L else None
        ride = {'in_proj': _Gather([own[1]]), 'attn_fwd': _Gather([own[3], own[2]])}
        w_in_rest = []
        if nxt:
            ride.update(ffn_up=_Gather([nxt[0]], ks=(0, 1, 2, 4, 6)), merge_fwd=_Gather([nxt[4]]))

            def rest(ride=ride, nxt=nxt, made=w_in_rest):
                made.append(_Gather([nxt[0]], ks=(3, 5, 7), out=ride['ffn_up'].results))
                return made[0]
            ride['ffn_down'] = rest

        def late(ride=ride):
            G = joined({1: ride['in_proj'].results[0], 3: ride['attn_fwd'].results[0], 2: ride['attn_fwd'].results[1]})
            return dict(Wout=G['w_out'][0], Wg=G['w_ffn_gate'][0], Wu=G['w_ffn_up'][0], Wdn=G['w_ffn_down'][0])

        We = {k: a[0] for k, a in _layer_weights(joined(early)).items()}
        We['W1s'], We['W1g'] = _w1(early[0])
        g_next = w['g_pre_mix'][l + 1][None, :] if nxt else None
        xt, h, sv, W = _layer_fwd(xt, h, We, late, layer_params(l), tabs, B, S, ride, g_next)
        saved.append(sv)
        Ws.append(W)
        if nxt:
            early = {0: w_in_rest[0].results[0], 4: ride['merge_fwd'].results[0]}

    def loss_f(y, t, f, g):
        e = y - t
        dy = e * (1.0 / D)
        df, dg = _rms_bwd(f, g, dy)
        return dy, df, jnp.sum(e * e, axis=0, keepdims=True), dg
    dy, df, sq, dg_ffn = _rw("loss", loss_f, [xt, loss_target.reshape(T, D), saved[-1]['f']], [w['g_post_ffn'][L - 1][None, :]],
                             [(D, F32), (D, BF16)], [(1, D), (1, D)], tile=512)
    loss = lax.psum(0.5 * jnp.sum(sq) / D, ("x", "y", "c"))

    recv = [lax.empty((N_DEV, L * r, PADDED.get(ns[0], shapes1[ns[0]])[1]), BF16) for ns, r in zip(GROUPS, group_rows)]

    def send_bufs(pg, groups):
        return _group({n: _shard_split(pg[n], SHARDED[n]).astype(BF16) for b in groups for n in GROUPS[b]}, 1, groups)

    small, w_in_first = [None] * L, []
    for l in reversed(range(L)):
        def own_ride(stage, grads, l=l):
            if stage == 'ffn':
                bufs = send_bufs({'w_ffn_gate': grads['Wg'][None], 'w_ffn_up': grads['Wu'][None],
                                  'w_ffn_down': grads['Wdn'][None], 'w_out': grads['Wout'][None]}, (1, 2, 3))
                return {'merge_bwd': _Scatter([bufs[1]], [recv[1]], [l * group_rows[1]]),
                        'attn_bwd': _Scatter([bufs[3], bufs[2]], [recv[3], recv[2]], [l * group_rows[3], l * group_rows[2]])}
            if stage == 'mixers':
                bufs = send_bufs(_mixer_grads(grads), (4,))
                return {'in_proj_g_dw': _Scatter([bufs[4]], [recv[4]], [l * group_rows[4]])}
            if stage == 'w_in':
                shards = _w_in_grad_shards(grads['W1s'], grads['W1g'])
                w_in_first.append(_Scatter([shards], [recv[0]], [l * group_rows[0]], ks=(0, 1, 2, 4, 6)))
                return {'in_proj_dx': w_in_first[-1]}
            first = w_in_first[-1]
            return {'edge': _Scatter(first.inputs[:1], first.results, [l * group_rows[0]], ks=(3, 5, 7))}

        prev = (saved[l - 1]['f'], w['g_post_ffn'][l - 1][None, :]) if l else None
        dy, df_prev, dg_prev, gl, ride = _layer_bwd(dy, df, saved[l], Ws[l], layer_params(l), tabs, B, S, own_ride, prev)
        gl['g_post_ffn'] = dg_ffn
        df, dg_ffn = df_prev, dg_prev
        for name, bs in (('in_proj_g_dw', (4,)), ('edge', (0,)), ('merge_bwd', (1,)), ('attn_bwd', (3, 2))):
            for b, res_b in zip(bs, ride[name].results):
                recv[b] = res_b
        pg = _small_grads(gl)
        rep = jnp.concatenate([pg[n].reshape(-1, 128) for n in REPLICATED], axis=0)
        small[l] = _pad_to(rep, -(-rep.shape[0] // 8) * 8, 128)
    grad_x = dy.reshape(B, S, D)

    small_rows = small[0].shape[0]
    got_small = _all_gather("gather_small_grads", [jnp.concatenate(small, axis=0)])[0].reshape(N_DEV, L, small_rows, 128)

    res = {}
    for n in names:
        b, off = where[n]
        r, c = shapes1[n]
        if n in NARROW:
            parts = recv[b].reshape(N_DEV, L, group_rows[b], -1)[:, :, off:off + r, :c].reshape(N_DEV, L * r, c)
            res[n] = _adamw("adamw_" + n, parts, _as2d(w[n]), _as2d(m[n]), _as2d(v[n]))
        else:
            res[n] = _adamw("adamw_" + n, recv[b], _as2d(w[n]), _as2d(m[n]), _as2d(v[n]), row_off=off, layers=L,
                            stride=group_rows[b])
    off = 0
    for n in REPLICATED:
        r = int(np.prod(w[n].shape[1:])) // 128
        shp2 = _as2d(w[n]).shape
        res[n] = _adamw("adamw_" + n, got_small[:, :, off:off + r].reshape((N_DEV,) + shp2), _as2d(w[n]), _as2d(m[n]), _as2d(v[n]))
        off += r
    outs = [loss, grad_x]
    for k in range(4):
        outs += [res[n][k].reshape(w[n].shape) for n in WEIGHTS]
    return tuple(outs)
```
